```python
import math
import jax, jax.numpy as jnp
from jax import lax
import numpy as np

D_MODEL = 1024
BATCH = 8
SEQ = 4096
DEPTH = 2

GRID_W = 64
CTX_LEN = 256
CONV_W = 3

HY_WIDTH = 256
HY_BANDS = 16
HY_EMB = 1 + 2 * HY_BANDS
HY_HIDDEN = 64

SC_WIDTH = 256

HG_HEADS = 4
HG_DK = 64
HG_DV = 64
HG_KEY = HG_HEADS * HG_DK
HG_VAL = HG_HEADS * HG_DV
HG_CHUNK = 16

SSD_HEADS = 4
SSD_HEADDIM = 64
SSD_INNER = SSD_HEADS * SSD_HEADDIM
SSD_STATE = 64
SSD_GROUPS = 2
SSD_CHUNK = 64
SSD_XBC = SSD_INNER + 2 * SSD_GROUPS * SSD_STATE

N_BRANCH = 4
BRANCH_W = 256

HY_COLS = 3 * HY_WIDTH
SC_COLS = 3 * SC_WIDTH
HG_COLS = 3 * HG_KEY + 2 * HG_VAL
SSD_COLS = SSD_INNER + SSD_XBC + 2 * SSD_HEADS
IN_COLS = HY_COLS + SC_COLS + HG_COLS + SSD_COLS
IN_SPLITS = (HY_COLS, HY_COLS + SC_COLS, HY_COLS + SC_COLS + HG_COLS)

N_EXPERTS = 16
N_EXPERT_GROUPS = 4
TOP_K = 2
D_FF_EXPERT = 512

DEEPNORM_ALPHA = (2 * DEPTH) ** 0.25
DEEPNORM_BETA = (8 * DEPTH) ** -0.25
LN_EPS = 1e-5
RMS_EPS = 1e-6

kernel_name = 'hybrid_dit_hyena_conv_hgrn2_ssd_moe'


def _layernorm(x, g, b):
    xf = x.astype(jnp.float32)
    mu = jnp.mean(xf, -1, keepdims=True)
    var = jnp.mean(jnp.square(xf - mu), -1, keepdims=True)
    return ((xf - mu) * lax.rsqrt(var + LN_EPS) * g + b).astype(x.dtype)


def _rmsnorm(x, g):
    xf = x.astype(jnp.float32)
    return (xf * lax.rsqrt(jnp.mean(jnp.square(xf), -1, keepdims=True) + RMS_EPS) * g).astype(x.dtype)


def _dwconv(u, w, b=None):
    L = u.shape[1]
    pad = CONV_W // 2
    up = jnp.pad(u, ((0, 0), (pad, pad), (0, 0)))
    y = sum(up[:, k:k + L] * w[k] for k in range(CONV_W))
    return y if b is None else y + b


def _sincos_1d(pos, dim):
    omega = 1.0 / (10000.0 ** (jnp.arange(dim // 2, dtype=jnp.float32) / (dim // 2)))
    ang = pos.astype(jnp.float32)[:, None] * omega[None]
    return jnp.concatenate([jnp.sin(ang), jnp.cos(ang)], -1)


def _grid_pos_embed(rows):
    row = jnp.repeat(jnp.arange(rows), GRID_W)
    col = jnp.tile(jnp.arange(GRID_W), rows)
    return jnp.concatenate([_sincos_1d(row, D_MODEL // 2), _sincos_1d(col, D_MODEL // 2)], -1)


def _hyena_filter(L, lp):
    t = jnp.linspace(0.0, 1.0, L, dtype=jnp.float32)[:, None]
    bands = jnp.linspace(1e-4, HY_BANDS - 1, HY_BANDS, dtype=jnp.float32)
    ang = (2.0 * math.pi / L) * jnp.arange(L, dtype=jnp.float32)[:, None] * bands[None]
    feat = jnp.concatenate([t, jnp.cos(ang), -jnp.sin(ang)], -1)
    h = jnp.sin(lp['hy_freq1'] * (feat @ lp['hy_w1'] + lp['hy_b1']))
    h = jnp.sin(lp['hy_freq2'] * (h @ lp['hy_w2'] + lp['hy_b2']))
    h = (h @ lp['hy_w3']).astype(jnp.float32)
    window = jnp.exp(-t * jnp.abs(lp['hy_decay'].astype(jnp.float32)))
    h_fwd = h[:, :HY_WIDTH] * window
    h_bwd = h[:, HY_WIDTH:] * window
    filt = jnp.concatenate([h_fwd, jnp.zeros((1, HY_WIDTH), jnp.float32), h_bwd[:0:-1]], 0)
    return filt / jnp.sum(jnp.abs(filt), 0, keepdims=True)


def _fftconv(u, filt, bias):
    L = u.shape[1]
    uf = u.astype(jnp.float32)
    spec = jnp.fft.rfft(uf, n=2 * L, axis=1) * jnp.fft.rfft(filt, n=2 * L, axis=0)[None]
    y = jnp.fft.irfft(spec, n=2 * L, axis=1)[:, :L]
    return (y + uf * bias.astype(jnp.float32)).astype(u.dtype)


def _hyena_branch(u, lp):
    u = _dwconv(u, lp['hy_conv_w'], lp['hy_conv_b'])
    x0, x1, v = jnp.split(u, 3, -1)
    filt = _hyena_filter(u.shape[1], lp)
    return x0 * _fftconv(x1 * v, filt, lp['hy_bias'])


def _shortconv_branch(u, conv_w):
    bg, cg, xs = jnp.split(u, 3, -1)
    return bg * _dwconv(cg * xs, conv_w)


def _gla_chunked(q, k, v, logf, s0):
    Bsz, L, H, DK = q.shape
    DV = v.shape[-1]
    n = L // HG_CHUNK
    blk = lambda t: t.astype(jnp.float32).reshape(Bsz, n, HG_CHUNK, H, t.shape[-1])
    q, k, v, logf = blk(q), blk(k), blk(v), blk(logf)
    b = jnp.cumsum(logf, axis=2)
    b_last = b[:, :, -1]
    q_dec = q * jnp.exp(b)
    k_inv = k * jnp.exp(-b)
    k_end = k * jnp.exp(b_last[:, :, None] - b)
    causal = jnp.tril(jnp.ones((HG_CHUNK, HG_CHUNK), bool))
    scores = jnp.where(causal, jnp.einsum('bnthd,bnshd->bnhts', q_dec, k_inv), 0.0)
    o_intra = jnp.einsum('bnhts,bnshv->bnthv', scores, v)

    def step(S, inp):
        qd, ke, vc, bl = inp
        o = jnp.einsum('bthd,bhdv->bthv', qd, S)
        S = jnp.exp(bl)[..., None] * S + jnp.einsum('bshd,bshv->bhdv', ke, vc)
        return S, o

    mv = lambda t: jnp.moveaxis(t, 1, 0)
    s_final, o_inter = lax.scan(step, s0, (mv(q_dec), mv(k_end), mv(v), mv(b_last)))
    o = o_intra + jnp.moveaxis(o_inter, 0, 1)
    return o.reshape(Bsz, L, H, DV), s_final


def _hgrn_lower_bounds(logits):
    p = jax.nn.softmax(logits.astype(jnp.float32), axis=1)
    return jnp.cumsum(p, axis=1) - p[:, :1]


def _hgrn_prep(u, lb):
    Bsz, L = u.shape[:2]
    q, f_f, f_b, i, g = jnp.split(u, (HG_KEY, 2 * HG_KEY, 3 * HG_KEY, 3 * HG_KEY + HG_VAL), -1)
    heads = lambda t: t.reshape(Bsz, L, HG_HEADS, -1)
    logf = [heads(jnp.log(lb[d] + (1.0 - lb[d]) * jax.nn.sigmoid(f.astype(jnp.float32))))
            for d, f in enumerate((f_f, f_b))]
    return heads(q.astype(jnp.float32)) * HG_DK ** -0.5, logf, heads(i), g


def _hgrn_bidir(q, logf, v, s_f, s_b):
    flip = lambda t: jnp.flip(t, axis=1)
    o_f, s_f = _gla_chunked(q, -jnp.expm1(logf[0]), v, logf[0], s_f)
    o_b, s_b = _gla_chunked(flip(q), flip(-jnp.expm1(logf[1])), flip(v), flip(logf[1]), s_b)
    return o_f + flip(o_b), s_f, s_b


def _hgrn_readout(o, g, norm_g):
    o = _rmsnorm(o, norm_g).reshape(g.shape)
    return (o * jax.nn.silu(g.astype(jnp.float32))).astype(g.dtype)


def _hgrn_branch(u_ctx, u_lat, lb, norm_g, ctx_out):
    s0 = jnp.zeros((u_lat.shape[0], HG_HEADS, HG_DK, HG_DV), jnp.float32)
    q_c, lf_c, v_c, g_c = _hgrn_prep(u_ctx, lb)
    q_l, lf_l, v_l, g_l = _hgrn_prep(u_lat, lb)
    o_c, s_f, s_b = _hgrn_bidir(q_c, lf_c, v_c, s0, s0)
    o_l, _, _ = _hgrn_bidir(q_l, lf_l, v_l, s_f, s_b)
    y_l = _hgrn_readout(o_l, g_l, norm_g)
    y_c = _hgrn_readout(o_c, g_c, norm_g) if ctx_out else None
    return y_c, y_l


def _ssd_chunked(x, a, bm, cm, s0):
    Bsz, L, H, P = x.shape
    n = L // SSD_CHUNK
    blk = lambda t: t.astype(jnp.float32).reshape((Bsz, n, SSD_CHUNK) + t.shape[2:])
    x, a, bm, cm = blk(x), blk(a), blk(bm), blk(cm)
    cum = jnp.cumsum(a, axis=2)
    causal = jnp.tril(jnp.ones((SSD_CHUNK, SSD_CHUNK), bool))[:, :, None]
    decay = jnp.exp(jnp.where(causal, cum[:, :, :, None, :] - cum[:, :, None, :, :], -jnp.inf))
    scores = jnp.einsum('bcthn,bcshn->bctsh', cm, bm) * decay
    y_diag = jnp.einsum('bctsh,bcshp->bcthp', scores, x)
    states = jnp.einsum('bcshn,bcshp->bchpn', bm * jnp.exp(cum[:, :, -1:] - cum)[..., None], x)
    c_dec = cm * jnp.exp(cum)[..., None]
    total = jnp.exp(cum[:, :, -1])

    def step(S, inp):
        cd, st, tot = inp
        y = jnp.einsum('bthn,bhpn->bthp', cd, S)
        return tot[:, :, None, None] * S + st, y

    mv = lambda t: jnp.moveaxis(t, 1, 0)
    s_final, y_off = lax.scan(step, s0, (mv(c_dec), mv(states), mv(total)))
    y = y_diag + jnp.moveaxis(y_off, 0, 1)
    return y.reshape(Bsz, L, H, P), s_final


def _ssd_prep(u, lp):
    Bsz, L = u.shape[:2]
    z, xbc, dt = jnp.split(u, (SSD_INNER, SSD_INNER + SSD_XBC), -1)
    xbc = jax.nn.silu(_dwconv(xbc, lp['ssd_conv_w'], lp['ssd_conv_b']))
    xs, bm, cm = jnp.split(xbc, (SSD_INNER, SSD_INNER + SSD_GROUPS * SSD_STATE), -1)
    rep = SSD_HEADS // SSD_GROUPS
    xs = xs.reshape(Bsz, L, SSD_HEADS, SSD_HEADDIM)
    bm = jnp.repeat(bm.reshape(Bsz, L, SSD_GROUPS, SSD_STATE), rep, axis=2)
    cm = jnp.repeat(cm.reshape(Bsz, L, SSD_GROUPS, SSD_STATE), rep, axis=2)
    dt = jax.nn.softplus(dt.astype(jnp.float32).reshape(Bsz, L, 2, SSD_HEADS) + lp['ssd_dt_bias'])
    a = -jnp.exp(lp['ssd_a_log'].astype(jnp.float32)) * dt
    return z, xs, bm, cm, dt, a


def _ssd_bidir(xs, bm, cm, dt, a, d_skip, s_f, s_b):
    flip = lambda t: jnp.flip(t, axis=1)
    xf = xs.astype(jnp.float32)
    y_f, s_f = _ssd_chunked(xf * dt[:, :, 0, :, None], a[:, :, 0], bm, cm, s_f)
    y_b, s_b = _ssd_chunked(flip(xf * dt[:, :, 1, :, None]), flip(a[:, :, 1]), flip(bm), flip(cm), s_b)
    y = y_f + flip(y_b) + xf * d_skip.astype(jnp.float32)[:, None]
    return y, s_f, s_b


def _ssd_readout(y, z, norm_g):
    y = y.reshape(z.shape)
    return _rmsnorm(y * jax.nn.silu(z.astype(jnp.float32)), norm_g).astype(z.dtype)


def _ssd_branch(u_ctx, u_lat, lp, ctx_out):
    s0 = jnp.zeros((u_lat.shape[0], SSD_HEADS, SSD_HEADDIM, SSD_STATE), jnp.float32)
    z_c, *rest_c = _ssd_prep(u_ctx, lp)
    z_l, *rest_l = _ssd_prep(u_lat, lp)
    y_c, s_f, s_b = _ssd_bidir(*rest_c, lp['ssd_d'], s0, s0)
    y_l, _, _ = _ssd_bidir(*rest_l, lp['ssd_d'], s_f, s_b)
    out_l = _ssd_readout(y_l, z_l, lp['ssd_norm_g'])
    out_c = _ssd_readout(y_c, z_c, lp['ssd_norm_g']) if ctx_out else None
    return out_c, out_l


def _merge(h, branches, lp):
    gates = jnp.split(jax.nn.sigmoid(h @ lp['w_gate'] + lp['b_gate']), N_BRANCH, -1)
    y = sum(g * (br @ lp['w_br'][k]) for k, (g, br) in enumerate(zip(gates, branches)))
    return y @ lp['w_o']


def _token_mixer(h_lat, h_ctx, lp, lb, ctx_out):
    u_lat = h_lat @ lp['w_in']
    u_ctx = h_ctx @ lp['w_in']
    hy_l, sc_l, hg_l, ssd_l = jnp.split(u_lat, IN_SPLITS, -1)
    hy_c, sc_c, hg_c, ssd_c = jnp.split(u_ctx, IN_SPLITS, -1)
    hg_y_c, hg_y_l = _hgrn_branch(hg_c, hg_l, lb, lp['hg_norm_g'], ctx_out)
    ssd_y_c, ssd_y_l = _ssd_branch(ssd_c, ssd_l, lp, ctx_out)
    y_lat = _merge(h_lat, (_hyena_branch(hy_l, lp), _shortconv_branch(sc_l, lp['sc_conv_w']), hg_y_l, ssd_y_l), lp)
    y_ctx = None
    if ctx_out:
        y_ctx = _merge(h_ctx, (_hyena_branch(hy_c, lp), _shortconv_branch(sc_c, lp['sc_conv_w']), hg_y_c, ssd_y_c), lp)
    return y_lat, y_ctx


def _moe(h, w_router, b_router, w1, w3, w2):
    scores = jax.nn.softmax((h @ w_router).astype(jnp.float32), -1)
    sel = scores + b_router.astype(jnp.float32)
    grp = sel.reshape(sel.shape[:-1] + (N_EXPERT_GROUPS, N_EXPERTS // N_EXPERT_GROUPS))
    grp_score = jnp.sum(lax.top_k(grp, TOP_K)[0], -1)
    in_grp = jax.nn.one_hot(jnp.argmax(grp_score, -1), N_EXPERT_GROUPS, dtype=bool)[..., None]
    masked = jnp.where(in_grp, grp, -jnp.inf).reshape(sel.shape)
    _, idx = lax.top_k(masked, TOP_K)
    w = jnp.take_along_axis(scores, idx, -1)
    w = w / jnp.sum(w, -1, keepdims=True)
    gate = jnp.sum(jax.nn.one_hot(idx, N_EXPERTS, dtype=jnp.float32) * w[..., None], -2).astype(h.dtype)
    out = 0
    for e in range(N_EXPERTS):
        ye = (jax.nn.silu(h @ w1[e]) * (h @ w3[e])) @ w2[e]
        out = out + gate[..., e:e + 1] * ye
    return out


def setup_inputs(seed: int = 0) -> dict:
    key = jax.random.key(seed)
    ks = iter(jax.random.split(key, 48))
    nrm = lambda shape, s: s * jax.random.normal(next(ks), shape, jnp.float32)
    D = D_MODEL
    dt0 = jnp.exp(jax.random.uniform(next(ks), (DEPTH, 2, SSD_HEADS), jnp.float32, math.log(1e-3), math.log(1e-1)))
    decay0 = jnp.abs(jnp.linspace(math.log(1e-2) / 0.3, math.log(1e-2) / 1.5, HY_WIDTH, dtype=jnp.float32))
    return {
        'x': nrm((BATCH, SEQ, D), 1.0),
        'c': nrm((BATCH, D), 1.0),
        'ctx': nrm((BATCH, CTX_LEN, D), 1.0),
        'c_ctx': nrm((D,), 1.0),
        'w_ada': nrm((DEPTH, D, 6 * D), 0.5 * D ** -0.5),
        'b_ada': nrm((DEPTH, 6 * D), 0.02),
        'w_in': nrm((DEPTH, D, IN_COLS), D ** -0.5),
        'hy_conv_w': nrm((DEPTH, CONV_W, HY_COLS), CONV_W ** -0.5),
        'hy_conv_b': nrm((DEPTH, HY_COLS), 0.02),
        'hy_w1': nrm((DEPTH, HY_EMB, HY_HIDDEN), HY_EMB ** -0.5),
        'hy_b1': nrm((DEPTH, HY_HIDDEN), 0.1),
        'hy_freq1': 1.0 + nrm((DEPTH, HY_HIDDEN), 0.1),
        'hy_w2': nrm((DEPTH, HY_HIDDEN, HY_HIDDEN), HY_HIDDEN ** -0.5),
        'hy_b2': nrm((DEPTH, HY_HIDDEN), 0.1),
        'hy_freq2': 1.0 + nrm((DEPTH, HY_HIDDEN), 0.1),
        'hy_w3': nrm((DEPTH, HY_HIDDEN, 2 * HY_WIDTH), HY_HIDDEN ** -0.5),
        'hy_decay': decay0[None] * (1.0 + nrm((DEPTH, HY_WIDTH), 0.05)),
        'hy_bias': nrm((DEPTH, HY_WIDTH), 0.1),
        'sc_conv_w': nrm((DEPTH, CONV_W, SC_WIDTH), CONV_W ** -0.5),
        'hg_lb_logits': nrm((2, DEPTH, HG_KEY), 0.5),
        'hg_norm_g': 1.0 + nrm((DEPTH, HG_DV), 0.02),
        'ssd_conv_w': nrm((DEPTH, CONV_W, SSD_XBC), CONV_W ** -0.5),
        'ssd_conv_b': nrm((DEPTH, SSD_XBC), 0.02),
        'ssd_a_log': jnp.log(jax.random.uniform(next(ks), (DEPTH, 2, SSD_HEADS), jnp.float32, 1.0, 16.0)),
        'ssd_dt_bias': dt0 + jnp.log(-jnp.expm1(-dt0)),
        'ssd_d': 1.0 + nrm((DEPTH, SSD_HEADS), 0.1),
        'ssd_norm_g': 1.0 + nrm((DEPTH, SSD_INNER), 0.02),
        'w_gate': nrm((DEPTH, D, N_BRANCH * D), D ** -0.5),
        'b_gate': nrm((DEPTH, N_BRANCH * D), 0.02),
        'w_br': nrm((DEPTH, N_BRANCH, BRANCH_W, D), BRANCH_W ** -0.5),
        'w_o': nrm((DEPTH, D, D), DEEPNORM_BETA * D ** -0.5),
        'ln1_g': 1.0 + nrm((DEPTH, D), 0.02),
        'ln1_b': nrm((DEPTH, D), 0.02),
        'ln2_g': 1.0 + nrm((DEPTH, D), 0.02),
        'ln2_b': nrm((DEPTH, D), 0.02),
        'w_router': nrm((D, N_EXPERTS), D ** -0.5),
        'b_router': nrm((N_EXPERTS,), 0.01),
        'w_e1': nrm((DEPTH, N_EXPERTS, D, D_FF_EXPERT), D ** -0.5),
        'w_e3': nrm((DEPTH, N_EXPERTS, D, D_FF_EXPERT), D ** -0.5),
        'w_e2': nrm((DEPTH, N_EXPERTS, D_FF_EXPERT, D), DEEPNORM_BETA * D_FF_EXPERT ** -0.5),
    }


def reference(x, c, ctx, c_ctx, w_ada, b_ada, w_in, hy_conv_w, hy_conv_b, hy_w1, hy_b1, hy_freq1,
              hy_w2, hy_b2, hy_freq2, hy_w3, hy_decay, hy_bias, sc_conv_w, hg_lb_logits, hg_norm_g,
              ssd_conv_w, ssd_conv_b, ssd_a_log, ssd_dt_bias, ssd_d, ssd_norm_g, w_gate, b_gate, w_br,
              w_o, ln1_g, ln1_b, ln2_g, ln2_b, w_router, b_router, w_e1, w_e3, w_e2):
    ROWS = x.shape[1] // GRID_W
    lat = x + _grid_pos_embed(ROWS).astype(x.dtype)[None]
    cx = ctx
    lower_bounds = _hgrn_lower_bounds(hg_lb_logits)
    for l in range(DEPTH):
        ctx_out = l < DEPTH - 1
        lp = dict(w_in=w_in[l], hy_conv_w=hy_conv_w[l], hy_conv_b=hy_conv_b[l], hy_w1=hy_w1[l],
                  hy_b1=hy_b1[l], hy_freq1=hy_freq1[l], hy_w2=hy_w2[l], hy_b2=hy_b2[l],
                  hy_freq2=hy_freq2[l], hy_w3=hy_w3[l], hy_decay=hy_decay[l], hy_bias=hy_bias[l],
                  sc_conv_w=sc_conv_w[l], hg_norm_g=hg_norm_g[l], ssd_conv_w=ssd_conv_w[l],
                  ssd_conv_b=ssd_conv_b[l], ssd_a_log=ssd_a_log[l], ssd_dt_bias=ssd_dt_bias[l],
                  ssd_d=ssd_d[l], ssd_norm_g=ssd_norm_g[l], w_gate=w_gate[l], b_gate=b_gate[l],
                  w_br=w_br[l], w_o=w_o[l])
        mod = jax.nn.silu(c) @ w_ada[l] + b_ada[l]
        mod_c = jax.nn.silu(c_ctx) @ w_ada[l] + b_ada[l]
        sh1, sc1, g1, sh2, sc2, g2 = jnp.split(mod[:, None, :], 6, -1)
        csh1, csc1, cg1, csh2, csc2, cg2 = jnp.split(mod_c, 6, -1)
        y_lat, y_ctx = _token_mixer(lat * (1 + sc1) + sh1, cx * (1 + csc1) + csh1, lp, lower_bounds[:, l], ctx_out)
        moe = lambda h: _moe(h, w_router, b_router, w_e1[l], w_e3[l], w_e2[l])
        lat = _layernorm(DEEPNORM_ALPHA * lat + g1 * y_lat, ln1_g[l], ln1_b[l])
        lat = _layernorm(DEEPNORM_ALPHA * lat + g2 * moe(lat * (1 + sc2) + sh2), ln2_g[l], ln2_b[l])
        if ctx_out:
            cx = _layernorm(DEEPNORM_ALPHA * cx + cg1 * y_ctx, ln1_g[l], ln1_b[l])
            cx = _layernorm(DEEPNORM_ALPHA * cx + cg2 * moe(cx * (1 + csc2) + csh2), ln2_g[l], ln2_b[l])
    return lat
```

```python
import functools
import math

import jax
import jax.numpy as jnp
import numpy as np
from jax import lax
from jax.experimental import pallas as pl
from jax.experimental.pallas import tpu as pltpu

F32 = jnp.float32
BF16 = jnp.bfloat16

N_EXPERTS = 16
N_EXPERT_GROUPS = 4
GROUP_SIZE = N_EXPERTS // N_EXPERT_GROUPS
DEPTH = 2
DEEPNORM_ALPHA = (2 * DEPTH) ** 0.25
LN_EPS = 1e-5
RMS_EPS = 1e-6

VMEM_LIMIT_BYTES = 52 * 1024 * 1024


def _layernorm_rows(z, g, b):
    mu = jnp.mean(z, axis=-1, keepdims=True)
    zc = z - mu
    var = jnp.mean(zc * zc, axis=-1, keepdims=True)
    return zc * lax.rsqrt(var + LN_EPS) * g + b


def _first_argmax(m, lane):
    top = jnp.max(m, axis=-1, keepdims=True)
    idx = jnp.min(jnp.where(m == top, lane, N_EXPERTS), axis=-1, keepdims=True)
    return top, idx


def _route(h, wr_ref, br_ref):
    logits = jnp.dot(h, wr_ref[...], precision=lax.Precision.HIGHEST, preferred_element_type=F32)
    logits = logits - jnp.max(logits, axis=-1, keepdims=True)
    ex = jnp.exp(logits)
    scores = ex / jnp.sum(ex, axis=-1, keepdims=True)
    sel = scores + br_ref[...]
    lane = lax.broadcasted_iota(jnp.int32, sel.shape, 1)
    neg = jnp.float32(-jnp.inf)
    best_score = None
    best_grp = None
    for g in range(N_EXPERT_GROUPS):
        in_g = (lane >= g * GROUP_SIZE) & (lane < (g + 1) * GROUP_SIZE)
        m = jnp.where(in_g, sel, neg)
        t1, i1 = _first_argmax(m, lane)
        t2 = jnp.max(jnp.where(lane == i1, neg, m), axis=-1, keepdims=True)
        s = t1 + t2
        if g == 0:
            best_score, best_grp = s, jnp.zeros_like(i1)
        else:
            better = s > best_score
            best_score = jnp.where(better, s, best_score)
            best_grp = jnp.where(better, g, best_grp)
    lo = best_grp * GROUP_SIZE
    masked = jnp.where((lane >= lo) & (lane < lo + GROUP_SIZE), sel, neg)
    _, ia = _first_argmax(masked, lane)
    _, ib = _first_argmax(jnp.where(lane == ia, neg, masked), lane)
    picked = (lane == ia) | (lane == ib)
    w = jnp.where(picked, scores, 0.0)
    return w / jnp.sum(w, axis=-1, keepdims=True)


def _moe_ln_kernel(x_ref, mod_ref, wr_ref, br_ref, w1_ref, w3_ref, w2_ref, lng_ref, lnb_ref,
                   o_ref, h_scr, gate_scr, acc_scr):
    e = pl.program_id(1)

    @pl.when(e == 0)
    def _():
        h = _modulate(x_ref[...], mod_ref, MOD_SHIFT2, MOD_SCALE2)
        h_scr[...] = h.astype(BF16)
        gate_scr[...] = _route(h, wr_ref, br_ref)
        acc_scr[...] = jnp.zeros_like(acc_scr)

    h = h_scr[...]
    a = jnp.dot(h, w1_ref[0], preferred_element_type=F32)
    b = jnp.dot(h, w3_ref[0], preferred_element_type=F32)
    mid = (a * jax.nn.sigmoid(a) * b).astype(BF16)
    y = jnp.dot(mid, w2_ref[0], preferred_element_type=F32)
    gate = gate_scr[...]
    lane = lax.broadcasted_iota(jnp.int32, gate.shape, 1)
    gcol = jnp.sum(jnp.where(lane == e, gate, 0.0), axis=-1, keepdims=True)
    acc_scr[...] += gcol * y

    @pl.when(e == N_EXPERTS - 1)
    def _():
        z = DEEPNORM_ALPHA * x_ref[...] + mod_ref[0, MOD_GATE2:MOD_GATE2 + 1, :] * acc_scr[...]
        o_ref[...] = _layernorm_rows(z, lng_ref[...], lnb_ref[...])


def moe_ln(x, mod, w_router, b_router, w1, w3, w2, ln_g, ln_b, *, rows_per_mod, tm):
    T, D = x.shape
    E, _, FF = w1.shape
    assert T % tm == 0 and rows_per_mod % tm == 0
    tiles_per_mod = rows_per_mod // tm
    return pl.pallas_call(
        _moe_ln_kernel,
        grid=(T // tm, E),
        in_specs=[
            pl.BlockSpec((tm, D), lambda i, e: (i, 0)),
            pl.BlockSpec((1, 6, D), lambda i, e: (i // tiles_per_mod, 0, 0)),
            pl.BlockSpec((D, E), lambda i, e: (0, 0)),
            pl.BlockSpec((1, E), lambda i, e: (0, 0)),
            pl.BlockSpec((1, D, FF), lambda i, e: (e, 0, 0)),
            pl.BlockSpec((1, D, FF), lambda i, e: (e, 0, 0)),
            pl.BlockSpec((1, FF, D), lambda i, e: (e, 0, 0)),
            pl.BlockSpec((1, D), lambda i, e: (0, 0)),
            pl.BlockSpec((1, D), lambda i, e: (0, 0)),
        ],
        out_specs=pl.BlockSpec((tm, D), lambda i, e: (i, 0)),
        out_shape=jax.ShapeDtypeStruct((T, D), F32),
        scratch_shapes=[
            pltpu.VMEM((tm, D), BF16),
            pltpu.VMEM((tm, E), F32),
            pltpu.VMEM((tm, D), F32),
        ],
        compiler_params=pltpu.CompilerParams(
            dimension_semantics=("parallel", "arbitrary"), vmem_limit_bytes=VMEM_LIMIT_BYTES),
        name="moe_ln",
    )(x, mod, w_router, b_router.reshape(1, E), w1, w3, w2, ln_g.reshape(1, D), ln_b.reshape(1, D))


def _ada_kernel(c_ref, w_ref, b_ref, o_ref):
    c = c_ref[...]
    s = c * jax.nn.sigmoid(c)
    o_ref[0] = jnp.dot(s, w_ref[0], precision=lax.Precision.HIGHEST, preferred_element_type=F32) + b_ref[0]


def ada_mod(cvecs, w_ada, b_ada, *, tn=1536):
    R, D = cvecs.shape
    depth, _, N = w_ada.shape
    assert N % tn == 0
    return pl.pallas_call(
        _ada_kernel,
        grid=(depth, N // tn),
        in_specs=[
            pl.BlockSpec((R, D), lambda l, j: (0, 0)),
            pl.BlockSpec((1, D, tn), lambda l, j: (l, 0, j)),
            pl.BlockSpec((1, 1, tn), lambda l, j: (l, 0, j)),
        ],
        out_specs=pl.BlockSpec((1, R, tn), lambda l, j: (l, 0, j)),
        out_shape=jax.ShapeDtypeStruct((depth, R, N), F32),
        compiler_params=pltpu.CompilerParams(
            dimension_semantics=("parallel", "parallel"), vmem_limit_bytes=VMEM_LIMIT_BYTES),
        name="ada_mod",
    )(cvecs, w_ada, b_ada.reshape(depth, 1, N))


MOD_SHIFT1, MOD_SCALE1, MOD_GATE1, MOD_SHIFT2, MOD_SCALE2, MOD_GATE2 = range(6)


def _modulate(x, mod_ref, shift_row, scale_row):
    return x * (1.0 + mod_ref[0, scale_row:scale_row + 1, :]) + mod_ref[0, shift_row:shift_row + 1, :]


def _in_proj_kernel(x_ref, mod_ref, w_ref, *out_refs):
    h = _modulate(x_ref[...], mod_ref, MOD_SHIFT1, MOD_SCALE1).astype(BF16)
    off = 0
    for o_ref in out_refs:
        n = o_ref.shape[1]
        o_ref[...] = jnp.dot(h, w_ref[:, off:off + n], preferred_element_type=F32).astype(o_ref.dtype)
        off += n


def in_proj(x, mod, w, widths, *, rows_per_mod, tm):
    T, D = x.shape
    assert T % tm == 0 and rows_per_mod % tm == 0 and w.shape[1] == sum(widths)
    tiles_per_mod = rows_per_mod // tm
    return pl.pallas_call(
        _in_proj_kernel,
        grid=(T // tm,),
        in_specs=[
            pl.BlockSpec((tm, D), lambda i: (i, 0)),
            pl.BlockSpec((1, 6, D), lambda i: (i // tiles_per_mod, 0, 0)),
            pl.BlockSpec(w.shape, lambda i: (0, 0), pipeline_mode=pl.Buffered(1)),
        ],
        out_specs=[pl.BlockSpec((tm, n), lambda i: (i, 0)) for n in widths],
        out_shape=[jax.ShapeDtypeStruct((T, n), F32) for n in widths],
        compiler_params=pltpu.CompilerParams(
            dimension_semantics=("parallel",), vmem_limit_bytes=VMEM_LIMIT_BYTES),
        name="in_proj",
    )(x, mod, w)


N_BRANCH = 4


def _merge_kernel(x_ref, mod_ref, hy_ref, sc_ref, hg_ref, ssd_ref, wg_ref, bg_ref, wbr_ref, wo_ref,
                  lng_ref, lnb_ref, o_ref):
    x = x_ref[...]
    D = x.shape[1]
    h = _modulate(x, mod_ref, MOD_SHIFT1, MOD_SCALE1).astype(BF16)
    y = None
    for k, br_ref in enumerate((hy_ref, sc_ref, hg_ref, ssd_ref)):
        gate = jax.nn.sigmoid(
            jnp.dot(h, wg_ref[:, k * D:(k + 1) * D], preferred_element_type=F32) + bg_ref[:, k * D:(k + 1) * D])
        term = gate * jnp.dot(br_ref[...].astype(BF16), wbr_ref[k], preferred_element_type=F32)
        y = term if y is None else y + term
    y = jnp.dot(y.astype(BF16), wo_ref[...], preferred_element_type=F32)
    z = DEEPNORM_ALPHA * x + mod_ref[0, MOD_GATE1:MOD_GATE1 + 1, :] * y
    o_ref[...] = _layernorm_rows(z, lng_ref[...], lnb_ref[...])


def merge_ln(x, mod, branches, w_gate, b_gate, w_br, w_o, ln_g, ln_b, *, rows_per_mod, tm):
    T, D = x.shape
    BW = branches[0].shape[1]
    assert T % tm == 0 and rows_per_mod % tm == 0
    tiles_per_mod = rows_per_mod // tm
    const = lambda shape: pl.BlockSpec(shape, lambda i: (0,) * len(shape), pipeline_mode=pl.Buffered(1))
    return pl.pallas_call(
        _merge_kernel,
        grid=(T // tm,),
        in_specs=[
            pl.BlockSpec((tm, D), lambda i: (i, 0)),
            pl.BlockSpec((1, 6, D), lambda i: (i // tiles_per_mod, 0, 0)),
        ] + [pl.BlockSpec((tm, BW), lambda i: (i, 0))] * N_BRANCH + [
            const((D, N_BRANCH * D)), const((1, N_BRANCH * D)), const((N_BRANCH, BW, D)), const((D, D)),
            const((1, D)), const((1, D)),
        ],
        out_specs=pl.BlockSpec((tm, D), lambda i: (i, 0)),
        out_shape=jax.ShapeDtypeStruct((T, D), F32),
        compiler_params=pltpu.CompilerParams(
            dimension_semantics=("parallel",), vmem_limit_bytes=VMEM_LIMIT_BYTES),
        name="merge_ln",
    )(x, mod, *branches, w_gate, b_gate.reshape(1, -1), w_br, w_o, ln_g.reshape(1, D), ln_b.reshape(1, D))


def _dwconv_rows(u, w_ref, col0=0):
    L, C = u.shape
    row = lax.broadcasted_iota(jnp.int32, u.shape, 0)
    prev = jnp.where(row == 0, 0.0, pltpu.roll(u, 1, 0))
    nxt = jnp.where(row == L - 1, 0.0, pltpu.roll(u, L - 1, 0))
    w = lambda k: w_ref[k:k + 1, col0:col0 + C]
    return prev * w(0) + u * w(1) + nxt * w(2)


def _hy_prep_kernel(x0_ref, x1_ref, v_ref, w0_ref, w1_ref, w2_ref, b0_ref, b1_ref, b2_ref, x0c_ref, hw_ref):
    x0c_ref[0] = _dwconv_rows(x0_ref[0], w0_ref) + b0_ref[...]
    x1c = _dwconv_rows(x1_ref[0], w1_ref) + b1_ref[...]
    vc = _dwconv_rows(v_ref[0], w2_ref) + b2_ref[...]
    hw_ref[0] = x1c * vc


def hy_prep(hy, conv_w, conv_b, *, cb=128):
    B, L, C3 = hy.shape
    W = C3 // 3
    nb = W // cb
    conv_b = conv_b.reshape(1, C3)
    data = lambda part: pl.BlockSpec((1, L, cb), lambda b, j: (b, 0, part * nb + j))
    wspec = lambda part: pl.BlockSpec((3, cb), lambda b, j: (0, part * nb + j))
    bspec = lambda part: pl.BlockSpec((1, cb), lambda b, j: (0, part * nb + j))
    out = pl.BlockSpec((1, L, cb), lambda b, j: (b, 0, j))
    return pl.pallas_call(
        _hy_prep_kernel,
        grid=(B, nb),
        in_specs=[data(0), data(1), data(2), wspec(0), wspec(1), wspec(2), bspec(0), bspec(1), bspec(2)],
        out_specs=[out, out],
        out_shape=[jax.ShapeDtypeStruct((B, L, W), F32), jax.ShapeDtypeStruct((B, L, W), F32)],
        compiler_params=pltpu.CompilerParams(
            dimension_semantics=("parallel", "parallel"), vmem_limit_bytes=VMEM_LIMIT_BYTES),
        name="hy_prep",
    )(hy, hy, hy, conv_w, conv_w, conv_w, conv_b, conv_b, conv_b)


def _sc_kernel(bg_ref, cg_ref, xs_ref, w_ref, o_ref):
    o_ref[0] = bg_ref[0] * _dwconv_rows(cg_ref[0] * xs_ref[0], w_ref)


def shortconv(sc, conv_w, *, cb=128):
    B, L, C3 = sc.shape
    W = C3 // 3
    nb = W // cb
    data = lambda part: pl.BlockSpec((1, L, cb), lambda b, j: (b, 0, part * nb + j))
    return pl.pallas_call(
        _sc_kernel,
        grid=(B, nb),
        in_specs=[data(0), data(1), data(2), pl.BlockSpec((3, cb), lambda b, j: (0, j))],
        out_specs=pl.BlockSpec((1, L, cb), lambda b, j: (b, 0, j)),
        out_shape=jax.ShapeDtypeStruct((B, L, W), F32),
        compiler_params=pltpu.CompilerParams(
            dimension_semantics=("parallel", "parallel"), vmem_limit_bytes=VMEM_LIMIT_BYTES),
        name="shortconv",
    )(sc, sc, sc, conv_w)


def _ssd_prep_kernel(u_ref, w_ref, b_ref, o_ref):
    y = _dwconv_rows(u_ref[0], w_ref) + b_ref[...]
    o_ref[0] = y * jax.nn.sigmoid(y)


def ssd_prep(ssd, conv_w, conv_b, *, col0, cb=128):
    B, L, _ = ssd.shape
    C = conv_w.shape[1]
    assert col0 % cb == 0 and C % cb == 0
    return pl.pallas_call(
        _ssd_prep_kernel,
        grid=(B, C // cb),
        in_specs=[
            pl.BlockSpec((1, L, cb), lambda b, j: (b, 0, col0 // cb + j)),
            pl.BlockSpec((3, cb), lambda b, j: (0, j)),
            pl.BlockSpec((1, cb), lambda b, j: (0, j)),
        ],
        out_specs=pl.BlockSpec((1, L, cb), lambda b, j: (b, 0, j)),
        out_shape=jax.ShapeDtypeStruct((B, L, C), F32),
        compiler_params=pltpu.CompilerParams(
            dimension_semantics=("parallel", "parallel"), vmem_limit_bytes=VMEM_LIMIT_BYTES),
        name="ssd_prep",
    )(ssd, conv_w, conv_b.reshape(1, C))


SSD_HEADS = 4
SSD_HEADDIM = 64
SSD_STATE = 64
SSD_GROUPS = 2
SSD_INNER = SSD_HEADS * SSD_HEADDIM
LANES = 128
NEG_INF = float("-inf")


def _scan_chunk_index(ph, j, n):
    return j + (1 - ph) * (n - 1 - 2 * j)


def _ssd_kernel(z_ref, xbc_ref, dt_ref, par_ref, dskip_ref, ng_ref, sf0_ref, sb0_ref,
                y_ref, sf_ref, sb_ref, yb_scr, st_scr, *, n_chunks):
    ph = pl.program_id(1)
    j = pl.program_id(2)
    is_fwd = ph == 1
    c = _scan_chunk_index(ph, j, n_chunks)
    R = xbc_ref.shape[1]

    @pl.when((j == 0) & is_fwd)
    def _():
        st_scr[...] = sf0_ref[0]

    @pl.when((j == 0) & jnp.logical_not(is_fwd))
    def _():
        st_scr[...] = sb0_ref[0]

    dsel = lambda v: jnp.where(is_fwd, v, pltpu.roll(v, LANES - SSD_HEADS, 1))
    dt_raw = dsel(dt_ref[0]) + dsel(par_ref[8:16, :])[0:1, :]
    dt = jnp.maximum(dt_raw, 0.0) + jnp.log(1.0 + jnp.exp(-jnp.abs(dt_raw)))
    a = -jnp.exp(dsel(par_ref[0:8, :])[0:1, :]) * dt
    row = lax.broadcasted_iota(jnp.int32, (R, R), 0)
    col = lax.broadcasted_iota(jnp.int32, (R, R), 1)
    mask = jnp.where(is_fwd, row - col, col - row) >= 0
    tri = jnp.where(mask, 1.0, 0.0)
    cs = jnp.dot(tri, a, precision=lax.Precision.HIGHEST, preferred_element_type=F32)
    cs_t = cs.T
    total = jnp.sum(a, axis=0, keepdims=True)

    xbc = xbc_ref[0]
    xs = xbc[:, :SSD_INNER]
    bm = xbc[:, SSD_INNER:SSD_INNER + LANES]
    cm = xbc[:, SSD_INNER + LANES:SSD_INNER + 2 * LANES]
    lane = lax.broadcasted_iota(jnp.int32, (R, LANES), 1)
    lo_half = lane < SSD_HEADDIM
    lo_half_st = lax.broadcasted_iota(jnp.int32, (LANES, LANES), 1) < SSD_HEADDIM
    cm_b = cm.astype(BF16)
    y_groups = []
    for g in range(SSD_GROUPS):
        h0, h1 = 2 * g, 2 * g + 1
        xs_g = xs[:, g * LANES:(g + 1) * LANES]
        xdt = (xs_g * jnp.where(lo_half, dt[:, h0:h0 + 1], dt[:, h1:h1 + 1])).astype(BF16)
        bm_g = jnp.where((lane >= g * SSD_STATE) & (lane < (g + 1) * SSD_STATE), bm, 0.0)
        G = lax.dot_general(cm_b, bm_g.astype(BF16), (((1,), (1,)), ((), ())), preferred_element_type=F32)
        st_old = st_scr[g]
        st_old_b = st_old.astype(BF16)
        y_h, st_h = [], []
        for h in (h0, h1):
            cs_col = cs[:, h:h + 1]
            decay = jnp.exp(jnp.where(mask, cs_col - cs_t[h:h + 1, :], NEG_INF))
            yd = jnp.dot((G * decay).astype(BF16), xdt, preferred_element_type=F32)
            yo = jnp.dot((cm * jnp.exp(cs_col)).astype(BF16), st_old_b, preferred_element_type=F32)
            y_h.append(yd + yo)
            tot = total[:, h:h + 1]
            upd = lax.dot_general((bm_g * jnp.exp(tot - cs_col)).astype(BF16), xdt,
                                  (((0,), (0,)), ((), ())), preferred_element_type=F32)
            st_h.append(jnp.exp(tot) * st_old + upd)
        y_groups.append(jnp.where(lo_half, y_h[0], y_h[1]))
        st_scr[g] = jnp.where(lo_half_st, st_h[0], st_h[1])
    y = jnp.concatenate(y_groups, axis=1)

    @pl.when(jnp.logical_not(is_fwd))
    def _():
        yb_scr[c] = y

    @pl.when(is_fwd)
    def _():
        z = z_ref[0]
        yt = (y + yb_scr[c] + xs * dskip_ref[...]) * (z * jax.nn.sigmoid(z))
        ms = jnp.mean(yt * yt, axis=-1, keepdims=True)
        y_ref[0] = yt * lax.rsqrt(ms + RMS_EPS) * ng_ref[...]

    @pl.when((j == n_chunks - 1) & is_fwd)
    def _():
        sf_ref[0] = st_scr[...]

    @pl.when((j == n_chunks - 1) & jnp.logical_not(is_fwd))
    def _():
        sb_ref[0] = st_scr[...]


def ssd_scan(ssd, xbc, par, d_skip, norm_g, s_f0, s_b0, *, z_blk, dt_blk, chunk=256):
    B, L, _ = ssd.shape
    R = min(chunk, L)
    assert L % R == 0
    n = L // R
    cidx = lambda ph, j: _scan_chunk_index(ph, j, n)
    st_spec = pl.BlockSpec((1, SSD_GROUPS, LANES, LANES), lambda b, ph, j: (b, 0, 0, 0))
    st_shape = jax.ShapeDtypeStruct((B, SSD_GROUPS, LANES, LANES), F32)
    return pl.pallas_call(
        functools.partial(_ssd_kernel, n_chunks=n),
        grid=(B, 2, n),
        in_specs=[
            pl.BlockSpec((1, R, SSD_INNER), lambda b, ph, j: (b, cidx(ph, j), z_blk)),
            pl.BlockSpec((1, R, xbc.shape[2]), lambda b, ph, j: (b, cidx(ph, j), 0)),
            pl.BlockSpec((1, R, LANES), lambda b, ph, j: (b, cidx(ph, j), dt_blk)),
            pl.BlockSpec((16, LANES), lambda b, ph, j: (0, 0)),
            pl.BlockSpec((1, SSD_INNER), lambda b, ph, j: (0, 0)),
            pl.BlockSpec((1, SSD_INNER), lambda b, ph, j: (0, 0)),
            st_spec, st_spec,
        ],
        out_specs=[pl.BlockSpec((1, R, SSD_INNER), lambda b, ph, j: (b, ph * j, 0)), st_spec, st_spec],
        out_shape=[jax.ShapeDtypeStruct((B, L, SSD_INNER), F32), st_shape, st_shape],
        scratch_shapes=[pltpu.VMEM((n, R, SSD_INNER), F32), pltpu.VMEM((SSD_GROUPS, LANES, LANES), F32)],
        compiler_params=pltpu.CompilerParams(
            dimension_semantics=("parallel", "arbitrary", "arbitrary"), vmem_limit_bytes=VMEM_LIMIT_BYTES),
        name="ssd_scan",
    )(ssd, xbc, ssd, par, d_skip.reshape(1, SSD_INNER), norm_g.reshape(1, SSD_INNER), s_f0, s_b0)


HG_HEADS = 4
HG_DK = 64
HG_DV = 64
HG_KEY = HG_HEADS * HG_DK
HG_SUB = 16
HG_CHUNK = 64
HG_SUBS_PER_CHUNK = HG_CHUNK // HG_SUB


def _split_dot(mask_b, x):
    hi = x.astype(BF16)
    lo = (x - hi.astype(F32)).astype(BF16)
    return (jnp.dot(mask_b, hi, preferred_element_type=F32) + jnp.dot(mask_b, lo, preferred_element_type=F32))


def _hgrn_kernel(q_ref, f_ref, v_ref, g_ref, lbl_ref, ng_ref, sf0_ref, sb0_ref,
                 y_ref, sf_ref, sb_ref, ob_scr, st_scr, qd_scr, ke_scr, vv_scr, tot_scr, oi_scr,
                 *, n_blocks, layer, depth):
    ph = pl.program_id(1)
    j = pl.program_id(2)
    is_fwd = ph == 1
    c = _scan_chunk_index(ph, j, n_blocks)
    R = q_ref.shape[1]
    W = HG_KEY

    @pl.when((j == 0) & is_fwd)
    def _():
        st_scr[...] = sf0_ref[0]

    @pl.when((j == 0) & jnp.logical_not(is_fwd))
    def _():
        st_scr[...] = sb0_ref[0]

    def lower_bound(d):
        x = lbl_ref[d * depth:(d + 1) * depth, :]
        e = jnp.exp(x - jnp.max(x, axis=0, keepdims=True))
        p = e / jnp.sum(e, axis=0, keepdims=True)
        return jnp.sum(p[1:layer + 1, :], axis=0, keepdims=True) if layer > 0 else jnp.zeros((1, W), F32)

    lb = jnp.where(is_fwd, lower_bound(0), lower_bound(1))
    f = f_ref[0]
    logf = jnp.log(lb + (1.0 - lb) * jax.nn.sigmoid(f))
    k = (1.0 - lb) * jax.nn.sigmoid(-f)
    q = q_ref[0] * (HG_DK ** -0.5)
    v_b = v_ref[0].astype(BF16)

    row = lax.broadcasted_iota(jnp.int32, (R, R), 0)
    col = lax.broadcasted_iota(jnp.int32, (R, R), 1)
    dirge = jnp.where(is_fwd, row - col, col - row) >= 0
    same_sub = (row // HG_SUB) == (col // HG_SUB)
    same_chunk = (row // HG_CHUNK) == (col // HG_CHUNK)
    one = lambda m: jnp.where(m, 1.0, 0.0).astype(BF16)
    b16 = _split_dot(one(same_sub & dirge), logf)
    tot16 = _split_dot(one(same_sub), logf)
    b64 = _split_dot(one(same_chunk & dirge), logf)
    tot64 = _split_dot(one(same_chunk), logf)

    shifted = lambda s: jnp.where(is_fwd, pltpu.roll(tot16, R - s, 0), pltpu.roll(tot16, s, 0))
    nx1 = shifted(HG_SUB)
    nx2 = nx1 + shifted(2 * HG_SUB)
    k_end16 = k * jnp.exp(tot16 - b16)
    keys = [(k * jnp.exp(-b16)).astype(BF16), k_end16.astype(BF16),
            (k_end16 * jnp.exp(nx1)).astype(BF16), (k_end16 * jnp.exp(nx2)).astype(BF16)]
    q16 = q * jnp.exp(b16)
    lane = lax.broadcasted_iota(jnp.int32, (R, W), 1)
    head_mask = [(lane // HG_DK) == h for h in range(HG_HEADS)]
    qm = jnp.concatenate([jnp.where(m, q16, 0.0) for m in head_mask], axis=0).astype(BF16)
    dist = jnp.where(is_fwd, row // HG_SUB - col // HG_SUB, col // HG_SUB - row // HG_SUB)
    dmask = [same_sub & dirge] + [same_chunk & (dist == d) for d in range(1, HG_SUBS_PER_CHUNK)]
    att = None
    for d in range(HG_SUBS_PER_CHUNK):
        sc = lax.dot_general(qm, keys[d], (((1,), (1,)), ((), ())), preferred_element_type=F32)
        sc = jnp.concatenate([jnp.where(dmask[d], sc[h * R:(h + 1) * R], 0.0) for h in range(HG_HEADS)], axis=0)
        att = sc if att is None else att + sc
    o_all = jnp.dot(att.astype(BF16), v_b, preferred_element_type=F32)
    o = None
    for h in range(HG_HEADS):
        t = jnp.where(head_mask[h], o_all[h * R:(h + 1) * R], 0.0)
        o = t if o is None else o + t

    qd_scr[...] = (q * jnp.exp(b64)).astype(BF16)
    ke_scr[...] = (k * jnp.exp(tot64 - b64)).astype(BF16)
    vv_scr[...] = v_b
    tot_scr[...] = tot64
    srow = lax.broadcasted_iota(jnp.int32, (W, W), 0)
    scol = lax.broadcasted_iota(jnp.int32, (W, W), 1)
    bd_mask = (srow // HG_DV) == (scol // HG_DK)
    st = st_scr[...]
    for i in range(R // HG_CHUNK):
        cc = jnp.where(is_fwd, i, R // HG_CHUNK - 1 - i)
        rows = pl.ds(pl.multiple_of(cc * HG_CHUNK, HG_CHUNK), HG_CHUNK)
        oi_scr[rows, :] = lax.dot_general(qd_scr[rows, :], st.astype(BF16), (((1,), (1,)), ((), ())),
                                          preferred_element_type=F32)
        upd = lax.dot_general(vv_scr[rows, :], ke_scr[rows, :], (((0,), (0,)), ((), ())),
                              preferred_element_type=F32)
        decay = jnp.exp(tot_scr[pl.ds(pl.multiple_of(cc * HG_CHUNK, HG_CHUNK), 1), :])
        st = decay * st + jnp.where(bd_mask, upd, 0.0)
    st_scr[...] = st
    o = o + oi_scr[...]

    @pl.when(jnp.logical_not(is_fwd))
    def _():
        ob_scr[c] = o

    @pl.when(is_fwd)
    def _():
        ot = o + ob_scr[c]
        sq = ot * ot
        ms = None
        for h in range(HG_HEADS):
            s = jnp.sum(jnp.where(head_mask[h], sq, 0.0), axis=-1, keepdims=True) * (1.0 / HG_DV)
            t = jnp.where(head_mask[h], s, 0.0)
            ms = t if ms is None else ms + t
        g = g_ref[0]
        y_ref[0] = ot * lax.rsqrt(ms + RMS_EPS) * ng_ref[...] * (g * jax.nn.sigmoid(g))

    @pl.when((j == n_blocks - 1) & is_fwd)
    def _():
        sf_ref[0] = st_scr[...]

    @pl.when((j == n_blocks - 1) & jnp.logical_not(is_fwd))
    def _():
        sb_ref[0] = st_scr[...]


def hgrn_scan(hg, lb_logits, norm_g, s_f0, s_b0, *, layer, block=256):
    B, L, _ = hg.shape
    W = HG_KEY
    R = min(block, L)
    assert L % R == 0 and R % HG_CHUNK == 0
    n = L // R
    depth = lb_logits.shape[0] // 2
    cidx = lambda ph, j: _scan_chunk_index(ph, j, n)
    col = lambda blk: pl.BlockSpec((1, R, W), lambda b, ph, j: (b, cidx(ph, j), blk))
    st_spec = pl.BlockSpec((1, W, W), lambda b, ph, j: (b, 0, 0))
    st_shape = jax.ShapeDtypeStruct((B, W, W), F32)
    return pl.pallas_call(
        functools.partial(_hgrn_kernel, n_blocks=n, layer=layer, depth=depth),
        grid=(B, 2, n),
        in_specs=[
            col(0),
            pl.BlockSpec((1, R, W), lambda b, ph, j: (b, cidx(ph, j), 2 - ph)),
            col(3), col(4),
            pl.BlockSpec(lb_logits.shape, lambda b, ph, j: (0, 0)),
            pl.BlockSpec((1, W), lambda b, ph, j: (0, 0)),
            st_spec, st_spec,
        ],
        out_specs=[pl.BlockSpec((1, R, W), lambda b, ph, j: (b, ph * j, 0)), st_spec, st_spec],
        out_shape=[jax.ShapeDtypeStruct((B, L, W), F32), st_shape, st_shape],
        scratch_shapes=[
            pltpu.VMEM((n, R, W), F32), pltpu.VMEM((W, W), F32),
            pltpu.VMEM((R, W), BF16), pltpu.VMEM((R, W), BF16), pltpu.VMEM((R, W), BF16),
            pltpu.VMEM((R, W), F32), pltpu.VMEM((R, W), F32),
        ],
        compiler_params=pltpu.CompilerParams(
            dimension_semantics=("parallel", "arbitrary", "arbitrary"), vmem_limit_bytes=VMEM_LIMIT_BYTES),
        name="hgrn_scan",
    )(hg, hg, hg, hg, lb_logits, jnp.tile(norm_g, HG_HEADS).reshape(1, W), s_f0, s_b0)


HY_WIDTH = 256
HY_BANDS = 16
HY_HIDDEN = 64
SUBLANES = 8


def _dot_hi(a, b):
    return jnp.dot(a, b, precision=lax.Precision.HIGHEST, preferred_element_type=F32)


def _hy_filter_kernel(bands_ref, w1t_ref, w1c_ref, w1s_ref, b1_ref, fr1_ref, w2_ref, b2_ref, fr2_ref, w3_ref,
                      decay_ref, f_ref, den_ref, *, L):
    i = pl.program_id(0)
    TR = f_ref.shape[1]
    m = i * TR + lax.broadcasted_iota(jnp.int32, (TR, LANES), 0)
    total = None
    for half in range(2):
        idx = m if half == 0 else jnp.where(m == 0, 0, L - m)
        idx_f = idx.astype(F32)
        t = idx_f * (1.0 / (L - 1))
        ang = ((2.0 * math.pi / L) * idx_f) * bands_ref[...]
        pre = (t[:, :HY_HIDDEN] * w1t_ref[...] + _dot_hi(jnp.cos(ang), w1c_ref[...])
               - _dot_hi(jnp.sin(ang), w1s_ref[...]) + b1_ref[...])
        h = jnp.sin(fr1_ref[...] * pre)
        h = jnp.sin(fr2_ref[...] * (_dot_hi(h, w2_ref[...]) + b2_ref[...]))
        h = _dot_hi(h, w3_ref[:, half * HY_WIDTH:(half + 1) * HY_WIDTH])
        tw = jnp.concatenate([t, t], axis=1)
        filt = h * jnp.exp(-tw * jnp.abs(decay_ref[...]))
        if half == 1:
            filt = jnp.where(jnp.concatenate([m, m], axis=1) == 0, 0.0, filt)
        f_ref[half] = filt
        s = jnp.sum(jnp.abs(filt), axis=0, keepdims=True)
        total = s if total is None else total + s

    @pl.when(i == 0)
    def _():
        den_ref[...] = jnp.zeros_like(den_ref)

    den_ref[...] += total


def hy_filter(L, w1, b1, fr1, w2, b2, fr2, w3, decay, *, tr=512):
    tr = min(tr, L)
    bands = np.zeros((1, LANES), np.float32)
    bands[0, :HY_BANDS] = np.linspace(1e-4, HY_BANDS - 1, HY_BANDS, dtype=np.float32)
    pad_rows = lambda w: jnp.pad(w, ((0, LANES - HY_BANDS), (0, 0)))
    row = lambda v: v.reshape(1, -1)
    args = (jnp.asarray(bands), w1[0:1], pad_rows(w1[1:1 + HY_BANDS]), pad_rows(w1[1 + HY_BANDS:]), row(b1), row(fr1),
            w2, row(b2), row(fr2), w3, row(decay))
    return pl.pallas_call(
        functools.partial(_hy_filter_kernel, L=L),
        grid=(L // tr,),
        in_specs=[pl.BlockSpec(a.shape, lambda i: (0, 0)) for a in args],
        out_specs=[pl.BlockSpec((2, tr, HY_WIDTH), lambda i: (0, i, 0)), pl.BlockSpec((1, HY_WIDTH), lambda i: (0, 0))],
        out_shape=[jax.ShapeDtypeStruct((2, L, HY_WIDTH), F32), jax.ShapeDtypeStruct((1, HY_WIDTH), F32)],
        compiler_params=pltpu.CompilerParams(
            dimension_semantics=("arbitrary",), vmem_limit_bytes=VMEM_LIMIT_BYTES),
        name="hy_filter",
    )(*args)


FFT_N1 = 64
FFT_N2 = 128
FFT_N = FFT_N1 * FFT_N2
FFT_L = FFT_N // 2
FFT_N1_NZ = FFT_N1 // 2


@functools.lru_cache(maxsize=None)
def _fft_constants():
    eye = np.eye(SUBLANES)
    k1 = np.arange(FFT_N1)[:, None]
    n1 = np.arange(FFT_N1_NZ)[None, :]
    ang1 = 2.0 * np.pi * ((k1 * n1) % FFT_N1) / FFT_N1
    kron_fwd = np.concatenate([np.kron(np.cos(ang1), eye), np.kron(np.sin(ang1), eye)], axis=0)
    kron_inv = np.concatenate([np.kron(np.cos(ang1).T, eye), np.kron(np.sin(ang1).T, eye)], axis=0) / FFT_N
    k = np.arange(FFT_N1)[:, None, None] + FFT_N1 * np.arange(FFT_N2)[None, :, None]
    n2 = np.arange(FFT_N2)[None, None, :]
    ang2 = 2.0 * np.pi * ((k * n2) % FFT_N) / FFT_N
    g_fwd = np.concatenate([np.cos(ang2), np.sin(ang2)], axis=1)
    g_inv = np.concatenate([np.cos(ang2).transpose(0, 2, 1), np.sin(ang2).transpose(0, 2, 1)], axis=1)
    to_b = lambda a: jnp.asarray(a, dtype=F32).astype(BF16)
    return to_b(kron_fwd), to_b(g_fwd), to_b(g_inv), to_b(kron_inv)


def _cplx_fwd(r, half_rows, cb):
    cr, sr = r[:half_rows], r[half_rows:]
    return cr[:, :cb] + sr[:, cb:], cr[:, cb:] - sr[:, :cb]


def _cplx_inv(r, half_rows, cb):
    cr, sr = r[:half_rows], r[half_rows:]
    return cr[:, :cb] - sr[:, cb:], cr[:, cb:] + sr[:, :cb]


def _fft_stage1(load_group, kron_ref, a_scr, cb):
    def body(jt, carry):
        rows = pl.ds(pl.multiple_of(jt * SUBLANES, SUBLANES), SUBLANES)
        d = load_group(rows).reshape(FFT_N1_NZ * SUBLANES, 2 * cb).astype(BF16)
        r = jnp.dot(kron_ref[...], d, preferred_element_type=F32)
        a_re, a_im = _cplx_fwd(r, FFT_N1 * SUBLANES, cb)
        a_scr[:, rows, :] = jnp.concatenate([a_re, a_im], axis=1).reshape(FFT_N1, SUBLANES, 2 * cb)
        return carry
    lax.fori_loop(0, FFT_N2 // SUBLANES, body, 0)


def _fft_fwd_data_kernel(z_ref, h_ref, kron_ref, g_ref, p_ref, a_scr):
    cb = z_ref.shape[4]
    load = lambda rows: jnp.concatenate([z_ref[0, 0, :, rows, :], z_ref[0, 1, :, rows, :]], axis=-1)
    _fft_stage1(load, kron_ref, a_scr, cb)

    def body(k1, carry):
        r = jnp.dot(g_ref[k1], a_scr[k1].astype(BF16), preferred_element_type=F32)
        x_re, x_im = _cplx_fwd(r, FFT_N2, cb)
        h = h_ref[0, k1]
        h_re, h_im = h[:, :cb], h[:, cb:]
        p_ref[0, 0, k1] = jnp.concatenate([x_re * h_re - x_im * h_im, x_re * h_im + x_im * h_re], axis=1).astype(BF16)
        return carry
    lax.fori_loop(0, FFT_N1, body, 0)


def _fft_fwd_filter_kernel(f_ref, den_ref, kron_ref, g_ref, h_ref, a_scr):
    half = pl.program_id(1)
    cb = f_ref.shape[3]
    inv_den = 1.0 / den_ref[...]

    def load(rows):
        re = f_ref[0, :, rows, :] * inv_den
        return jnp.concatenate([re, jnp.zeros_like(re)], axis=-1)
    _fft_stage1(load, kron_ref, a_scr, cb)

    def body(k1, carry):
        r = jnp.dot(g_ref[k1], a_scr[k1].astype(BF16), preferred_element_type=F32)
        x_re, x_im = _cplx_fwd(r, FFT_N2, cb)
        x = jnp.concatenate([x_re, x_im], axis=1)

        @pl.when(half == 0)
        def _():
            h_ref[0, k1] = x

        @pl.when(half == 1)
        def _():
            h_ref[0, k1] += jnp.where(k1 % 2 == 0, 1.0, -1.0) * x
        return carry
    lax.fori_loop(0, FFT_N1, body, 0)


def _fft_inv_kernel(p_ref, x0_ref, w_ref, bias_ref, gi_ref, kron_ref, o_ref, b_scr):
    cb = o_ref.shape[4]

    def body3(k1, carry):
        r = jnp.dot(gi_ref[k1], p_ref[0, 0, k1], preferred_element_type=F32)
        b_re, b_im = _cplx_inv(r, FFT_N2, cb)
        b_scr[k1] = jnp.concatenate([b_re, b_im], axis=1)
        return carry
    lax.fori_loop(0, FFT_N1, body3, 0)

    bias = bias_ref[...].reshape(1, 1, cb)

    def body4(jt, carry):
        rows = pl.ds(pl.multiple_of(jt * SUBLANES, SUBLANES), SUBLANES)
        d = b_scr[:, rows, :].reshape(FFT_N1 * SUBLANES, 2 * cb).astype(BF16)
        r = jnp.dot(kron_ref[...], d, preferred_element_type=F32)
        y_re, y_im = _cplx_inv(r, FFT_N1_NZ * SUBLANES, cb)
        for which, y in enumerate((y_re, y_im)):
            y3 = y.reshape(FFT_N1_NZ, SUBLANES, cb)
            o_ref[0, which, :, rows, :] = x0_ref[0, which, :, rows, :] * (y3 + w_ref[0, which, :, rows, :] * bias)
        return carry
    lax.fori_loop(0, FFT_N2 // SUBLANES, body4, 0)


def hyena_spectrum(fu, den, *, cb=128):
    kron_fwd, g_fwd, _, _ = _fft_constants()
    W = fu.shape[2]
    ncb = W // cb
    f4 = fu.reshape(2, FFT_N1_NZ, FFT_N2, W)
    const = lambda a: pl.BlockSpec(a.shape, lambda c, h: (0,) * a.ndim, pipeline_mode=pl.Buffered(1))
    return pl.pallas_call(
        _fft_fwd_filter_kernel,
        grid=(ncb, 2),
        in_specs=[
            pl.BlockSpec((1, FFT_N1_NZ, FFT_N2, cb), lambda c, h: (h, 0, 0, c)),
            pl.BlockSpec((1, cb), lambda c, h: (0, c)),
            const(kron_fwd), const(g_fwd),
        ],
        out_specs=pl.BlockSpec((1, FFT_N1, FFT_N2, 2 * cb), lambda c, h: (c, 0, 0, 0)),
        out_shape=jax.ShapeDtypeStruct((ncb, FFT_N1, FFT_N2, 2 * cb), F32),
        scratch_shapes=[pltpu.VMEM((FFT_N1, FFT_N2, 2 * cb), F32)],
        compiler_params=pltpu.CompilerParams(
            dimension_semantics=("parallel", "arbitrary"), vmem_limit_bytes=VMEM_LIMIT_BYTES),
        name="hyena_spectrum",
    )(f4, den, kron_fwd, g_fwd)


def hyena_conv(x0c, hw, spec, bias, *, cb=128):
    kron_fwd, g_fwd, g_inv, kron_inv = _fft_constants()
    B, L, W = hw.shape
    assert L == FFT_L and B % 2 == 0
    ncb = W // cb
    pair_shape = (B // 2, 2, FFT_N1_NZ, FFT_N2, W)
    x5 = x0c.reshape(pair_shape)
    w5 = hw.reshape(pair_shape)
    const = lambda a: pl.BlockSpec(a.shape, lambda c, p: (0,) * a.ndim, pipeline_mode=pl.Buffered(1))
    pair = pl.BlockSpec((1, 2, FFT_N1_NZ, FFT_N2, cb), lambda c, p: (p, 0, 0, 0, c))
    pspec = pl.BlockSpec((1, 1, FFT_N1, FFT_N2, 2 * cb), lambda c, p: (c, p, 0, 0, 0))
    prod = pl.pallas_call(
        _fft_fwd_data_kernel,
        grid=(ncb, B // 2),
        in_specs=[pair, pl.BlockSpec((1, FFT_N1, FFT_N2, 2 * cb), lambda c, p: (c, 0, 0, 0)),
                  const(kron_fwd), const(g_fwd)],
        out_specs=pspec,
        out_shape=jax.ShapeDtypeStruct((ncb, B // 2, FFT_N1, FFT_N2, 2 * cb), BF16),
        scratch_shapes=[pltpu.VMEM((FFT_N1, FFT_N2, 2 * cb), F32)],
        compiler_params=pltpu.CompilerParams(
            dimension_semantics=("parallel", "parallel"), vmem_limit_bytes=VMEM_LIMIT_BYTES),
        name="hyena_fft_fwd",
    )(w5, spec, kron_fwd, g_fwd)
    out = pl.pallas_call(
        _fft_inv_kernel,
        grid=(ncb, B // 2),
        in_specs=[pspec, pair, pair, pl.BlockSpec((1, cb), lambda c, p: (0, c)), const(g_inv), const(kron_inv)],
        out_specs=pair,
        out_shape=jax.ShapeDtypeStruct(pair_shape, F32),
        scratch_shapes=[pltpu.VMEM((FFT_N1, FFT_N2, 2 * cb), F32)],
        compiler_params=pltpu.CompilerParams(
            dimension_semantics=("parallel", "parallel"), vmem_limit_bytes=VMEM_LIMIT_BYTES),
        name="hyena_fft_inv",
    )(prod, x5, w5, bias.reshape(1, W), g_inv, kron_inv)
    return out.reshape(B, L, W)


@functools.lru_cache(maxsize=None)
def _dense_dft_constants(L):
    n_full = 2 * L
    k = np.arange(n_full)[:, None]
    n = np.arange(n_full)[None, :]
    ang = 2.0 * np.pi * ((k * n) % n_full) / n_full
    fwd = np.concatenate([np.cos(ang), np.sin(ang)], axis=0)
    inv = np.concatenate([np.cos(ang[:L]), np.sin(ang[:L])], axis=0) / n_full
    return jnp.asarray(fwd, dtype=F32), jnp.asarray(inv, dtype=F32)


def _hyena_short_kernel(x0_ref, w_ref, f_ref, den_ref, bias_ref, ff_ref, fi_ref, o_ref):
    L, cb = o_ref.shape[2], o_ref.shape[3]
    filt = jnp.concatenate([f_ref[0], f_ref[1]], axis=0) / den_ref[...]
    r = _dot_hi(ff_ref[...], filt)
    h_re, h_im = r[:2 * L], -r[2 * L:]
    z = jnp.concatenate([w_ref[0, 0], w_ref[0, 1]], axis=1)
    x_re, x_im = _cplx_fwd(_dot_hi(ff_ref[:, :L], z), 2 * L, cb)
    p = jnp.concatenate([x_re * h_re - x_im * h_im, x_re * h_im + x_im * h_re], axis=1)
    y_re, y_im = _cplx_inv(_dot_hi(fi_ref[...], p), L, cb)
    for which, y in enumerate((y_re, y_im)):
        o_ref[0, which] = x0_ref[0, which] * (y + w_ref[0, which] * bias_ref[...])


def hyena_conv_short(x0c, hw, fu, den, bias, *, cb=128):
    B, L, W = hw.shape
    fwd, inv = _dense_dft_constants(L)
    pair_shape = (B // 2, 2, L, W)
    pair = pl.BlockSpec((1, 2, L, cb), lambda p, c: (p, 0, 0, c))
    vec = pl.BlockSpec((1, cb), lambda p, c: (0, c))
    const = lambda a: pl.BlockSpec(a.shape, lambda p, c: (0, 0))
    out = pl.pallas_call(
        _hyena_short_kernel,
        grid=(B // 2, W // cb),
        in_specs=[pair, pair, pl.BlockSpec((2, L, cb), lambda p, c: (0, 0, c)), vec, vec, const(fwd), const(inv)],
        out_specs=pair,
        out_shape=jax.ShapeDtypeStruct(pair_shape, F32),
        compiler_params=pltpu.CompilerParams(
            dimension_semantics=("parallel", "parallel"), vmem_limit_bytes=VMEM_LIMIT_BYTES),
        name="hyena_conv_short",
    )(x0c.reshape(pair_shape), hw.reshape(pair_shape), fu, den, bias.reshape(1, W), fwd, inv)
    return out.reshape(B, L, W)


GRID_W = 64
HY_COLS = 3 * HY_WIDTH
SC_COLS = 3 * 256
HG_COLS = 3 * HG_KEY + 2 * HG_HEADS * HG_DV
SSD_XBC = SSD_INNER + 2 * SSD_GROUPS * SSD_STATE
SSD_COLS = SSD_INNER + SSD_XBC + 2 * SSD_HEADS


def _sincos_1d(pos, dim):
    omega = 1.0 / (10000.0 ** (jnp.arange(dim // 2, dtype=F32) / (dim // 2)))
    ang = pos.astype(F32)[:, None] * omega[None]
    return jnp.concatenate([jnp.sin(ang), jnp.cos(ang)], -1)


def _grid_pos_embed(rows, dim):
    row = jnp.repeat(jnp.arange(rows), GRID_W)
    col = jnp.tile(jnp.arange(GRID_W), rows)
    return jnp.concatenate([_sincos_1d(row, dim // 2), _sincos_1d(col, dim // 2)], -1)


def kernel(x, c, ctx, c_ctx, w_ada, b_ada, w_in, hy_conv_w, hy_conv_b, hy_w1, hy_b1, hy_freq1,
           hy_w2, hy_b2, hy_freq2, hy_w3, hy_decay, hy_bias, sc_conv_w, hg_lb_logits, hg_norm_g,
           ssd_conv_w, ssd_conv_b, ssd_a_log, ssd_dt_bias, ssd_d, ssd_norm_g, w_gate, b_gate, w_br,
           w_o, ln1_g, ln1_b, ln2_g, ln2_b, w_router, b_router, w_e1, w_e3, w_e2):
    B, L, D = x.shape
    LC = ctx.shape[1]
    depth = w_in.shape[0]
    lat = (x + _grid_pos_embed(L // GRID_W, D).astype(x.dtype)[None]).reshape(B * L, D)
    cx = ctx.reshape(B * LC, D)

    n_vec = -(-(B + 1) // SUBLANES) * SUBLANES
    cvecs = jnp.concatenate([c, c_ctx[None], jnp.zeros((n_vec - B - 1, D), c.dtype)], axis=0)
    mods = ada_mod(cvecs, w_ada, b_ada).reshape(depth, n_vec, 6, D)
    lb_logits = hg_lb_logits.reshape(2 * depth, HG_KEY)

    in_cols = w_in.shape[2]
    widths = (HY_COLS, SC_COLS, HG_COLS, -(-SSD_COLS // LANES) * LANES)
    z_state = lambda *shape: jnp.zeros(shape, F32)

    for l in range(depth):
        ctx_out = l < depth - 1
        mod_lat, mod_ctx = mods[l, :B], mods[l, B:B + 1]
        w_in_b = jnp.pad(w_in[l], ((0, 0), (0, sum(widths) - in_cols))).astype(BF16)
        ssd_par = jnp.zeros((2 * SUBLANES, LANES), F32)
        ssd_par = ssd_par.at[:SUBLANES, :2 * SSD_HEADS].set(jnp.broadcast_to(ssd_a_log[l].reshape(1, -1), (SUBLANES, 2 * SSD_HEADS)))
        ssd_par = ssd_par.at[SUBLANES:, :2 * SSD_HEADS].set(jnp.broadcast_to(ssd_dt_bias[l].reshape(1, -1), (SUBLANES, 2 * SSD_HEADS)))
        d_skip = jnp.repeat(ssd_d[l], SSD_HEADDIM)
        filt_args = (hy_w1[l], hy_b1[l], hy_freq1[l], hy_w2[l], hy_b2[l], hy_freq2[l], hy_w3[l], hy_decay[l])

        def mixers(tokens, mod, seg_len, hg_state, ssd_state, want_out):
            n_seq = tokens.shape[0] // seg_len
            hy, sc, hg, ssd = in_proj(tokens, mod, w_in_b, widths, rows_per_mod=mod_rows(mod, tokens), tm=512)
            seq = lambda a: a.reshape(n_seq, seg_len, a.shape[1])
            hg_y, hg_f, hg_b = hgrn_scan(seq(hg), lb_logits, hg_norm_g[l], *hg_state, layer=l)
            ssd3 = seq(ssd)
            xbc = ssd_prep(ssd3, ssd_conv_w[l], ssd_conv_b[l], col0=SSD_INNER)
            ssd_y, ssd_f, ssd_b = ssd_scan(ssd3, xbc, ssd_par, d_skip, ssd_norm_g[l], *ssd_state,
                                           z_blk=0, dt_blk=(SSD_INNER + SSD_XBC) // LANES)
            branches = None
            if want_out:
                x0c, hw = hy_prep(seq(hy), hy_conv_w[l], hy_conv_b[l])
                fu, den = hy_filter(seg_len, *filt_args)
                if seg_len == FFT_L:
                    hy_y = hyena_conv(x0c, hw, hyena_spectrum(fu, den), hy_bias[l])
                else:
                    hy_y = hyena_conv_short(x0c, hw, fu, den, hy_bias[l])
                sc_y = shortconv(seq(sc), sc_conv_w[l])
                flat = lambda a: a.reshape(tokens.shape[0], a.shape[2])
                branches = [flat(hy_y), flat(sc_y), flat(hg_y), flat(ssd_y)]
            return branches, (hg_f, hg_b), (ssd_f, ssd_b)

        def mod_rows(mod, tokens):
            return tokens.shape[0] // mod.shape[0]

        def finish(tokens, mod, branches):
            rows = mod_rows(mod, tokens)
            t1 = merge_ln(tokens, mod, branches, w_gate[l].astype(BF16), b_gate[l], w_br[l].astype(BF16),
                          w_o[l].astype(BF16), ln1_g[l], ln1_b[l], rows_per_mod=rows, tm=512)
            return moe_ln(t1, mod, w_router, b_router, w_e1[l].astype(BF16), w_e3[l].astype(BF16),
                          w_e2[l].astype(BF16), ln2_g[l], ln2_b[l], rows_per_mod=rows, tm=1024)

        zero_hg = (z_state(B, HG_KEY, HG_KEY),) * 2
        zero_ssd = (z_state(B, SSD_GROUPS, LANES, LANES),) * 2
        br_ctx, hg_state, ssd_state = mixers(cx, mod_ctx, LC, zero_hg, zero_ssd, ctx_out)
        br_lat, _, _ = mixers(lat, mod_lat, L, hg_state, ssd_state, True)
        lat = finish(lat, mod_lat, br_lat)
        if ctx_out:
            cx = finish(cx, mod_ctx, br_ctx)
    return lat.reshape(B, L, D)
```

```python
import functools
import math

import jax
import jax.numpy as jnp
import numpy as np
from jax import lax
from jax.experimental import pallas as pl
from jax.experimental.pallas import tpu as pltpu

F32 = jnp.float32
BF16 = jnp.bfloat16

N_EXPERTS = 16
N_EXPERT_GROUPS = 4
GROUP_SIZE = N_EXPERTS // N_EXPERT_GROUPS
DEPTH = 2
DEEPNORM_ALPHA = (2 * DEPTH) ** 0.25
LN_EPS = 1e-5
RMS_EPS = 1e-6

VMEM_LIMIT_BYTES = 52 * 1024 * 1024


def _layernorm_rows(z, g, b):
    mu = jnp.mean(z, axis=-1, keepdims=True)
    zc = z - mu
    var = jnp.mean(zc * zc, axis=-1, keepdims=True)
    return zc * lax.rsqrt(var + LN_EPS) * g + b


def _route_t(logits_t, bias_col):
    z = logits_t - jnp.max(logits_t, axis=0, keepdims=True)
    ex = jnp.exp(z)
    scores = ex / jnp.sum(ex, axis=0, keepdims=True)
    sel = scores + bias_col
    eidx = lax.broadcasted_iota(jnp.int32, sel.shape, 0)
    neg = jnp.float32(-jnp.inf)

    def first_argmax(m):
        top = jnp.max(m, axis=0, keepdims=True)
        return top, jnp.min(jnp.where(m == top, eidx, N_EXPERTS), axis=0, keepdims=True)

    best_score = None
    best_grp = None
    for g in range(N_EXPERT_GROUPS):
        m = jnp.where((eidx >= g * GROUP_SIZE) & (eidx < (g + 1) * GROUP_SIZE), sel, neg)
        t1, i1 = first_argmax(m)
        t2 = jnp.max(jnp.where(eidx == i1, neg, m), axis=0, keepdims=True)
        s = t1 + t2
        if g == 0:
            best_score, best_grp = s, jnp.zeros_like(i1)
        else:
            better = s > best_score
            best_score = jnp.where(better, s, best_score)
            best_grp = jnp.where(better, g, best_grp)
    lo = best_grp * GROUP_SIZE
    masked = jnp.where((eidx >= lo) & (eidx < lo + GROUP_SIZE), sel, neg)
    _, ia = first_argmax(masked)
    _, ib = first_argmax(jnp.where(eidx == ia, neg, masked))
    w = jnp.where((eidx == ia) | (eidx == ib), scores, 0.0)
    return w / jnp.sum(w, axis=0, keepdims=True), best_grp


MOE_CHUNK = 128
MOE_ARMS = (1024, 512, 256, 128)
GATE_TERMS = 3
POS_LANE = GATE_TERMS * N_EXPERTS


def _moe_ln_kernel(x_ref, mod_ref, wr_ref, br_ref, w1_ref, w3_ref, w2_ref, lng_ref, lnb_ref,
                   o_ref, pt_scr, hc_scr, gc_scr, yc_scr, seg_smem):
    e = pl.program_id(1)
    TM, R = pt_scr.shape
    D = x_ref.shape[1]
    G = N_EXPERT_GROUPS

    @pl.when(e == 0)
    def _():
        h = _modulate(x_ref[...], mod_ref, MOD_SHIFT2, MOD_SCALE2)
        h_hi = h.astype(BF16)
        h_lo = (h - h_hi.astype(F32)).astype(BF16)
        r_hi = jnp.dot(h_hi, wr_ref[...], preferred_element_type=F32)
        r_lo = jnp.dot(h_lo, wr_ref[:, :LANES], preferred_element_type=F32)
        logits_t = (r_hi[:, :LANES] + r_hi[:, LANES:] + r_lo).T[:N_EXPERTS]
        gate_t, grp = _route_t(logits_t, br_ref[:, 0:1])

        gidx = lax.broadcasted_iota(jnp.int32, (SUBLANES, TM), 0)
        tok = lax.broadcasted_iota(jnp.int32, (SUBLANES, TM), 1)
        member = jnp.where(gidx == grp, 1.0, 0.0)
        incl = member
        shift = 1
        while shift < TM:
            incl = incl + jnp.where(tok >= shift, pltpu.roll(incl, shift, 1), 0.0)
            shift *= 2
        rank = incl - member
        offset = jnp.int32(0)
        pos_row = jnp.zeros((1, TM), F32)
        for g in range(G):
            n_chunks = (jnp.sum(member[g:g + 1, :]).astype(jnp.int32) + (MOE_CHUNK - 1)) // MOE_CHUNK
            seg_smem[g] = n_chunks
            seg_smem[G + g] = offset
            pos_row = pos_row + member[g:g + 1, :] * (rank[g:g + 1, :] + offset.astype(F32))
            offset = offset + n_chunks * MOE_CHUNK

        terms, rest = [], gate_t
        for _ in range(GATE_TERMS):
            part = rest.astype(BF16).astype(F32)
            terms.append(part)
            rest = rest - part
        pad = jnp.zeros((LANES - POS_LANE - 1, TM), F32)
        side = jnp.concatenate(terms + [pos_row, pad], axis=0).T
        pos_col = side[:, POS_LANE:POS_LANE + 1].astype(jnp.int32)
        pt_scr[...] = jnp.where(lax.broadcasted_iota(jnp.int32, (TM, R), 1) == pos_col, 1.0, 0.0).astype(BF16)
        p = jnp.where(lax.broadcasted_iota(jnp.int32, (R, TM), 0) == pos_row.astype(jnp.int32), 1.0, 0.0).astype(BF16)
        gathered = jnp.dot(p, jnp.concatenate([h_hi, side.astype(BF16)], axis=1), preferred_element_type=F32)
        hc_scr[...] = gathered[:, :D].astype(BF16)
        gc_scr[...] = gathered[:, D:]
        yc_scr[...] = jnp.zeros_like(yc_scr)

    g = e // GROUP_SIZE
    n_chunks = seg_smem[g]
    base = seg_smem[G + g]

    def expert_rows(start, size):
        rows = pl.ds(pl.multiple_of(start, MOE_CHUNK), size)
        hc = hc_scr[rows, :]
        a = jnp.dot(hc, w1_ref[0], preferred_element_type=F32)
        b = jnp.dot(hc, w3_ref[0], preferred_element_type=F32)
        mid = (a * jax.nn.sigmoid(a) * b).astype(BF16)
        y = jnp.dot(mid, w2_ref[0], preferred_element_type=F32)
        gc = gc_scr[rows, :]
        lane = lax.broadcasted_iota(jnp.int32, gc.shape, 1)
        gcol = jnp.sum(jnp.where((lane % N_EXPERTS == e) & (lane < POS_LANE), gc, 0.0), axis=-1, keepdims=True)
        yc_scr[rows, :] += gcol * y

    for size in MOE_ARMS:
        if size > TM:
            continue
        taken = (n_chunks & (size // MOE_CHUNK)) != 0

        @pl.when(taken)
        def _(base=base, size=size):
            expert_rows(base, size)

        base = base + jnp.where(taken, size, 0)

    @pl.when(e == N_EXPERTS - 1)
    def _():
        moe = jnp.dot(pt_scr[...], yc_scr[...].astype(BF16), preferred_element_type=F32)
        z = DEEPNORM_ALPHA * x_ref[...] + mod_ref[0, MOD_GATE2:MOD_GATE2 + 1, :] * moe
        o_ref[...] = _layernorm_rows(z, lng_ref[...], lnb_ref[...])


def moe_ln(x, mod, w_router, b_router, w1, w3, w2, ln_g, ln_b, *, rows_per_mod, tm):
    T, D = x.shape
    E, _, FF = w1.shape
    assert T % tm == 0 and rows_per_mod % tm == 0 and tm <= MOE_ARMS[0] and tm % MOE_CHUNK == 0
    tiles_per_mod = rows_per_mod // tm
    sorted_rows = tm + N_EXPERT_GROUPS * MOE_CHUNK
    wr_hi = w_router.astype(BF16)
    wr_lo = (w_router - wr_hi.astype(F32)).astype(BF16)
    lane_pad = lambda w: jnp.pad(w, ((0, 0), (0, LANES - E)))
    wr_split = jnp.concatenate([lane_pad(wr_hi), lane_pad(wr_lo)], axis=1)
    br_col = jnp.broadcast_to(b_router.reshape(E, 1), (E, LANES))
    return pl.pallas_call(
        _moe_ln_kernel,
        grid=(T // tm, E),
        in_specs=[
            pl.BlockSpec((tm, D), lambda i, e: (i, 0)),
            pl.BlockSpec((1, 6, D), lambda i, e: (i // tiles_per_mod, 0, 0)),
            pl.BlockSpec((D, 2 * LANES), lambda i, e: (0, 0)),
            pl.BlockSpec((E, LANES), lambda i, e: (0, 0)),
            pl.BlockSpec((1, D, FF), lambda i, e: (e, 0, 0)),
            pl.BlockSpec((1, D, FF), lambda i, e: (e, 0, 0)),
            pl.BlockSpec((1, FF, D), lambda i, e: (e, 0, 0)),
            pl.BlockSpec((1, D), lambda i, e: (0, 0)),
            pl.BlockSpec((1, D), lambda i, e: (0, 0)),
        ],
        out_specs=pl.BlockSpec((tm, D), lambda i, e: (i, 0)),
        out_shape=jax.ShapeDtypeStruct((T, D), F32),
        scratch_shapes=[
            pltpu.VMEM((tm, sorted_rows), BF16),
            pltpu.VMEM((sorted_rows, D), BF16),
            pltpu.VMEM((sorted_rows, LANES), F32),
            pltpu.VMEM((sorted_rows, D), F32),
            pltpu.SMEM((2 * N_EXPERT_GROUPS,), jnp.int32),
        ],
        compiler_params=pltpu.CompilerParams(
            dimension_semantics=("parallel", "arbitrary"), vmem_limit_bytes=VMEM_LIMIT_BYTES),
        name="moe_ln",
    )(x, mod, wr_split, br_col, w1, w3, w2, ln_g.reshape(1, D), ln_b.reshape(1, D))


def _ada_kernel(c_ref, w_ref, b_ref, o_ref):
    c = c_ref[...]
    s = c * jax.nn.sigmoid(c)
    o_ref[0] = jnp.dot(s, w_ref[0], precision=lax.Precision.HIGHEST, preferred_element_type=F32) + b_ref[0]


def ada_mod(cvecs, w_ada, b_ada, *, tn=1536):
    R, D = cvecs.shape
    depth, _, N = w_ada.shape
    assert N % tn == 0
    return pl.pallas_call(
        _ada_kernel,
        grid=(depth, N // tn),
        in_specs=[
            pl.BlockSpec((R, D), lambda l, j: (0, 0)),
            pl.BlockSpec((1, D, tn), lambda l, j: (l, 0, j)),
            pl.BlockSpec((1, 1, tn), lambda l, j: (l, 0, j)),
        ],
        out_specs=pl.BlockSpec((1, R, tn), lambda l, j: (l, 0, j)),
        out_shape=jax.ShapeDtypeStruct((depth, R, N), F32),
        compiler_params=pltpu.CompilerParams(
            dimension_semantics=("parallel", "parallel"), vmem_limit_bytes=VMEM_LIMIT_BYTES),
        name="ada_mod",
    )(cvecs, w_ada, b_ada.reshape(depth, 1, N))


MOD_SHIFT1, MOD_SCALE1, MOD_GATE1, MOD_SHIFT2, MOD_SCALE2, MOD_GATE2 = range(6)


def _modulate(x, mod_ref, shift_row, scale_row):
    return x * (1.0 + mod_ref[0, scale_row:scale_row + 1, :]) + mod_ref[0, shift_row:shift_row + 1, :]


def _in_proj_kernel(x_ref, mod_ref, w_ref, *out_refs):
    h = _modulate(x_ref[...], mod_ref, MOD_SHIFT1, MOD_SCALE1).astype(BF16)
    off = 0
    for o_ref in out_refs:
        n = o_ref.shape[1]
        o_ref[...] = jnp.dot(h, w_ref[:, off:off + n], preferred_element_type=F32).astype(o_ref.dtype)
        off += n


def in_proj(x, mod, w, widths, *, rows_per_mod, tm):
    T, D = x.shape
    assert T % tm == 0 and rows_per_mod % tm == 0 and w.shape[1] == sum(widths)
    tiles_per_mod = rows_per_mod // tm
    return pl.pallas_call(
        _in_proj_kernel,
        grid=(T // tm,),
        in_specs=[
            pl.BlockSpec((tm, D), lambda i: (i, 0)),
            pl.BlockSpec((1, 6, D), lambda i: (i // tiles_per_mod, 0, 0)),
            pl.BlockSpec(w.shape, lambda i: (0, 0), pipeline_mode=pl.Buffered(1)),
        ],
        out_specs=[pl.BlockSpec((tm, n), lambda i: (i, 0)) for n in widths],
        out_shape=[jax.ShapeDtypeStruct((T, n), F32) for n in widths],
        compiler_params=pltpu.CompilerParams(
            dimension_semantics=("parallel",), vmem_limit_bytes=VMEM_LIMIT_BYTES),
        name="in_proj",
    )(x, mod, w)


N_BRANCH = 4


def _merge_kernel(x_ref, mod_ref, hy_ref, sc_ref, hg_ref, ssd_ref, wg_ref, bg_ref, wbr_ref, wo_ref,
                  lng_ref, lnb_ref, o_ref):
    x = x_ref[...]
    D = x.shape[1]
    h = _modulate(x, mod_ref, MOD_SHIFT1, MOD_SCALE1).astype(BF16)
    y = None
    for k, br_ref in enumerate((hy_ref, sc_ref, hg_ref, ssd_ref)):
        gate = jax.nn.sigmoid(
            jnp.dot(h, wg_ref[:, k * D:(k + 1) * D], preferred_element_type=F32) + bg_ref[:, k * D:(k + 1) * D])
        term = gate * jnp.dot(br_ref[...].astype(BF16), wbr_ref[k], preferred_element_type=F32)
        y = term if y is None else y + term
    y = jnp.dot(y.astype(BF16), wo_ref[...], preferred_element_type=F32)
    z = DEEPNORM_ALPHA * x + mod_ref[0, MOD_GATE1:MOD_GATE1 + 1, :] * y
    o_ref[...] = _layernorm_rows(z, lng_ref[...], lnb_ref[...])


def merge_ln(x, mod, branches, w_gate, b_gate, w_br, w_o, ln_g, ln_b, *, rows_per_mod, tm):
    T, D = x.shape
    BW = branches[0].shape[1]
    assert T % tm == 0 and rows_per_mod % tm == 0
    tiles_per_mod = rows_per_mod // tm
    const = lambda shape: pl.BlockSpec(shape, lambda i: (0,) * len(shape), pipeline_mode=pl.Buffered(1))
    return pl.pallas_call(
        _merge_kernel,
        grid=(T // tm,),
        in_specs=[
            pl.BlockSpec((tm, D), lambda i: (i, 0)),
            pl.BlockSpec((1, 6, D), lambda i: (i // tiles_per_mod, 0, 0)),
        ] + [pl.BlockSpec((tm, BW), lambda i: (i, 0))] * N_BRANCH + [
            const((D, N_BRANCH * D)), const((1, N_BRANCH * D)), const((N_BRANCH, BW, D)), const((D, D)),
            const((1, D)), const((1, D)),
        ],
        out_specs=pl.BlockSpec((tm, D), lambda i: (i, 0)),
        out_shape=jax.ShapeDtypeStruct((T, D), F32),
        compiler_params=pltpu.CompilerParams(
            dimension_semantics=("parallel",), vmem_limit_bytes=VMEM_LIMIT_BYTES),
        name="merge_ln",
    )(x, mod, *branches, w_gate, b_gate.reshape(1, -1), w_br, w_o, ln_g.reshape(1, D), ln_b.reshape(1, D))


def _dwconv_rows(u, w_ref, col0=0):
    L, C = u.shape
    row = lax.broadcasted_iota(jnp.int32, u.shape, 0)
    prev = jnp.where(row == 0, 0.0, pltpu.roll(u, 1, 0))
    nxt = jnp.where(row == L - 1, 0.0, pltpu.roll(u, L - 1, 0))
    w = lambda k: w_ref[k:k + 1, col0:col0 + C]
    return prev * w(0) + u * w(1) + nxt * w(2)


def _hy_prep_kernel(x0_ref, x1_ref, v_ref, w0_ref, w1_ref, w2_ref, b0_ref, b1_ref, b2_ref, x0c_ref, hw_ref):
    x0c_ref[0] = _dwconv_rows(x0_ref[0], w0_ref) + b0_ref[...]
    x1c = _dwconv_rows(x1_ref[0], w1_ref) + b1_ref[...]
    vc = _dwconv_rows(v_ref[0], w2_ref) + b2_ref[...]
    hw_ref[0] = x1c * vc


def hy_prep(hy, conv_w, conv_b, *, cb=128):
    B, L, C3 = hy.shape
    W = C3 // 3
    nb = W // cb
    conv_b = conv_b.reshape(1, C3)
    data = lambda part: pl.BlockSpec((1, L, cb), lambda b, j: (b, 0, part * nb + j))
    wspec = lambda part: pl.BlockSpec((3, cb), lambda b, j: (0, part * nb + j))
    bspec = lambda part: pl.BlockSpec((1, cb), lambda b, j: (0, part * nb + j))
    out = pl.BlockSpec((1, L, cb), lambda b, j: (b, 0, j))
    return pl.pallas_call(
        _hy_prep_kernel,
        grid=(B, nb),
        in_specs=[data(0), data(1), data(2), wspec(0), wspec(1), wspec(2), bspec(0), bspec(1), bspec(2)],
        out_specs=[out, out],
        out_shape=[jax.ShapeDtypeStruct((B, L, W), F32), jax.ShapeDtypeStruct((B, L, W), F32)],
        compiler_params=pltpu.CompilerParams(
            dimension_semantics=("parallel", "parallel"), vmem_limit_bytes=VMEM_LIMIT_BYTES),
        name="hy_prep",
    )(hy, hy, hy, conv_w, conv_w, conv_w, conv_b, conv_b, conv_b)


def _sc_kernel(bg_ref, cg_ref, xs_ref, w_ref, o_ref):
    o_ref[0] = bg_ref[0] * _dwconv_rows(cg_ref[0] * xs_ref[0], w_ref)


def shortconv(sc, conv_w, *, cb=128):
    B, L, C3 = sc.shape
    W = C3 // 3
    nb = W // cb
    data = lambda part: pl.BlockSpec((1, L, cb), lambda b, j: (b, 0, part * nb + j))
    return pl.pallas_call(
        _sc_kernel,
        grid=(B, nb),
        in_specs=[data(0), data(1), data(2), pl.BlockSpec((3, cb), lambda b, j: (0, j))],
        out_specs=pl.BlockSpec((1, L, cb), lambda b, j: (b, 0, j)),
        out_shape=jax.ShapeDtypeStruct((B, L, W), F32),
        compiler_params=pltpu.CompilerParams(
            dimension_semantics=("parallel", "parallel"), vmem_limit_bytes=VMEM_LIMIT_BYTES),
        name="shortconv",
    )(sc, sc, sc, conv_w)


def _ssd_prep_kernel(u_ref, w_ref, b_ref, o_ref):
    y = _dwconv_rows(u_ref[0], w_ref) + b_ref[...]
    o_ref[0] = y * jax.nn.sigmoid(y)


def ssd_prep(ssd, conv_w, conv_b, *, col0, cb=128):
    B, L, _ = ssd.shape
    C = conv_w.shape[1]
    assert col0 % cb == 0 and C % cb == 0
    return pl.pallas_call(
        _ssd_prep_kernel,
        grid=(B, C // cb),
        in_specs=[
            pl.BlockSpec((1, L, cb), lambda b, j: (b, 0, col0 // cb + j)),
            pl.BlockSpec((3, cb), lambda b, j: (0, j)),
            pl.BlockSpec((1, cb), lambda b, j: (0, j)),
        ],
        out_specs=pl.BlockSpec((1, L, cb), lambda b, j: (b, 0, j)),
        out_shape=jax.ShapeDtypeStruct((B, L, C), F32),
        compiler_params=pltpu.CompilerParams(
            dimension_semantics=("parallel", "parallel"), vmem_limit_bytes=VMEM_LIMIT_BYTES),
        name="ssd_prep",
    )(ssd, conv_w, conv_b.reshape(1, C))


SSD_HEADS = 4
SSD_HEADDIM = 64
SSD_STATE = 64
SSD_GROUPS = 2
SSD_INNER = SSD_HEADS * SSD_HEADDIM
LANES = 128
NEG_INF = float("-inf")


def _scan_chunk_index(ph, j, n):
    return j + (1 - ph) * (n - 1 - 2 * j)


def _ssd_kernel(z_ref, xbc_ref, dt_ref, par_ref, dskip_ref, ng_ref, sf0_ref, sb0_ref,
                y_ref, sf_ref, sb_ref, yb_scr, st_scr, *, n_chunks):
    ph = pl.program_id(1)
    j = pl.program_id(2)
    is_fwd = ph == 1
    c = _scan_chunk_index(ph, j, n_chunks)
    R = xbc_ref.shape[1]

    @pl.when((j == 0) & is_fwd)
    def _():
        st_scr[...] = sf0_ref[0]

    @pl.when((j == 0) & jnp.logical_not(is_fwd))
    def _():
        st_scr[...] = sb0_ref[0]

    dsel = lambda v: jnp.where(is_fwd, v, pltpu.roll(v, LANES - SSD_HEADS, 1))
    dt_raw = dsel(dt_ref[0]) + dsel(par_ref[8:16, :])[0:1, :]
    dt = jnp.maximum(dt_raw, 0.0) + jnp.log(1.0 + jnp.exp(-jnp.abs(dt_raw)))
    a = -jnp.exp(dsel(par_ref[0:8, :])[0:1, :]) * dt
    row = lax.broadcasted_iota(jnp.int32, (R, R), 0)
    col = lax.broadcasted_iota(jnp.int32, (R, R), 1)
    mask = jnp.where(is_fwd, row - col, col - row) >= 0
    tri = jnp.where(mask, 1.0, 0.0)
    cs = jnp.dot(tri, a, precision=lax.Precision.HIGHEST, preferred_element_type=F32)
    cs_t = cs.T
    total = jnp.sum(a, axis=0, keepdims=True)

    xbc = xbc_ref[0]
    xs = xbc[:, :SSD_INNER]
    bm = xbc[:, SSD_INNER:SSD_INNER + LANES]
    cm = xbc[:, SSD_INNER + LANES:SSD_INNER + 2 * LANES]
    lane = lax.broadcasted_iota(jnp.int32, (R, LANES), 1)
    lo_half = lane < SSD_HEADDIM
    lo_half_st = lax.broadcasted_iota(jnp.int32, (LANES, LANES), 1) < SSD_HEADDIM
    cm_b = cm.astype(BF16)
    y_groups = []
    for g in range(SSD_GROUPS):
        h0, h1 = 2 * g, 2 * g + 1
        xs_g = xs[:, g * LANES:(g + 1) * LANES]
        xdt = (xs_g * jnp.where(lo_half, dt[:, h0:h0 + 1], dt[:, h1:h1 + 1])).astype(BF16)
        bm_g = jnp.where((lane >= g * SSD_STATE) & (lane < (g + 1) * SSD_STATE), bm, 0.0)
        G = lax.dot_general(cm_b, bm_g.astype(BF16), (((1,), (1,)), ((), ())), preferred_element_type=F32)
        st_old = st_scr[g]
        st_old_b = st_old.astype(BF16)
        y_h, st_h = [], []
        for h in (h0, h1):
            cs_col = cs[:, h:h + 1]
            decay = jnp.exp(jnp.where(mask, cs_col - cs_t[h:h + 1, :], NEG_INF))
            yd = jnp.dot((G * decay).astype(BF16), xdt, preferred_element_type=F32)
            yo = jnp.dot((cm * jnp.exp(cs_col)).astype(BF16), st_old_b, preferred_element_type=F32)
            y_h.append(yd + yo)
            tot = total[:, h:h + 1]
            upd = lax.dot_general((bm_g * jnp.exp(tot - cs_col)).astype(BF16), xdt,
                                  (((0,), (0,)), ((), ())), preferred_element_type=F32)
            st_h.append(jnp.exp(tot) * st_old + upd)
        y_groups.append(jnp.where(lo_half, y_h[0], y_h[1]))
        st_scr[g] = jnp.where(lo_half_st, st_h[0], st_h[1])
    y = jnp.concatenate(y_groups, axis=1)

    @pl.when(jnp.logical_not(is_fwd))
    def _():
        yb_scr[c] = y

    @pl.when(is_fwd)
    def _():
        z = z_ref[0]
        yt = (y + yb_scr[c] + xs * dskip_ref[...]) * (z * jax.nn.sigmoid(z))
        ms = jnp.mean(yt * yt, axis=-1, keepdims=True)
        y_ref[0] = yt * lax.rsqrt(ms + RMS_EPS) * ng_ref[...]

    @pl.when((j == n_chunks - 1) & is_fwd)
    def _():
        sf_ref[0] = st_scr[...]

    @pl.when((j == n_chunks - 1) & jnp.logical_not(is_fwd))
    def _():
        sb_ref[0] = st_scr[...]


def ssd_scan(ssd, xbc, par, d_skip, norm_g, s_f0, s_b0, *, z_blk, dt_blk, chunk=256):
    B, L, _ = ssd.shape
    R = min(chunk, L)
    assert L % R == 0
    n = L // R
    cidx = lambda ph, j: _scan_chunk_index(ph, j, n)
    st_spec = pl.BlockSpec((1, SSD_GROUPS, LANES, LANES), lambda b, ph, j: (b, 0, 0, 0))
    st_shape = jax.ShapeDtypeStruct((B, SSD_GROUPS, LANES, LANES), F32)
    return pl.pallas_call(
        functools.partial(_ssd_kernel, n_chunks=n),
        grid=(B, 2, n),
        in_specs=[
            pl.BlockSpec((1, R, SSD_INNER), lambda b, ph, j: (b, cidx(ph, j), z_blk)),
            pl.BlockSpec((1, R, xbc.shape[2]), lambda b, ph, j: (b, cidx(ph, j), 0)),
            pl.BlockSpec((1, R, LANES), lambda b, ph, j: (b, cidx(ph, j), dt_blk)),
            pl.BlockSpec((16, LANES), lambda b, ph, j: (0, 0)),
            pl.BlockSpec((1, SSD_INNER), lambda b, ph, j: (0, 0)),
            pl.BlockSpec((1, SSD_INNER), lambda b, ph, j: (0, 0)),
            st_spec, st_spec,
        ],
        out_specs=[pl.BlockSpec((1, R, SSD_INNER), lambda b, ph, j: (b, ph * j, 0)), st_spec, st_spec],
        out_shape=[jax.ShapeDtypeStruct((B, L, SSD_INNER), F32), st_shape, st_shape],
        scratch_shapes=[pltpu.VMEM((n, R, SSD_INNER), F32), pltpu.VMEM((SSD_GROUPS, LANES, LANES), F32)],
        compiler_params=pltpu.CompilerParams(
            dimension_semantics=("parallel", "arbitrary", "arbitrary"), vmem_limit_bytes=VMEM_LIMIT_BYTES),
        name="ssd_scan",
    )(ssd, xbc, ssd, par, d_skip.reshape(1, SSD_INNER), norm_g.reshape(1, SSD_INNER), s_f0, s_b0)


HG_HEADS = 4
HG_DK = 64
HG_DV = 64
HG_KEY = HG_HEADS * HG_DK
HG_SUB = 16
HG_CHUNK = 64
HG_SUBS_PER_CHUNK = HG_CHUNK // HG_SUB


def _split_dot(mask_b, x):
    hi = x.astype(BF16)
    lo = (x - hi.astype(F32)).astype(BF16)
    return (jnp.dot(mask_b, hi, preferred_element_type=F32) + jnp.dot(mask_b, lo, preferred_element_type=F32))


def _hgrn_kernel(q_ref, f_ref, v_ref, g_ref, lbl_ref, ng_ref, sf0_ref, sb0_ref,
                 y_ref, sf_ref, sb_ref, ob_scr, st_scr, qd_scr, ke_scr, vv_scr, tot_scr, oi_scr,
                 *, n_blocks, layer, depth):
    ph = pl.program_id(1)
    j = pl.program_id(2)
    is_fwd = ph == 1
    c = _scan_chunk_index(ph, j, n_blocks)
    R = q_ref.shape[1]
    W = HG_KEY

    @pl.when((j == 0) & is_fwd)
    def _():
        st_scr[...] = sf0_ref[0]

    @pl.when((j == 0) & jnp.logical_not(is_fwd))
    def _():
        st_scr[...] = sb0_ref[0]

    def lower_bound(d):
        x = lbl_ref[d * depth:(d + 1) * depth, :]
        e = jnp.exp(x - jnp.max(x, axis=0, keepdims=True))
        p = e / jnp.sum(e, axis=0, keepdims=True)
        return jnp.sum(p[1:layer + 1, :], axis=0, keepdims=True) if layer > 0 else jnp.zeros((1, W), F32)

    lb = jnp.where(is_fwd, lower_bound(0), lower_bound(1))
    f = f_ref[0]
    logf = jnp.log(lb + (1.0 - lb) * jax.nn.sigmoid(f))
    k = (1.0 - lb) * jax.nn.sigmoid(-f)
    q = q_ref[0] * (HG_DK ** -0.5)
    v_b = v_ref[0].astype(BF16)

    row = lax.broadcasted_iota(jnp.int32, (R, R), 0)
    col = lax.broadcasted_iota(jnp.int32, (R, R), 1)
    dirge = jnp.where(is_fwd, row - col, col - row) >= 0
    same_sub = (row // HG_SUB) == (col // HG_SUB)
    same_chunk = (row // HG_CHUNK) == (col // HG_CHUNK)
    one = lambda m: jnp.where(m, 1.0, 0.0).astype(BF16)
    b16 = _split_dot(one(same_sub & dirge), logf)
    tot16 = _split_dot(one(same_sub), logf)
    b64 = _split_dot(one(same_chunk & dirge), logf)
    tot64 = _split_dot(one(same_chunk), logf)

    shifted = lambda s: jnp.where(is_fwd, pltpu.roll(tot16, R - s, 0), pltpu.roll(tot16, s, 0))
    nx1 = shifted(HG_SUB)
    nx2 = nx1 + shifted(2 * HG_SUB)
    k_end16 = k * jnp.exp(tot16 - b16)
    keys = [(k * jnp.exp(-b16)).astype(BF16), k_end16.astype(BF16),
            (k_end16 * jnp.exp(nx1)).astype(BF16), (k_end16 * jnp.exp(nx2)).astype(BF16)]
    q16 = q * jnp.exp(b16)
    lane = lax.broadcasted_iota(jnp.int32, (R, W), 1)
    head_mask = [(lane // HG_DK) == h for h in range(HG_HEADS)]
    qm = jnp.concatenate([jnp.where(m, q16, 0.0) for m in head_mask], axis=0).astype(BF16)
    dist = jnp.where(is_fwd, row // HG_SUB - col // HG_SUB, col // HG_SUB - row // HG_SUB)
    dmask = [same_sub & dirge] + [same_chunk & (dist == d) for d in range(1, HG_SUBS_PER_CHUNK)]
    att = None
    for d in range(HG_SUBS_PER_CHUNK):
        sc = lax.dot_general(qm, keys[d], (((1,), (1,)), ((), ())), preferred_element_type=F32)
        sc = jnp.concatenate([jnp.where(dmask[d], sc[h * R:(h + 1) * R], 0.0) for h in range(HG_HEADS)], axis=0)
        att = sc if att is None else att + sc
    o_all = jnp.dot(att.astype(BF16), v_b, preferred_element_type=F32)
    o = None
    for h in range(HG_HEADS):
        t = jnp.where(head_mask[h], o_all[h * R:(h + 1) * R], 0.0)
        o = t if o is None else o + t

    qd_scr[...] = (q * jnp.exp(b64)).astype(BF16)
    ke_scr[...] = (k * jnp.exp(tot64 - b64)).astype(BF16)
    vv_scr[...] = v_b
    tot_scr[...] = tot64
    srow = lax.broadcasted_iota(jnp.int32, (W, W), 0)
    scol = lax.broadcasted_iota(jnp.int32, (W, W), 1)
    bd_mask = (srow // HG_DV) == (scol // HG_DK)
    st = st_scr[...]
    for i in range(R // HG_CHUNK):
        cc = jnp.where(is_fwd, i, R // HG_CHUNK - 1 - i)
        rows = pl.ds(pl.multiple_of(cc * HG_CHUNK, HG_CHUNK), HG_CHUNK)
        oi_scr[rows, :] = lax.dot_general(qd_scr[rows, :], st.astype(BF16), (((1,), (1,)), ((), ())),
                                          preferred_element_type=F32)
        upd = lax.dot_general(vv_scr[rows, :], ke_scr[rows, :], (((0,), (0,)), ((), ())),
                              preferred_element_type=F32)
        decay = jnp.exp(tot_scr[pl.ds(pl.multiple_of(cc * HG_CHUNK, HG_CHUNK), 1), :])
        st = decay * st + jnp.where(bd_mask, upd, 0.0)
    st_scr[...] = st
    o = o + oi_scr[...]

    @pl.when(jnp.logical_not(is_fwd))
    def _():
        ob_scr[c] = o

    @pl.when(is_fwd)
    def _():
        ot = o + ob_scr[c]
        sq = ot * ot
        ms = None
        for h in range(HG_HEADS):
            s = jnp.sum(jnp.where(head_mask[h], sq, 0.0), axis=-1, keepdims=True) * (1.0 / HG_DV)
            t = jnp.where(head_mask[h], s, 0.0)
            ms = t if ms is None else ms + t
        g = g_ref[0]
        y_ref[0] = ot * lax.rsqrt(ms + RMS_EPS) * ng_ref[...] * (g * jax.nn.sigmoid(g))

    @pl.when((j == n_blocks - 1) & is_fwd)
    def _():
        sf_ref[0] = st_scr[...]

    @pl.when((j == n_blocks - 1) & jnp.logical_not(is_fwd))
    def _():
        sb_ref[0] = st_scr[...]


def hgrn_scan(hg, lb_logits, norm_g, s_f0, s_b0, *, layer, block=256):
    B, L, _ = hg.shape
    W = HG_KEY
    R = min(block, L)
    assert L % R == 0 and R % HG_CHUNK == 0
    n = L // R
    depth = lb_logits.shape[0] // 2
    cidx = lambda ph, j: _scan_chunk_index(ph, j, n)
    col = lambda blk: pl.BlockSpec((1, R, W), lambda b, ph, j: (b, cidx(ph, j), blk))
    st_spec = pl.BlockSpec((1, W, W), lambda b, ph, j: (b, 0, 0))
    st_shape = jax.ShapeDtypeStruct((B, W, W), F32)
    return pl.pallas_call(
        functools.partial(_hgrn_kernel, n_blocks=n, layer=layer, depth=depth),
        grid=(B, 2, n),
        in_specs=[
            col(0),
            pl.BlockSpec((1, R, W), lambda b, ph, j: (b, cidx(ph, j), 2 - ph)),
            col(3), col(4),
            pl.BlockSpec(lb_logits.shape, lambda b, ph, j: (0, 0)),
            pl.BlockSpec((1, W), lambda b, ph, j: (0, 0)),
            st_spec, st_spec,
        ],
        out_specs=[pl.BlockSpec((1, R, W), lambda b, ph, j: (b, ph * j, 0)), st_spec, st_spec],
        out_shape=[jax.ShapeDtypeStruct((B, L, W), F32), st_shape, st_shape],
        scratch_shapes=[
            pltpu.VMEM((n, R, W), F32), pltpu.VMEM((W, W), F32),
            pltpu.VMEM((R, W), BF16), pltpu.VMEM((R, W), BF16), pltpu.VMEM((R, W), BF16),
            pltpu.VMEM((R, W), F32), pltpu.VMEM((R, W), F32),
        ],
        compiler_params=pltpu.CompilerParams(
            dimension_semantics=("parallel", "arbitrary", "arbitrary"), vmem_limit_bytes=VMEM_LIMIT_BYTES),
        name="hgrn_scan",
    )(hg, hg, hg, hg, lb_logits, jnp.tile(norm_g, HG_HEADS).reshape(1, W), s_f0, s_b0)


HY_WIDTH = 256
HY_BANDS = 16
HY_HIDDEN = 64
SUBLANES = 8


def _dot_hi(a, b):
    return jnp.dot(a, b, precision=lax.Precision.HIGHEST, preferred_element_type=F32)


def _hy_filter_kernel(bands_ref, w1t_ref, w1c_ref, w1s_ref, b1_ref, fr1_ref, w2_ref, b2_ref, fr2_ref, w3_ref,
                      decay_ref, f_ref, den_ref, *, L):
    i = pl.program_id(0)
    TR = f_ref.shape[1]
    m = i * TR + lax.broadcasted_iota(jnp.int32, (TR, LANES), 0)
    total = None
    for half in range(2):
        idx = m if half == 0 else jnp.where(m == 0, 0, L - m)
        idx_f = idx.astype(F32)
        t = idx_f * (1.0 / (L - 1))
        ang = ((2.0 * math.pi / L) * idx_f) * bands_ref[...]
        pre = (t[:, :HY_HIDDEN] * w1t_ref[...] + _dot_hi(jnp.cos(ang), w1c_ref[...])
               - _dot_hi(jnp.sin(ang), w1s_ref[...]) + b1_ref[...])
        h = jnp.sin(fr1_ref[...] * pre)
        h = jnp.sin(fr2_ref[...] * (_dot_hi(h, w2_ref[...]) + b2_ref[...]))
        h = _dot_hi(h, w3_ref[:, half * HY_WIDTH:(half + 1) * HY_WIDTH])
        tw = jnp.concatenate([t, t], axis=1)
        filt = h * jnp.exp(-tw * jnp.abs(decay_ref[...]))
        if half == 1:
            filt = jnp.where(jnp.concatenate([m, m], axis=1) == 0, 0.0, filt)
        f_ref[half] = filt
        s = jnp.sum(jnp.abs(filt), axis=0, keepdims=True)
        total = s if total is None else total + s

    @pl.when(i == 0)
    def _():
        den_ref[...] = jnp.zeros_like(den_ref)

    den_ref[...] += total


def hy_filter(L, w1, b1, fr1, w2, b2, fr2, w3, decay, *, tr=512):
    tr = min(tr, L)
    bands = np.zeros((1, LANES), np.float32)
    bands[0, :HY_BANDS] = np.linspace(1e-4, HY_BANDS - 1, HY_BANDS, dtype=np.float32)
    pad_rows = lambda w: jnp.pad(w, ((0, LANES - HY_BANDS), (0, 0)))
    row = lambda v: v.reshape(1, -1)
    args = (jnp.asarray(bands), w1[0:1], pad_rows(w1[1:1 + HY_BANDS]), pad_rows(w1[1 + HY_BANDS:]), row(b1), row(fr1),
            w2, row(b2), row(fr2), w3, row(decay))
    return pl.pallas_call(
        functools.partial(_hy_filter_kernel, L=L),
        grid=(L // tr,),
        in_specs=[pl.BlockSpec(a.shape, lambda i: (0, 0)) for a in args],
        out_specs=[pl.BlockSpec((2, tr, HY_WIDTH), lambda i: (0, i, 0)), pl.BlockSpec((1, HY_WIDTH), lambda i: (0, 0))],
        out_shape=[jax.ShapeDtypeStruct((2, L, HY_WIDTH), F32), jax.ShapeDtypeStruct((1, HY_WIDTH), F32)],
        compiler_params=pltpu.CompilerParams(
            dimension_semantics=("arbitrary",), vmem_limit_bytes=VMEM_LIMIT_BYTES),
        name="hy_filter",
    )(*args)


FFT_N1 = 64
FFT_N2 = 128
FFT_N = FFT_N1 * FFT_N2
FFT_L = FFT_N // 2
FFT_N1_NZ = FFT_N1 // 2


@functools.lru_cache(maxsize=None)
def _fft_constants():
    eye = np.eye(SUBLANES)
    k1 = np.arange(FFT_N1)[:, None]
    n1 = np.arange(FFT_N1_NZ)[None, :]
    ang1 = 2.0 * np.pi * ((k1 * n1) % FFT_N1) / FFT_N1
    kron_fwd = np.concatenate([np.kron(np.cos(ang1), eye), np.kron(np.sin(ang1), eye)], axis=0)
    kron_inv = np.concatenate([np.kron(np.cos(ang1).T, eye), np.kron(np.sin(ang1).T, eye)], axis=0) / FFT_N
    k = np.arange(FFT_N1)[:, None, None] + FFT_N1 * np.arange(FFT_N2)[None, :, None]
    n2 = np.arange(FFT_N2)[None, None, :]
    ang2 = 2.0 * np.pi * ((k * n2) % FFT_N) / FFT_N
    g_fwd = np.concatenate([np.cos(ang2), np.sin(ang2)], axis=1)
    g_inv = np.concatenate([np.cos(ang2).transpose(0, 2, 1), np.sin(ang2).transpose(0, 2, 1)], axis=1)
    to_b = lambda a: jnp.asarray(a, dtype=F32).astype(BF16)
    return to_b(kron_fwd), to_b(g_fwd), to_b(g_inv), to_b(kron_inv)


def _cplx_fwd(r, half_rows, cb):
    cr, sr = r[:half_rows], r[half_rows:]
    return cr[:, :cb] + sr[:, cb:], cr[:, cb:] - sr[:, :cb]


def _cplx_inv(r, half_rows, cb):
    cr, sr = r[:half_rows], r[half_rows:]
    return cr[:, :cb] - sr[:, cb:], cr[:, cb:] + sr[:, :cb]


def _fft_stage1(load_group, kron_ref, a_scr, cb):
    def body(jt, carry):
        rows = pl.ds(pl.multiple_of(jt * SUBLANES, SUBLANES), SUBLANES)
        d = load_group(rows).reshape(FFT_N1_NZ * SUBLANES, 2 * cb).astype(BF16)
        r = jnp.dot(kron_ref[...], d, preferred_element_type=F32)
        a_re, a_im = _cplx_fwd(r, FFT_N1 * SUBLANES, cb)
        a_scr[:, rows, :] = jnp.concatenate([a_re, a_im], axis=1).reshape(FFT_N1, SUBLANES, 2 * cb)
        return carry
    lax.fori_loop(0, FFT_N2 // SUBLANES, body, 0)


def _fft_fwd_data_kernel(z_ref, h_ref, kron_ref, g_ref, p_ref, a_scr):
    cb = z_ref.shape[4]
    load = lambda rows: jnp.concatenate([z_ref[0, 0, :, rows, :], z_ref[0, 1, :, rows, :]], axis=-1)
    _fft_stage1(load, kron_ref, a_scr, cb)

    def body(k1, carry):
        r = jnp.dot(g_ref[k1], a_scr[k1].astype(BF16), preferred_element_type=F32)
        x_re, x_im = _cplx_fwd(r, FFT_N2, cb)
        h = h_ref[0, k1]
        h_re, h_im = h[:, :cb], h[:, cb:]
        p_ref[0, 0, k1] = jnp.concatenate([x_re * h_re - x_im * h_im, x_re * h_im + x_im * h_re], axis=1).astype(BF16)
        return carry
    lax.fori_loop(0, FFT_N1, body, 0)


def _fft_fwd_filter_kernel(f_ref, den_ref, kron_ref, g_ref, h_ref, a_scr):
    half = pl.program_id(1)
    cb = f_ref.shape[3]
    inv_den = 1.0 / den_ref[...]

    def load(rows):
        re = f_ref[0, :, rows, :] * inv_den
        return jnp.concatenate([re, jnp.zeros_like(re)], axis=-1)
    _fft_stage1(load, kron_ref, a_scr, cb)

    def body(k1, carry):
        r = jnp.dot(g_ref[k1], a_scr[k1].astype(BF16), preferred_element_type=F32)
        x_re, x_im = _cplx_fwd(r, FFT_N2, cb)
        x = jnp.concatenate([x_re, x_im], axis=1)

        @pl.when(half == 0)
        def _():
            h_ref[0, k1] = x

        @pl.when(half == 1)
        def _():
            h_ref[0, k1] += jnp.where(k1 % 2 == 0, 1.0, -1.0) * x
        return carry
    lax.fori_loop(0, FFT_N1, body, 0)


def _fft_inv_kernel(p_ref, x0_ref, w_ref, bias_ref, gi_ref, kron_ref, o_ref, b_scr):
    cb = o_ref.shape[4]

    def body3(k1, carry):
        r = jnp.dot(gi_ref[k1], p_ref[0, 0, k1], preferred_element_type=F32)
        b_re, b_im = _cplx_inv(r, FFT_N2, cb)
        b_scr[k1] = jnp.concatenate([b_re, b_im], axis=1)
        return carry
    lax.fori_loop(0, FFT_N1, body3, 0)

    bias = bias_ref[...].reshape(1, 1, cb)

    def body4(jt, carry):
        rows = pl.ds(pl.multiple_of(jt * SUBLANES, SUBLANES), SUBLANES)
        d = b_scr[:, rows, :].reshape(FFT_N1 * SUBLANES, 2 * cb).astype(BF16)
        r = jnp.dot(kron_ref[...], d, preferred_element_type=F32)
        y_re, y_im = _cplx_inv(r, FFT_N1_NZ * SUBLANES, cb)
        for which, y in enumerate((y_re, y_im)):
            y3 = y.reshape(FFT_N1_NZ, SUBLANES, cb)
            o_ref[0, which, :, rows, :] = x0_ref[0, which, :, rows, :] * (y3 + w_ref[0, which, :, rows, :] * bias)
        return carry
    lax.fori_loop(0, FFT_N2 // SUBLANES, body4, 0)


def hyena_spectrum(fu, den, *, cb=128):
    kron_fwd, g_fwd, _, _ = _fft_constants()
    W = fu.shape[2]
    ncb = W // cb
    f4 = fu.reshape(2, FFT_N1_NZ, FFT_N2, W)
    const = lambda a: pl.BlockSpec(a.shape, lambda c, h: (0,) * a.ndim, pipeline_mode=pl.Buffered(1))
    return pl.pallas_call(
        _fft_fwd_filter_kernel,
        grid=(ncb, 2),
        in_specs=[
            pl.BlockSpec((1, FFT_N1_NZ, FFT_N2, cb), lambda c, h: (h, 0, 0, c)),
            pl.BlockSpec((1, cb), lambda c, h: (0, c)),
            const(kron_fwd), const(g_fwd),
        ],
        out_specs=pl.BlockSpec((1, FFT_N1, FFT_N2, 2 * cb), lambda c, h: (c, 0, 0, 0)),
        out_shape=jax.ShapeDtypeStruct((ncb, FFT_N1, FFT_N2, 2 * cb), F32),
        scratch_shapes=[pltpu.VMEM((FFT_N1, FFT_N2, 2 * cb), F32)],
        compiler_params=pltpu.CompilerParams(
            dimension_semantics=("parallel", "arbitrary"), vmem_limit_bytes=VMEM_LIMIT_BYTES),
        name="hyena_spectrum",
    )(f4, den, kron_fwd, g_fwd)


def hyena_conv(x0c, hw, spec, bias, *, cb=128):
    kron_fwd, g_fwd, g_inv, kron_inv = _fft_constants()
    B, L, W = hw.shape
    assert L == FFT_L and B % 2 == 0
    ncb = W // cb
    pair_shape = (B // 2, 2, FFT_N1_NZ, FFT_N2, W)
    x5 = x0c.reshape(pair_shape)
    w5 = hw.reshape(pair_shape)
    const = lambda a: pl.BlockSpec(a.shape, lambda c, p: (0,) * a.ndim, pipeline_mode=pl.Buffered(1))
    pair = pl.BlockSpec((1, 2, FFT_N1_NZ, FFT_N2, cb), lambda c, p: (p, 0, 0, 0, c))
    pspec = pl.BlockSpec((1, 1, FFT_N1, FFT_N2, 2 * cb), lambda c, p: (c, p, 0, 0, 0))
    prod = pl.pallas_call(
        _fft_fwd_data_kernel,
        grid=(ncb, B // 2),
        in_specs=[pair, pl.BlockSpec((1, FFT_N1, FFT_N2, 2 * cb), lambda c, p: (c, 0, 0, 0)),
                  const(kron_fwd), const(g_fwd)],
        out_specs=pspec,
        out_shape=jax.ShapeDtypeStruct((ncb, B // 2, FFT_N1, FFT_N2, 2 * cb), BF16),
        scratch_shapes=[pltpu.VMEM((FFT_N1, FFT_N2, 2 * cb), F32)],
        compiler_params=pltpu.CompilerParams(
            dimension_semantics=("parallel", "parallel"), vmem_limit_bytes=VMEM_LIMIT_BYTES),
        name="hyena_fft_fwd",
    )(w5, spec, kron_fwd, g_fwd)
    out = pl.pallas_call(
        _fft_inv_kernel,
        grid=(ncb, B // 2),
        in_specs=[pspec, pair, pair, pl.BlockSpec((1, cb), lambda c, p: (0, c)), const(g_inv), const(kron_inv)],
        out_specs=pair,
        out_shape=jax.ShapeDtypeStruct(pair_shape, F32),
        scratch_shapes=[pltpu.VMEM((FFT_N1, FFT_N2, 2 * cb), F32)],
        compiler_params=pltpu.CompilerParams(
            dimension_semantics=("parallel", "parallel"), vmem_limit_bytes=VMEM_LIMIT_BYTES),
        name="hyena_fft_inv",
    )(prod, x5, w5, bias.reshape(1, W), g_inv, kron_inv)
    return out.reshape(B, L, W)


@functools.lru_cache(maxsize=None)
def _dense_dft_constants(L):
    n_full = 2 * L
    k = np.arange(n_full)[:, None]
    n = np.arange(n_full)[None, :]
    ang = 2.0 * np.pi * ((k * n) % n_full) / n_full
    fwd = np.concatenate([np.cos(ang), np.sin(ang)], axis=0)
    inv = np.concatenate([np.cos(ang[:L]), np.sin(ang[:L])], axis=0) / n_full
    return jnp.asarray(fwd, dtype=F32), jnp.asarray(inv, dtype=F32)


def _hyena_short_kernel(x0_ref, w_ref, f_ref, den_ref, bias_ref, ff_ref, fi_ref, o_ref):
    L, cb = o_ref.shape[2], o_ref.shape[3]
    filt = jnp.concatenate([f_ref[0], f_ref[1]], axis=0) / den_ref[...]
    r = _dot_hi(ff_ref[...], filt)
    h_re, h_im = r[:2 * L], -r[2 * L:]
    z = jnp.concatenate([w_ref[0, 0], w_ref[0, 1]], axis=1)
    x_re, x_im = _cplx_fwd(_dot_hi(ff_ref[:, :L], z), 2 * L, cb)
    p = jnp.concatenate([x_re * h_re - x_im * h_im, x_re * h_im + x_im * h_re], axis=1)
    y_re, y_im = _cplx_inv(_dot_hi(fi_ref[...], p), L, cb)
    for which, y in enumerate((y_re, y_im)):
        o_ref[0, which] = x0_ref[0, which] * (y + w_ref[0, which] * bias_ref[...])


def hyena_conv_short(x0c, hw, fu, den, bias, *, cb=128):
    B, L, W = hw.shape
    fwd, inv = _dense_dft_constants(L)
    pair_shape = (B // 2, 2, L, W)
    pair = pl.BlockSpec((1, 2, L, cb), lambda p, c: (p, 0, 0, c))
    vec = pl.BlockSpec((1, cb), lambda p, c: (0, c))
    const = lambda a: pl.BlockSpec(a.shape, lambda p, c: (0, 0))
    out = pl.pallas_call(
        _hyena_short_kernel,
        grid=(B // 2, W // cb),
        in_specs=[pair, pair, pl.BlockSpec((2, L, cb), lambda p, c: (0, 0, c)), vec, vec, const(fwd), const(inv)],
        out_specs=pair,
        out_shape=jax.ShapeDtypeStruct(pair_shape, F32),
        compiler_params=pltpu.CompilerParams(
            dimension_semantics=("parallel", "parallel"), vmem_limit_bytes=VMEM_LIMIT_BYTES),
        name="hyena_conv_short",
    )(x0c.reshape(pair_shape), hw.reshape(pair_shape), fu, den, bias.reshape(1, W), fwd, inv)
    return out.reshape(B, L, W)


GRID_W = 64
HY_COLS = 3 * HY_WIDTH
SC_COLS = 3 * 256
HG_COLS = 3 * HG_KEY + 2 * HG_HEADS * HG_DV
SSD_XBC = SSD_INNER + 2 * SSD_GROUPS * SSD_STATE
SSD_COLS = SSD_INNER + SSD_XBC + 2 * SSD_HEADS


def _sincos_1d(pos, dim):
    omega = 1.0 / (10000.0 ** (jnp.arange(dim // 2, dtype=F32) / (dim // 2)))
    ang = pos.astype(F32)[:, None] * omega[None]
    return jnp.concatenate([jnp.sin(ang), jnp.cos(ang)], -1)


def _grid_pos_embed(rows, dim):
    row = jnp.repeat(jnp.arange(rows), GRID_W)
    col = jnp.tile(jnp.arange(GRID_W), rows)
    return jnp.concatenate([_sincos_1d(row, dim // 2), _sincos_1d(col, dim // 2)], -1)


def kernel(x, c, ctx, c_ctx, w_ada, b_ada, w_in, hy_conv_w, hy_conv_b, hy_w1, hy_b1, hy_freq1,
           hy_w2, hy_b2, hy_freq2, hy_w3, hy_decay, hy_bias, sc_conv_w, hg_lb_logits, hg_norm_g,
           ssd_conv_w, ssd_conv_b, ssd_a_log, ssd_dt_bias, ssd_d, ssd_norm_g, w_gate, b_gate, w_br,
           w_o, ln1_g, ln1_b, ln2_g, ln2_b, w_router, b_router, w_e1, w_e3, w_e2):
    B, L, D = x.shape
    LC = ctx.shape[1]
    depth = w_in.shape[0]
    lat = (x + _grid_pos_embed(L // GRID_W, D).astype(x.dtype)[None]).reshape(B * L, D)
    cx = ctx.reshape(B * LC, D)

    n_vec = -(-(B + 1) // SUBLANES) * SUBLANES
    cvecs = jnp.concatenate([c, c_ctx[None], jnp.zeros((n_vec - B - 1, D), c.dtype)], axis=0)
    mods = ada_mod(cvecs, w_ada, b_ada).reshape(depth, n_vec, 6, D)
    lb_logits = hg_lb_logits.reshape(2 * depth, HG_KEY)

    in_cols = w_in.shape[2]
    widths = (HY_COLS, SC_COLS, HG_COLS, -(-SSD_COLS // LANES) * LANES)
    z_state = lambda *shape: jnp.zeros(shape, F32)

    for l in range(depth):
        ctx_out = l < depth - 1
        mod_lat, mod_ctx = mods[l, :B], mods[l, B:B + 1]
        w_in_b = jnp.pad(w_in[l], ((0, 0), (0, sum(widths) - in_cols))).astype(BF16)
        ssd_par = jnp.zeros((2 * SUBLANES, LANES), F32)
        ssd_par = ssd_par.at[:SUBLANES, :2 * SSD_HEADS].set(jnp.broadcast_to(ssd_a_log[l].reshape(1, -1), (SUBLANES, 2 * SSD_HEADS)))
        ssd_par = ssd_par.at[SUBLANES:, :2 * SSD_HEADS].set(jnp.broadcast_to(ssd_dt_bias[l].reshape(1, -1), (SUBLANES, 2 * SSD_HEADS)))
        d_skip = jnp.repeat(ssd_d[l], SSD_HEADDIM)
        filt_args = (hy_w1[l], hy_b1[l], hy_freq1[l], hy_w2[l], hy_b2[l], hy_freq2[l], hy_w3[l], hy_decay[l])

        def mixers(tokens, mod, seg_len, hg_state, ssd_state, want_out):
            n_seq = tokens.shape[0] // seg_len
            hy, sc, hg, ssd = in_proj(tokens, mod, w_in_b, widths, rows_per_mod=mod_rows(mod, tokens), tm=512)
            seq = lambda a: a.reshape(n_seq, seg_len, a.shape[1])
            hg_y, hg_f, hg_b = hgrn_scan(seq(hg), lb_logits, hg_norm_g[l], *hg_state, layer=l)
            ssd3 = seq(ssd)
            xbc = ssd_prep(ssd3, ssd_conv_w[l], ssd_conv_b[l], col0=SSD_INNER)
            ssd_y, ssd_f, ssd_b = ssd_scan(ssd3, xbc, ssd_par, d_skip, ssd_norm_g[l], *ssd_state,
                                           z_blk=0, dt_blk=(SSD_INNER + SSD_XBC) // LANES)
            branches = None
            if want_out:
                x0c, hw = hy_prep(seq(hy), hy_conv_w[l], hy_conv_b[l])
                fu, den = hy_filter(seg_len, *filt_args)
                if seg_len == FFT_L:
                    hy_y = hyena_conv(x0c, hw, hyena_spectrum(fu, den), hy_bias[l])
                else:
                    hy_y = hyena_conv_short(x0c, hw, fu, den, hy_bias[l])
                sc_y = shortconv(seq(sc), sc_conv_w[l])
                flat = lambda a: a.reshape(tokens.shape[0], a.shape[2])
                branches = [flat(hy_y), flat(sc_y), flat(hg_y), flat(ssd_y)]
            return branches, (hg_f, hg_b), (ssd_f, ssd_b)

        def mod_rows(mod, tokens):
            return tokens.shape[0] // mod.shape[0]

        def finish(tokens, mod, branches):
            rows = mod_rows(mod, tokens)
            t1 = merge_ln(tokens, mod, branches, w_gate[l].astype(BF16), b_gate[l], w_br[l].astype(BF16),
                          w_o[l].astype(BF16), ln1_g[l], ln1_b[l], rows_per_mod=rows, tm=512)
            return moe_ln(t1, mod, w_router, b_router, w_e1[l].astype(BF16), w_e3[l].astype(BF16),
                          w_e2[l].astype(BF16), ln2_g[l], ln2_b[l], rows_per_mod=rows, tm=1024)

        zero_hg = (z_state(B, HG_KEY, HG_KEY),) * 2
        zero_ssd = (z_state(B, SSD_GROUPS, LANES, LANES),) * 2
        br_ctx, hg_state, ssd_state = mixers(cx, mod_ctx, LC, zero_hg, zero_ssd, ctx_out)
        br_lat, _, _ = mixers(lat, mod_lat, L, hg_state, ssd_state, True)
        lat = finish(lat, mod_lat, br_lat)
        if ctx_out:
            cx = finish(cx, mod_ctx, br_ctx)
    return lat.reshape(B, L, D)
```

```python
import functools
import math

import jax
import jax.numpy as jnp
import numpy as np
from jax import lax
from jax.experimental import pallas as pl
from jax.experimental.pallas import tpu as pltpu

F32 = jnp.float32
BF16 = jnp.bfloat16

N_EXPERTS = 16
N_EXPERT_GROUPS = 4
GROUP_SIZE = N_EXPERTS // N_EXPERT_GROUPS
DEPTH = 2
DEEPNORM_ALPHA = (2 * DEPTH) ** 0.25
LN_EPS = 1e-5
RMS_EPS = 1e-6

VMEM_LIMIT_BYTES = 52 * 1024 * 1024


def _layernorm_rows(z, g, b):
    mu = jnp.mean(z, axis=-1, keepdims=True)
    zc = z - mu
    var = jnp.mean(zc * zc, axis=-1, keepdims=True)
    return zc * lax.rsqrt(var + LN_EPS) * g + b


def _route_t(logits_t, bias_col):
    z = logits_t - jnp.max(logits_t, axis=0, keepdims=True)
    ex = jnp.exp(z)
    scores = ex / jnp.sum(ex, axis=0, keepdims=True)
    sel = scores + bias_col
    eidx = lax.broadcasted_iota(jnp.int32, sel.shape, 0)
    neg = jnp.float32(-jnp.inf)

    def first_argmax(m):
        top = jnp.max(m, axis=0, keepdims=True)
        return top, jnp.min(jnp.where(m == top, eidx, N_EXPERTS), axis=0, keepdims=True)

    best_score = None
    best_grp = None
    for g in range(N_EXPERT_GROUPS):
        m = jnp.where((eidx >= g * GROUP_SIZE) & (eidx < (g + 1) * GROUP_SIZE), sel, neg)
        t1, i1 = first_argmax(m)
        t2 = jnp.max(jnp.where(eidx == i1, neg, m), axis=0, keepdims=True)
        s = t1 + t2
        if g == 0:
            best_score, best_grp = s, jnp.zeros_like(i1)
        else:
            better = s > best_score
            best_score = jnp.where(better, s, best_score)
            best_grp = jnp.where(better, g, best_grp)
    lo = best_grp * GROUP_SIZE
    masked = jnp.where((eidx >= lo) & (eidx < lo + GROUP_SIZE), sel, neg)
    _, ia = first_argmax(masked)
    _, ib = first_argmax(jnp.where(eidx == ia, neg, masked))
    w = jnp.where((eidx == ia) | (eidx == ib), scores, 0.0)
    return w / jnp.sum(w, axis=0, keepdims=True), best_grp


MOE_CHUNK = 64
MOE_ARMS = (1024, 512, 256, 128, 64)
MOE_STEP_EXPERTS = 2
GATE_TERMS = 3
POS_LANE = GATE_TERMS * N_EXPERTS


def _moe_ln_kernel(x_ref, mod_ref, wr_ref, br_ref, w1_ref, w3_ref, w2_ref, lng_ref, lnb_ref,
                   o_ref, pt_scr, hc_scr, gc_scr, yc_scr, seg_smem):
    step = pl.program_id(1)
    TM, R = pt_scr.shape
    D = x_ref.shape[1]
    G = N_EXPERT_GROUPS

    @pl.when(step == 0)
    def _():
        h = _modulate(x_ref[...], mod_ref, MOD_SHIFT2, MOD_SCALE2)
        h_hi = h.astype(BF16)
        h_lo = (h - h_hi.astype(F32)).astype(BF16)
        r_hi = jnp.dot(h_hi, wr_ref[...], preferred_element_type=F32)
        r_lo = jnp.dot(h_lo, wr_ref[:, :LANES], preferred_element_type=F32)
        logits_t = (r_hi[:, :LANES] + r_hi[:, LANES:] + r_lo).T[:N_EXPERTS]
        gate_t, grp = _route_t(logits_t, br_ref[:, 0:1])

        gidx = lax.broadcasted_iota(jnp.int32, (SUBLANES, TM), 0)
        tok = lax.broadcasted_iota(jnp.int32, (SUBLANES, TM), 1)
        member = jnp.where(gidx == grp, 1.0, 0.0)
        incl = member
        shift = 1
        while shift < TM:
            incl = incl + jnp.where(tok >= shift, pltpu.roll(incl, shift, 1), 0.0)
            shift *= 2
        rank = incl - member
        offset = jnp.int32(0)
        pos_row = jnp.zeros((1, TM), F32)
        for g in range(G):
            n_chunks = (jnp.sum(member[g:g + 1, :]).astype(jnp.int32) + (MOE_CHUNK - 1)) // MOE_CHUNK
            seg_smem[g] = n_chunks
            seg_smem[G + g] = offset
            pos_row = pos_row + member[g:g + 1, :] * (rank[g:g + 1, :] + offset.astype(F32))
            offset = offset + n_chunks * MOE_CHUNK

        terms, rest = [], gate_t
        for _ in range(GATE_TERMS):
            part = rest.astype(BF16).astype(F32)
            terms.append(part)
            rest = rest - part
        pad = jnp.zeros((LANES - POS_LANE - 1, TM), F32)
        side = jnp.concatenate(terms + [pos_row, pad], axis=0).T
        pos_col = side[:, POS_LANE:POS_LANE + 1].astype(jnp.int32)
        pt_scr[...] = jnp.where(lax.broadcasted_iota(jnp.int32, (TM, R), 1) == pos_col, 1.0, 0.0).astype(BF16)
        p = jnp.where(lax.broadcasted_iota(jnp.int32, (R, TM), 0) == pos_row.astype(jnp.int32), 1.0, 0.0).astype(BF16)
        gathered = jnp.dot(p, jnp.concatenate([h_hi, side.astype(BF16)], axis=1), preferred_element_type=F32)
        hc_scr[...] = gathered[:, :D].astype(BF16)
        gc_scr[...] = gathered[:, D:]
        yc_scr[...] = jnp.zeros_like(yc_scr)

    g = step // (GROUP_SIZE // MOE_STEP_EXPERTS)
    n_chunks = seg_smem[g]
    base = seg_smem[G + g]

    def expert_rows(start, size):
        rows = pl.ds(pl.multiple_of(start, MOE_CHUNK), size)
        hc = hc_scr[rows, :]
        gc = gc_scr[rows, :]
        lane = lax.broadcasted_iota(jnp.int32, gc.shape, 1)
        acc = None
        for k in range(MOE_STEP_EXPERTS):
            e = step * MOE_STEP_EXPERTS + k
            a = jnp.dot(hc, w1_ref[k], preferred_element_type=F32)
            b = jnp.dot(hc, w3_ref[k], preferred_element_type=F32)
            mid = (a * jax.nn.sigmoid(a) * b).astype(BF16)
            y = jnp.dot(mid, w2_ref[k], preferred_element_type=F32)
            gcol = jnp.sum(jnp.where((lane % N_EXPERTS == e) & (lane < POS_LANE), gc, 0.0), axis=-1, keepdims=True)
            acc = gcol * y if acc is None else acc + gcol * y
        yc_scr[rows, :] += acc

    for size in MOE_ARMS:
        if size > TM:
            continue
        taken = (n_chunks & (size // MOE_CHUNK)) != 0

        @pl.when(taken)
        def _(base=base, size=size):
            expert_rows(base, size)

        base = base + jnp.where(taken, size, 0)

    @pl.when(step == N_EXPERTS // MOE_STEP_EXPERTS - 1)
    def _():
        moe = jnp.dot(pt_scr[...], yc_scr[...].astype(BF16), preferred_element_type=F32)
        z = DEEPNORM_ALPHA * x_ref[...] + mod_ref[0, MOD_GATE2:MOD_GATE2 + 1, :] * moe
        o_ref[...] = _layernorm_rows(z, lng_ref[...], lnb_ref[...])


def moe_ln(x, mod, w_router, b_router, w1, w3, w2, ln_g, ln_b, *, rows_per_mod, tm):
    T, D = x.shape
    E, _, FF = w1.shape
    assert T % tm == 0 and rows_per_mod % tm == 0 and tm <= MOE_ARMS[0] and tm % MOE_CHUNK == 0
    tiles_per_mod = rows_per_mod // tm
    sorted_rows = tm + N_EXPERT_GROUPS * MOE_CHUNK
    wr_hi = w_router.astype(BF16)
    wr_lo = (w_router - wr_hi.astype(F32)).astype(BF16)
    lane_pad = lambda w: jnp.pad(w, ((0, 0), (0, LANES - E)))
    wr_split = jnp.concatenate([lane_pad(wr_hi), lane_pad(wr_lo)], axis=1)
    br_col = jnp.broadcast_to(b_router.reshape(E, 1), (E, LANES))
    return pl.pallas_call(
        _moe_ln_kernel,
        grid=(T // tm, E // MOE_STEP_EXPERTS),
        in_specs=[
            pl.BlockSpec((tm, D), lambda i, e: (i, 0)),
            pl.BlockSpec((1, 6, D), lambda i, e: (i // tiles_per_mod, 0, 0)),
            pl.BlockSpec((D, 2 * LANES), lambda i, e: (0, 0)),
            pl.BlockSpec((E, LANES), lambda i, e: (0, 0)),
            pl.BlockSpec((MOE_STEP_EXPERTS, D, FF), lambda i, e: (e, 0, 0)),
            pl.BlockSpec((MOE_STEP_EXPERTS, D, FF), lambda i, e: (e, 0, 0)),
            pl.BlockSpec((MOE_STEP_EXPERTS, FF, D), lambda i, e: (e, 0, 0)),
            pl.BlockSpec((1, D), lambda i, e: (0, 0)),
            pl.BlockSpec((1, D), lambda i, e: (0, 0)),
        ],
        out_specs=pl.BlockSpec((tm, D), lambda i, e: (i, 0)),
        out_shape=jax.ShapeDtypeStruct((T, D), F32),
        scratch_shapes=[
            pltpu.VMEM((tm, sorted_rows), BF16),
            pltpu.VMEM((sorted_rows, D), BF16),
            pltpu.VMEM((sorted_rows, LANES), F32),
            pltpu.VMEM((sorted_rows, D), F32),
            pltpu.SMEM((2 * N_EXPERT_GROUPS,), jnp.int32),
        ],
        compiler_params=pltpu.CompilerParams(
            dimension_semantics=("parallel", "arbitrary"), vmem_limit_bytes=VMEM_LIMIT_BYTES),
        name="moe_ln",
    )(x, mod, wr_split, br_col, w1, w3, w2, ln_g.reshape(1, D), ln_b.reshape(1, D))


def _ada_kernel(c_ref, w_ref, b_ref, o_ref):
    c = c_ref[...]
    s = c * jax.nn.sigmoid(c)
    o_ref[0] = jnp.dot(s, w_ref[0], precision=lax.Precision.HIGHEST, preferred_element_type=F32) + b_ref[0]


def ada_mod(cvecs, w_ada, b_ada, *, tn=1536):
    R, D = cvecs.shape
    depth, _, N = w_ada.shape
    assert N % tn == 0
    return pl.pallas_call(
        _ada_kernel,
        grid=(depth, N // tn),
        in_specs=[
            pl.BlockSpec((R, D), lambda l, j: (0, 0)),
            pl.BlockSpec((1, D, tn), lambda l, j: (l, 0, j)),
            pl.BlockSpec((1, 1, tn), lambda l, j: (l, 0, j)),
        ],
        out_specs=pl.BlockSpec((1, R, tn), lambda l, j: (l, 0, j)),
        out_shape=jax.ShapeDtypeStruct((depth, R, N), F32),
        compiler_params=pltpu.CompilerParams(
            dimension_semantics=("parallel", "parallel"), vmem_limit_bytes=VMEM_LIMIT_BYTES),
        name="ada_mod",
    )(cvecs, w_ada, b_ada.reshape(depth, 1, N))


MOD_SHIFT1, MOD_SCALE1, MOD_GATE1, MOD_SHIFT2, MOD_SCALE2, MOD_GATE2 = range(6)


def _modulate(x, mod_ref, shift_row, scale_row):
    return x * (1.0 + mod_ref[0, scale_row:scale_row + 1, :]) + mod_ref[0, shift_row:shift_row + 1, :]


def _in_proj_kernel(x_ref, mod_ref, w_ref, *out_refs):
    h = _modulate(x_ref[...], mod_ref, MOD_SHIFT1, MOD_SCALE1).astype(BF16)
    off = 0
    for o_ref in out_refs:
        n = o_ref.shape[1]
        o_ref[...] = jnp.dot(h, w_ref[:, off:off + n], preferred_element_type=F32).astype(o_ref.dtype)
        off += n


def in_proj(x, mod, w, widths, *, rows_per_mod, tm):
    T, D = x.shape
    assert T % tm == 0 and rows_per_mod % tm == 0 and w.shape[1] == sum(widths)
    tiles_per_mod = rows_per_mod // tm
    return pl.pallas_call(
        _in_proj_kernel,
        grid=(T // tm,),
        in_specs=[
            pl.BlockSpec((tm, D), lambda i: (i, 0)),
            pl.BlockSpec((1, 6, D), lambda i: (i // tiles_per_mod, 0, 0)),
            pl.BlockSpec(w.shape, lambda i: (0, 0), pipeline_mode=pl.Buffered(1)),
        ],
        out_specs=[pl.BlockSpec((tm, n), lambda i: (i, 0)) for n in widths],
        out_shape=[jax.ShapeDtypeStruct((T, n), F32) for n in widths],
        compiler_params=pltpu.CompilerParams(
            dimension_semantics=("parallel",), vmem_limit_bytes=VMEM_LIMIT_BYTES),
        name="in_proj",
    )(x, mod, w)


N_BRANCH = 4


def _merge_kernel(x_ref, mod_ref, hy_ref, sc_ref, hg_ref, ssd_ref, wg_ref, bg_ref, wbr_ref, wo_ref,
                  lng_ref, lnb_ref, o_ref):
    x = x_ref[...]
    D = x.shape[1]
    h = _modulate(x, mod_ref, MOD_SHIFT1, MOD_SCALE1).astype(BF16)
    y = None
    for k, br_ref in enumerate((hy_ref, sc_ref, hg_ref, ssd_ref)):
        gate = jax.nn.sigmoid(
            jnp.dot(h, wg_ref[:, k * D:(k + 1) * D], preferred_element_type=F32) + bg_ref[:, k * D:(k + 1) * D])
        term = gate * jnp.dot(br_ref[...].astype(BF16), wbr_ref[k], preferred_element_type=F32)
        y = term if y is None else y + term
    y = jnp.dot(y.astype(BF16), wo_ref[...], preferred_element_type=F32)
    z = DEEPNORM_ALPHA * x + mod_ref[0, MOD_GATE1:MOD_GATE1 + 1, :] * y
    o_ref[...] = _layernorm_rows(z, lng_ref[...], lnb_ref[...])


def merge_ln(x, mod, branches, w_gate, b_gate, w_br, w_o, ln_g, ln_b, *, rows_per_mod, tm):
    T, D = x.shape
    BW = branches[0].shape[1]
    assert T % tm == 0 and rows_per_mod % tm == 0
    tiles_per_mod = rows_per_mod // tm
    const = lambda shape: pl.BlockSpec(shape, lambda i: (0,) * len(shape), pipeline_mode=pl.Buffered(1))
    return pl.pallas_call(
        _merge_kernel,
        grid=(T // tm,),
        in_specs=[
            pl.BlockSpec((tm, D), lambda i: (i, 0)),
            pl.BlockSpec((1, 6, D), lambda i: (i // tiles_per_mod, 0, 0)),
        ] + [pl.BlockSpec((tm, BW), lambda i: (i, 0))] * N_BRANCH + [
            const((D, N_BRANCH * D)), const((1, N_BRANCH * D)), const((N_BRANCH, BW, D)), const((D, D)),
            const((1, D)), const((1, D)),
        ],
        out_specs=pl.BlockSpec((tm, D), lambda i: (i, 0)),
        out_shape=jax.ShapeDtypeStruct((T, D), F32),
        compiler_params=pltpu.CompilerParams(
            dimension_semantics=("parallel",), vmem_limit_bytes=VMEM_LIMIT_BYTES),
        name="merge_ln",
    )(x, mod, *branches, w_gate, b_gate.reshape(1, -1), w_br, w_o, ln_g.reshape(1, D), ln_b.reshape(1, D))


def _dwconv_rows(u, w_ref, col0=0):
    L, C = u.shape
    row = lax.broadcasted_iota(jnp.int32, u.shape, 0)
    prev = jnp.where(row == 0, 0.0, pltpu.roll(u, 1, 0))
    nxt = jnp.where(row == L - 1, 0.0, pltpu.roll(u, L - 1, 0))
    w = lambda k: w_ref[k:k + 1, col0:col0 + C]
    return prev * w(0) + u * w(1) + nxt * w(2)


def _hy_prep_kernel(x0_ref, x1_ref, v_ref, w0_ref, w1_ref, w2_ref, b0_ref, b1_ref, b2_ref, x0c_ref, hw_ref):
    x0c_ref[0] = _dwconv_rows(x0_ref[0], w0_ref) + b0_ref[...]
    x1c = _dwconv_rows(x1_ref[0], w1_ref) + b1_ref[...]
    vc = _dwconv_rows(v_ref[0], w2_ref) + b2_ref[...]
    hw_ref[0] = x1c * vc


def hy_prep(hy, conv_w, conv_b, *, cb=128):
    B, L, C3 = hy.shape
    W = C3 // 3
    nb = W // cb
    conv_b = conv_b.reshape(1, C3)
    data = lambda part: pl.BlockSpec((1, L, cb), lambda b, j: (b, 0, part * nb + j))
    wspec = lambda part: pl.BlockSpec((3, cb), lambda b, j: (0, part * nb + j))
    bspec = lambda part: pl.BlockSpec((1, cb), lambda b, j: (0, part * nb + j))
    out = pl.BlockSpec((1, L, cb), lambda b, j: (b, 0, j))
    return pl.pallas_call(
        _hy_prep_kernel,
        grid=(B, nb),
        in_specs=[data(0), data(1), data(2), wspec(0), wspec(1), wspec(2), bspec(0), bspec(1), bspec(2)],
        out_specs=[out, out],
        out_shape=[jax.ShapeDtypeStruct((B, L, W), F32), jax.ShapeDtypeStruct((B, L, W), F32)],
        compiler_params=pltpu.CompilerParams(
            dimension_semantics=("parallel", "parallel"), vmem_limit_bytes=VMEM_LIMIT_BYTES),
        name="hy_prep",
    )(hy, hy, hy, conv_w, conv_w, conv_w, conv_b, conv_b, conv_b)


def _sc_kernel(bg_ref, cg_ref, xs_ref, w_ref, o_ref):
    o_ref[0] = bg_ref[0] * _dwconv_rows(cg_ref[0] * xs_ref[0], w_ref)


def shortconv(sc, conv_w, *, cb=128):
    B, L, C3 = sc.shape
    W = C3 // 3
    nb = W // cb
    data = lambda part: pl.BlockSpec((1, L, cb), lambda b, j: (b, 0, part * nb + j))
    return pl.pallas_call(
        _sc_kernel,
        grid=(B, nb),
        in_specs=[data(0), data(1), data(2), pl.BlockSpec((3, cb), lambda b, j: (0, j))],
        out_specs=pl.BlockSpec((1, L, cb), lambda b, j: (b, 0, j)),
        out_shape=jax.ShapeDtypeStruct((B, L, W), F32),
        compiler_params=pltpu.CompilerParams(
            dimension_semantics=("parallel", "parallel"), vmem_limit_bytes=VMEM_LIMIT_BYTES),
        name="shortconv",
    )(sc, sc, sc, conv_w)


def _ssd_prep_kernel(u_ref, w_ref, b_ref, o_ref):
    y = _dwconv_rows(u_ref[0], w_ref) + b_ref[...]
    o_ref[0] = y * jax.nn.sigmoid(y)


def ssd_prep(ssd, conv_w, conv_b, *, col0, cb=128):
    B, L, _ = ssd.shape
    C = conv_w.shape[1]
    assert col0 % cb == 0 and C % cb == 0
    return pl.pallas_call(
        _ssd_prep_kernel,
        grid=(B, C // cb),
        in_specs=[
            pl.BlockSpec((1, L, cb), lambda b, j: (b, 0, col0 // cb + j)),
            pl.BlockSpec((3, cb), lambda b, j: (0, j)),
            pl.BlockSpec((1, cb), lambda b, j: (0, j)),
        ],
        out_specs=pl.BlockSpec((1, L, cb), lambda b, j: (b, 0, j)),
        out_shape=jax.ShapeDtypeStruct((B, L, C), F32),
        compiler_params=pltpu.CompilerParams(
            dimension_semantics=("parallel", "parallel"), vmem_limit_bytes=VMEM_LIMIT_BYTES),
        name="ssd_prep",
    )(ssd, conv_w, conv_b.reshape(1, C))


SSD_HEADS = 4
SSD_HEADDIM = 64
SSD_STATE = 64
SSD_GROUPS = 2
SSD_INNER = SSD_HEADS * SSD_HEADDIM
LANES = 128
NEG_INF = float("-inf")


SCAN_BATCH = 2


def _scan_chunk_index(ph, j, n):
    return j + (1 - ph) * (n - 1 - 2 * j)


def _ssd_kernel(z_ref, xbc_ref, dt_ref, par_ref, dskip_ref, ng_ref, sf0_ref, sb0_ref,
                y_ref, sf_ref, sb_ref, yb_scr, st_scr, *, n_chunks):
    ph = pl.program_id(1)
    j = pl.program_id(2)
    is_fwd = ph == 1
    c = _scan_chunk_index(ph, j, n_chunks)
    NB, R = xbc_ref.shape[0], xbc_ref.shape[1]

    @pl.when((j == 0) & is_fwd)
    def _():
        st_scr[...] = sf0_ref[...]

    @pl.when((j == 0) & jnp.logical_not(is_fwd))
    def _():
        st_scr[...] = sb0_ref[...]

    row = lax.broadcasted_iota(jnp.int32, (R, R), 0)
    col = lax.broadcasted_iota(jnp.int32, (R, R), 1)
    mask = jnp.where(is_fwd, row - col, col - row) >= 0
    tri = jnp.where(mask, 1.0, 0.0).astype(BF16)
    head_of_lane = lax.broadcasted_iota(jnp.int32, (LANES, SSD_INNER), 1) // SSD_HEADDIM
    expand = jnp.where(lax.broadcasted_iota(jnp.int32, (LANES, SSD_INNER), 0) == head_of_lane, 1.0, 0.0).astype(BF16)
    ys = [_ssd_chunk(bi, is_fwd, mask, tri, expand, xbc_ref, dt_ref, par_ref, st_scr) for bi in range(NB)]

    @pl.when(jnp.logical_not(is_fwd))
    def _():
        for bi in range(NB):
            yb_scr[bi, c] = ys[bi]

    @pl.when(is_fwd)
    def _():
        for bi in range(NB):
            z = z_ref[bi]
            xs = xbc_ref[bi, :, :SSD_INNER]
            yt = (ys[bi] + yb_scr[bi, c] + xs * dskip_ref[...]) * (z * jax.nn.sigmoid(z))
            ms = jnp.mean(yt * yt, axis=-1, keepdims=True)
            y_ref[bi] = yt * lax.rsqrt(ms + RMS_EPS) * ng_ref[...]

    @pl.when((j == n_chunks - 1) & is_fwd)
    def _():
        sf_ref[...] = st_scr[...]

    @pl.when((j == n_chunks - 1) & jnp.logical_not(is_fwd))
    def _():
        sb_ref[...] = st_scr[...]


def _ssd_chunk(bi, is_fwd, mask, tri, expand, xbc_ref, dt_ref, par_ref, st_scr):
    R = xbc_ref.shape[1]
    dsel = lambda v: jnp.where(is_fwd, v, pltpu.roll(v, LANES - SSD_HEADS, 1))
    dt_raw = dsel(dt_ref[bi]) + dsel(par_ref[8:16, :])[0:1, :]
    dt = jnp.maximum(dt_raw, 0.0) + jnp.log(1.0 + jnp.exp(-jnp.abs(dt_raw)))
    a = -jnp.exp(dsel(par_ref[0:8, :])[0:1, :]) * dt
    cs = _split_dot(tri, a, terms=3)
    cs_t = cs.T
    total = jnp.sum(a, axis=0, keepdims=True)

    spread = lambda v: jnp.dot(v.astype(BF16), expand, preferred_element_type=F32)
    dt_bc = spread(dt)
    in_decay = spread(jnp.exp(cs))
    out_decay = spread(jnp.exp(total - cs))
    e_total = jnp.exp(total)

    xbc = xbc_ref[bi]
    xdt = xbc[:, :SSD_INNER] * dt_bc
    xdt_b = xdt.astype(BF16)
    xout_b = (xdt * out_decay).astype(BF16)
    bm = xbc[:, SSD_INNER:SSD_INNER + LANES]
    cm_b = xbc[:, SSD_INNER + LANES:SSD_INNER + 2 * LANES].astype(BF16)
    lane = lax.broadcasted_iota(jnp.int32, (R, LANES), 1)
    lo_half = lane < SSD_HEADDIM
    lo_half_st = lax.broadcasted_iota(jnp.int32, (LANES, LANES), 1) < SSD_HEADDIM
    y_groups = []
    for g in range(SSD_GROUPS):
        h0, h1 = 2 * g, 2 * g + 1
        cols = slice(g * LANES, (g + 1) * LANES)
        bm_g = jnp.where((lane >= g * SSD_STATE) & (lane < (g + 1) * SSD_STATE), bm, 0.0).astype(BF16)
        G = lax.dot_general(cm_b, bm_g, (((1,), (1,)), ((), ())), preferred_element_type=F32)
        st_old = st_scr[bi, g]
        y_off = in_decay[:, cols] * jnp.dot(cm_b, st_old.astype(BF16), preferred_element_type=F32)
        y_diag = []
        for h in (h0, h1):
            decay = jnp.exp(jnp.where(mask, cs[:, h:h + 1] - cs_t[h:h + 1, :], NEG_INF))
            y_diag.append(jnp.dot((G * decay).astype(BF16), xdt_b[:, cols], preferred_element_type=F32))
        y_groups.append(jnp.where(lo_half, y_diag[0], y_diag[1]) + y_off)
        upd = lax.dot_general(bm_g, xout_b[:, cols], (((0,), (0,)), ((), ())), preferred_element_type=F32)
        st_scr[bi, g] = jnp.where(lo_half_st, e_total[:, h0:h0 + 1], e_total[:, h1:h1 + 1]) * st_old + upd
    return jnp.concatenate(y_groups, axis=1)


def ssd_scan(ssd, xbc, par, d_skip, norm_g, s_f0, s_b0, *, z_blk, dt_blk, chunk=256):
    B, L, _ = ssd.shape
    R = min(chunk, L)
    assert L % R == 0 and B % SCAN_BATCH == 0
    n = L // R
    NB = SCAN_BATCH
    cidx = lambda ph, j: _scan_chunk_index(ph, j, n)
    st_spec = pl.BlockSpec((NB, SSD_GROUPS, LANES, LANES), lambda b, ph, j: (b, 0, 0, 0))
    st_shape = jax.ShapeDtypeStruct((B, SSD_GROUPS, LANES, LANES), F32)
    return pl.pallas_call(
        functools.partial(_ssd_kernel, n_chunks=n),
        grid=(B // NB, 2, n),
        in_specs=[
            pl.BlockSpec((NB, R, SSD_INNER), lambda b, ph, j: (b, cidx(ph, j), z_blk)),
            pl.BlockSpec((NB, R, xbc.shape[2]), lambda b, ph, j: (b, cidx(ph, j), 0)),
            pl.BlockSpec((NB, R, LANES), lambda b, ph, j: (b, cidx(ph, j), dt_blk)),
            pl.BlockSpec((16, LANES), lambda b, ph, j: (0, 0)),
            pl.BlockSpec((1, SSD_INNER), lambda b, ph, j: (0, 0)),
            pl.BlockSpec((1, SSD_INNER), lambda b, ph, j: (0, 0)),
            st_spec, st_spec,
        ],
        out_specs=[pl.BlockSpec((NB, R, SSD_INNER), lambda b, ph, j: (b, ph * j, 0)), st_spec, st_spec],
        out_shape=[jax.ShapeDtypeStruct((B, L, SSD_INNER), F32), st_shape, st_shape],
        scratch_shapes=[pltpu.VMEM((NB, n, R, SSD_INNER), F32), pltpu.VMEM((NB, SSD_GROUPS, LANES, LANES), F32)],
        compiler_params=pltpu.CompilerParams(
            dimension_semantics=("parallel", "arbitrary", "arbitrary"), vmem_limit_bytes=VMEM_LIMIT_BYTES),
        name="ssd_scan",
    )(ssd, xbc, ssd, par, d_skip.reshape(1, SSD_INNER), norm_g.reshape(1, SSD_INNER), s_f0, s_b0)


HG_HEADS = 4
HG_DK = 64
HG_DV = 64
HG_KEY = HG_HEADS * HG_DK
HG_SUB = 16
HG_CHUNK = 64
HG_SUBS_PER_CHUNK = HG_CHUNK // HG_SUB


def _split_dot(mask_b, x, terms=2):
    out = None
    for _ in range(terms):
        part = x.astype(BF16)
        prod = jnp.dot(mask_b, part, preferred_element_type=F32)
        out = prod if out is None else out + prod
        x = x - part.astype(F32)
    return out


def _hgrn_kernel(q_ref, f_ref, v_ref, g_ref, lbl_ref, ng_ref, sf0_ref, sb0_ref,
                 y_ref, sf_ref, sb_ref, ob_scr, st_scr, qd_scr, ke_scr, vv_scr, tot_scr, oi_scr,
                 *, n_blocks, layer, depth):
    ph = pl.program_id(1)
    j = pl.program_id(2)
    is_fwd = ph == 1
    c = _scan_chunk_index(ph, j, n_blocks)
    NB, R = q_ref.shape[0], q_ref.shape[1]
    W = HG_KEY

    @pl.when((j == 0) & is_fwd)
    def _():
        st_scr[...] = sf0_ref[...]

    @pl.when((j == 0) & jnp.logical_not(is_fwd))
    def _():
        st_scr[...] = sb0_ref[...]

    def lower_bound(d):
        x = lbl_ref[d * depth:(d + 1) * depth, :]
        e = jnp.exp(x - jnp.max(x, axis=0, keepdims=True))
        p = e / jnp.sum(e, axis=0, keepdims=True)
        return jnp.sum(p[1:layer + 1, :], axis=0, keepdims=True) if layer > 0 else jnp.zeros((1, W), F32)

    lb = jnp.where(is_fwd, lower_bound(0), lower_bound(1))
    row = lax.broadcasted_iota(jnp.int32, (R, R), 0)
    col = lax.broadcasted_iota(jnp.int32, (R, R), 1)
    dirge = jnp.where(is_fwd, row - col, col - row) >= 0
    same_sub = (row // HG_SUB) == (col // HG_SUB)
    same_chunk = (row // HG_CHUNK) == (col // HG_CHUNK)
    one = lambda m: jnp.where(m, 1.0, 0.0).astype(BF16)
    dist = jnp.where(is_fwd, row // HG_SUB - col // HG_SUB, col // HG_SUB - row // HG_SUB)
    masks = dict(
        cum16=one(same_sub & dirge), tot16=one(same_sub), cum64=one(same_chunk & dirge), tot64=one(same_chunk),
        dist=[same_sub & dirge] + [same_chunk & (dist == d) for d in range(1, HG_SUBS_PER_CHUNK)])
    lane = lax.broadcasted_iota(jnp.int32, (R, W), 1)
    head_mask = [(lane // HG_DK) == h for h in range(HG_HEADS)]
    outs = [_hgrn_block(bi, is_fwd, lb, masks, head_mask, q_ref, f_ref, v_ref,
                        st_scr, qd_scr, ke_scr, vv_scr, tot_scr, oi_scr) for bi in range(NB)]

    @pl.when(jnp.logical_not(is_fwd))
    def _():
        for bi in range(NB):
            ob_scr[bi, c] = outs[bi]

    @pl.when(is_fwd)
    def _():
        for bi in range(NB):
            ot = outs[bi] + ob_scr[bi, c]
            sq = ot * ot
            ms = jnp.zeros((R, W), F32)
            for h in range(HG_HEADS):
                s = jnp.sum(jnp.where(head_mask[h], sq, 0.0), axis=-1, keepdims=True) * (1.0 / HG_DV)
                ms = jnp.where(head_mask[h], s, ms)
            g = g_ref[bi]
            y_ref[bi] = ot * lax.rsqrt(ms + RMS_EPS) * ng_ref[...] * (g * jax.nn.sigmoid(g))

    @pl.when((j == n_blocks - 1) & is_fwd)
    def _():
        sf_ref[...] = st_scr[...]

    @pl.when((j == n_blocks - 1) & jnp.logical_not(is_fwd))
    def _():
        sb_ref[...] = st_scr[...]


def _hgrn_block(bi, is_fwd, lb, masks, head_mask, q_ref, f_ref, v_ref,
                st_scr, qd_scr, ke_scr, vv_scr, tot_scr, oi_scr):
    R = q_ref.shape[1]
    W = HG_KEY
    f = f_ref[bi]
    logf = jnp.log(lb + (1.0 - lb) * jax.nn.sigmoid(f))
    k = (1.0 - lb) * jax.nn.sigmoid(-f)
    q = q_ref[bi] * (HG_DK ** -0.5)
    v_b = v_ref[bi].astype(BF16)
    b16 = _split_dot(masks["cum16"], logf)
    tot16 = _split_dot(masks["tot16"], logf)
    b64 = _split_dot(masks["cum64"], logf)
    tot64 = _split_dot(masks["tot64"], logf)

    shifted = lambda s: jnp.where(is_fwd, pltpu.roll(tot16, R - s, 0), pltpu.roll(tot16, s, 0))
    nx1 = shifted(HG_SUB)
    nx2 = nx1 + shifted(2 * HG_SUB)
    k_end16 = k * jnp.exp(tot16 - b16)
    keys = [(k * jnp.exp(-b16)).astype(BF16), k_end16.astype(BF16),
            (k_end16 * jnp.exp(nx1)).astype(BF16), (k_end16 * jnp.exp(nx2)).astype(BF16)]
    q16 = q * jnp.exp(b16)
    qm =jnp.concatenate([jnp.where(m, q16, 0.0) for m in head_mask], axis=0).astype(BF16)
    att = [jnp.zeros((R, R), F32)] * HG_HEADS
    for d in range(HG_SUBS_PER_CHUNK):
        sc = lax.dot_general(qm, keys[d], (((1,), (1,)), ((), ())), preferred_element_type=F32)
        att = [jnp.where(masks["dist"][d], sc[h * R:(h + 1) * R], att[h]) for h in range(HG_HEADS)]
    o_all = jnp.dot(jnp.concatenate(att, axis=0).astype(BF16), v_b, preferred_element_type=F32)
    o = jnp.zeros((R, W), F32)
    for h in range(HG_HEADS):
        o = jnp.where(head_mask[h], o_all[h * R:(h + 1) * R], o)

    qd_scr[bi] = (q * jnp.exp(b64)).astype(BF16)
    ke_scr[bi] = (k * jnp.exp(tot64 - b64)).astype(BF16)
    vv_scr[bi] = v_b
    tot_scr[bi] = tot64
    srow = lax.broadcasted_iota(jnp.int32, (W, W), 0)
    scol = lax.broadcasted_iota(jnp.int32, (W, W), 1)
    bd_mask = (srow // HG_DV) == (scol // HG_DK)
    st = st_scr[bi]
    for i in range(R // HG_CHUNK):
        cc = jnp.where(is_fwd, i, R // HG_CHUNK - 1 - i)
        start = pl.multiple_of(cc * HG_CHUNK, HG_CHUNK)
        rows = pl.ds(start, HG_CHUNK)
        oi_scr[bi, rows, :] = lax.dot_general(qd_scr[bi, rows, :], st.astype(BF16), (((1,), (1,)), ((), ())),
                                              preferred_element_type=F32)
        upd = lax.dot_general(vv_scr[bi, rows, :], ke_scr[bi, rows, :], (((0,), (0,)), ((), ())),
                              preferred_element_type=F32)
        decay = jnp.exp(tot_scr[bi, pl.ds(start, 1), :])
        st = decay * st + jnp.where(bd_mask, upd, 0.0)
    st_scr[bi] = st
    return o + oi_scr[bi]


def hgrn_scan(hg, lb_logits, norm_g, s_f0, s_b0, *, layer, block=256):
    B, L, _ = hg.shape
    W = HG_KEY
    R = min(block, L)
    assert L % R == 0 and R % HG_CHUNK == 0 and B % SCAN_BATCH == 0
    n = L // R
    NB = SCAN_BATCH
    depth = lb_logits.shape[0] // 2
    cidx = lambda ph, j: _scan_chunk_index(ph, j, n)
    col = lambda blk: pl.BlockSpec((NB, R, W), lambda b, ph, j: (b, cidx(ph, j), blk))
    st_spec = pl.BlockSpec((NB, W, W), lambda b, ph, j: (b, 0, 0))
    st_shape = jax.ShapeDtypeStruct((B, W, W), F32)
    return pl.pallas_call(
        functools.partial(_hgrn_kernel, n_blocks=n, layer=layer, depth=depth),
        grid=(B // NB, 2, n),
        in_specs=[
            col(0),
            pl.BlockSpec((NB, R, W), lambda b, ph, j: (b, cidx(ph, j), 2 - ph)),
            col(3), col(4),
            pl.BlockSpec(lb_logits.shape, lambda b, ph, j: (0, 0)),
            pl.BlockSpec((1, W), lambda b, ph, j: (0, 0)),
            st_spec, st_spec,
        ],
        out_specs=[pl.BlockSpec((NB, R, W), lambda b, ph, j: (b, ph * j, 0)), st_spec, st_spec],
        out_shape=[jax.ShapeDtypeStruct((B, L, W), F32), st_shape, st_shape],
        scratch_shapes=[
            pltpu.VMEM((NB, n, R, W), F32), pltpu.VMEM((NB, W, W), F32),
            pltpu.VMEM((NB, R, W), BF16), pltpu.VMEM((NB, R, W), BF16), pltpu.VMEM((NB, R, W), BF16),
            pltpu.VMEM((NB, R, W), F32), pltpu.VMEM((NB, R, W), F32),
        ],
        compiler_params=pltpu.CompilerParams(
            dimension_semantics=("parallel", "arbitrary", "arbitrary"), vmem_limit_bytes=VMEM_LIMIT_BYTES),
        name="hgrn_scan",
    )(hg, hg, hg, hg, lb_logits, jnp.tile(norm_g, HG_HEADS).reshape(1, W), s_f0, s_b0)


HY_WIDTH = 256
HY_BANDS = 16
HY_HIDDEN = 64
SUBLANES = 8


def _dot_hi(a, b):
    return jnp.dot(a, b, precision=lax.Precision.HIGHEST, preferred_element_type=F32)


def _hy_filter_kernel(bands_ref, w1t_ref, w1c_ref, w1s_ref, b1_ref, fr1_ref, w2_ref, b2_ref, fr2_ref, w3_ref,
                      decay_ref, f_ref, den_ref, *, L):
    i = pl.program_id(0)
    TR = f_ref.shape[1]
    m = i * TR + lax.broadcasted_iota(jnp.int32, (TR, LANES), 0)
    total = None
    for half in range(2):
        idx = m if half == 0 else jnp.where(m == 0, 0, L - m)
        idx_f = idx.astype(F32)
        t = idx_f * (1.0 / (L - 1))
        ang = ((2.0 * math.pi / L) * idx_f) * bands_ref[...]
        pre = (t[:, :HY_HIDDEN] * w1t_ref[...] + _dot_hi(jnp.cos(ang), w1c_ref[...])
               - _dot_hi(jnp.sin(ang), w1s_ref[...]) + b1_ref[...])
        h = jnp.sin(fr1_ref[...] * pre)
        h = jnp.sin(fr2_ref[...] * (_dot_hi(h, w2_ref[...]) + b2_ref[...]))
        h = _dot_hi(h, w3_ref[:, half * HY_WIDTH:(half + 1) * HY_WIDTH])
        tw = jnp.concatenate([t, t], axis=1)
        filt = h * jnp.exp(-tw * jnp.abs(decay_ref[...]))
        if half == 1:
            filt = jnp.where(jnp.concatenate([m, m], axis=1) == 0, 0.0, filt)
        f_ref[half] = filt
        s = jnp.sum(jnp.abs(filt), axis=0, keepdims=True)
        total = s if total is None else total + s

    @pl.when(i == 0)
    def _():
        den_ref[...] = jnp.zeros_like(den_ref)

    den_ref[...] += total


def hy_filter(L, w1, b1, fr1, w2, b2, fr2, w3, decay, *, tr=512):
    tr = min(tr, L)
    bands = np.zeros((1, LANES), np.float32)
    bands[0, :HY_BANDS] = np.linspace(1e-4, HY_BANDS - 1, HY_BANDS, dtype=np.float32)
    pad_rows = lambda w: jnp.pad(w, ((0, LANES - HY_BANDS), (0, 0)))
    row = lambda v: v.reshape(1, -1)
    args = (jnp.asarray(bands), w1[0:1], pad_rows(w1[1:1 + HY_BANDS]), pad_rows(w1[1 + HY_BANDS:]), row(b1), row(fr1),
            w2, row(b2), row(fr2), w3, row(decay))
    return pl.pallas_call(
        functools.partial(_hy_filter_kernel, L=L),
        grid=(L // tr,),
        in_specs=[pl.BlockSpec(a.shape, lambda i: (0, 0)) for a in args],
        out_specs=[pl.BlockSpec((2, tr, HY_WIDTH), lambda i: (0, i, 0)), pl.BlockSpec((1, HY_WIDTH), lambda i: (0, 0))],
        out_shape=[jax.ShapeDtypeStruct((2, L, HY_WIDTH), F32), jax.ShapeDtypeStruct((1, HY_WIDTH), F32)],
        compiler_params=pltpu.CompilerParams(
            dimension_semantics=("arbitrary",), vmem_limit_bytes=VMEM_LIMIT_BYTES),
        name="hy_filter",
    )(*args)


FFT_N1 = 64
FFT_N2 = 128
FFT_N = FFT_N1 * FFT_N2
FFT_L = FFT_N // 2
FFT_N1_NZ = FFT_N1 // 2


@functools.lru_cache(maxsize=None)
def _fft_constants():
    eye = np.eye(SUBLANES)
    k1 = np.arange(FFT_N1)[:, None]
    n1 = np.arange(FFT_N1_NZ)[None, :]
    ang1 = 2.0 * np.pi * ((k1 * n1) % FFT_N1) / FFT_N1
    kron_fwd = np.concatenate([np.kron(np.cos(ang1), eye), np.kron(np.sin(ang1), eye)], axis=0)
    kron_inv = np.concatenate([np.kron(np.cos(ang1).T, eye), np.kron(np.sin(ang1).T, eye)], axis=0) / FFT_N
    k = np.arange(FFT_N1)[:, None, None] + FFT_N1 * np.arange(FFT_N2)[None, :, None]
    n2 = np.arange(FFT_N2)[None, None, :]
    ang2 = 2.0 * np.pi * ((k * n2) % FFT_N) / FFT_N
    g_fwd = np.concatenate([np.cos(ang2), np.sin(ang2)], axis=1)
    g_inv = np.concatenate([np.cos(ang2).transpose(0, 2, 1), np.sin(ang2).transpose(0, 2, 1)], axis=1)
    to_b = lambda a: jnp.asarray(a, dtype=F32).astype(BF16)
    return to_b(kron_fwd), to_b(g_fwd), to_b(g_inv), to_b(kron_inv)


def _cplx_fwd(r, half_rows, cb):
    cr, sr = r[:half_rows], r[half_rows:]
    return cr[:, :cb] + sr[:, cb:], cr[:, cb:] - sr[:, :cb]


def _cplx_inv(r, half_rows, cb):
    cr, sr = r[:half_rows], r[half_rows:]
    return cr[:, :cb] - sr[:, cb:], cr[:, cb:] + sr[:, :cb]


def _fft_stage1(load_group, kron_ref, a_scr, cb):
    def body(jt, carry):
        rows = pl.ds(pl.multiple_of(jt * SUBLANES, SUBLANES), SUBLANES)
        d = load_group(rows).reshape(FFT_N1_NZ * SUBLANES, 2 * cb).astype(BF16)
        r = jnp.dot(kron_ref[...], d, preferred_element_type=F32)
        a_re, a_im = _cplx_fwd(r, FFT_N1 * SUBLANES, cb)
        a_scr[:, rows, :] = jnp.concatenate([a_re, a_im], axis=1).reshape(FFT_N1, SUBLANES, 2 * cb)
        return carry
    lax.fori_loop(0, FFT_N2 // SUBLANES, body, 0)


def _fft_fwd_data_kernel(z_ref, h_ref, kron_ref, g_ref, p_ref, a_scr):
    cb = z_ref.shape[4]
    load = lambda rows: jnp.concatenate([z_ref[0, 0, :, rows, :], z_ref[0, 1, :, rows, :]], axis=-1)
    _fft_stage1(load, kron_ref, a_scr, cb)

    def body(k1, carry):
        r = jnp.dot(g_ref[k1], a_scr[k1].astype(BF16), preferred_element_type=F32)
        x_re, x_im = _cplx_fwd(r, FFT_N2, cb)
        h = h_ref[0, k1]
        h_re, h_im = h[:, :cb], h[:, cb:]
        p_ref[0, 0, k1] = jnp.concatenate([x_re * h_re - x_im * h_im, x_re * h_im + x_im * h_re], axis=1).astype(BF16)
        return carry
    lax.fori_loop(0, FFT_N1, body, 0)


def _fft_fwd_filter_kernel(f_ref, den_ref, kron_ref, g_ref, h_ref, a_scr):
    half = pl.program_id(1)
    cb = f_ref.shape[3]
    inv_den = 1.0 / den_ref[...]

    def load(rows):
        re = f_ref[0, :, rows, :] * inv_den
        return jnp.concatenate([re, jnp.zeros_like(re)], axis=-1)
    _fft_stage1(load, kron_ref, a_scr, cb)

    def body(k1, carry):
        r = jnp.dot(g_ref[k1], a_scr[k1].astype(BF16), preferred_element_type=F32)
        x_re, x_im = _cplx_fwd(r, FFT_N2, cb)
        x = jnp.concatenate([x_re, x_im], axis=1)

        @pl.when(half == 0)
        def _():
            h_ref[0, k1] = x

        @pl.when(half == 1)
        def _():
            h_ref[0, k1] += jnp.where(k1 % 2 == 0, 1.0, -1.0) * x
        return carry
    lax.fori_loop(0, FFT_N1, body, 0)


def _fft_inv_kernel(p_ref, x0_ref, w_ref, bias_ref, gi_ref, kron_ref, o_ref, b_scr):
    cb = o_ref.shape[4]

    def body3(k1, carry):
        r = jnp.dot(gi_ref[k1], p_ref[0, 0, k1], preferred_element_type=F32)
        b_re, b_im = _cplx_inv(r, FFT_N2, cb)
        b_scr[k1] = jnp.concatenate([b_re, b_im], axis=1)
        return carry
    lax.fori_loop(0, FFT_N1, body3, 0)

    bias = bias_ref[...].reshape(1, 1, cb)

    def body4(jt, carry):
        rows = pl.ds(pl.multiple_of(jt * SUBLANES, SUBLANES), SUBLANES)
        d = b_scr[:, rows, :].reshape(FFT_N1 * SUBLANES, 2 * cb).astype(BF16)
        r = jnp.dot(kron_ref[...], d, preferred_element_type=F32)
        y_re, y_im = _cplx_inv(r, FFT_N1_NZ * SUBLANES, cb)
        for which, y in enumerate((y_re, y_im)):
            y3 = y.reshape(FFT_N1_NZ, SUBLANES, cb)
            o_ref[0, which, :, rows, :] = x0_ref[0, which, :, rows, :] * (y3 + w_ref[0, which, :, rows, :] * bias)
        return carry
    lax.fori_loop(0, FFT_N2 // SUBLANES, body4, 0)


def hyena_spectrum(fu, den, *, cb=128):
    kron_fwd, g_fwd, _, _ = _fft_constants()
    W = fu.shape[2]
    ncb = W // cb
    f4 = fu.reshape(2, FFT_N1_NZ, FFT_N2, W)
    const = lambda a: pl.BlockSpec(a.shape, lambda c, h: (0,) * a.ndim, pipeline_mode=pl.Buffered(1))
    return pl.pallas_call(
        _fft_fwd_filter_kernel,
        grid=(ncb, 2),
        in_specs=[
            pl.BlockSpec((1, FFT_N1_NZ, FFT_N2, cb), lambda c, h: (h, 0, 0, c)),
            pl.BlockSpec((1, cb), lambda c, h: (0, c)),
            const(kron_fwd), const(g_fwd),
        ],
        out_specs=pl.BlockSpec((1, FFT_N1, FFT_N2, 2 * cb), lambda c, h: (c, 0, 0, 0)),
        out_shape=jax.ShapeDtypeStruct((ncb, FFT_N1, FFT_N2, 2 * cb), F32),
        scratch_shapes=[pltpu.VMEM((FFT_N1, FFT_N2, 2 * cb), F32)],
        compiler_params=pltpu.CompilerParams(
            dimension_semantics=("parallel", "arbitrary"), vmem_limit_bytes=VMEM_LIMIT_BYTES),
        name="hyena_spectrum",
    )(f4, den, kron_fwd, g_fwd)


def hyena_conv(x0c, hw, spec, bias, *, cb=128):
    kron_fwd, g_fwd, g_inv, kron_inv = _fft_constants()
    B, L, W = hw.shape
    assert L == FFT_L and B % 2 == 0
    ncb = W // cb
    pair_shape = (B // 2, 2, FFT_N1_NZ, FFT_N2, W)
    x5 = x0c.reshape(pair_shape)
    w5 = hw.reshape(pair_shape)
    const = lambda a: pl.BlockSpec(a.shape, lambda c, p: (0,) * a.ndim, pipeline_mode=pl.Buffered(1))
    pair = pl.BlockSpec((1, 2, FFT_N1_NZ, FFT_N2, cb), lambda c, p: (p, 0, 0, 0, c))
    pspec = pl.BlockSpec((1, 1, FFT_N1, FFT_N2, 2 * cb), lambda c, p: (c, p, 0, 0, 0))
    prod = pl.pallas_call(
        _fft_fwd_data_kernel,
        grid=(ncb, B // 2),
        in_specs=[pair, pl.BlockSpec((1, FFT_N1, FFT_N2, 2 * cb), lambda c, p: (c, 0, 0, 0)),
                  const(kron_fwd), const(g_fwd)],
        out_specs=pspec,
        out_shape=jax.ShapeDtypeStruct((ncb, B // 2, FFT_N1, FFT_N2, 2 * cb), BF16),
        scratch_shapes=[pltpu.VMEM((FFT_N1, FFT_N2, 2 * cb), F32)],
        compiler_params=pltpu.CompilerParams(
            dimension_semantics=("parallel", "parallel"), vmem_limit_bytes=VMEM_LIMIT_BYTES),
        name="hyena_fft_fwd",
    )(w5, spec, kron_fwd, g_fwd)
    out = pl.pallas_call(
        _fft_inv_kernel,
        grid=(ncb, B // 2),
        in_specs=[pspec, pair, pair, pl.BlockSpec((1, cb), lambda c, p: (0, c)), const(g_inv), const(kron_inv)],
        out_specs=pair,
        out_shape=jax.ShapeDtypeStruct(pair_shape, F32),
        scratch_shapes=[pltpu.VMEM((FFT_N1, FFT_N2, 2 * cb), F32)],
        compiler_params=pltpu.CompilerParams(
            dimension_semantics=("parallel", "parallel"), vmem_limit_bytes=VMEM_LIMIT_BYTES),
        name="hyena_fft_inv",
    )(prod, x5, w5, bias.reshape(1, W), g_inv, kron_inv)
    return out.reshape(B, L, W)


@functools.lru_cache(maxsize=None)
def _dense_dft_constants(L):
    n_full = 2 * L
    k = np.arange(n_full)[:, None]
    n = np.arange(n_full)[None, :]
    ang = 2.0 * np.pi * ((k * n) % n_full) / n_full
    fwd = np.concatenate([np.cos(ang), np.sin(ang)], axis=0)
    inv = np.concatenate([np.cos(ang[:L]), np.sin(ang[:L])], axis=0) / n_full
    return jnp.asarray(fwd, dtype=F32), jnp.asarray(inv, dtype=F32)


def _hyena_short_kernel(x0_ref, w_ref, f_ref, den_ref, bias_ref, ff_ref, fi_ref, o_ref):
    L, cb = o_ref.shape[2], o_ref.shape[3]
    filt = jnp.concatenate([f_ref[0], f_ref[1]], axis=0) / den_ref[...]
    r = _dot_hi(ff_ref[...], filt)
    h_re, h_im = r[:2 * L], -r[2 * L:]
    z = jnp.concatenate([w_ref[0, 0], w_ref[0, 1]], axis=1)
    x_re, x_im = _cplx_fwd(_dot_hi(ff_ref[:, :L], z), 2 * L, cb)
    p = jnp.concatenate([x_re * h_re - x_im * h_im, x_re * h_im + x_im * h_re], axis=1)
    y_re, y_im = _cplx_inv(_dot_hi(fi_ref[...], p), L, cb)
    for which, y in enumerate((y_re, y_im)):
        o_ref[0, which] = x0_ref[0, which] * (y + w_ref[0, which] * bias_ref[...])


def hyena_conv_short(x0c, hw, fu, den, bias, *, cb=128):
    B, L, W = hw.shape
    fwd, inv = _dense_dft_constants(L)
    pair_shape = (B // 2, 2, L, W)
    pair = pl.BlockSpec((1, 2, L, cb), lambda p, c: (p, 0, 0, c))
    vec = pl.BlockSpec((1, cb), lambda p, c: (0, c))
    const = lambda a: pl.BlockSpec(a.shape, lambda p, c: (0, 0))
    out = pl.pallas_call(
        _hyena_short_kernel,
        grid=(B // 2, W // cb),
        in_specs=[pair, pair, pl.BlockSpec((2, L, cb), lambda p, c: (0, 0, c)), vec, vec, const(fwd), const(inv)],
        out_specs=pair,
        out_shape=jax.ShapeDtypeStruct(pair_shape, F32),
        compiler_params=pltpu.CompilerParams(
            dimension_semantics=("parallel", "parallel"), vmem_limit_bytes=VMEM_LIMIT_BYTES),
        name="hyena_conv_short",
    )(x0c.reshape(pair_shape), hw.reshape(pair_shape), fu, den, bias.reshape(1, W), fwd, inv)
    return out.reshape(B, L, W)


GRID_W = 64
HY_COLS = 3 * HY_WIDTH
SC_COLS = 3 * 256
HG_COLS = 3 * HG_KEY + 2 * HG_HEADS * HG_DV
SSD_XBC = SSD_INNER + 2 * SSD_GROUPS * SSD_STATE
SSD_COLS = SSD_INNER + SSD_XBC + 2 * SSD_HEADS


def _sincos_1d(pos, dim):
    omega = 1.0 / (10000.0 ** (jnp.arange(dim // 2, dtype=F32) / (dim // 2)))
    ang = pos.astype(F32)[:, None] * omega[None]
    return jnp.concatenate([jnp.sin(ang), jnp.cos(ang)], -1)


def _grid_pos_embed(rows, dim):
    row = jnp.repeat(jnp.arange(rows), GRID_W)
    col = jnp.tile(jnp.arange(GRID_W), rows)
    return jnp.concatenate([_sincos_1d(row, dim // 2), _sincos_1d(col, dim // 2)], -1)


def kernel(x, c, ctx, c_ctx, w_ada, b_ada, w_in, hy_conv_w, hy_conv_b, hy_w1, hy_b1, hy_freq1,
           hy_w2, hy_b2, hy_freq2, hy_w3, hy_decay, hy_bias, sc_conv_w, hg_lb_logits, hg_norm_g,
           ssd_conv_w, ssd_conv_b, ssd_a_log, ssd_dt_bias, ssd_d, ssd_norm_g, w_gate, b_gate, w_br,
           w_o, ln1_g, ln1_b, ln2_g, ln2_b, w_router, b_router, w_e1, w_e3, w_e2):
    B, L, D = x.shape
    LC = ctx.shape[1]
    depth = w_in.shape[0]
    lat = (x + _grid_pos_embed(L // GRID_W, D).astype(x.dtype)[None]).reshape(B * L, D)
    cx = ctx.reshape(B * LC, D)

    n_vec = -(-(B + 1) // SUBLANES) * SUBLANES
    cvecs = jnp.concatenate([c, c_ctx[None], jnp.zeros((n_vec - B - 1, D), c.dtype)], axis=0)
    mods = ada_mod(cvecs, w_ada, b_ada).reshape(depth, n_vec, 6, D)
    lb_logits = hg_lb_logits.reshape(2 * depth, HG_KEY)

    in_cols = w_in.shape[2]
    widths = (HY_COLS, SC_COLS, HG_COLS, -(-SSD_COLS // LANES) * LANES)
    z_state = lambda *shape: jnp.zeros(shape, F32)

    for l in range(depth):
        ctx_out = l < depth - 1
        mod_lat, mod_ctx = mods[l, :B], mods[l, B:B + 1]
        w_in_b = jnp.pad(w_in[l], ((0, 0), (0, sum(widths) - in_cols))).astype(BF16)
        ssd_par = jnp.zeros((2 * SUBLANES, LANES), F32)
        ssd_par = ssd_par.at[:SUBLANES, :2 * SSD_HEADS].set(jnp.broadcast_to(ssd_a_log[l].reshape(1, -1), (SUBLANES, 2 * SSD_HEADS)))
        ssd_par = ssd_par.at[SUBLANES:, :2 * SSD_HEADS].set(jnp.broadcast_to(ssd_dt_bias[l].reshape(1, -1), (SUBLANES, 2 * SSD_HEADS)))
        d_skip = jnp.repeat(ssd_d[l], SSD_HEADDIM)
        filt_args = (hy_w1[l], hy_b1[l], hy_freq1[l], hy_w2[l], hy_b2[l], hy_freq2[l], hy_w3[l], hy_decay[l])

        def mixers(tokens, mod, seg_len, hg_state, ssd_state, want_out):
            n_seq = tokens.shape[0] // seg_len
            hy, sc, hg, ssd = in_proj(tokens, mod, w_in_b, widths, rows_per_mod=mod_rows(mod, tokens), tm=512)
            seq = lambda a: a.reshape(n_seq, seg_len, a.shape[1])
            hg_y, hg_f, hg_b = hgrn_scan(seq(hg), lb_logits, hg_norm_g[l], *hg_state, layer=l)
            ssd3 = seq(ssd)
            xbc = ssd_prep(ssd3, ssd_conv_w[l], ssd_conv_b[l], col0=SSD_INNER)
            ssd_y, ssd_f, ssd_b = ssd_scan(ssd3, xbc, ssd_par, d_skip, ssd_norm_g[l], *ssd_state,
                                           z_blk=0, dt_blk=(SSD_INNER + SSD_XBC) // LANES)
            branches = None
            if want_out:
                x0c, hw = hy_prep(seq(hy), hy_conv_w[l], hy_conv_b[l])
                fu, den = hy_filter(seg_len, *filt_args)
                if seg_len == FFT_L:
                    hy_y = hyena_conv(x0c, hw, hyena_spectrum(fu, den), hy_bias[l])
                else:
                    hy_y = hyena_conv_short(x0c, hw, fu, den, hy_bias[l])
                sc_y = shortconv(seq(sc), sc_conv_w[l])
                flat = lambda a: a.reshape(tokens.shape[0], a.shape[2])
                branches = [flat(hy_y), flat(sc_y), flat(hg_y), flat(ssd_y)]
            return branches, (hg_f, hg_b), (ssd_f, ssd_b)

        def mod_rows(mod, tokens):
            return tokens.shape[0] // mod.shape[0]

        def finish(tokens, mod, branches):
            rows = mod_rows(mod, tokens)
            t1 = merge_ln(tokens, mod, branches, w_gate[l].astype(BF16), b_gate[l], w_br[l].astype(BF16),
                          w_o[l].astype(BF16), ln1_g[l], ln1_b[l], rows_per_mod=rows, tm=512)
            return moe_ln(t1, mod, w_router, b_router, w_e1[l].astype(BF16), w_e3[l].astype(BF16),
                          w_e2[l].astype(BF16), ln2_g[l], ln2_b[l], rows_per_mod=rows, tm=1024)

        zero_hg = (z_state(B, HG_KEY, HG_KEY),) * 2
        zero_ssd = (z_state(B, SSD_GROUPS, LANES, LANES),) * 2
        br_ctx, hg_state, ssd_state = mixers(cx, mod_ctx, LC, zero_hg, zero_ssd, ctx_out)
        br_lat, _, _ = mixers(lat, mod_lat, L, hg_state, ssd_state, True)
        lat = finish(lat, mod_lat, br_lat)
        if ctx_out:
            cx = finish(cx, mod_ctx, br_ctx)
    return lat.reshape(B, L, D)
```

```python
import functools
import math

import jax
import jax.numpy as jnp
import numpy as np
from jax import lax
from jax.experimental import pallas as pl
from jax.experimental.pallas import tpu as pltpu

F32 = jnp.float32
BF16 = jnp.bfloat16

N_EXPERTS = 16
N_EXPERT_GROUPS = 4
GROUP_SIZE = N_EXPERTS // N_EXPERT_GROUPS
DEPTH = 2
DEEPNORM_ALPHA = (2 * DEPTH) ** 0.25
LN_EPS = 1e-5
RMS_EPS = 1e-6

VMEM_LIMIT_BYTES = 52 * 1024 * 1024


def _layernorm_rows(z, g, b):
    mu = jnp.mean(z, axis=-1, keepdims=True)
    zc = z - mu
    var = jnp.mean(zc * zc, axis=-1, keepdims=True)
    return zc * lax.rsqrt(var + LN_EPS) * g + b


def _route_t(logits_t, bias_col):
    z = logits_t - jnp.max(logits_t, axis=0, keepdims=True)
    ex = jnp.exp(z)
    scores = ex / jnp.sum(ex, axis=0, keepdims=True)
    sel = scores + bias_col
    eidx = lax.broadcasted_iota(jnp.int32, sel.shape, 0)
    neg = jnp.float32(-jnp.inf)

    def first_argmax(m):
        top = jnp.max(m, axis=0, keepdims=True)
        return top, jnp.min(jnp.where(m == top, eidx, N_EXPERTS), axis=0, keepdims=True)

    best_score = None
    best_grp = None
    for g in range(N_EXPERT_GROUPS):
        m = jnp.where((eidx >= g * GROUP_SIZE) & (eidx < (g + 1) * GROUP_SIZE), sel, neg)
        t1, i1 = first_argmax(m)
        t2 = jnp.max(jnp.where(eidx == i1, neg, m), axis=0, keepdims=True)
        s = t1 + t2
        if g == 0:
            best_score, best_grp = s, jnp.zeros_like(i1)
        else:
            better = s > best_score
            best_score = jnp.where(better, s, best_score)
            best_grp = jnp.where(better, g, best_grp)
    lo = best_grp * GROUP_SIZE
    masked = jnp.where((eidx >= lo) & (eidx < lo + GROUP_SIZE), sel, neg)
    _, ia = first_argmax(masked)
    _, ib = first_argmax(jnp.where(eidx == ia, neg, masked))
    w = jnp.where((eidx == ia) | (eidx == ib), scores, 0.0)
    return w / jnp.sum(w, axis=0, keepdims=True), best_grp


MOE_CHUNK = 64
MOE_ARMS = (1024, 512, 256, 128, 64)
MOE_STEP_EXPERTS = 2
GATE_TERMS = 3
POS_LANE = GATE_TERMS * N_EXPERTS


def _moe_ln_kernel(x_ref, mod_ref, wr_ref, br_ref, w1_ref, w3_ref, w2_ref, lng_ref, lnb_ref,
                   o_ref, pt_scr, hc_scr, gc_scr, yc_scr, seg_smem):
    step = pl.program_id(1)
    TM, R = pt_scr.shape
    D = x_ref.shape[1]
    G = N_EXPERT_GROUPS

    @pl.when(step == 0)
    def _():
        h = _modulate(x_ref[...], mod_ref, MOD_SHIFT2, MOD_SCALE2)
        h_hi = h.astype(BF16)
        h_lo = (h - h_hi.astype(F32)).astype(BF16)
        r_hi = jnp.dot(h_hi, wr_ref[...], preferred_element_type=F32)
        r_lo = jnp.dot(h_lo, wr_ref[:, :LANES], preferred_element_type=F32)
        logits_t = (r_hi[:, :LANES] + r_hi[:, LANES:] + r_lo).T[:N_EXPERTS]
        gate_t, grp = _route_t(logits_t, br_ref[:, 0:1])

        gidx = lax.broadcasted_iota(jnp.int32, (SUBLANES, TM), 0)
        tok = lax.broadcasted_iota(jnp.int32, (SUBLANES, TM), 1)
        member = jnp.where(gidx == grp, 1.0, 0.0)
        incl = member
        shift = 1
        while shift < TM:
            incl = incl + jnp.where(tok >= shift, pltpu.roll(incl, shift, 1), 0.0)
            shift *= 2
        rank = incl - member
        offset = jnp.int32(0)
        pos_row = jnp.zeros((1, TM), F32)
        for g in range(G):
            n_chunks = (jnp.sum(member[g:g + 1, :]).astype(jnp.int32) + (MOE_CHUNK - 1)) // MOE_CHUNK
            seg_smem[g] = n_chunks
            seg_smem[G + g] = offset
            pos_row = pos_row + member[g:g + 1, :] * (rank[g:g + 1, :] + offset.astype(F32))
            offset = offset + n_chunks * MOE_CHUNK

        terms, rest = [], gate_t
        for _ in range(GATE_TERMS):
            part = rest.astype(BF16).astype(F32)
            terms.append(part)
            rest = rest - part
        pad = jnp.zeros((LANES - POS_LANE - 1, TM), F32)
        side = jnp.concatenate(terms + [pos_row, pad], axis=0).T
        pos_col = side[:, POS_LANE:POS_LANE + 1].astype(jnp.int32)
        pt_scr[...] = jnp.where(lax.broadcasted_iota(jnp.int32, (TM, R), 1) == pos_col, 1.0, 0.0).astype(BF16)
        p = jnp.where(lax.broadcasted_iota(jnp.int32, (R, TM), 0) == pos_row.astype(jnp.int32), 1.0, 0.0).astype(BF16)
        gathered = jnp.dot(p, jnp.concatenate([h_hi, side.astype(BF16)], axis=1), preferred_element_type=F32)
        hc_scr[...] = gathered[:, :D].astype(BF16)
        gc_scr[...] = gathered[:, D:]
        yc_scr[...] = jnp.zeros_like(yc_scr)

    g = step // (GROUP_SIZE // MOE_STEP_EXPERTS)
    n_chunks = seg_smem[g]
    base = seg_smem[G + g]

    def expert_rows(start, size):
        rows = pl.ds(pl.multiple_of(start, MOE_CHUNK), size)
        hc = hc_scr[rows, :]
        gc = gc_scr[rows, :]
        lane = lax.broadcasted_iota(jnp.int32, gc.shape, 1)
        acc = None
        for k in range(MOE_STEP_EXPERTS):
            e = step * MOE_STEP_EXPERTS + k
            a = jnp.dot(hc, w1_ref[k], preferred_element_type=F32)
            b = jnp.dot(hc, w3_ref[k], preferred_element_type=F32)
            mid = (a * jax.nn.sigmoid(a) * b).astype(BF16)
            y = jnp.dot(mid, w2_ref[k], preferred_element_type=F32)
            gcol = jnp.sum(jnp.where((lane % N_EXPERTS == e) & (lane < POS_LANE), gc, 0.0), axis=-1, keepdims=True)
            acc = gcol * y if acc is None else acc + gcol * y
        yc_scr[rows, :] += acc

    for size in MOE_ARMS:
        if size > TM:
            continue
        taken = (n_chunks & (size // MOE_CHUNK)) != 0

        @pl.when(taken)
        def _(base=base, size=size):
            expert_rows(base, size)

        base = base + jnp.where(taken, size, 0)

    @pl.when(step == N_EXPERTS // MOE_STEP_EXPERTS - 1)
    def _():
        moe = jnp.dot(pt_scr[...], yc_scr[...].astype(BF16), preferred_element_type=F32)
        z = DEEPNORM_ALPHA * x_ref[...] + mod_ref[0, MOD_GATE2:MOD_GATE2 + 1, :] * moe
        o_ref[...] = _layernorm_rows(z, lng_ref[...], lnb_ref[...])


def moe_ln(x, mod, w_router, b_router, w1, w3, w2, ln_g, ln_b, *, rows_per_mod, tm):
    T, D = x.shape
    E, _, FF = w1.shape
    assert T % tm == 0 and rows_per_mod % tm == 0 and tm <= MOE_ARMS[0] and tm % MOE_CHUNK == 0
    tiles_per_mod = rows_per_mod // tm
    sorted_rows = tm + N_EXPERT_GROUPS * MOE_CHUNK
    wr_hi = w_router.astype(BF16)
    wr_lo = (w_router - wr_hi.astype(F32)).astype(BF16)
    lane_pad = lambda w: jnp.pad(w, ((0, 0), (0, LANES - E)))
    wr_split = jnp.concatenate([lane_pad(wr_hi), lane_pad(wr_lo)], axis=1)
    br_col = jnp.broadcast_to(b_router.reshape(E, 1), (E, LANES))
    return pl.pallas_call(
        _moe_ln_kernel,
        grid=(T // tm, E // MOE_STEP_EXPERTS),
        in_specs=[
            pl.BlockSpec((tm, D), lambda i, e: (i, 0)),
            pl.BlockSpec((1, 6, D), lambda i, e: (i // tiles_per_mod, 0, 0)),
            pl.BlockSpec((D, 2 * LANES), lambda i, e: (0, 0)),
            pl.BlockSpec((E, LANES), lambda i, e: (0, 0)),
            pl.BlockSpec((MOE_STEP_EXPERTS, D, FF), lambda i, e: (e, 0, 0)),
            pl.BlockSpec((MOE_STEP_EXPERTS, D, FF), lambda i, e: (e, 0, 0)),
            pl.BlockSpec((MOE_STEP_EXPERTS, FF, D), lambda i, e: (e, 0, 0)),
            pl.BlockSpec((1, D), lambda i, e: (0, 0)),
            pl.BlockSpec((1, D), lambda i, e: (0, 0)),
        ],
        out_specs=pl.BlockSpec((tm, D), lambda i, e: (i, 0)),
        out_shape=jax.ShapeDtypeStruct((T, D), F32),
        scratch_shapes=[
            pltpu.VMEM((tm, sorted_rows), BF16),
            pltpu.VMEM((sorted_rows, D), BF16),
            pltpu.VMEM((sorted_rows, LANES), F32),
            pltpu.VMEM((sorted_rows, D), F32),
            pltpu.SMEM((2 * N_EXPERT_GROUPS,), jnp.int32),
        ],
        compiler_params=pltpu.CompilerParams(
            dimension_semantics=("parallel", "arbitrary"), vmem_limit_bytes=VMEM_LIMIT_BYTES),
        name="moe_ln",
    )(x, mod, wr_split, br_col, w1, w3, w2, ln_g.reshape(1, D), ln_b.reshape(1, D))


def _ada_kernel(c_ref, w_ref, b_ref, o_ref):
    c = c_ref[...]
    s = c * jax.nn.sigmoid(c)
    o_ref[0] = jnp.dot(s, w_ref[0], precision=lax.Precision.HIGHEST, preferred_element_type=F32) + b_ref[0]


def ada_mod(cvecs, w_ada, b_ada, *, tn=1536):
    R, D = cvecs.shape
    depth, _, N = w_ada.shape
    assert N % tn == 0
    return pl.pallas_call(
        _ada_kernel,
        grid=(depth, N // tn),
        in_specs=[
            pl.BlockSpec((R, D), lambda l, j: (0, 0)),
            pl.BlockSpec((1, D, tn), lambda l, j: (l, 0, j)),
            pl.BlockSpec((1, 1, tn), lambda l, j: (l, 0, j)),
        ],
        out_specs=pl.BlockSpec((1, R, tn), lambda l, j: (l, 0, j)),
        out_shape=jax.ShapeDtypeStruct((depth, R, N), F32),
        compiler_params=pltpu.CompilerParams(
            dimension_semantics=("parallel", "parallel"), vmem_limit_bytes=VMEM_LIMIT_BYTES),
        name="ada_mod",
    )(cvecs, w_ada, b_ada.reshape(depth, 1, N))


MOD_SHIFT1, MOD_SCALE1, MOD_GATE1, MOD_SHIFT2, MOD_SCALE2, MOD_GATE2 = range(6)


def _modulate(x, mod_ref, shift_row, scale_row):
    return x * (1.0 + mod_ref[0, scale_row:scale_row + 1, :]) + mod_ref[0, shift_row:shift_row + 1, :]


def _in_proj_kernel(x_ref, mod_ref, w_ref, *rest, has_pos):
    x = x_ref[...] + rest[0][...] if has_pos else x_ref[...]
    out_refs = rest[1:] if has_pos else rest
    h = _modulate(x, mod_ref, MOD_SHIFT1, MOD_SCALE1).astype(BF16)
    off = 0
    for o_ref in out_refs:
        n = o_ref.shape[1]
        o_ref[...] = jnp.dot(h, w_ref[:, off:off + n], preferred_element_type=F32).astype(o_ref.dtype)
        off += n


def _pos_spec(pos, rows_per_mod, tm):
    assert pos.shape[0] == rows_per_mod
    tiles = rows_per_mod // tm
    return pl.BlockSpec((tm, pos.shape[1]), lambda i: (i % tiles, 0))


def in_proj(x, mod, w, widths, *, rows_per_mod, tm, pos=None):
    T, D = x.shape
    assert T % tm == 0 and rows_per_mod % tm == 0 and w.shape[1] == sum(widths)
    tiles_per_mod = rows_per_mod // tm
    extra = [] if pos is None else [pos]
    return pl.pallas_call(
        functools.partial(_in_proj_kernel, has_pos=pos is not None),
        grid=(T // tm,),
        in_specs=[
            pl.BlockSpec((tm, D), lambda i: (i, 0)),
            pl.BlockSpec((1, 6, D), lambda i: (i // tiles_per_mod, 0, 0)),
            pl.BlockSpec(w.shape, lambda i: (0, 0), pipeline_mode=pl.Buffered(1)),
        ] + [_pos_spec(p, rows_per_mod, tm) for p in extra],
        out_specs=[pl.BlockSpec((tm, n), lambda i: (i, 0)) for n in widths],
        out_shape=[jax.ShapeDtypeStruct((T, n), F32) for n in widths],
        compiler_params=pltpu.CompilerParams(
            dimension_semantics=("parallel",), vmem_limit_bytes=VMEM_LIMIT_BYTES),
        name="in_proj",
    )(x, mod, w, *extra)


N_BRANCH = 4


def _merge_kernel(x_ref, mod_ref, hy_ref, sc_ref, hg_ref, ssd_ref, wg_ref, bg_ref, wbr_ref, wo_ref,
                  lng_ref, lnb_ref, *rest, has_pos):
    x = x_ref[...] + rest[0][...] if has_pos else x_ref[...]
    o_ref = rest[-1]
    D = x.shape[1]
    h = _modulate(x, mod_ref, MOD_SHIFT1, MOD_SCALE1).astype(BF16)
    y = None
    for k, br_ref in enumerate((hy_ref, sc_ref, hg_ref, ssd_ref)):
        gate = jax.nn.sigmoid(
            jnp.dot(h, wg_ref[:, k * D:(k + 1) * D], preferred_element_type=F32) + bg_ref[:, k * D:(k + 1) * D])
        term = gate * jnp.dot(br_ref[...].astype(BF16), wbr_ref[k], preferred_element_type=F32)
        y = term if y is None else y + term
    y = jnp.dot(y.astype(BF16), wo_ref[...], preferred_element_type=F32)
    z = DEEPNORM_ALPHA * x + mod_ref[0, MOD_GATE1:MOD_GATE1 + 1, :] * y
    o_ref[...] = _layernorm_rows(z, lng_ref[...], lnb_ref[...])


def merge_ln(x, mod, branches, w_gate, b_gate, w_br, w_o, ln_g, ln_b, *, rows_per_mod, tm, pos=None):
    T, D = x.shape
    BW = branches[0].shape[1]
    assert T % tm == 0 and rows_per_mod % tm == 0
    tiles_per_mod = rows_per_mod // tm
    const = lambda shape: pl.BlockSpec(shape, lambda i: (0,) * len(shape), pipeline_mode=pl.Buffered(1))
    extra = [] if pos is None else [pos]
    return pl.pallas_call(
        functools.partial(_merge_kernel, has_pos=pos is not None),
        grid=(T // tm,),
        in_specs=[
            pl.BlockSpec((tm, D), lambda i: (i, 0)),
            pl.BlockSpec((1, 6, D), lambda i: (i // tiles_per_mod, 0, 0)),
        ] + [pl.BlockSpec((tm, BW), lambda i: (i, 0))] * N_BRANCH + [
            const((D, N_BRANCH * D)), const((1, N_BRANCH * D)), const((N_BRANCH, BW, D)), const((D, D)),
            const((1, D)), const((1, D)),
        ] + [_pos_spec(p, rows_per_mod, tm) for p in extra],
        out_specs=pl.BlockSpec((tm, D), lambda i: (i, 0)),
        out_shape=jax.ShapeDtypeStruct((T, D), F32),
        compiler_params=pltpu.CompilerParams(
            dimension_semantics=("parallel",), vmem_limit_bytes=VMEM_LIMIT_BYTES),
        name="merge_ln",
    )(x, mod, *branches, w_gate, b_gate.reshape(1, -1), w_br, w_o, ln_g.reshape(1, D), ln_b.reshape(1, D), *extra)


def _dwconv_rows(u, w_ref, col0=0):
    L, C = u.shape
    row = lax.broadcasted_iota(jnp.int32, u.shape, 0)
    prev = jnp.where(row == 0, 0.0, pltpu.roll(u, 1, 0))
    nxt = jnp.where(row == L - 1, 0.0, pltpu.roll(u, L - 1, 0))
    w = lambda k: w_ref[k:k + 1, col0:col0 + C]
    return prev * w(0) + u * w(1) + nxt * w(2)


def _hy_prep_kernel(x0_ref, x1_ref, v_ref, w0_ref, w1_ref, w2_ref, b0_ref, b1_ref, b2_ref, x0c_ref, hw_ref):
    x0c_ref[0] = _dwconv_rows(x0_ref[0], w0_ref) + b0_ref[...]
    x1c = _dwconv_rows(x1_ref[0], w1_ref) + b1_ref[...]
    vc = _dwconv_rows(v_ref[0], w2_ref) + b2_ref[...]
    hw_ref[0] = x1c * vc


def hy_prep(hy, conv_w, conv_b, *, cb=128):
    B, L, C3 = hy.shape
    W = C3 // 3
    nb = W // cb
    conv_b = conv_b.reshape(1, C3)
    data = lambda part: pl.BlockSpec((1, L, cb), lambda b, j: (b, 0, part * nb + j))
    wspec = lambda part: pl.BlockSpec((3, cb), lambda b, j: (0, part * nb + j))
    bspec = lambda part: pl.BlockSpec((1, cb), lambda b, j: (0, part * nb + j))
    out = pl.BlockSpec((1, L, cb), lambda b, j: (b, 0, j))
    return pl.pallas_call(
        _hy_prep_kernel,
        grid=(B, nb),
        in_specs=[data(0), data(1), data(2), wspec(0), wspec(1), wspec(2), bspec(0), bspec(1), bspec(2)],
        out_specs=[out, out],
        out_shape=[jax.ShapeDtypeStruct((B, L, W), F32), jax.ShapeDtypeStruct((B, L, W), F32)],
        compiler_params=pltpu.CompilerParams(
            dimension_semantics=("parallel", "parallel"), vmem_limit_bytes=VMEM_LIMIT_BYTES),
        name="hy_prep",
    )(hy, hy, hy, conv_w, conv_w, conv_w, conv_b, conv_b, conv_b)


def _sc_kernel(bg_ref, cg_ref, xs_ref, w_ref, o_ref):
    o_ref[0] = bg_ref[0] * _dwconv_rows(cg_ref[0] * xs_ref[0], w_ref)


def shortconv(sc, conv_w, *, cb=128):
    B, L, C3 = sc.shape
    W = C3 // 3
    nb = W // cb
    data = lambda part: pl.BlockSpec((1, L, cb), lambda b, j: (b, 0, part * nb + j))
    return pl.pallas_call(
        _sc_kernel,
        grid=(B, nb),
        in_specs=[data(0), data(1), data(2), pl.BlockSpec((3, cb), lambda b, j: (0, j))],
        out_specs=pl.BlockSpec((1, L, cb), lambda b, j: (b, 0, j)),
        out_shape=jax.ShapeDtypeStruct((B, L, W), F32),
        compiler_params=pltpu.CompilerParams(
            dimension_semantics=("parallel", "parallel"), vmem_limit_bytes=VMEM_LIMIT_BYTES),
        name="shortconv",
    )(sc, sc, sc, conv_w)


def _ssd_prep_kernel(u_ref, w_ref, b_ref, o_ref):
    y = _dwconv_rows(u_ref[0], w_ref) + b_ref[...]
    o_ref[0] = y * jax.nn.sigmoid(y)


def ssd_prep(ssd, conv_w, conv_b, *, col0, cb=128):
    B, L, _ = ssd.shape
    C = conv_w.shape[1]
    assert col0 % cb == 0 and C % cb == 0
    return pl.pallas_call(
        _ssd_prep_kernel,
        grid=(B, C // cb),
        in_specs=[
            pl.BlockSpec((1, L, cb), lambda b, j: (b, 0, col0 // cb + j)),
            pl.BlockSpec((3, cb), lambda b, j: (0, j)),
            pl.BlockSpec((1, cb), lambda b, j: (0, j)),
        ],
        out_specs=pl.BlockSpec((1, L, cb), lambda b, j: (b, 0, j)),
        out_shape=jax.ShapeDtypeStruct((B, L, C), F32),
        compiler_params=pltpu.CompilerParams(
            dimension_semantics=("parallel", "parallel"), vmem_limit_bytes=VMEM_LIMIT_BYTES),
        name="ssd_prep",
    )(ssd, conv_w, conv_b.reshape(1, C))


SSD_HEADS = 4
SSD_HEADDIM = 64
SSD_STATE = 64
SSD_GROUPS = 2
SSD_INNER = SSD_HEADS * SSD_HEADDIM
LANES = 128
NEG_INF = float("-inf")


SCAN_BATCH = 2


def _scan_chunk_index(ph, j, n):
    return j + (1 - ph) * (n - 1 - 2 * j)


def _ssd_kernel(z_ref, xbc_ref, dt_ref, par_ref, dskip_ref, ng_ref, sf0_ref, sb0_ref,
                y_ref, sf_ref, sb_ref, yb_scr, st_scr, *, n_chunks):
    ph = pl.program_id(1)
    j = pl.program_id(2)
    is_fwd = ph == 1
    c = _scan_chunk_index(ph, j, n_chunks)
    NB, R = xbc_ref.shape[0], xbc_ref.shape[1]

    @pl.when((j == 0) & is_fwd)
    def _():
        st_scr[...] = sf0_ref[...]

    @pl.when((j == 0) & jnp.logical_not(is_fwd))
    def _():
        st_scr[...] = sb0_ref[...]

    row = lax.broadcasted_iota(jnp.int32, (R, R), 0)
    col = lax.broadcasted_iota(jnp.int32, (R, R), 1)
    mask = jnp.where(is_fwd, row - col, col - row) >= 0
    tri = jnp.where(mask, 1.0, 0.0).astype(BF16)
    head_of_lane = lax.broadcasted_iota(jnp.int32, (LANES, SSD_INNER), 1) // SSD_HEADDIM
    expand = jnp.where(lax.broadcasted_iota(jnp.int32, (LANES, SSD_INNER), 0) == head_of_lane, 1.0, 0.0).astype(BF16)
    ys = [_ssd_chunk(bi, is_fwd, mask, tri, expand, xbc_ref, dt_ref, par_ref, st_scr) for bi in range(NB)]

    @pl.when(jnp.logical_not(is_fwd))
    def _():
        for bi in range(NB):
            yb_scr[bi, c] = ys[bi]

    @pl.when(is_fwd)
    def _():
        for bi in range(NB):
            z = z_ref[bi]
            xs = xbc_ref[bi, :, :SSD_INNER]
            yt = (ys[bi] + yb_scr[bi, c] + xs * dskip_ref[...]) * (z * jax.nn.sigmoid(z))
            ms = jnp.mean(yt * yt, axis=-1, keepdims=True)
            y_ref[bi] = yt * lax.rsqrt(ms + RMS_EPS) * ng_ref[...]

    @pl.when((j == n_chunks - 1) & is_fwd)
    def _():
        sf_ref[...] = st_scr[...]

    @pl.when((j == n_chunks - 1) & jnp.logical_not(is_fwd))
    def _():
        sb_ref[...] = st_scr[...]


def _ssd_chunk(bi, is_fwd, mask, tri, expand, xbc_ref, dt_ref, par_ref, st_scr):
    R = xbc_ref.shape[1]
    dsel = lambda v: jnp.where(is_fwd, v, pltpu.roll(v, LANES - SSD_HEADS, 1))
    dt_raw = dsel(dt_ref[bi]) + dsel(par_ref[8:16, :])[0:1, :]
    dt = jnp.maximum(dt_raw, 0.0) + jnp.log(1.0 + jnp.exp(-jnp.abs(dt_raw)))
    a = -jnp.exp(dsel(par_ref[0:8, :])[0:1, :]) * dt
    cs = _split_dot(tri, a, terms=3)
    cs_t = cs.T
    total = jnp.sum(a, axis=0, keepdims=True)

    spread = lambda v: jnp.dot(v.astype(BF16), expand, preferred_element_type=F32)
    dt_bc = spread(dt)
    in_decay = spread(jnp.exp(cs))
    out_decay = spread(jnp.exp(total - cs))
    e_total = jnp.exp(total)

    xbc = xbc_ref[bi]
    xdt = xbc[:, :SSD_INNER] * dt_bc
    xdt_b = xdt.astype(BF16)
    xout_b = (xdt * out_decay).astype(BF16)
    bm = xbc[:, SSD_INNER:SSD_INNER + LANES]
    cm_b = xbc[:, SSD_INNER + LANES:SSD_INNER + 2 * LANES].astype(BF16)
    lane = lax.broadcasted_iota(jnp.int32, (R, LANES), 1)
    lo_half = lane < SSD_HEADDIM
    lo_half_st = lax.broadcasted_iota(jnp.int32, (LANES, LANES), 1) < SSD_HEADDIM
    y_groups = []
    for g in range(SSD_GROUPS):
        h0, h1 = 2 * g, 2 * g + 1
        cols = slice(g * LANES, (g + 1) * LANES)
        bm_g = jnp.where((lane >= g * SSD_STATE) & (lane < (g + 1) * SSD_STATE), bm, 0.0).astype(BF16)
        G = lax.dot_general(cm_b, bm_g, (((1,), (1,)), ((), ())), preferred_element_type=F32)
        st_old = st_scr[bi, g]
        y_off = in_decay[:, cols] * jnp.dot(cm_b, st_old.astype(BF16), preferred_element_type=F32)
        y_diag = []
        for h in (h0, h1):
            decay = jnp.exp(jnp.where(mask, cs[:, h:h + 1] - cs_t[h:h + 1, :], NEG_INF))
            y_diag.append(jnp.dot((G * decay).astype(BF16), xdt_b[:, cols], preferred_element_type=F32))
        y_groups.append(jnp.where(lo_half, y_diag[0], y_diag[1]) + y_off)
        upd = lax.dot_general(bm_g, xout_b[:, cols], (((0,), (0,)), ((), ())), preferred_element_type=F32)
        st_scr[bi, g] = jnp.where(lo_half_st, e_total[:, h0:h0 + 1], e_total[:, h1:h1 + 1]) * st_old + upd
    return jnp.concatenate(y_groups, axis=1)


def ssd_scan(ssd, xbc, par, d_skip, norm_g, s_f0, s_b0, *, z_blk, dt_blk, chunk=256):
    B, L, _ = ssd.shape
    R = min(chunk, L)
    assert L % R == 0 and B % SCAN_BATCH == 0
    n = L // R
    NB = SCAN_BATCH
    cidx = lambda ph, j: _scan_chunk_index(ph, j, n)
    st_spec = pl.BlockSpec((NB, SSD_GROUPS, LANES, LANES), lambda b, ph, j: (b, 0, 0, 0))
    st_shape = jax.ShapeDtypeStruct((B, SSD_GROUPS, LANES, LANES), F32)
    return pl.pallas_call(
        functools.partial(_ssd_kernel, n_chunks=n),
        grid=(B // NB, 2, n),
        in_specs=[
            pl.BlockSpec((NB, R, SSD_INNER), lambda b, ph, j: (b, cidx(ph, j), z_blk)),
            pl.BlockSpec((NB, R, xbc.shape[2]), lambda b, ph, j: (b, cidx(ph, j), 0)),
            pl.BlockSpec((NB, R, LANES), lambda b, ph, j: (b, cidx(ph, j), dt_blk)),
            pl.BlockSpec((16, LANES), lambda b, ph, j: (0, 0)),
            pl.BlockSpec((1, SSD_INNER), lambda b, ph, j: (0, 0)),
            pl.BlockSpec((1, SSD_INNER), lambda b, ph, j: (0, 0)),
            st_spec, st_spec,
        ],
        out_specs=[pl.BlockSpec((NB, R, SSD_INNER), lambda b, ph, j: (b, ph * j, 0)), st_spec, st_spec],
        out_shape=[jax.ShapeDtypeStruct((B, L, SSD_INNER), F32), st_shape, st_shape],
        scratch_shapes=[pltpu.VMEM((NB, n, R, SSD_INNER), F32), pltpu.VMEM((NB, SSD_GROUPS, LANES, LANES), F32)],
        compiler_params=pltpu.CompilerParams(
            dimension_semantics=("parallel", "arbitrary", "arbitrary"), vmem_limit_bytes=VMEM_LIMIT_BYTES),
        name="ssd_scan",
    )(ssd, xbc, ssd, par, d_skip.reshape(1, SSD_INNER), norm_g.reshape(1, SSD_INNER), s_f0, s_b0)


HG_HEADS = 4
HG_DK = 64
HG_DV = 64
HG_KEY = HG_HEADS * HG_DK
HG_SUB = 16
HG_CHUNK = 64
HG_SUBS_PER_CHUNK = HG_CHUNK // HG_SUB


def _split_dot(mask_b, x, terms=2):
    out = None
    for _ in range(terms):
        part = x.astype(BF16)
        prod = jnp.dot(mask_b, part, preferred_element_type=F32)
        out = prod if out is None else out + prod
        x = x - part.astype(F32)
    return out


def _hgrn_kernel(q_ref, f_ref, v_ref, g_ref, lbl_ref, ng_ref, sf0_ref, sb0_ref,
                 y_ref, sf_ref, sb_ref, ob_scr, st_scr, qd_scr, ke_scr, vv_scr, tot_scr, oi_scr,
                 *, n_blocks, layer, depth):
    ph = pl.program_id(1)
    j = pl.program_id(2)
    is_fwd = ph == 1
    c = _scan_chunk_index(ph, j, n_blocks)
    NB, R = q_ref.shape[0], q_ref.shape[1]
    W = HG_KEY

    @pl.when((j == 0) & is_fwd)
    def _():
        st_scr[...] = sf0_ref[...]

    @pl.when((j == 0) & jnp.logical_not(is_fwd))
    def _():
        st_scr[...] = sb0_ref[...]

    def lower_bound(d):
        x = lbl_ref[d * depth:(d + 1) * depth, :]
        e = jnp.exp(x - jnp.max(x, axis=0, keepdims=True))
        p = e / jnp.sum(e, axis=0, keepdims=True)
        return jnp.sum(p[1:layer + 1, :], axis=0, keepdims=True) if layer > 0 else jnp.zeros((1, W), F32)

    lb = jnp.where(is_fwd, lower_bound(0), lower_bound(1))
    row = lax.broadcasted_iota(jnp.int32, (R, R), 0)
    col = lax.broadcasted_iota(jnp.int32, (R, R), 1)
    dirge = jnp.where(is_fwd, row - col, col - row) >= 0
    same_sub = (row // HG_SUB) == (col // HG_SUB)
    same_chunk = (row // HG_CHUNK) == (col // HG_CHUNK)
    one = lambda m: jnp.where(m, 1.0, 0.0).astype(BF16)
    dist = jnp.where(is_fwd, row // HG_SUB - col // HG_SUB, col // HG_SUB - row // HG_SUB)
    masks = dict(
        cum16=one(same_sub & dirge), tot16=one(same_sub), cum64=one(same_chunk & dirge), tot64=one(same_chunk),
        dist=[same_sub & dirge] + [same_chunk & (dist == d) for d in range(1, HG_SUBS_PER_CHUNK)])
    lane = lax.broadcasted_iota(jnp.int32, (R, W), 1)
    head_mask = [(lane // HG_DK) == h for h in range(HG_HEADS)]
    outs = [_hgrn_block(bi, is_fwd, lb, masks, head_mask, q_ref, f_ref, v_ref,
                        st_scr, qd_scr, ke_scr, vv_scr, tot_scr, oi_scr) for bi in range(NB)]

    @pl.when(jnp.logical_not(is_fwd))
    def _():
        for bi in range(NB):
            ob_scr[bi, c] = outs[bi]

    @pl.when(is_fwd)
    def _():
        for bi in range(NB):
            ot = outs[bi] + ob_scr[bi, c]
            sq = ot * ot
            ms = jnp.zeros((R, W), F32)
            for h in range(HG_HEADS):
                s = jnp.sum(jnp.where(head_mask[h], sq, 0.0), axis=-1, keepdims=True) * (1.0 / HG_DV)
                ms = jnp.where(head_mask[h], s, ms)
            g = g_ref[bi]
            y_ref[bi] = ot * lax.rsqrt(ms + RMS_EPS) * ng_ref[...] * (g * jax.nn.sigmoid(g))

    @pl.when((j == n_blocks - 1) & is_fwd)
    def _():
        sf_ref[...] = st_scr[...]

    @pl.when((j == n_blocks - 1) & jnp.logical_not(is_fwd))
    def _():
        sb_ref[...] = st_scr[...]


def _hgrn_block(bi, is_fwd, lb, masks, head_mask, q_ref, f_ref, v_ref,
                st_scr, qd_scr, ke_scr, vv_scr, tot_scr, oi_scr):
    R = q_ref.shape[1]
    W = HG_KEY
    f = f_ref[bi]
    logf = jnp.log(lb + (1.0 - lb) * jax.nn.sigmoid(f))
    k = (1.0 - lb) * jax.nn.sigmoid(-f)
    q = q_ref[bi] * (HG_DK ** -0.5)
    v_b = v_ref[bi].astype(BF16)
    b16 = _split_dot(masks["cum16"], logf)
    tot16 = _split_dot(masks["tot16"], logf)
    b64 = _split_dot(masks["cum64"], logf)
    tot64 = _split_dot(masks["tot64"], logf)

    shifted = lambda s: jnp.where(is_fwd, pltpu.roll(tot16, R - s, 0), pltpu.roll(tot16, s, 0))
    nx1 = shifted(HG_SUB)
    nx2 = nx1 + shifted(2 * HG_SUB)
    k_end16 = k * jnp.exp(tot16 - b16)
    keys = [(k * jnp.exp(-b16)).astype(BF16), k_end16.astype(BF16),
            (k_end16 * jnp.exp(nx1)).astype(BF16), (k_end16 * jnp.exp(nx2)).astype(BF16)]
    q16 = q * jnp.exp(b16)
    qm =jnp.concatenate([jnp.where(m, q16, 0.0) for m in head_mask], axis=0).astype(BF16)
    att = [jnp.zeros((R, R), F32)] * HG_HEADS
    for d in range(HG_SUBS_PER_CHUNK):
        sc = lax.dot_general(qm, keys[d], (((1,), (1,)), ((), ())), preferred_element_type=F32)
        att = [jnp.where(masks["dist"][d], sc[h * R:(h + 1) * R], att[h]) for h in range(HG_HEADS)]
    o_all = jnp.dot(jnp.concatenate(att, axis=0).astype(BF16), v_b, preferred_element_type=F32)
    o = jnp.zeros((R, W), F32)
    for h in range(HG_HEADS):
        o = jnp.where(head_mask[h], o_all[h * R:(h + 1) * R], o)

    qd_scr[bi] = (q * jnp.exp(b64)).astype(BF16)
    ke_scr[bi] = (k * jnp.exp(tot64 - b64)).astype(BF16)
    vv_scr[bi] = v_b
    tot_scr[bi] = tot64
    srow = lax.broadcasted_iota(jnp.int32, (W, W), 0)
    scol = lax.broadcasted_iota(jnp.int32, (W, W), 1)
    bd_mask = (srow // HG_DV) == (scol // HG_DK)
    st = st_scr[bi]
    for i in range(R // HG_CHUNK):
        cc = jnp.where(is_fwd, i, R // HG_CHUNK - 1 - i)
        start = pl.multiple_of(cc * HG_CHUNK, HG_CHUNK)
        rows = pl.ds(start, HG_CHUNK)
        oi_scr[bi, rows, :] = lax.dot_general(qd_scr[bi, rows, :], st.astype(BF16), (((1,), (1,)), ((), ())),
                                              preferred_element_type=F32)
        upd = lax.dot_general(vv_scr[bi, rows, :], ke_scr[bi, rows, :], (((0,), (0,)), ((), ())),
                              preferred_element_type=F32)
        decay = jnp.exp(tot_scr[bi, pl.ds(start, 1), :])
        st = decay * st + jnp.where(bd_mask, upd, 0.0)
    st_scr[bi] = st
    return o + oi_scr[bi]


def hgrn_scan(hg, lb_logits, norm_g, s_f0, s_b0, *, layer, block=256):
    B, L, _ = hg.shape
    W = HG_KEY
    R = min(block, L)
    assert L % R == 0 and R % HG_CHUNK == 0 and B % SCAN_BATCH == 0
    n = L // R
    NB = SCAN_BATCH
    depth = lb_logits.shape[0] // 2
    cidx = lambda ph, j: _scan_chunk_index(ph, j, n)
    col = lambda blk: pl.BlockSpec((NB, R, W), lambda b, ph, j: (b, cidx(ph, j), blk))
    st_spec = pl.BlockSpec((NB, W, W), lambda b, ph, j: (b, 0, 0))
    st_shape = jax.ShapeDtypeStruct((B, W, W), F32)
    return pl.pallas_call(
        functools.partial(_hgrn_kernel, n_blocks=n, layer=layer, depth=depth),
        grid=(B // NB, 2, n),
        in_specs=[
            col(0),
            pl.BlockSpec((NB, R, W), lambda b, ph, j: (b, cidx(ph, j), 2 - ph)),
            col(3), col(4),
            pl.BlockSpec(lb_logits.shape, lambda b, ph, j: (0, 0)),
            pl.BlockSpec((1, W), lambda b, ph, j: (0, 0)),
            st_spec, st_spec,
        ],
        out_specs=[pl.BlockSpec((NB, R, W), lambda b, ph, j: (b, ph * j, 0)), st_spec, st_spec],
        out_shape=[jax.ShapeDtypeStruct((B, L, W), F32), st_shape, st_shape],
        scratch_shapes=[
            pltpu.VMEM((NB, n, R, W), F32), pltpu.VMEM((NB, W, W), F32),
            pltpu.VMEM((NB, R, W), BF16), pltpu.VMEM((NB, R, W), BF16), pltpu.VMEM((NB, R, W), BF16),
            pltpu.VMEM((NB, R, W), F32), pltpu.VMEM((NB, R, W), F32),
        ],
        compiler_params=pltpu.CompilerParams(
            dimension_semantics=("parallel", "arbitrary", "arbitrary"), vmem_limit_bytes=VMEM_LIMIT_BYTES),
        name="hgrn_scan",
    )(hg, hg, hg, hg, lb_logits, jnp.tile(norm_g, HG_HEADS).reshape(1, W), s_f0, s_b0)


HY_WIDTH = 256
HY_BANDS = 16
HY_HIDDEN = 64
SUBLANES = 8


def _dot_hi(a, b):
    return jnp.dot(a, b, precision=lax.Precision.HIGHEST, preferred_element_type=F32)


def _hy_filter_kernel(bands_ref, w1t_ref, w1c_ref, w1s_ref, b1_ref, fr1_ref, w2_ref, b2_ref, fr2_ref, w3_ref,
                      decay_ref, f_ref, den_ref, *, L):
    i = pl.program_id(0)
    TR = f_ref.shape[1]
    m = i * TR + lax.broadcasted_iota(jnp.int32, (TR, LANES), 0)
    total = None
    for half in range(2):
        idx = m if half == 0 else jnp.where(m == 0, 0, L - m)
        idx_f = idx.astype(F32)
        t = idx_f * (1.0 / (L - 1))
        ang = ((2.0 * math.pi / L) * idx_f) * bands_ref[...]
        pre = (t[:, :HY_HIDDEN] * w1t_ref[...] + _dot_hi(jnp.cos(ang), w1c_ref[...])
               - _dot_hi(jnp.sin(ang), w1s_ref[...]) + b1_ref[...])
        h = jnp.sin(fr1_ref[...] * pre)
        h = jnp.sin(fr2_ref[...] * (_dot_hi(h, w2_ref[...]) + b2_ref[...]))
        h = _dot_hi(h, w3_ref[:, half * HY_WIDTH:(half + 1) * HY_WIDTH])
        tw = jnp.concatenate([t, t], axis=1)
        filt = h * jnp.exp(-tw * jnp.abs(decay_ref[...]))
        if half == 1:
            filt = jnp.where(jnp.concatenate([m, m], axis=1) == 0, 0.0, filt)
        f_ref[half] = filt
        s = jnp.sum(jnp.abs(filt), axis=0, keepdims=True)
        total = s if total is None else total + s

    @pl.when(i == 0)
    def _():
        den_ref[...] = jnp.zeros_like(den_ref)

    den_ref[...] += total


def hy_filter(L, w1, b1, fr1, w2, b2, fr2, w3, decay, *, tr=512):
    tr = min(tr, L)
    bands = np.zeros((1, LANES), np.float32)
    bands[0, :HY_BANDS] = np.linspace(1e-4, HY_BANDS - 1, HY_BANDS, dtype=np.float32)
    pad_rows = lambda w: jnp.pad(w, ((0, LANES - HY_BANDS), (0, 0)))
    row = lambda v: v.reshape(1, -1)
    args = (jnp.asarray(bands), w1[0:1], pad_rows(w1[1:1 + HY_BANDS]), pad_rows(w1[1 + HY_BANDS:]), row(b1), row(fr1),
            w2, row(b2), row(fr2), w3, row(decay))
    return pl.pallas_call(
        functools.partial(_hy_filter_kernel, L=L),
        grid=(L // tr,),
        in_specs=[pl.BlockSpec(a.shape, lambda i: (0, 0)) for a in args],
        out_specs=[pl.BlockSpec((2, tr, HY_WIDTH), lambda i: (0, i, 0)), pl.BlockSpec((1, HY_WIDTH), lambda i: (0, 0))],
        out_shape=[jax.ShapeDtypeStruct((2, L, HY_WIDTH), F32), jax.ShapeDtypeStruct((1, HY_WIDTH), F32)],
        compiler_params=pltpu.CompilerParams(
            dimension_semantics=("arbitrary",), vmem_limit_bytes=VMEM_LIMIT_BYTES),
        name="hy_filter",
    )(*args)


FFT_N1 = 64
FFT_N2 = 128
FFT_N = FFT_N1 * FFT_N2
FFT_L = FFT_N // 2
FFT_N1_NZ = FFT_N1 // 2
FFT_UNROLL_N1_STAGE = 2
FFT_UNROLL_N2_STAGE = 8


@functools.lru_cache(maxsize=None)
def _fft_constants():
    eye = np.eye(SUBLANES)
    k1 = np.arange(FFT_N1)[:, None]
    n1 = np.arange(FFT_N1_NZ)[None, :]
    ang1 = 2.0 * np.pi * ((k1 * n1) % FFT_N1) / FFT_N1
    kron_fwd = np.concatenate([np.kron(np.cos(ang1), eye), np.kron(np.sin(ang1), eye)], axis=0)
    kron_inv = np.concatenate([np.kron(np.cos(ang1).T, eye), np.kron(np.sin(ang1).T, eye)], axis=0) / FFT_N
    k = np.arange(FFT_N1)[:, None, None] + FFT_N1 * np.arange(FFT_N2)[None, :, None]
    n2 = np.arange(FFT_N2)[None, None, :]
    ang2 = 2.0 * np.pi * ((k * n2) % FFT_N) / FFT_N
    g_fwd = np.concatenate([np.cos(ang2), np.sin(ang2)], axis=1)
    g_inv = np.concatenate([np.cos(ang2).transpose(0, 2, 1), np.sin(ang2).transpose(0, 2, 1)], axis=1)
    to_b = lambda a: jnp.asarray(a, dtype=F32).astype(BF16)
    return to_b(kron_fwd), to_b(g_fwd), to_b(g_inv), to_b(kron_inv)


def _cplx_fwd(r, half_rows, cb):
    cr, sr = r[:half_rows], r[half_rows:]
    return cr[:, :cb] + sr[:, cb:], cr[:, cb:] - sr[:, :cb]


def _cplx_inv(r, half_rows, cb):
    cr, sr = r[:half_rows], r[half_rows:]
    return cr[:, :cb] - sr[:, cb:], cr[:, cb:] + sr[:, :cb]


def _fft_stage1(load_group, kron_ref, a_scr, cb):
    def body(jt, carry):
        rows = pl.ds(pl.multiple_of(jt * SUBLANES, SUBLANES), SUBLANES)
        d = load_group(rows).reshape(FFT_N1_NZ * SUBLANES, 2 * cb).astype(BF16)
        r = jnp.dot(kron_ref[...], d, preferred_element_type=F32)
        a_re, a_im = _cplx_fwd(r, FFT_N1 * SUBLANES, cb)
        a_scr[:, rows, :] = jnp.concatenate([a_re, a_im], axis=1).reshape(FFT_N1, SUBLANES, 2 * cb)
        return carry
    lax.fori_loop(0, FFT_N2 // SUBLANES, body, 0, unroll=FFT_UNROLL_N1_STAGE)


def _fft_fwd_data_kernel(z_ref, h_ref, kron_ref, g_ref, p_ref, a_scr):
    cb = z_ref.shape[4]
    load = lambda rows: jnp.concatenate([z_ref[0, 0, :, rows, :], z_ref[0, 1, :, rows, :]], axis=-1)
    _fft_stage1(load, kron_ref, a_scr, cb)

    def body(k1, carry):
        r = jnp.dot(g_ref[k1], a_scr[k1].astype(BF16), preferred_element_type=F32)
        x_re, x_im = _cplx_fwd(r, FFT_N2, cb)
        h = h_ref[0, k1]
        h_re, h_im = h[:, :cb], h[:, cb:]
        p_ref[0, 0, k1] = jnp.concatenate([x_re * h_re - x_im * h_im, x_re * h_im + x_im * h_re], axis=1).astype(BF16)
        return carry
    lax.fori_loop(0, FFT_N1, body, 0, unroll=FFT_UNROLL_N2_STAGE)


def _fft_fwd_filter_kernel(f_ref, den_ref, kron_ref, g_ref, h_ref, a_scr):
    half = pl.program_id(1)
    cb = f_ref.shape[3]
    inv_den = 1.0 / den_ref[...]

    def load(rows):
        re = f_ref[0, :, rows, :] * inv_den
        return jnp.concatenate([re, jnp.zeros_like(re)], axis=-1)
    _fft_stage1(load, kron_ref, a_scr, cb)

    def body(k1, carry):
        r = jnp.dot(g_ref[k1], a_scr[k1].astype(BF16), preferred_element_type=F32)
        x_re, x_im = _cplx_fwd(r, FFT_N2, cb)
        x = jnp.concatenate([x_re, x_im], axis=1)

        @pl.when(half == 0)
        def _():
            h_ref[0, k1] = x

        @pl.when(half == 1)
        def _():
            h_ref[0, k1] += jnp.where(k1 % 2 == 0, 1.0, -1.0) * x
        return carry
    lax.fori_loop(0, FFT_N1, body, 0, unroll=FFT_UNROLL_N2_STAGE)


def _fft_inv_kernel(p_ref, x0_ref, w_ref, bias_ref, gi_ref, kron_ref, o_ref, b_scr):
    cb = o_ref.shape[4]

    def body3(k1, carry):
        r = jnp.dot(gi_ref[k1], p_ref[0, 0, k1], preferred_element_type=F32)
        b_re, b_im = _cplx_inv(r, FFT_N2, cb)
        b_scr[k1] = jnp.concatenate([b_re, b_im], axis=1)
        return carry
    lax.fori_loop(0, FFT_N1, body3, 0, unroll=FFT_UNROLL_N2_STAGE)

    bias = bias_ref[...].reshape(1, 1, cb)

    def body4(jt, carry):
        rows = pl.ds(pl.multiple_of(jt * SUBLANES, SUBLANES), SUBLANES)
        d = b_scr[:, rows, :].reshape(FFT_N1 * SUBLANES, 2 * cb).astype(BF16)
        r = jnp.dot(kron_ref[...], d, preferred_element_type=F32)
        y_re, y_im = _cplx_inv(r, FFT_N1_NZ * SUBLANES, cb)
        for which, y in enumerate((y_re, y_im)):
            y3 = y.reshape(FFT_N1_NZ, SUBLANES, cb)
            o_ref[0, which, :, rows, :] = x0_ref[0, which, :, rows, :] * (y3 + w_ref[0, which, :, rows, :] * bias)
        return carry
    lax.fori_loop(0, FFT_N2 // SUBLANES, body4, 0, unroll=FFT_UNROLL_N1_STAGE)


def hyena_spectrum(fu, den, *, cb=128):
    kron_fwd, g_fwd, _, _ = _fft_constants()
    W = fu.shape[2]
    ncb = W // cb
    f4 = fu.reshape(2, FFT_N1_NZ, FFT_N2, W)
    const = lambda a: pl.BlockSpec(a.shape, lambda c, h: (0,) * a.ndim, pipeline_mode=pl.Buffered(1))
    return pl.pallas_call(
        _fft_fwd_filter_kernel,
        grid=(ncb, 2),
        in_specs=[
            pl.BlockSpec((1, FFT_N1_NZ, FFT_N2, cb), lambda c, h: (h, 0, 0, c)),
            pl.BlockSpec((1, cb), lambda c, h: (0, c)),
            const(kron_fwd), const(g_fwd),
        ],
        out_specs=pl.BlockSpec((1, FFT_N1, FFT_N2, 2 * cb), lambda c, h: (c, 0, 0, 0)),
        out_shape=jax.ShapeDtypeStruct((ncb, FFT_N1, FFT_N2, 2 * cb), F32),
        scratch_shapes=[pltpu.VMEM((FFT_N1, FFT_N2, 2 * cb), F32)],
        compiler_params=pltpu.CompilerParams(
            dimension_semantics=("parallel", "arbitrary"), vmem_limit_bytes=VMEM_LIMIT_BYTES),
        name="hyena_spectrum",
    )(f4, den, kron_fwd, g_fwd)


def hyena_conv(x0c, hw, spec, bias, *, cb=128):
    kron_fwd, g_fwd, g_inv, kron_inv = _fft_constants()
    B, L, W = hw.shape
    assert L == FFT_L and B % 2 == 0
    ncb = W // cb
    pair_shape = (B // 2, 2, FFT_N1_NZ, FFT_N2, W)
    x5 = x0c.reshape(pair_shape)
    w5 = hw.reshape(pair_shape)
    const = lambda a: pl.BlockSpec(a.shape, lambda c, p: (0,) * a.ndim, pipeline_mode=pl.Buffered(1))
    pair = pl.BlockSpec((1, 2, FFT_N1_NZ, FFT_N2, cb), lambda c, p: (p, 0, 0, 0, c))
    pspec = pl.BlockSpec((1, 1, FFT_N1, FFT_N2, 2 * cb), lambda c, p: (c, p, 0, 0, 0))
    prod = pl.pallas_call(
        _fft_fwd_data_kernel,
        grid=(ncb, B // 2),
        in_specs=[pair, pl.BlockSpec((1, FFT_N1, FFT_N2, 2 * cb), lambda c, p: (c, 0, 0, 0)),
                  const(kron_fwd), const(g_fwd)],
        out_specs=pspec,
        out_shape=jax.ShapeDtypeStruct((ncb, B // 2, FFT_N1, FFT_N2, 2 * cb), BF16),
        scratch_shapes=[pltpu.VMEM((FFT_N1, FFT_N2, 2 * cb), F32)],
        compiler_params=pltpu.CompilerParams(
            dimension_semantics=("parallel", "parallel"), vmem_limit_bytes=VMEM_LIMIT_BYTES),
        name="hyena_fft_fwd",
    )(w5, spec, kron_fwd, g_fwd)
    out = pl.pallas_call(
        _fft_inv_kernel,
        grid=(ncb, B // 2),
        in_specs=[pspec, pair, pair, pl.BlockSpec((1, cb), lambda c, p: (0, c)), const(g_inv), const(kron_inv)],
        out_specs=pair,
        out_shape=jax.ShapeDtypeStruct(pair_shape, F32),
        scratch_shapes=[pltpu.VMEM((FFT_N1, FFT_N2, 2 * cb), F32)],
        compiler_params=pltpu.CompilerParams(
            dimension_semantics=("parallel", "parallel"), vmem_limit_bytes=VMEM_LIMIT_BYTES),
        name="hyena_fft_inv",
    )(prod, x5, w5, bias.reshape(1, W), g_inv, kron_inv)
    return out.reshape(B, L, W)


@functools.lru_cache(maxsize=None)
def _dense_dft_constants(L):
    n_full = 2 * L
    k = np.arange(n_full)[:, None]
    n = np.arange(n_full)[None, :]
    ang = 2.0 * np.pi * ((k * n) % n_full) / n_full
    fwd = np.concatenate([np.cos(ang), np.sin(ang)], axis=0)
    inv = np.concatenate([np.cos(ang[:L]), np.sin(ang[:L])], axis=0) / n_full
    return jnp.asarray(fwd, dtype=F32), jnp.asarray(inv, dtype=F32)


def _hyena_short_kernel(x0_ref, w_ref, f_ref, den_ref, bias_ref, ff_ref, fi_ref, o_ref):
    L, cb = o_ref.shape[2], o_ref.shape[3]
    filt = jnp.concatenate([f_ref[0], f_ref[1]], axis=0) / den_ref[...]
    r = _dot_hi(ff_ref[...], filt)
    h_re, h_im = r[:2 * L], -r[2 * L:]
    z = jnp.concatenate([w_ref[0, 0], w_ref[0, 1]], axis=1)
    x_re, x_im = _cplx_fwd(_dot_hi(ff_ref[:, :L], z), 2 * L, cb)
    p = jnp.concatenate([x_re * h_re - x_im * h_im, x_re * h_im + x_im * h_re], axis=1)
    y_re, y_im = _cplx_inv(_dot_hi(fi_ref[...], p), L, cb)
    for which, y in enumerate((y_re, y_im)):
        o_ref[0, which] = x0_ref[0, which] * (y + w_ref[0, which] * bias_ref[...])


def hyena_conv_short(x0c, hw, fu, den, bias, *, cb=128):
    B, L, W = hw.shape
    fwd, inv = _dense_dft_constants(L)
    pair_shape = (B // 2, 2, L, W)
    pair = pl.BlockSpec((1, 2, L, cb), lambda p, c: (p, 0, 0, c))
    vec = pl.BlockSpec((1, cb), lambda p, c: (0, c))
    const = lambda a: pl.BlockSpec(a.shape, lambda p, c: (0, 0))
    out = pl.pallas_call(
        _hyena_short_kernel,
        grid=(B // 2, W // cb),
        in_specs=[pair, pair, pl.BlockSpec((2, L, cb), lambda p, c: (0, 0, c)), vec, vec, const(fwd), const(inv)],
        out_specs=pair,
        out_shape=jax.ShapeDtypeStruct(pair_shape, F32),
        compiler_params=pltpu.CompilerParams(
            dimension_semantics=("parallel", "parallel"), vmem_limit_bytes=VMEM_LIMIT_BYTES),
        name="hyena_conv_short",
    )(x0c.reshape(pair_shape), hw.reshape(pair_shape), fu, den, bias.reshape(1, W), fwd, inv)
    return out.reshape(B, L, W)


GRID_W = 64
HY_COLS = 3 * HY_WIDTH
SC_COLS = 3 * 256
HG_COLS = 3 * HG_KEY + 2 * HG_HEADS * HG_DV
SSD_XBC = SSD_INNER + 2 * SSD_GROUPS * SSD_STATE
SSD_COLS = SSD_INNER + SSD_XBC + 2 * SSD_HEADS


def _sincos_1d(pos, dim):
    omega = 1.0 / (10000.0 ** (jnp.arange(dim // 2, dtype=F32) / (dim // 2)))
    ang = pos.astype(F32)[:, None] * omega[None]
    return jnp.concatenate([jnp.sin(ang), jnp.cos(ang)], -1)


def _grid_pos_embed(rows, dim):
    row = jnp.repeat(jnp.arange(rows), GRID_W)
    col = jnp.tile(jnp.arange(GRID_W), rows)
    return jnp.concatenate([_sincos_1d(row, dim // 2), _sincos_1d(col, dim // 2)], -1)


def kernel(x, c, ctx, c_ctx, w_ada, b_ada, w_in, hy_conv_w, hy_conv_b, hy_w1, hy_b1, hy_freq1,
           hy_w2, hy_b2, hy_freq2, hy_w3, hy_decay, hy_bias, sc_conv_w, hg_lb_logits, hg_norm_g,
           ssd_conv_w, ssd_conv_b, ssd_a_log, ssd_dt_bias, ssd_d, ssd_norm_g, w_gate, b_gate, w_br,
           w_o, ln1_g, ln1_b, ln2_g, ln2_b, w_router, b_router, w_e1, w_e3, w_e2):
    B, L, D = x.shape
    LC = ctx.shape[1]
    depth = w_in.shape[0]
    pos_table = _grid_pos_embed(L // GRID_W, D).astype(x.dtype)
    lat = x.reshape(B * L, D)
    cx = ctx.reshape(B * LC, D)

    n_vec = -(-(B + 1) // SUBLANES) * SUBLANES
    cvecs = jnp.concatenate([c, c_ctx[None], jnp.zeros((n_vec - B - 1, D), c.dtype)], axis=0)
    mods = ada_mod(cvecs, w_ada, b_ada).reshape(depth, n_vec, 6, D)
    lb_logits = hg_lb_logits.reshape(2 * depth, HG_KEY)

    in_cols = w_in.shape[2]
    widths = (HY_COLS, SC_COLS, HG_COLS, -(-SSD_COLS // LANES) * LANES)
    z_state = lambda *shape: jnp.zeros(shape, F32)

    for l in range(depth):
        ctx_out = l < depth - 1
        mod_lat, mod_ctx = mods[l, :B], mods[l, B:B + 1]
        w_in_b = jnp.pad(w_in[l], ((0, 0), (0, sum(widths) - in_cols))).astype(BF16)
        ssd_par = jnp.zeros((2 * SUBLANES, LANES), F32)
        ssd_par = ssd_par.at[:SUBLANES, :2 * SSD_HEADS].set(jnp.broadcast_to(ssd_a_log[l].reshape(1, -1), (SUBLANES, 2 * SSD_HEADS)))
        ssd_par = ssd_par.at[SUBLANES:, :2 * SSD_HEADS].set(jnp.broadcast_to(ssd_dt_bias[l].reshape(1, -1), (SUBLANES, 2 * SSD_HEADS)))
        d_skip = jnp.repeat(ssd_d[l], SSD_HEADDIM)
        filt_args = (hy_w1[l], hy_b1[l], hy_freq1[l], hy_w2[l], hy_b2[l], hy_freq2[l], hy_w3[l], hy_decay[l])

        def mixers(tokens, mod, seg_len, hg_state, ssd_state, want_out, pos=None):
            n_seq = tokens.shape[0] // seg_len
            hy, sc, hg, ssd = in_proj(tokens, mod, w_in_b, widths, rows_per_mod=mod_rows(mod, tokens), tm=512, pos=pos)
            seq = lambda a: a.reshape(n_seq, seg_len, a.shape[1])
            hg_y, hg_f, hg_b = hgrn_scan(seq(hg), lb_logits, hg_norm_g[l], *hg_state, layer=l)
            ssd3 = seq(ssd)
            xbc = ssd_prep(ssd3, ssd_conv_w[l], ssd_conv_b[l], col0=SSD_INNER)
            ssd_y, ssd_f, ssd_b = ssd_scan(ssd3, xbc, ssd_par, d_skip, ssd_norm_g[l], *ssd_state,
                                           z_blk=0, dt_blk=(SSD_INNER + SSD_XBC) // LANES)
            branches = None
            if want_out:
                x0c, hw = hy_prep(seq(hy), hy_conv_w[l], hy_conv_b[l])
                fu, den = hy_filter(seg_len, *filt_args)
                if seg_len == FFT_L:
                    hy_y = hyena_conv(x0c, hw, hyena_spectrum(fu, den), hy_bias[l])
                else:
                    hy_y = hyena_conv_short(x0c, hw, fu, den, hy_bias[l])
                sc_y = shortconv(seq(sc), sc_conv_w[l])
                flat = lambda a: a.reshape(tokens.shape[0], a.shape[2])
                branches = [flat(hy_y), flat(sc_y), flat(hg_y), flat(ssd_y)]
            return branches, (hg_f, hg_b), (ssd_f, ssd_b)

        def mod_rows(mod, tokens):
            return tokens.shape[0] // mod.shape[0]

        def finish(tokens, mod, branches, pos=None):
            rows = mod_rows(mod, tokens)
            t1 = merge_ln(tokens, mod, branches, w_gate[l].astype(BF16), b_gate[l], w_br[l].astype(BF16),
                          w_o[l].astype(BF16), ln1_g[l], ln1_b[l], rows_per_mod=rows, tm=512, pos=pos)
            return moe_ln(t1, mod, w_router, b_router, w_e1[l].astype(BF16), w_e3[l].astype(BF16),
                          w_e2[l].astype(BF16), ln2_g[l], ln2_b[l], rows_per_mod=rows, tm=1024)

        zero_hg = (z_state(B, HG_KEY, HG_KEY),) * 2
        zero_ssd = (z_state(B, SSD_GROUPS, LANES, LANES),) * 2
        br_ctx, hg_state, ssd_state = mixers(cx, mod_ctx, LC, zero_hg, zero_ssd, ctx_out)
        pos = pos_table if l == 0 else None
        br_lat, _, _ = mixers(lat, mod_lat, L, hg_state, ssd_state, True, pos=pos)
        lat = finish(lat, mod_lat, br_lat, pos=pos)
        if ctx_out:
            cx = finish(cx, mod_ctx, br_ctx)
    return lat.reshape(B, L, D)
```

```python
import functools
import math

import jax
import jax.numpy as jnp
import numpy as np
from jax import lax
from jax.experimental import pallas as pl
from jax.experimental.pallas import tpu as pltpu

F32 = jnp.float32
BF16 = jnp.bfloat16

N_EXPERTS = 16
N_EXPERT_GROUPS = 4
GROUP_SIZE = N_EXPERTS // N_EXPERT_GROUPS
DEPTH = 2
DEEPNORM_ALPHA = (2 * DEPTH) ** 0.25
LN_EPS = 1e-5
RMS_EPS = 1e-6

VMEM_LIMIT_BYTES = 52 * 1024 * 1024


def _layernorm_rows(z, g, b):
    mu = jnp.mean(z, axis=-1, keepdims=True)
    zc = z - mu
    var = jnp.mean(zc * zc, axis=-1, keepdims=True)
    return zc * lax.rsqrt(var + LN_EPS) * g + b


def _route_t(logits_t, bias_col):
    z = logits_t - jnp.max(logits_t, axis=0, keepdims=True)
    ex = jnp.exp(z)
    scores = ex / jnp.sum(ex, axis=0, keepdims=True)
    sel = scores + bias_col
    eidx = lax.broadcasted_iota(jnp.int32, sel.shape, 0)
    neg = jnp.float32(-jnp.inf)

    def first_argmax(m):
        top = jnp.max(m, axis=0, keepdims=True)
        return top, jnp.min(jnp.where(m == top, eidx, N_EXPERTS), axis=0, keepdims=True)

    best_score = None
    best_grp = None
    for g in range(N_EXPERT_GROUPS):
        m = jnp.where((eidx >= g * GROUP_SIZE) & (eidx < (g + 1) * GROUP_SIZE), sel, neg)
        t1, i1 = first_argmax(m)
        t2 = jnp.max(jnp.where(eidx == i1, neg, m), axis=0, keepdims=True)
        s = t1 + t2
        if g == 0:
            best_score, best_grp = s, jnp.zeros_like(i1)
        else:
            better = s > best_score
            best_score = jnp.where(better, s, best_score)
            best_grp = jnp.where(better, g, best_grp)
    lo = best_grp * GROUP_SIZE
    masked = jnp.where((eidx >= lo) & (eidx < lo + GROUP_SIZE), sel, neg)
    _, ia = first_argmax(masked)
    _, ib = first_argmax(jnp.where(eidx == ia, neg, masked))
    w = jnp.where((eidx == ia) | (eidx == ib), scores, 0.0)
    return w / jnp.sum(w, axis=0, keepdims=True), best_grp


MOE_CHUNK = 64
MOE_ARMS = (1024, 512, 256, 128, 64)
MOE_STEP_EXPERTS = 2
GATE_TERMS = 3
POS_LANE = GATE_TERMS * N_EXPERTS


def _moe_ln_kernel(x_ref, mod_ref, wr_ref, br_ref, w1_ref, w3_ref, w2_ref, lng_ref, lnb_ref,
                   o_ref, pt_scr, hc_scr, gc_scr, yc_scr, seg_smem):
    step = pl.program_id(1)
    TM, R = pt_scr.shape
    D = x_ref.shape[1]
    G = N_EXPERT_GROUPS

    @pl.when(step == 0)
    def _():
        h = _modulate(x_ref[...], mod_ref, MOD_SHIFT2, MOD_SCALE2)
        h_hi = h.astype(BF16)
        h_lo = (h - h_hi.astype(F32)).astype(BF16)
        r_hi = jnp.dot(h_hi, wr_ref[...], preferred_element_type=F32)
        r_lo = jnp.dot(h_lo, wr_ref[:, :LANES], preferred_element_type=F32)
        logits_t = (r_hi[:, :LANES] + r_hi[:, LANES:] + r_lo).T[:N_EXPERTS]
        gate_t, grp = _route_t(logits_t, br_ref[:, 0:1])

        gidx = lax.broadcasted_iota(jnp.int32, (SUBLANES, TM), 0)
        tok = lax.broadcasted_iota(jnp.int32, (SUBLANES, TM), 1)
        member = jnp.where(gidx == grp, 1.0, 0.0)
        incl = member
        shift = 1
        while shift < TM:
            incl = incl + jnp.where(tok >= shift, pltpu.roll(incl, shift, 1), 0.0)
            shift *= 2
        rank = incl - member
        offset = jnp.int32(0)
        pos_row = jnp.zeros((1, TM), F32)
        for g in range(G):
            n_chunks = (jnp.sum(member[g:g + 1, :]).astype(jnp.int32) + (MOE_CHUNK - 1)) // MOE_CHUNK
            seg_smem[g] = n_chunks
            seg_smem[G + g] = offset
            pos_row = pos_row + member[g:g + 1, :] * (rank[g:g + 1, :] + offset.astype(F32))
            offset = offset + n_chunks * MOE_CHUNK

        terms, rest = [], gate_t
        for _ in range(GATE_TERMS):
            part = rest.astype(BF16).astype(F32)
            terms.append(part)
            rest = rest - part
        pad = jnp.zeros((LANES - POS_LANE - 1, TM), F32)
        side = jnp.concatenate(terms + [pos_row, pad], axis=0).T
        pos_col = side[:, POS_LANE:POS_LANE + 1].astype(jnp.int32)
        pt_scr[...] = jnp.where(lax.broadcasted_iota(jnp.int32, (TM, R), 1) == pos_col, 1.0, 0.0).astype(BF16)
        p = jnp.where(lax.broadcasted_iota(jnp.int32, (R, TM), 0) == pos_row.astype(jnp.int32), 1.0, 0.0).astype(BF16)
        gathered = jnp.dot(p, jnp.concatenate([h_hi, side.astype(BF16)], axis=1), preferred_element_type=F32)
        hc_scr[...] = gathered[:, :D].astype(BF16)
        gc_scr[...] = gathered[:, D:]
        yc_scr[...] = jnp.zeros_like(yc_scr)

    g = step // (GROUP_SIZE // MOE_STEP_EXPERTS)
    n_chunks = seg_smem[g]
    base = seg_smem[G + g]

    def expert_rows(start, size):
        rows = pl.ds(pl.multiple_of(start, MOE_CHUNK), size)
        hc = hc_scr[rows, :]
        gc = gc_scr[rows, :]
        lane = lax.broadcasted_iota(jnp.int32, gc.shape, 1)
        acc = None
        for k in range(MOE_STEP_EXPERTS):
            e = step * MOE_STEP_EXPERTS + k
            a = jnp.dot(hc, w1_ref[k], preferred_element_type=F32)
            b = jnp.dot(hc, w3_ref[k], preferred_element_type=F32)
            mid = (a * jax.nn.sigmoid(a) * b).astype(BF16)
            y = jnp.dot(mid, w2_ref[k], preferred_element_type=F32)
            gcol = jnp.sum(jnp.where((lane % N_EXPERTS == e) & (lane < POS_LANE), gc, 0.0), axis=-1, keepdims=True)
            acc = gcol * y if acc is None else acc + gcol * y
        yc_scr[rows, :] += acc

    for size in MOE_ARMS:
        if size > TM:
            continue
        taken = (n_chunks & (size // MOE_CHUNK)) != 0

        @pl.when(taken)
        def _(base=base, size=size):
            expert_rows(base, size)

        base = base + jnp.where(taken, size, 0)

    @pl.when(step == N_EXPERTS // MOE_STEP_EXPERTS - 1)
    def _():
        moe = jnp.dot(pt_scr[...], yc_scr[...].astype(BF16), preferred_element_type=F32)
        z = DEEPNORM_ALPHA * x_ref[...] + mod_ref[0, MOD_GATE2:MOD_GATE2 + 1, :] * moe
        o_ref[...] = _layernorm_rows(z, lng_ref[...], lnb_ref[...])


def moe_ln(x, mod, w_router, b_router, w1, w3, w2, ln_g, ln_b, *, rows_per_mod, tm):
    T, D = x.shape
    E, _, FF = w1.shape
    assert T % tm == 0 and rows_per_mod % tm == 0 and tm <= MOE_ARMS[0] and tm % MOE_CHUNK == 0
    tiles_per_mod = rows_per_mod // tm
    sorted_rows = tm + N_EXPERT_GROUPS * MOE_CHUNK
    wr_hi = w_router.astype(BF16)
    wr_lo = (w_router - wr_hi.astype(F32)).astype(BF16)
    lane_pad = lambda w: jnp.pad(w, ((0, 0), (0, LANES - E)))
    wr_split = jnp.concatenate([lane_pad(wr_hi), lane_pad(wr_lo)], axis=1)
    br_col = jnp.broadcast_to(b_router.reshape(E, 1), (E, LANES))
    return pl.pallas_call(
        _moe_ln_kernel,
        grid=(T // tm, E // MOE_STEP_EXPERTS),
        in_specs=[
            pl.BlockSpec((tm, D), lambda i, e: (i, 0)),
            pl.BlockSpec((1, 6, D), lambda i, e: (i // tiles_per_mod, 0, 0)),
            pl.BlockSpec((D, 2 * LANES), lambda i, e: (0, 0)),
            pl.BlockSpec((E, LANES), lambda i, e: (0, 0)),
            pl.BlockSpec((MOE_STEP_EXPERTS, D, FF), lambda i, e: (e, 0, 0)),
            pl.BlockSpec((MOE_STEP_EXPERTS, D, FF), lambda i, e: (e, 0, 0)),
            pl.BlockSpec((MOE_STEP_EXPERTS, FF, D), lambda i, e: (e, 0, 0)),
            pl.BlockSpec((1, D), lambda i, e: (0, 0)),
            pl.BlockSpec((1, D), lambda i, e: (0, 0)),
        ],
        out_specs=pl.BlockSpec((tm, D), lambda i, e: (i, 0)),
        out_shape=jax.ShapeDtypeStruct((T, D), F32),
        scratch_shapes=[
            pltpu.VMEM((tm, sorted_rows), BF16),
            pltpu.VMEM((sorted_rows, D), BF16),
            pltpu.VMEM((sorted_rows, LANES), F32),
            pltpu.VMEM((sorted_rows, D), F32),
            pltpu.SMEM((2 * N_EXPERT_GROUPS,), jnp.int32),
        ],
        compiler_params=pltpu.CompilerParams(
            dimension_semantics=("parallel", "arbitrary"), vmem_limit_bytes=VMEM_LIMIT_BYTES),
        name="moe_ln",
    )(x, mod, wr_split, br_col, w1, w3, w2, ln_g.reshape(1, D), ln_b.reshape(1, D))


def _ada_kernel(c_ref, w_ref, b_ref, o_ref):
    c = c_ref[...]
    s = c * jax.nn.sigmoid(c)
    o_ref[0] = jnp.dot(s, w_ref[0], precision=lax.Precision.HIGHEST, preferred_element_type=F32) + b_ref[0]


def ada_mod(cvecs, w_ada, b_ada, *, tn=1536):
    R, D = cvecs.shape
    depth, _, N = w_ada.shape
    assert N % tn == 0
    return pl.pallas_call(
        _ada_kernel,
        grid=(depth, N // tn),
        in_specs=[
            pl.BlockSpec((R, D), lambda l, j: (0, 0)),
            pl.BlockSpec((1, D, tn), lambda l, j: (l, 0, j)),
            pl.BlockSpec((1, 1, tn), lambda l, j: (l, 0, j)),
        ],
        out_specs=pl.BlockSpec((1, R, tn), lambda l, j: (l, 0, j)),
        out_shape=jax.ShapeDtypeStruct((depth, R, N), F32),
        compiler_params=pltpu.CompilerParams(
            dimension_semantics=("parallel", "parallel"), vmem_limit_bytes=VMEM_LIMIT_BYTES),
        name="ada_mod",
    )(cvecs, w_ada, b_ada.reshape(depth, 1, N))


MOD_SHIFT1, MOD_SCALE1, MOD_GATE1, MOD_SHIFT2, MOD_SCALE2, MOD_GATE2 = range(6)


def _modulate(x, mod_ref, shift_row, scale_row):
    return x * (1.0 + mod_ref[0, scale_row:scale_row + 1, :]) + mod_ref[0, shift_row:shift_row + 1, :]


def _in_proj_kernel(x_ref, mod_ref, w_ref, *rest, has_pos):
    x = x_ref[...] + rest[0][...] if has_pos else x_ref[...]
    out_refs = rest[1:] if has_pos else rest
    h = _modulate(x, mod_ref, MOD_SHIFT1, MOD_SCALE1).astype(BF16)
    off = 0
    for o_ref in out_refs:
        n = o_ref.shape[1]
        o_ref[...] = jnp.dot(h, w_ref[:, off:off + n], preferred_element_type=F32).astype(o_ref.dtype)
        off += n


def _pos_spec(pos, rows_per_mod, tm):
    assert pos.shape[0] == rows_per_mod
    tiles = rows_per_mod // tm
    return pl.BlockSpec((tm, pos.shape[1]), lambda i: (i % tiles, 0))


def in_proj(x, mod, w, widths, *, rows_per_mod, tm, pos=None):
    T, D = x.shape
    assert T % tm == 0 and rows_per_mod % tm == 0 and w.shape[1] == sum(widths)
    tiles_per_mod = rows_per_mod // tm
    extra = [] if pos is None else [pos]
    return pl.pallas_call(
        functools.partial(_in_proj_kernel, has_pos=pos is not None),
        grid=(T // tm,),
        in_specs=[
            pl.BlockSpec((tm, D), lambda i: (i, 0)),
            pl.BlockSpec((1, 6, D), lambda i: (i // tiles_per_mod, 0, 0)),
            pl.BlockSpec(w.shape, lambda i: (0, 0), pipeline_mode=pl.Buffered(1)),
        ] + [_pos_spec(p, rows_per_mod, tm) for p in extra],
        out_specs=[pl.BlockSpec((tm, n), lambda i: (i, 0)) for n in widths],
        out_shape=[jax.ShapeDtypeStruct((T, n), F32) for n in widths],
        compiler_params=pltpu.CompilerParams(
            dimension_semantics=("parallel",), vmem_limit_bytes=VMEM_LIMIT_BYTES),
        name="in_proj",
    )(x, mod, w, *extra)


N_BRANCH = 4


def _merge_kernel(x_ref, mod_ref, hy_ref, sc_ref, hg_ref, ssd_ref, wg_ref, bg_ref, wbr_ref, wo_ref,
                  lng_ref, lnb_ref, *rest, has_pos):
    x = x_ref[...] + rest[0][...] if has_pos else x_ref[...]
    o_ref = rest[-1]
    D = x.shape[1]
    h = _modulate(x, mod_ref, MOD_SHIFT1, MOD_SCALE1).astype(BF16)
    y = None
    for k, br_ref in enumerate((hy_ref, sc_ref, hg_ref, ssd_ref)):
        gate = jax.nn.sigmoid(
            jnp.dot(h, wg_ref[:, k * D:(k + 1) * D], preferred_element_type=F32) + bg_ref[:, k * D:(k + 1) * D])
        term = gate * jnp.dot(br_ref[...].astype(BF16), wbr_ref[k], preferred_element_type=F32)
        y = term if y is None else y + term
    y = jnp.dot(y.astype(BF16), wo_ref[...], preferred_element_type=F32)
    z = DEEPNORM_ALPHA * x + mod_ref[0, MOD_GATE1:MOD_GATE1 + 1, :] * y
    o_ref[...] = _layernorm_rows(z, lng_ref[...], lnb_ref[...])


def merge_ln(x, mod, branches, w_gate, b_gate, w_br, w_o, ln_g, ln_b, *, rows_per_mod, tm, pos=None):
    T, D = x.shape
    BW = branches[0].shape[1]
    assert T % tm == 0 and rows_per_mod % tm == 0
    tiles_per_mod = rows_per_mod // tm
    const = lambda shape: pl.BlockSpec(shape, lambda i: (0,) * len(shape), pipeline_mode=pl.Buffered(1))
    extra = [] if pos is None else [pos]
    return pl.pallas_call(
        functools.partial(_merge_kernel, has_pos=pos is not None),
        grid=(T // tm,),
        in_specs=[
            pl.BlockSpec((tm, D), lambda i: (i, 0)),
            pl.BlockSpec((1, 6, D), lambda i: (i // tiles_per_mod, 0, 0)),
        ] + [pl.BlockSpec((tm, BW), lambda i: (i, 0))] * N_BRANCH + [
            const((D, N_BRANCH * D)), const((1, N_BRANCH * D)), const((N_BRANCH, BW, D)), const((D, D)),
            const((1, D)), const((1, D)),
        ] + [_pos_spec(p, rows_per_mod, tm) for p in extra],
        out_specs=pl.BlockSpec((tm, D), lambda i: (i, 0)),
        out_shape=jax.ShapeDtypeStruct((T, D), F32),
        compiler_params=pltpu.CompilerParams(
            dimension_semantics=("parallel",), vmem_limit_bytes=VMEM_LIMIT_BYTES),
        name="merge_ln",
    )(x, mod, *branches, w_gate, b_gate.reshape(1, -1), w_br, w_o, ln_g.reshape(1, D), ln_b.reshape(1, D), *extra)


def _dwconv_rows(u, w_ref, col0=0):
    L, C = u.shape
    row = lax.broadcasted_iota(jnp.int32, u.shape, 0)
    prev = jnp.where(row == 0, 0.0, pltpu.roll(u, 1, 0))
    nxt = jnp.where(row == L - 1, 0.0, pltpu.roll(u, L - 1, 0))
    w = lambda k: w_ref[k:k + 1, col0:col0 + C]
    return prev * w(0) + u * w(1) + nxt * w(2)


def _hy_prep_kernel(x0_ref, x1_ref, v_ref, w0_ref, w1_ref, w2_ref, b0_ref, b1_ref, b2_ref, x0c_ref, hw_ref):
    x0c_ref[0] = _dwconv_rows(x0_ref[0], w0_ref) + b0_ref[...]
    x1c = _dwconv_rows(x1_ref[0], w1_ref) + b1_ref[...]
    vc = _dwconv_rows(v_ref[0], w2_ref) + b2_ref[...]
    hw_ref[0] = x1c * vc


def hy_prep(hy, conv_w, conv_b, *, cb=128):
    B, L, C3 = hy.shape
    W = C3 // 3
    nb = W // cb
    conv_b = conv_b.reshape(1, C3)
    data = lambda part: pl.BlockSpec((1, L, cb), lambda b, j: (b, 0, part * nb + j))
    wspec = lambda part: pl.BlockSpec((3, cb), lambda b, j: (0, part * nb + j))
    bspec = lambda part: pl.BlockSpec((1, cb), lambda b, j: (0, part * nb + j))
    out = pl.BlockSpec((1, L, cb), lambda b, j: (b, 0, j))
    return pl.pallas_call(
        _hy_prep_kernel,
        grid=(B, nb),
        in_specs=[data(0), data(1), data(2), wspec(0), wspec(1), wspec(2), bspec(0), bspec(1), bspec(2)],
        out_specs=[out, out],
        out_shape=[jax.ShapeDtypeStruct((B, L, W), F32), jax.ShapeDtypeStruct((B, L, W), F32)],
        compiler_params=pltpu.CompilerParams(
            dimension_semantics=("parallel", "parallel"), vmem_limit_bytes=VMEM_LIMIT_BYTES),
        name="hy_prep",
    )(hy, hy, hy, conv_w, conv_w, conv_w, conv_b, conv_b, conv_b)


def _sc_kernel(bg_ref, cg_ref, xs_ref, w_ref, o_ref):
    o_ref[0] = bg_ref[0] * _dwconv_rows(cg_ref[0] * xs_ref[0], w_ref)


def shortconv(sc, conv_w, *, cb=128):
    B, L, C3 = sc.shape
    W = C3 // 3
    nb = W // cb
    data = lambda part: pl.BlockSpec((1, L, cb), lambda b, j: (b, 0, part * nb + j))
    return pl.pallas_call(
        _sc_kernel,
        grid=(B, nb),
        in_specs=[data(0), data(1), data(2), pl.BlockSpec((3, cb), lambda b, j: (0, j))],
        out_specs=pl.BlockSpec((1, L, cb), lambda b, j: (b, 0, j)),
        out_shape=jax.ShapeDtypeStruct((B, L, W), F32),
        compiler_params=pltpu.CompilerParams(
            dimension_semantics=("parallel", "parallel"), vmem_limit_bytes=VMEM_LIMIT_BYTES),
        name="shortconv",
    )(sc, sc, sc, conv_w)


def _ssd_prep_kernel(u_ref, w_ref, b_ref, o_ref):
    y = _dwconv_rows(u_ref[0], w_ref) + b_ref[...]
    o_ref[0] = y * jax.nn.sigmoid(y)


def ssd_prep(ssd, conv_w, conv_b, *, col0, cb=128):
    B, L, _ = ssd.shape
    C = conv_w.shape[1]
    assert col0 % cb == 0 and C % cb == 0
    return pl.pallas_call(
        _ssd_prep_kernel,
        grid=(B, C // cb),
        in_specs=[
            pl.BlockSpec((1, L, cb), lambda b, j: (b, 0, col0 // cb + j)),
            pl.BlockSpec((3, cb), lambda b, j: (0, j)),
            pl.BlockSpec((1, cb), lambda b, j: (0, j)),
        ],
        out_specs=pl.BlockSpec((1, L, cb), lambda b, j: (b, 0, j)),
        out_shape=jax.ShapeDtypeStruct((B, L, C), F32),
        compiler_params=pltpu.CompilerParams(
            dimension_semantics=("parallel", "parallel"), vmem_limit_bytes=VMEM_LIMIT_BYTES),
        name="ssd_prep",
    )(ssd, conv_w, conv_b.reshape(1, C))


SSD_HEADS = 4
SSD_HEADDIM = 64
SSD_STATE = 64
SSD_GROUPS = 2
SSD_INNER = SSD_HEADS * SSD_HEADDIM
LANES = 128
NEG_INF = float("-inf")


SCAN_BATCH = 2


def _scan_chunk_index(ph, j, n):
    return j + (1 - ph) * (n - 1 - 2 * j)


def _ssd_kernel(z_ref, xbc_ref, dt_ref, par_ref, dskip_ref, ng_ref, sf0_ref, sb0_ref,
                y_ref, sf_ref, sb_ref, yb_scr, st_scr, *, n_chunks):
    ph = pl.program_id(1)
    j = pl.program_id(2)
    is_fwd = ph == 1
    c = _scan_chunk_index(ph, j, n_chunks)
    NB, R = xbc_ref.shape[0], xbc_ref.shape[1]

    @pl.when((j == 0) & is_fwd)
    def _():
        st_scr[...] = sf0_ref[...]

    @pl.when((j == 0) & jnp.logical_not(is_fwd))
    def _():
        st_scr[...] = sb0_ref[...]

    row = lax.broadcasted_iota(jnp.int32, (R, R), 0)
    col = lax.broadcasted_iota(jnp.int32, (R, R), 1)
    mask = jnp.where(is_fwd, row - col, col - row) >= 0
    tri = jnp.where(mask, 1.0, 0.0).astype(BF16)
    head_of_lane = lax.broadcasted_iota(jnp.int32, (LANES, SSD_INNER), 1) // SSD_HEADDIM
    expand = jnp.where(lax.broadcasted_iota(jnp.int32, (LANES, SSD_INNER), 0) == head_of_lane, 1.0, 0.0).astype(BF16)
    ys = [_ssd_chunk(bi, is_fwd, mask, tri, expand, xbc_ref, dt_ref, par_ref, st_scr) for bi in range(NB)]

    @pl.when(jnp.logical_not(is_fwd))
    def _():
        for bi in range(NB):
            yb_scr[bi, c] = ys[bi]

    @pl.when(is_fwd)
    def _():
        for bi in range(NB):
            z = z_ref[bi]
            xs = xbc_ref[bi, :, :SSD_INNER]
            yt = (ys[bi] + yb_scr[bi, c] + xs * dskip_ref[...]) * (z * jax.nn.sigmoid(z))
            ms = jnp.mean(yt * yt, axis=-1, keepdims=True)
            y_ref[bi] = yt * lax.rsqrt(ms + RMS_EPS) * ng_ref[...]

    @pl.when((j == n_chunks - 1) & is_fwd)
    def _():
        sf_ref[...] = st_scr[...]

    @pl.when((j == n_chunks - 1) & jnp.logical_not(is_fwd))
    def _():
        sb_ref[...] = st_scr[...]


def _ssd_chunk(bi, is_fwd, mask, tri, expand, xbc_ref, dt_ref, par_ref, st_scr):
    R = xbc_ref.shape[1]
    dsel = lambda v: jnp.where(is_fwd, v, pltpu.roll(v, LANES - SSD_HEADS, 1))
    dt_raw = dsel(dt_ref[bi]) + dsel(par_ref[8:16, :])[0:1, :]
    dt = jnp.maximum(dt_raw, 0.0) + jnp.log(1.0 + jnp.exp(-jnp.abs(dt_raw)))
    a = -jnp.exp(dsel(par_ref[0:8, :])[0:1, :]) * dt
    cs = _split_dot(tri, a, terms=3)
    cs_t = cs.T
    total = jnp.sum(a, axis=0, keepdims=True)

    spread = lambda v: jnp.dot(v.astype(BF16), expand, preferred_element_type=F32)
    dt_bc = spread(dt)
    in_decay = spread(jnp.exp(cs))
    out_decay = spread(jnp.exp(total - cs))
    e_total = jnp.exp(total)

    xbc = xbc_ref[bi]
    xdt = xbc[:, :SSD_INNER] * dt_bc
    xdt_b = xdt.astype(BF16)
    xout_b = (xdt * out_decay).astype(BF16)
    bm = xbc[:, SSD_INNER:SSD_INNER + LANES]
    cm_b = xbc[:, SSD_INNER + LANES:SSD_INNER + 2 * LANES].astype(BF16)
    lane = lax.broadcasted_iota(jnp.int32, (R, LANES), 1)
    lo_half = lane < SSD_HEADDIM
    lo_half_st = lax.broadcasted_iota(jnp.int32, (LANES, LANES), 1) < SSD_HEADDIM
    y_groups = []
    for g in range(SSD_GROUPS):
        h0, h1 = 2 * g, 2 * g + 1
        cols = slice(g * LANES, (g + 1) * LANES)
        bm_g = jnp.where((lane >= g * SSD_STATE) & (lane < (g + 1) * SSD_STATE), bm, 0.0).astype(BF16)
        G = lax.dot_general(cm_b, bm_g, (((1,), (1,)), ((), ())), preferred_element_type=F32)
        st_old = st_scr[bi, g]
        y_off = in_decay[:, cols] * jnp.dot(cm_b, st_old.astype(BF16), preferred_element_type=F32)
        y_diag = []
        for h in (h0, h1):
            decay = jnp.exp(jnp.where(mask, cs[:, h:h + 1] - cs_t[h:h + 1, :], NEG_INF))
            y_diag.append(jnp.dot((G * decay).astype(BF16), xdt_b[:, cols], preferred_element_type=F32))
        y_groups.append(jnp.where(lo_half, y_diag[0], y_diag[1]) + y_off)
        upd = lax.dot_general(bm_g, xout_b[:, cols], (((0,), (0,)), ((), ())), preferred_element_type=F32)
        st_scr[bi, g] = jnp.where(lo_half_st, e_total[:, h0:h0 + 1], e_total[:, h1:h1 + 1]) * st_old + upd
    return jnp.concatenate(y_groups, axis=1)


def ssd_scan(ssd, xbc, par, d_skip, norm_g, s_f0, s_b0, *, z_blk, dt_blk, chunk=256):
    B, L, _ = ssd.shape
    R = min(chunk, L)
    assert L % R == 0 and B % SCAN_BATCH == 0
    n = L // R
    NB = SCAN_BATCH
    cidx = lambda ph, j: _scan_chunk_index(ph, j, n)
    st_spec = pl.BlockSpec((NB, SSD_GROUPS, LANES, LANES), lambda b, ph, j: (b, 0, 0, 0))
    st_shape = jax.ShapeDtypeStruct((B, SSD_GROUPS, LANES, LANES), F32)
    return pl.pallas_call(
        functools.partial(_ssd_kernel, n_chunks=n),
        grid=(B // NB, 2, n),
        in_specs=[
            pl.BlockSpec((NB, R, SSD_INNER), lambda b, ph, j: (b, cidx(ph, j), z_blk)),
            pl.BlockSpec((NB, R, xbc.shape[2]), lambda b, ph, j: (b, cidx(ph, j), 0)),
            pl.BlockSpec((NB, R, LANES), lambda b, ph, j: (b, cidx(ph, j), dt_blk)),
            pl.BlockSpec((16, LANES), lambda b, ph, j: (0, 0)),
            pl.BlockSpec((1, SSD_INNER), lambda b, ph, j: (0, 0)),
            pl.BlockSpec((1, SSD_INNER), lambda b, ph, j: (0, 0)),
            st_spec, st_spec,
        ],
        out_specs=[pl.BlockSpec((NB, R, SSD_INNER), lambda b, ph, j: (b, ph * j, 0)), st_spec, st_spec],
        out_shape=[jax.ShapeDtypeStruct((B, L, SSD_INNER), F32), st_shape, st_shape],
        scratch_shapes=[pltpu.VMEM((NB, n, R, SSD_INNER), F32), pltpu.VMEM((NB, SSD_GROUPS, LANES, LANES), F32)],
        compiler_params=pltpu.CompilerParams(
            dimension_semantics=("parallel", "arbitrary", "arbitrary"), vmem_limit_bytes=VMEM_LIMIT_BYTES),
        name="ssd_scan",
    )(ssd, xbc, ssd, par, d_skip.reshape(1, SSD_INNER), norm_g.reshape(1, SSD_INNER), s_f0, s_b0)


HG_HEADS = 4
HG_DK = 64
HG_DV = 64
HG_KEY = HG_HEADS * HG_DK
HG_SUB = 16
HG_CHUNK = 64
HG_SUBS_PER_CHUNK = HG_CHUNK // HG_SUB


def _split_dot(mask_b, x, terms=2):
    out = None
    for _ in range(terms):
        part = x.astype(BF16)
        prod = jnp.dot(mask_b, part, preferred_element_type=F32)
        out = prod if out is None else out + prod
        x = x - part.astype(F32)
    return out


def _hgrn_kernel(q_ref, f_ref, v_ref, g_ref, lbl_ref, ng_ref, sf0_ref, sb0_ref,
                 y_ref, sf_ref, sb_ref, ob_scr, st_scr, *, n_blocks, layer, depth):
    ph = pl.program_id(1)
    j = pl.program_id(2)
    is_fwd = ph == 1
    c = _scan_chunk_index(ph, j, n_blocks)
    NB, R = q_ref.shape[0], q_ref.shape[1]
    W = HG_KEY

    @pl.when((j == 0) & is_fwd)
    def _():
        st_scr[...] = sf0_ref[...]

    @pl.when((j == 0) & jnp.logical_not(is_fwd))
    def _():
        st_scr[...] = sb0_ref[...]

    def lower_bound(d):
        x = lbl_ref[d * depth:(d + 1) * depth, :]
        e = jnp.exp(x - jnp.max(x, axis=0, keepdims=True))
        p = e / jnp.sum(e, axis=0, keepdims=True)
        return jnp.sum(p[1:layer + 1, :], axis=0, keepdims=True) if layer > 0 else jnp.zeros((1, W), F32)

    lb = jnp.where(is_fwd, lower_bound(0), lower_bound(1))
    row = lax.broadcasted_iota(jnp.int32, (R, R), 0)
    col = lax.broadcasted_iota(jnp.int32, (R, R), 1)
    dirge = jnp.where(is_fwd, row - col, col - row) >= 0
    same_sub = (row // HG_SUB) == (col // HG_SUB)
    same_chunk = (row // HG_CHUNK) == (col // HG_CHUNK)
    one = lambda m: jnp.where(m, 1.0, 0.0).astype(BF16)
    dist = jnp.where(is_fwd, row // HG_SUB - col // HG_SUB, col // HG_SUB - row // HG_SUB)
    masks = dict(
        cum16=one(same_sub & dirge), tot16=one(same_sub), cum64=one(same_chunk & dirge), tot64=one(same_chunk),
        dist=[same_sub & dirge] + [same_chunk & (dist == d) for d in range(1, HG_SUBS_PER_CHUNK)])
    lane = lax.broadcasted_iota(jnp.int32, (R, W), 1)
    head_mask = [(lane // HG_DK) == h for h in range(HG_HEADS)]
    blocks = [_hgrn_block(bi, is_fwd, lb, masks, head_mask, q_ref, f_ref, v_ref) for bi in range(NB)]
    srow = lax.broadcasted_iota(jnp.int32, (W, W), 0)
    scol = lax.broadcasted_iota(jnp.int32, (W, W), 1)
    bd_mask = (srow // HG_DV) == (scol // HG_DK)

    def through_state(bi, reverse):
        o_intra, qd, ke, v_b, tot64 = blocks[bi]
        n_chunks = R // HG_CHUNK
        st = st_scr[bi]
        parts = [None] * n_chunks
        for i in (reversed(range(n_chunks)) if reverse else range(n_chunks)):
            rows = slice(i * HG_CHUNK, (i + 1) * HG_CHUNK)
            parts[i] = lax.dot_general(qd[rows], st.astype(BF16), (((1,), (1,)), ((), ())), preferred_element_type=F32)
            upd = lax.dot_general(v_b[rows], ke[rows], (((0,), (0,)), ((), ())), preferred_element_type=F32)
            st = jnp.exp(tot64[i * HG_CHUNK:i * HG_CHUNK + 1, :]) * st + jnp.where(bd_mask, upd, 0.0)
        st_scr[bi] = st
        return o_intra + jnp.concatenate(parts, axis=0)

    @pl.when(jnp.logical_not(is_fwd))
    def _():
        for bi in range(NB):
            ob_scr[bi, c] = through_state(bi, reverse=True)

    @pl.when(is_fwd)
    def _():
        for bi in range(NB):
            ot = through_state(bi, reverse=False) + ob_scr[bi, c]
            sq = ot * ot
            ms = jnp.zeros((R, W), F32)
            for h in range(HG_HEADS):
                s = jnp.sum(jnp.where(head_mask[h], sq, 0.0), axis=-1, keepdims=True) * (1.0 / HG_DV)
                ms = jnp.where(head_mask[h], s, ms)
            g = g_ref[bi]
            y_ref[bi] = ot * lax.rsqrt(ms + RMS_EPS) * ng_ref[...] * (g * jax.nn.sigmoid(g))

    @pl.when((j == n_blocks - 1) & is_fwd)
    def _():
        sf_ref[...] = st_scr[...]

    @pl.when((j == n_blocks - 1) & jnp.logical_not(is_fwd))
    def _():
        sb_ref[...] = st_scr[...]


def _hgrn_block(bi, is_fwd, lb, masks, head_mask, q_ref, f_ref, v_ref):
    R = q_ref.shape[1]
    W = HG_KEY
    sig = jax.nn.sigmoid(f_ref[bi])
    logf = jnp.log(lb + (1.0 - lb) * sig)
    k = (1.0 - lb) * (1.0 - sig)
    q = q_ref[bi] * (HG_DK ** -0.5)
    v_b = v_ref[bi].astype(BF16)
    logf_hi = logf.astype(BF16)
    logf_lo = (logf - logf_hi.astype(F32)).astype(BF16)
    msum = lambda m: (jnp.dot(m, logf_hi, preferred_element_type=F32) + jnp.dot(m, logf_lo, preferred_element_type=F32))
    b16 = msum(masks["cum16"])
    tot16 = msum(masks["tot16"])
    b64 = msum(masks["cum64"])
    tot64 = msum(masks["tot64"])

    shifted = lambda s: jnp.where(is_fwd, pltpu.roll(tot16, R - s, 0), pltpu.roll(tot16, s, 0))
    nx1 = shifted(HG_SUB)
    nx2 = nx1 + shifted(2 * HG_SUB)
    k_end16 = k * jnp.exp(tot16 - b16)
    keys = [(k * jnp.exp(-b16)).astype(BF16), k_end16.astype(BF16),
            (k_end16 * jnp.exp(nx1)).astype(BF16), (k_end16 * jnp.exp(nx2)).astype(BF16)]
    q16 = q * jnp.exp(b16)
    qm =jnp.concatenate([jnp.where(m, q16, 0.0) for m in head_mask], axis=0).astype(BF16)
    att = [jnp.zeros((R, R), F32)] * HG_HEADS
    for d in range(HG_SUBS_PER_CHUNK):
        sc = lax.dot_general(qm, keys[d], (((1,), (1,)), ((), ())), preferred_element_type=F32)
        att = [jnp.where(masks["dist"][d], sc[h * R:(h + 1) * R], att[h]) for h in range(HG_HEADS)]
    o_all = jnp.dot(jnp.concatenate(att, axis=0).astype(BF16), v_b, preferred_element_type=F32)
    o = jnp.zeros((R, W), F32)
    for h in range(HG_HEADS):
        o = jnp.where(head_mask[h], o_all[h * R:(h + 1) * R], o)

    return o, (q * jnp.exp(b64)).astype(BF16), (k * jnp.exp(tot64 - b64)).astype(BF16), v_b, tot64


def hgrn_scan(hg, lb_logits, norm_g, s_f0, s_b0, *, layer, block=256):
    B, L, _ = hg.shape
    W = HG_KEY
    R = min(block, L)
    assert L % R == 0 and R % HG_CHUNK == 0 and B % SCAN_BATCH == 0
    n = L // R
    NB = SCAN_BATCH
    depth = lb_logits.shape[0] // 2
    cidx = lambda ph, j: _scan_chunk_index(ph, j, n)
    col = lambda blk: pl.BlockSpec((NB, R, W), lambda b, ph, j: (b, cidx(ph, j), blk))
    st_spec = pl.BlockSpec((NB, W, W), lambda b, ph, j: (b, 0, 0))
    st_shape = jax.ShapeDtypeStruct((B, W, W), F32)
    return pl.pallas_call(
        functools.partial(_hgrn_kernel, n_blocks=n, layer=layer, depth=depth),
        grid=(B // NB, 2, n),
        in_specs=[
            col(0),
            pl.BlockSpec((NB, R, W), lambda b, ph, j: (b, cidx(ph, j), 2 - ph)),
            col(3), col(4),
            pl.BlockSpec(lb_logits.shape, lambda b, ph, j: (0, 0)),
            pl.BlockSpec((1, W), lambda b, ph, j: (0, 0)),
            st_spec, st_spec,
        ],
        out_specs=[pl.BlockSpec((NB, R, W), lambda b, ph, j: (b, ph * j, 0)), st_spec, st_spec],
        out_shape=[jax.ShapeDtypeStruct((B, L, W), F32), st_shape, st_shape],
        scratch_shapes=[pltpu.VMEM((NB, n, R, W), F32), pltpu.VMEM((NB, W, W), F32)],
        compiler_params=pltpu.CompilerParams(
            dimension_semantics=("parallel", "arbitrary", "arbitrary"), vmem_limit_bytes=VMEM_LIMIT_BYTES),
        name="hgrn_scan",
    )(hg, hg, hg, hg, lb_logits, jnp.tile(norm_g, HG_HEADS).reshape(1, W), s_f0, s_b0)


HY_WIDTH = 256
HY_BANDS = 16
HY_HIDDEN = 64
SUBLANES = 8


def _dot_hi(a, b):
    return jnp.dot(a, b, precision=lax.Precision.HIGHEST, preferred_element_type=F32)


def _hy_filter_kernel(bands_ref, w1t_ref, w1c_ref, w1s_ref, b1_ref, fr1_ref, w2_ref, b2_ref, fr2_ref, w3_ref,
                      decay_ref, f_ref, den_ref, *, L):
    i = pl.program_id(0)
    TR = f_ref.shape[1]
    m = i * TR + lax.broadcasted_iota(jnp.int32, (TR, LANES), 0)
    total = None
    for half in range(2):
        idx = m if half == 0 else jnp.where(m == 0, 0, L - m)
        idx_f = idx.astype(F32)
        t = idx_f * (1.0 / (L - 1))
        ang = ((2.0 * math.pi / L) * idx_f) * bands_ref[...]
        pre = (t[:, :HY_HIDDEN] * w1t_ref[...] + _dot_hi(jnp.cos(ang), w1c_ref[...])
               - _dot_hi(jnp.sin(ang), w1s_ref[...]) + b1_ref[...])
        h = jnp.sin(fr1_ref[...] * pre)
        h = jnp.sin(fr2_ref[...] * (_dot_hi(h, w2_ref[...]) + b2_ref[...]))
        h = _dot_hi(h, w3_ref[:, half * HY_WIDTH:(half + 1) * HY_WIDTH])
        tw = jnp.concatenate([t, t], axis=1)
        filt = h * jnp.exp(-tw * jnp.abs(decay_ref[...]))
        if half == 1:
            filt = jnp.where(jnp.concatenate([m, m], axis=1) == 0, 0.0, filt)
        f_ref[half] = filt
        s = jnp.sum(jnp.abs(filt), axis=0, keepdims=True)
        total = s if total is None else total + s

    @pl.when(i == 0)
    def _():
        den_ref[...] = jnp.zeros_like(den_ref)

    den_ref[...] += total


def hy_filter(L, w1, b1, fr1, w2, b2, fr2, w3, decay, *, tr=512):
    tr = min(tr, L)
    bands = np.zeros((1, LANES), np.float32)
    bands[0, :HY_BANDS] = np.linspace(1e-4, HY_BANDS - 1, HY_BANDS, dtype=np.float32)
    pad_rows = lambda w: jnp.pad(w, ((0, LANES - HY_BANDS), (0, 0)))
    row = lambda v: v.reshape(1, -1)
    args = (jnp.asarray(bands), w1[0:1], pad_rows(w1[1:1 + HY_BANDS]), pad_rows(w1[1 + HY_BANDS:]), row(b1), row(fr1),
            w2, row(b2), row(fr2), w3, row(decay))
    return pl.pallas_call(
        functools.partial(_hy_filter_kernel, L=L),
        grid=(L // tr,),
        in_specs=[pl.BlockSpec(a.shape, lambda i: (0, 0)) for a in args],
        out_specs=[pl.BlockSpec((2, tr, HY_WIDTH), lambda i: (0, i, 0)), pl.BlockSpec((1, HY_WIDTH), lambda i: (0, 0))],
        out_shape=[jax.ShapeDtypeStruct((2, L, HY_WIDTH), F32), jax.ShapeDtypeStruct((1, HY_WIDTH), F32)],
        compiler_params=pltpu.CompilerParams(
            dimension_semantics=("arbitrary",), vmem_limit_bytes=VMEM_LIMIT_BYTES),
        name="hy_filter",
    )(*args)


FFT_N1 = 64
FFT_N2 = 128
FFT_N = FFT_N1 * FFT_N2
FFT_L = FFT_N // 2
FFT_N1_NZ = FFT_N1 // 2
FFT_UNROLL_N1_STAGE = 2
FFT_UNROLL_N2_STAGE = 8


@functools.lru_cache(maxsize=None)
def _fft_constants():
    eye = np.eye(SUBLANES)
    k1 = np.arange(FFT_N1)[:, None]
    n1 = np.arange(FFT_N1_NZ)[None, :]
    ang1 = 2.0 * np.pi * ((k1 * n1) % FFT_N1) / FFT_N1
    kron_fwd = np.concatenate([np.kron(np.cos(ang1), eye), np.kron(np.sin(ang1), eye)], axis=0)
    kron_inv = np.concatenate([np.kron(np.cos(ang1).T, eye), np.kron(np.sin(ang1).T, eye)], axis=0) / FFT_N
    k = np.arange(FFT_N1)[:, None, None] + FFT_N1 * np.arange(FFT_N2)[None, :, None]
    n2 = np.arange(FFT_N2)[None, None, :]
    ang2 = 2.0 * np.pi * ((k * n2) % FFT_N) / FFT_N
    g_fwd = np.concatenate([np.cos(ang2), np.sin(ang2)], axis=1)
    g_inv = np.concatenate([np.cos(ang2).transpose(0, 2, 1), np.sin(ang2).transpose(0, 2, 1)], axis=1)
    n1_all = np.arange(FFT_N1)[None, :]
    ang1_all = 2.0 * np.pi * ((k1 * n1_all) % FFT_N1) / FFT_N1
    kron_full = np.concatenate([np.kron(np.cos(ang1_all), eye), np.kron(np.sin(ang1_all), eye)], axis=0)
    to_b = lambda a: jnp.asarray(a, dtype=F32).astype(BF16)
    return to_b(kron_fwd), to_b(g_fwd), to_b(g_inv), to_b(kron_inv), to_b(kron_full)


def _cplx_fwd(r, half_rows, cb):
    cr, sr = r[:half_rows], r[half_rows:]
    return cr[:, :cb] + sr[:, cb:], cr[:, cb:] - sr[:, :cb]


def _cplx_inv(r, half_rows, cb):
    cr, sr = r[:half_rows], r[half_rows:]
    return cr[:, :cb] - sr[:, cb:], cr[:, cb:] + sr[:, :cb]


def _fft_stage1(load_group, kron_ref, a_scr, cb):
    def body(jt, carry):
        rows = pl.ds(pl.multiple_of(jt * SUBLANES, SUBLANES), SUBLANES)
        d = load_group(rows).reshape(kron_ref.shape[1], 2 * cb).astype(BF16)
        r = jnp.dot(kron_ref[...], d, preferred_element_type=F32)
        a_re, a_im = _cplx_fwd(r, FFT_N1 * SUBLANES, cb)
        a_scr[:, rows, :] = jnp.concatenate([a_re, a_im], axis=1).reshape(FFT_N1, SUBLANES, 2 * cb)
        return carry
    lax.fori_loop(0, FFT_N2 // SUBLANES, body, 0, unroll=FFT_UNROLL_N1_STAGE)


def _fft_fwd_data_kernel(z_ref, h_ref, kron_ref, g_ref, p_ref, a_scr):
    cb = z_ref.shape[4]
    load = lambda rows: jnp.concatenate([z_ref[0, 0, :, rows, :], z_ref[0, 1, :, rows, :]], axis=-1)
    _fft_stage1(load, kron_ref, a_scr, cb)

    def body(k1, carry):
        r = jnp.dot(g_ref[k1], a_scr[k1].astype(BF16), preferred_element_type=F32)
        x_re, x_im = _cplx_fwd(r, FFT_N2, cb)
        h = h_ref[0, k1]
        h_re, h_im = h[:, :cb], h[:, cb:]
        p_ref[0, 0, k1] = jnp.concatenate([x_re * h_re - x_im * h_im, x_re * h_im + x_im * h_re], axis=1).astype(BF16)
        return carry
    lax.fori_loop(0, FFT_N1, body, 0, unroll=FFT_UNROLL_N2_STAGE)


def _fft_fwd_filter_kernel(f_ref, den_ref, kron_ref, g_ref, h_ref, a_scr):
    cb = f_ref.shape[3]
    inv_den = 1.0 / den_ref[...]

    def load(rows):
        re = jnp.concatenate([f_ref[0, :, rows, :], f_ref[1, :, rows, :]], axis=0) * inv_den
        return jnp.concatenate([re, jnp.zeros_like(re)], axis=-1)
    _fft_stage1(load, kron_ref, a_scr, cb)

    def body(k1, carry):
        r = jnp.dot(g_ref[k1], a_scr[k1].astype(BF16), preferred_element_type=F32)
        x_re, x_im = _cplx_fwd(r, FFT_N2, cb)
        h_ref[0, k1] = jnp.concatenate([x_re, x_im], axis=1)
        return carry
    lax.fori_loop(0, FFT_N1, body, 0, unroll=FFT_UNROLL_N2_STAGE)


def _fft_inv_kernel(p_ref, x0_ref, w_ref, bias_ref, gi_ref, kron_ref, o_ref, b_scr):
    cb = o_ref.shape[4]

    def body3(k1, carry):
        r = jnp.dot(gi_ref[k1], p_ref[0, 0, k1], preferred_element_type=F32)
        b_re, b_im = _cplx_inv(r, FFT_N2, cb)
        b_scr[k1] = jnp.concatenate([b_re, b_im], axis=1)
        return carry
    lax.fori_loop(0, FFT_N1, body3, 0, unroll=FFT_UNROLL_N2_STAGE)

    bias = bias_ref[...].reshape(1, 1, cb)

    def body4(jt, carry):
        rows = pl.ds(pl.multiple_of(jt * SUBLANES, SUBLANES), SUBLANES)
        d = b_scr[:, rows, :].reshape(FFT_N1 * SUBLANES, 2 * cb).astype(BF16)
        r = jnp.dot(kron_ref[...], d, preferred_element_type=F32)
        y_re, y_im = _cplx_inv(r, FFT_N1_NZ * SUBLANES, cb)
        for which, y in enumerate((y_re, y_im)):
            y3 = y.reshape(FFT_N1_NZ, SUBLANES, cb)
            o_ref[0, which, :, rows, :] = x0_ref[0, which, :, rows, :] * (y3 + w_ref[0, which, :, rows, :] * bias)
        return carry
    lax.fori_loop(0, FFT_N2 // SUBLANES, body4, 0, unroll=FFT_UNROLL_N1_STAGE)


def hyena_spectrum(fu, den, *, cb=128):
    _, g_fwd, _, _, kron_full = _fft_constants()
    W = fu.shape[2]
    ncb = W // cb
    f4 = fu.reshape(2, FFT_N1_NZ, FFT_N2, W)
    const = lambda a: pl.BlockSpec(a.shape, lambda c: (0,) * a.ndim, pipeline_mode=pl.Buffered(1))
    return pl.pallas_call(
        _fft_fwd_filter_kernel,
        grid=(ncb,),
        in_specs=[
            pl.BlockSpec((2, FFT_N1_NZ, FFT_N2, cb), lambda c: (0, 0, 0, c)),
            pl.BlockSpec((1, cb), lambda c: (0, c)),
            const(kron_full), const(g_fwd),
        ],
        out_specs=pl.BlockSpec((1, FFT_N1, FFT_N2, 2 * cb), lambda c: (c, 0, 0, 0)),
        out_shape=jax.ShapeDtypeStruct((ncb, FFT_N1, FFT_N2, 2 * cb), F32),
        scratch_shapes=[pltpu.VMEM((FFT_N1, FFT_N2, 2 * cb), F32)],
        compiler_params=pltpu.CompilerParams(
            dimension_semantics=("parallel",), vmem_limit_bytes=VMEM_LIMIT_BYTES),
        name="hyena_spectrum",
    )(f4, den, kron_full, g_fwd)


def hyena_conv(x0c, hw, spec, bias, *, cb=128):
    kron_fwd, g_fwd, g_inv, kron_inv, _ = _fft_constants()
    B, L, W = hw.shape
    assert L == FFT_L and B % 2 == 0
    ncb = W // cb
    pair_shape = (B // 2, 2, FFT_N1_NZ, FFT_N2, W)
    x5 = x0c.reshape(pair_shape)
    w5 = hw.reshape(pair_shape)
    const = lambda a: pl.BlockSpec(a.shape, lambda c, p: (0,) * a.ndim, pipeline_mode=pl.Buffered(1))
    pair = pl.BlockSpec((1, 2, FFT_N1_NZ, FFT_N2, cb), lambda c, p: (p, 0, 0, 0, c))
    pspec = pl.BlockSpec((1, 1, FFT_N1, FFT_N2, 2 * cb), lambda c, p: (c, p, 0, 0, 0))
    prod = pl.pallas_call(
        _fft_fwd_data_kernel,
        grid=(ncb, B // 2),
        in_specs=[pair, pl.BlockSpec((1, FFT_N1, FFT_N2, 2 * cb), lambda c, p: (c, 0, 0, 0)),
                  const(kron_fwd), const(g_fwd)],
        out_specs=pspec,
        out_shape=jax.ShapeDtypeStruct((ncb, B // 2, FFT_N1, FFT_N2, 2 * cb), BF16),
        scratch_shapes=[pltpu.VMEM((FFT_N1, FFT_N2, 2 * cb), F32)],
        compiler_params=pltpu.CompilerParams(
            dimension_semantics=("parallel", "parallel"), vmem_limit_bytes=VMEM_LIMIT_BYTES),
        name="hyena_fft_fwd",
    )(w5, spec, kron_fwd, g_fwd)
    out = pl.pallas_call(
        _fft_inv_kernel,
        grid=(ncb, B // 2),
        in_specs=[pspec, pair, pair, pl.BlockSpec((1, cb), lambda c, p: (0, c)), const(g_inv), const(kron_inv)],
        out_specs=pair,
        out_shape=jax.ShapeDtypeStruct(pair_shape, F32),
        scratch_shapes=[pltpu.VMEM((FFT_N1, FFT_N2, 2 * cb), F32)],
        compiler_params=pltpu.CompilerParams(
            dimension_semantics=("parallel", "parallel"), vmem_limit_bytes=VMEM_LIMIT_BYTES),
        name="hyena_fft_inv",
    )(prod, x5, w5, bias.reshape(1, W), g_inv, kron_inv)
    return out.reshape(B, L, W)


@functools.lru_cache(maxsize=None)
def _dense_dft_constants(L):
    n_full = 2 * L
    k = np.arange(n_full)[:, None]
    n = np.arange(n_full)[None, :]
    ang = 2.0 * np.pi * ((k * n) % n_full) / n_full
    fwd = np.concatenate([np.cos(ang), np.sin(ang)], axis=0)
    inv = np.concatenate([np.cos(ang[:L]), np.sin(ang[:L])], axis=0) / n_full
    return jnp.asarray(fwd, dtype=F32), jnp.asarray(inv, dtype=F32)


def _hyena_short_kernel(x0_ref, w_ref, f_ref, den_ref, bias_ref, ff_ref, fi_ref, o_ref):
    L, cb = o_ref.shape[2], o_ref.shape[3]
    filt = jnp.concatenate([f_ref[0], f_ref[1]], axis=0) / den_ref[...]
    r = _dot_hi(ff_ref[...], filt)
    h_re, h_im = r[:2 * L], -r[2 * L:]
    z = jnp.concatenate([w_ref[0, 0], w_ref[0, 1]], axis=1)
    x_re, x_im = _cplx_fwd(_dot_hi(ff_ref[:, :L], z), 2 * L, cb)
    p = jnp.concatenate([x_re * h_re - x_im * h_im, x_re * h_im + x_im * h_re], axis=1)
    y_re, y_im = _cplx_inv(_dot_hi(fi_ref[...], p), L, cb)
    for which, y in enumerate((y_re, y_im)):
        o_ref[0, which] = x0_ref[0, which] * (y + w_ref[0, which] * bias_ref[...])


def hyena_conv_short(x0c, hw, fu, den, bias, *, cb=128):
    B, L, W = hw.shape
    fwd, inv = _dense_dft_constants(L)
    pair_shape = (B // 2, 2, L, W)
    pair = pl.BlockSpec((1, 2, L, cb), lambda p, c: (p, 0, 0, c))
    vec = pl.BlockSpec((1, cb), lambda p, c: (0, c))
    const = lambda a: pl.BlockSpec(a.shape, lambda p, c: (0, 0))
    out = pl.pallas_call(
        _hyena_short_kernel,
        grid=(B // 2, W // cb),
        in_specs=[pair, pair, pl.BlockSpec((2, L, cb), lambda p, c: (0, 0, c)), vec, vec, const(fwd), const(inv)],
        out_specs=pair,
        out_shape=jax.ShapeDtypeStruct(pair_shape, F32),
        compiler_params=pltpu.CompilerParams(
            dimension_semantics=("parallel", "parallel"), vmem_limit_bytes=VMEM_LIMIT_BYTES),
        name="hyena_conv_short",
    )(x0c.reshape(pair_shape), hw.reshape(pair_shape), fu, den, bias.reshape(1, W), fwd, inv)
    return out.reshape(B, L, W)


GRID_W = 64
HY_COLS = 3 * HY_WIDTH
SC_COLS = 3 * 256
HG_COLS = 3 * HG_KEY + 2 * HG_HEADS * HG_DV
SSD_XBC = SSD_INNER + 2 * SSD_GROUPS * SSD_STATE
SSD_COLS = SSD_INNER + SSD_XBC + 2 * SSD_HEADS


def _sincos_1d(pos, dim):
    omega = 1.0 / (10000.0 ** (jnp.arange(dim // 2, dtype=F32) / (dim // 2)))
    ang = pos.astype(F32)[:, None] * omega[None]
    return jnp.concatenate([jnp.sin(ang), jnp.cos(ang)], -1)


def _grid_pos_embed(rows, dim):
    row = jnp.repeat(jnp.arange(rows), GRID_W)
    col = jnp.tile(jnp.arange(GRID_W), rows)
    return jnp.concatenate([_sincos_1d(row, dim // 2), _sincos_1d(col, dim // 2)], -1)


def kernel(x, c, ctx, c_ctx, w_ada, b_ada, w_in, hy_conv_w, hy_conv_b, hy_w1, hy_b1, hy_freq1,
           hy_w2, hy_b2, hy_freq2, hy_w3, hy_decay, hy_bias, sc_conv_w, hg_lb_logits, hg_norm_g,
           ssd_conv_w, ssd_conv_b, ssd_a_log, ssd_dt_bias, ssd_d, ssd_norm_g, w_gate, b_gate, w_br,
           w_o, ln1_g, ln1_b, ln2_g, ln2_b, w_router, b_router, w_e1, w_e3, w_e2):
    B, L, D = x.shape
    LC = ctx.shape[1]
    depth = w_in.shape[0]
    pos_table = _grid_pos_embed(L // GRID_W, D).astype(x.dtype)
    lat = x.reshape(B * L, D)
    cx = ctx.reshape(B * LC, D)

    n_vec = -(-(B + 1) // SUBLANES) * SUBLANES
    cvecs = jnp.concatenate([c, c_ctx[None], jnp.zeros((n_vec - B - 1, D), c.dtype)], axis=0)
    mods = ada_mod(cvecs, w_ada, b_ada).reshape(depth, n_vec, 6, D)
    lb_logits = hg_lb_logits.reshape(2 * depth, HG_KEY)

    in_cols = w_in.shape[2]
    widths = (HY_COLS, SC_COLS, HG_COLS, -(-SSD_COLS // LANES) * LANES)
    z_state = lambda *shape: jnp.zeros(shape, F32)

    for l in range(depth):
        ctx_out = l < depth - 1
        mod_lat, mod_ctx = mods[l, :B], mods[l, B:B + 1]
        w_in_b = jnp.pad(w_in[l], ((0, 0), (0, sum(widths) - in_cols))).astype(BF16)
        ssd_par = jnp.zeros((2 * SUBLANES, LANES), F32)
        ssd_par = ssd_par.at[:SUBLANES, :2 * SSD_HEADS].set(jnp.broadcast_to(ssd_a_log[l].reshape(1, -1), (SUBLANES, 2 * SSD_HEADS)))
        ssd_par = ssd_par.at[SUBLANES:, :2 * SSD_HEADS].set(jnp.broadcast_to(ssd_dt_bias[l].reshape(1, -1), (SUBLANES, 2 * SSD_HEADS)))
        d_skip = jnp.repeat(ssd_d[l], SSD_HEADDIM)
        filt_args = (hy_w1[l], hy_b1[l], hy_freq1[l], hy_w2[l], hy_b2[l], hy_freq2[l], hy_w3[l], hy_decay[l])

        def mixers(tokens, mod, seg_len, hg_state, ssd_state, want_out, pos=None):
            n_seq = tokens.shape[0] // seg_len
            hy, sc, hg, ssd = in_proj(tokens, mod, w_in_b, widths, rows_per_mod=mod_rows(mod, tokens), tm=512, pos=pos)
            seq = lambda a: a.reshape(n_seq, seg_len, a.shape[1])
            hg_y, hg_f, hg_b = hgrn_scan(seq(hg), lb_logits, hg_norm_g[l], *hg_state, layer=l)
            ssd3 = seq(ssd)
            xbc = ssd_prep(ssd3, ssd_conv_w[l], ssd_conv_b[l], col0=SSD_INNER)
            ssd_y, ssd_f, ssd_b = ssd_scan(ssd3, xbc, ssd_par, d_skip, ssd_norm_g[l], *ssd_state,
                                           z_blk=0, dt_blk=(SSD_INNER + SSD_XBC) // LANES)
            branches = None
            if want_out:
                x0c, hw = hy_prep(seq(hy), hy_conv_w[l], hy_conv_b[l])
                fu, den = hy_filter(seg_len, *filt_args)
                if seg_len == FFT_L:
                    hy_y = hyena_conv(x0c, hw, hyena_spectrum(fu, den), hy_bias[l])
                else:
                    hy_y = hyena_conv_short(x0c, hw, fu, den, hy_bias[l])
                sc_y = shortconv(seq(sc), sc_conv_w[l])
                flat = lambda a: a.reshape(tokens.shape[0], a.shape[2])
                branches = [flat(hy_y), flat(sc_y), flat(hg_y), flat(ssd_y)]
            return branches, (hg_f, hg_b), (ssd_f, ssd_b)

        def mod_rows(mod, tokens):
            return tokens.shape[0] // mod.shape[0]

        def finish(tokens, mod, branches, pos=None):
            rows = mod_rows(mod, tokens)
            t1 = merge_ln(tokens, mod, branches, w_gate[l].astype(BF16), b_gate[l], w_br[l].astype(BF16),
                          w_o[l].astype(BF16), ln1_g[l], ln1_b[l], rows_per_mod=rows, tm=512, pos=pos)
            return moe_ln(t1, mod, w_router, b_router, w_e1[l].astype(BF16), w_e3[l].astype(BF16),
                          w_e2[l].astype(BF16), ln2_g[l], ln2_b[l], rows_per_mod=rows, tm=1024)

        zero_hg = (z_state(B, HG_KEY, HG_KEY),) * 2
        zero_ssd = (z_state(B, SSD_GROUPS, LANES, LANES),) * 2
        br_ctx, hg_state, ssd_state = mixers(cx, mod_ctx, LC, zero_hg, zero_ssd, ctx_out)
        pos = pos_table if l == 0 else None
        br_lat, _, _ = mixers(lat, mod_lat, L, hg_state, ssd_state, True, pos=pos)
        lat = finish(lat, mod_lat, br_lat, pos=pos)
        if ctx_out:
            cx = finish(cx, mod_ctx, br_ctx)
    return lat.reshape(B, L, D)
```

```python
import functools
import math

import jax
import jax.numpy as jnp
import numpy as np
from jax import lax
from jax.experimental import pallas as pl
from jax.experimental.pallas import tpu as pltpu

F32 = jnp.float32
BF16 = jnp.bfloat16

N_EXPERTS = 16
N_EXPERT_GROUPS = 4
GROUP_SIZE = N_EXPERTS // N_EXPERT_GROUPS
DEPTH = 2
DEEPNORM_ALPHA = (2 * DEPTH) ** 0.25
LN_EPS = 1e-5
RMS_EPS = 1e-6

VMEM_LIMIT_BYTES = 52 * 1024 * 1024


def _layernorm_rows(z, g, b):
    mu = jnp.mean(z, axis=-1, keepdims=True)
    zc = z - mu
    var = jnp.mean(zc * zc, axis=-1, keepdims=True)
    return zc * lax.rsqrt(var + LN_EPS) * g + b


def _route_t(logits_t, bias_col):
    z = logits_t - jnp.max(logits_t, axis=0, keepdims=True)
    ex = jnp.exp(z)
    scores = ex / jnp.sum(ex, axis=0, keepdims=True)
    sel = scores + bias_col
    eidx = lax.broadcasted_iota(jnp.int32, sel.shape, 0)
    neg = jnp.float32(-jnp.inf)

    def first_argmax(m):
        top = jnp.max(m, axis=0, keepdims=True)
        return top, jnp.min(jnp.where(m == top, eidx, N_EXPERTS), axis=0, keepdims=True)

    best_score = None
    best_grp = None
    for g in range(N_EXPERT_GROUPS):
        m = jnp.where((eidx >= g * GROUP_SIZE) & (eidx < (g + 1) * GROUP_SIZE), sel, neg)
        t1, i1 = first_argmax(m)
        t2 = jnp.max(jnp.where(eidx == i1, neg, m), axis=0, keepdims=True)
        s = t1 + t2
        if g == 0:
            best_score, best_grp = s, jnp.zeros_like(i1)
        else:
            better = s > best_score
            best_score = jnp.where(better, s, best_score)
            best_grp = jnp.where(better, g, best_grp)
    lo = best_grp * GROUP_SIZE
    masked = jnp.where((eidx >= lo) & (eidx < lo + GROUP_SIZE), sel, neg)
    _, ia = first_argmax(masked)
    _, ib = first_argmax(jnp.where(eidx == ia, neg, masked))
    w = jnp.where((eidx == ia) | (eidx == ib), scores, 0.0)
    return w / jnp.sum(w, axis=0, keepdims=True), best_grp


MOE_CHUNK = 64
MOE_ARMS = (1024, 512, 256, 128, 64)
MOE_STEP_EXPERTS = 2
GATE_TERMS = 3
POS_LANE = GATE_TERMS * N_EXPERTS


def _moe_ln_kernel(x_ref, mod_ref, wr_ref, br_ref, w1_ref, w3_ref, w2_ref, lng_ref, lnb_ref,
                   o_ref, pt_scr, hc_scr, gc_scr, yc_scr, seg_smem):
    step = pl.program_id(1)
    TM, R = pt_scr.shape
    D = x_ref.shape[1]
    G = N_EXPERT_GROUPS

    @pl.when(step == 0)
    def _():
        h = _modulate(x_ref[...], mod_ref, MOD_SHIFT2, MOD_SCALE2)
        h_hi = h.astype(BF16)
        h_lo = (h - h_hi.astype(F32)).astype(BF16)
        r_hi = jnp.dot(h_hi, wr_ref[...], preferred_element_type=F32)
        r_lo = jnp.dot(h_lo, wr_ref[:, :LANES], preferred_element_type=F32)
        logits_t = (r_hi[:, :LANES] + r_hi[:, LANES:] + r_lo).T[:N_EXPERTS]
        gate_t, grp = _route_t(logits_t, br_ref[:, 0:1])

        gidx = lax.broadcasted_iota(jnp.int32, (SUBLANES, TM), 0)
        tok = lax.broadcasted_iota(jnp.int32, (SUBLANES, TM), 1)
        member = jnp.where(gidx == grp, 1.0, 0.0)
        incl = member
        shift = 1
        while shift < TM:
            incl = incl + jnp.where(tok >= shift, pltpu.roll(incl, shift, 1), 0.0)
            shift *= 2
        rank = incl - member
        offset = jnp.int32(0)
        pos_row = jnp.zeros((1, TM), F32)
        for g in range(G):
            n_chunks = (jnp.sum(member[g:g + 1, :]).astype(jnp.int32) + (MOE_CHUNK - 1)) // MOE_CHUNK
            seg_smem[g] = n_chunks
            seg_smem[G + g] = offset
            pos_row = pos_row + member[g:g + 1, :] * (rank[g:g + 1, :] + offset.astype(F32))
            offset = offset + n_chunks * MOE_CHUNK

        terms, rest = [], gate_t
        for _ in range(GATE_TERMS):
            part = rest.astype(BF16).astype(F32)
            terms.append(part)
            rest = rest - part
        pad = jnp.zeros((LANES - POS_LANE - 1, TM), F32)
        side = jnp.concatenate(terms + [pos_row, pad], axis=0).T
        pos_col = side[:, POS_LANE:POS_LANE + 1].astype(jnp.int32)
        pt_scr[...] = jnp.where(lax.broadcasted_iota(jnp.int32, (TM, R), 1) == pos_col, 1.0, 0.0).astype(BF16)
        p = jnp.where(lax.broadcasted_iota(jnp.int32, (R, TM), 0) == pos_row.astype(jnp.int32), 1.0, 0.0).astype(BF16)
        gathered = jnp.dot(p, jnp.concatenate([h_hi, side.astype(BF16)], axis=1), preferred_element_type=F32)
        hc_scr[...] = gathered[:, :D].astype(BF16)
        gc_scr[...] = gathered[:, D:]
        yc_scr[...] = jnp.zeros_like(yc_scr)

    g = step // (GROUP_SIZE // MOE_STEP_EXPERTS)
    n_chunks = seg_smem[g]
    base = seg_smem[G + g]

    def expert_rows(start, size):
        rows = pl.ds(pl.multiple_of(start, MOE_CHUNK), size)
        hc = hc_scr[rows, :]
        gc = gc_scr[rows, :]
        lane = lax.broadcasted_iota(jnp.int32, gc.shape, 1)
        acc = None
        for k in range(MOE_STEP_EXPERTS):
            e = step * MOE_STEP_EXPERTS + k
            a = jnp.dot(hc, w1_ref[k], preferred_element_type=F32)
            b = jnp.dot(hc, w3_ref[k], preferred_element_type=F32)
            mid = (a * jax.nn.sigmoid(a) * b).astype(BF16)
            y = jnp.dot(mid, w2_ref[k], preferred_element_type=F32)
            gcol = jnp.sum(jnp.where((lane % N_EXPERTS == e) & (lane < POS_LANE), gc, 0.0), axis=-1, keepdims=True)
            acc = gcol * y if acc is None else acc + gcol * y
        yc_scr[rows, :] += acc

    for size in MOE_ARMS:
        if size > TM:
            continue
        taken = (n_chunks & (size // MOE_CHUNK)) != 0

        @pl.when(taken)
        def _(base=base, size=size):
            expert_rows(base, size)

        base = base + jnp.where(taken, size, 0)

    @pl.when(step == N_EXPERTS // MOE_STEP_EXPERTS - 1)
    def _():
        moe = jnp.dot(pt_scr[...], yc_scr[...].astype(BF16), preferred_element_type=F32)
        z = DEEPNORM_ALPHA * x_ref[...] + mod_ref[0, MOD_GATE2:MOD_GATE2 + 1, :] * moe
        o_ref[...] = _layernorm_rows(z, lng_ref[...], lnb_ref[...])


def moe_ln(x, mod, w_router, b_router, w1, w3, w2, ln_g, ln_b, *, rows_per_mod, tm):
    T, D = x.shape
    E, _, FF = w1.shape
    assert T % tm == 0 and rows_per_mod % tm == 0 and tm <= MOE_ARMS[0] and tm % MOE_CHUNK == 0
    tiles_per_mod = rows_per_mod // tm
    sorted_rows = tm + N_EXPERT_GROUPS * MOE_CHUNK
    wr_hi = w_router.astype(BF16)
    wr_lo = (w_router - wr_hi.astype(F32)).astype(BF16)
    lane_pad = lambda w: jnp.pad(w, ((0, 0), (0, LANES - E)))
    wr_split = jnp.concatenate([lane_pad(wr_hi), lane_pad(wr_lo)], axis=1)
    br_col = jnp.broadcast_to(b_router.reshape(E, 1), (E, LANES))
    return pl.pallas_call(
        _moe_ln_kernel,
        grid=(T // tm, E // MOE_STEP_EXPERTS),
        in_specs=[
            pl.BlockSpec((tm, D), lambda i, e: (i, 0)),
            pl.BlockSpec((1, 6, D), lambda i, e: (i // tiles_per_mod, 0, 0)),
            pl.BlockSpec((D, 2 * LANES), lambda i, e: (0, 0)),
            pl.BlockSpec((E, LANES), lambda i, e: (0, 0)),
            pl.BlockSpec((MOE_STEP_EXPERTS, D, FF), lambda i, e: (e, 0, 0)),
            pl.BlockSpec((MOE_STEP_EXPERTS, D, FF), lambda i, e: (e, 0, 0)),
            pl.BlockSpec((MOE_STEP_EXPERTS, FF, D), lambda i, e: (e, 0, 0)),
            pl.BlockSpec((1, D), lambda i, e: (0, 0)),
            pl.BlockSpec((1, D), lambda i, e: (0, 0)),
        ],
        out_specs=pl.BlockSpec((tm, D), lambda i, e: (i, 0)),
        out_shape=jax.ShapeDtypeStruct((T, D), F32),
        scratch_shapes=[
            pltpu.VMEM((tm, sorted_rows), BF16),
            pltpu.VMEM((sorted_rows, D), BF16),
            pltpu.VMEM((sorted_rows, LANES), F32),
            pltpu.VMEM((sorted_rows, D), F32),
            pltpu.SMEM((2 * N_EXPERT_GROUPS,), jnp.int32),
        ],
        compiler_params=pltpu.CompilerParams(
            dimension_semantics=("parallel", "arbitrary"), vmem_limit_bytes=VMEM_LIMIT_BYTES),
        name="moe_ln",
    )(x, mod, wr_split, br_col, w1, w3, w2, ln_g.reshape(1, D), ln_b.reshape(1, D))


def _ada_kernel(c_ref, w_ref, b_ref, o_ref):
    c = c_ref[...]
    s = c * jax.nn.sigmoid(c)
    o_ref[0] = jnp.dot(s, w_ref[0], precision=lax.Precision.HIGHEST, preferred_element_type=F32) + b_ref[0]


def ada_mod(cvecs, w_ada, b_ada, *, tn=1536):
    R, D = cvecs.shape
    depth, _, N = w_ada.shape
    assert N % tn == 0
    return pl.pallas_call(
        _ada_kernel,
        grid=(depth, N // tn),
        in_specs=[
            pl.BlockSpec((R, D), lambda l, j: (0, 0)),
            pl.BlockSpec((1, D, tn), lambda l, j: (l, 0, j)),
            pl.BlockSpec((1, 1, tn), lambda l, j: (l, 0, j)),
        ],
        out_specs=pl.BlockSpec((1, R, tn), lambda l, j: (l, 0, j)),
        out_shape=jax.ShapeDtypeStruct((depth, R, N), F32),
        compiler_params=pltpu.CompilerParams(
            dimension_semantics=("parallel", "parallel"), vmem_limit_bytes=VMEM_LIMIT_BYTES),
        name="ada_mod",
    )(cvecs, w_ada, b_ada.reshape(depth, 1, N))


MOD_SHIFT1, MOD_SCALE1, MOD_GATE1, MOD_SHIFT2, MOD_SCALE2, MOD_GATE2 = range(6)


def _modulate(x, mod_ref, shift_row, scale_row):
    return x * (1.0 + mod_ref[0, scale_row:scale_row + 1, :]) + mod_ref[0, shift_row:shift_row + 1, :]


HALO = 8
BRANCH_W = 256
HY_COLS = 3 * BRANCH_W
SSD_XBC_COLS = 512
IN_CONV_COLS = HY_COLS + 2 * BRANCH_W + SSD_XBC_COLS


def _conv3(u, w_ref, tm):
    return (u[HALO - 1:HALO - 1 + tm] * w_ref[0:1, :] + u[HALO:HALO + tm] * w_ref[1:2, :]
            + u[HALO + 1:HALO + 1 + tm] * w_ref[2:3, :])


def _in_proj_kernel(x_ref, xp_ref, xn_ref, mod_ref, w_ref, hcw_ref, hcb_ref, scw_ref, xcw_ref, xcb_ref, *rest,
                    has_pos, tiles_per_seq):
    if has_pos:
        pos_ref, pp_ref, pn_ref = rest[:3]
        rest = rest[3:]
    x0c_ref, hw_ref, scy_ref, xbc_ref, hg_ref, zdt_ref = rest
    tm = x_ref.shape[0]
    t = pl.program_id(0) % tiles_per_seq
    keep_prev = jnp.where(t > 0, 1.0, 0.0)
    keep_next = jnp.where(t < tiles_per_seq - 1, 1.0, 0.0)

    def tokens(main_ref, prev_ref, next_ref):
        return main_ref[...], prev_ref[...], next_ref[...]

    xm, xp, xn = tokens(x_ref, xp_ref, xn_ref)
    if has_pos:
        pm, pp, pn = tokens(pos_ref, pp_ref, pn_ref)
        xm, xp, xn = xm + pm, xp + pp, xn + pn
    mod = lambda v: _modulate(v, mod_ref, MOD_SHIFT1, MOD_SCALE1)
    h_main = mod(xm).astype(BF16)
    h_all = jnp.concatenate([mod(xp) * keep_prev, mod(xm), mod(xn) * keep_next], axis=0).astype(BF16)
    uc = jnp.dot(h_all, w_ref[:, :IN_CONV_COLS], preferred_element_type=F32)
    ur = jnp.dot(h_main, w_ref[:, IN_CONV_COLS:], preferred_element_type=F32)

    hy = _conv3(uc[:, :HY_COLS], hcw_ref, tm) + hcb_ref[...]
    x0c_ref[...] = hy[:, :BRANCH_W]
    hw_ref[...] = hy[:, BRANCH_W:2 * BRANCH_W] * hy[:, 2 * BRANCH_W:]
    cg_xs = uc[:, HY_COLS:HY_COLS + BRANCH_W] * uc[:, HY_COLS + BRANCH_W:HY_COLS + 2 * BRANCH_W]
    scy_ref[...] = ur[:, :BRANCH_W] * _conv3(cg_xs, scw_ref, tm)
    xa = _conv3(uc[:, HY_COLS + 2 * BRANCH_W:], xcw_ref, tm) + xcb_ref[...]
    xbc_ref[...] = xa * jax.nn.sigmoid(xa)
    n_hg = hg_ref.shape[1]
    hg_ref[...] = ur[:, BRANCH_W:BRANCH_W + n_hg]
    zdt_ref[...] = ur[:, BRANCH_W + n_hg:]


def in_proj(x, mod, w, conv, out_widths, *, seq_len, rows_per_mod, tm, pos=None):
    T, D = x.shape
    assert T % tm == 0 and seq_len % tm == 0 and rows_per_mod % tm == 0 and tm % HALO == 0
    tiles_per_seq = seq_len // tm
    tiles_per_mod = rows_per_mod // tm
    hb = tm // HALO
    last_blk = T // HALO - 1
    hy_w, hy_b, sc_w, xbc_w, xbc_b = conv
    row = lambda v: v.reshape(1, -1)
    const = lambda a: pl.BlockSpec(a.shape, lambda i: (0,) * a.ndim)
    consts = [hy_w, row(hy_b), sc_w, xbc_w, row(xbc_b)]
    in_specs = [
        pl.BlockSpec((tm, D), lambda i: (i, 0)),
        pl.BlockSpec((HALO, D), lambda i: (jnp.maximum(i * hb - 1, 0), 0)),
        pl.BlockSpec((HALO, D), lambda i: (jnp.minimum((i + 1) * hb, last_blk), 0)),
        pl.BlockSpec((1, 6, D), lambda i: (i // tiles_per_mod, 0, 0)),
        pl.BlockSpec(w.shape, lambda i: (0, 0), pipeline_mode=pl.Buffered(1)),
    ] + [const(a) for a in consts]
    args = [x, x, x, mod, w] + consts
    if pos is not None:
        assert pos.shape[0] == seq_len
        last_pos = seq_len // HALO - 1
        in_specs += [
            pl.BlockSpec((tm, D), lambda i: (i % tiles_per_seq, 0)),
            pl.BlockSpec((HALO, D), lambda i: (jnp.maximum((i % tiles_per_seq) * hb - 1, 0), 0)),
            pl.BlockSpec((HALO, D), lambda i: (jnp.minimum((i % tiles_per_seq + 1) * hb, last_pos), 0)),
        ]
        args += [pos, pos, pos]
    return pl.pallas_call(
        functools.partial(_in_proj_kernel, has_pos=pos is not None, tiles_per_seq=tiles_per_seq),
        grid=(T // tm,),
        in_specs=in_specs,
        out_specs=[pl.BlockSpec((tm, n), lambda i: (i, 0)) for n in out_widths],
        out_shape=[jax.ShapeDtypeStruct((T, n), F32) for n in out_widths],
        compiler_params=pltpu.CompilerParams(
            dimension_semantics=("parallel",), vmem_limit_bytes=VMEM_LIMIT_BYTES),
        name="in_proj",
    )(*args)


def _pos_spec(pos, rows_per_mod, tm):
    assert pos.shape[0] == rows_per_mod
    tiles = rows_per_mod // tm
    return pl.BlockSpec((tm, pos.shape[1]), lambda i: (i % tiles, 0))


N_BRANCH = 4


def _merge_kernel(x_ref, mod_ref, hy_ref, sc_ref, hg_ref, ssd_ref, wg_ref, bg_ref, wbr_ref, wo_ref,
                  lng_ref, lnb_ref, *rest, has_pos):
    x = x_ref[...] + rest[0][...] if has_pos else x_ref[...]
    o_ref = rest[-1]
    D = x.shape[1]
    h = _modulate(x, mod_ref, MOD_SHIFT1, MOD_SCALE1).astype(BF16)
    y = None
    for k, br_ref in enumerate((hy_ref, sc_ref, hg_ref, ssd_ref)):
        gate = jax.nn.sigmoid(
            jnp.dot(h, wg_ref[:, k * D:(k + 1) * D], preferred_element_type=F32) + bg_ref[:, k * D:(k + 1) * D])
        term = gate * jnp.dot(br_ref[...].astype(BF16), wbr_ref[k], preferred_element_type=F32)
        y = term if y is None else y + term
    y = jnp.dot(y.astype(BF16), wo_ref[...], preferred_element_type=F32)
    z = DEEPNORM_ALPHA * x + mod_ref[0, MOD_GATE1:MOD_GATE1 + 1, :] * y
    o_ref[...] = _layernorm_rows(z, lng_ref[...], lnb_ref[...])


def merge_ln(x, mod, branches, w_gate, b_gate, w_br, w_o, ln_g, ln_b, *, rows_per_mod, tm, pos=None):
    T, D = x.shape
    BW = branches[0].shape[1]
    assert T % tm == 0 and rows_per_mod % tm == 0
    tiles_per_mod = rows_per_mod // tm
    const = lambda shape: pl.BlockSpec(shape, lambda i: (0,) * len(shape), pipeline_mode=pl.Buffered(1))
    extra = [] if pos is None else [pos]
    return pl.pallas_call(
        functools.partial(_merge_kernel, has_pos=pos is not None),
        grid=(T // tm,),
        in_specs=[
            pl.BlockSpec((tm, D), lambda i: (i, 0)),
            pl.BlockSpec((1, 6, D), lambda i: (i // tiles_per_mod, 0, 0)),
        ] + [pl.BlockSpec((tm, BW), lambda i: (i, 0))] * N_BRANCH + [
            const((D, N_BRANCH * D)), const((1, N_BRANCH * D)), const((N_BRANCH, BW, D)), const((D, D)),
            const((1, D)), const((1, D)),
        ] + [_pos_spec(p, rows_per_mod, tm) for p in extra],
        out_specs=pl.BlockSpec((tm, D), lambda i: (i, 0)),
        out_shape=jax.ShapeDtypeStruct((T, D), F32),
        compiler_params=pltpu.CompilerParams(
            dimension_semantics=("parallel",), vmem_limit_bytes=VMEM_LIMIT_BYTES),
        name="merge_ln",
    )(x, mod, *branches, w_gate, b_gate.reshape(1, -1), w_br, w_o, ln_g.reshape(1, D), ln_b.reshape(1, D), *extra)


SSD_HEADS = 4
SSD_HEADDIM = 64
SSD_STATE = 64
SSD_GROUPS = 2
SSD_INNER = SSD_HEADS * SSD_HEADDIM
LANES = 128
NEG_INF = float("-inf")


SCAN_BATCH = 2


def _scan_chunk_index(ph, j, n):
    return j + (1 - ph) * (n - 1 - 2 * j)


def _ssd_kernel(z_ref, xbc_ref, dt_ref, par_ref, dskip_ref, ng_ref, sf0_ref, sb0_ref,
                y_ref, sf_ref, sb_ref, yb_scr, st_scr, *, n_chunks):
    ph = pl.program_id(1)
    j = pl.program_id(2)
    is_fwd = ph == 1
    c = _scan_chunk_index(ph, j, n_chunks)
    NB, R = xbc_ref.shape[0], xbc_ref.shape[1]

    @pl.when((j == 0) & is_fwd)
    def _():
        st_scr[...] = sf0_ref[...]

    @pl.when((j == 0) & jnp.logical_not(is_fwd))
    def _():
        st_scr[...] = sb0_ref[...]

    row = lax.broadcasted_iota(jnp.int32, (R, R), 0)
    col = lax.broadcasted_iota(jnp.int32, (R, R), 1)
    mask = jnp.where(is_fwd, row - col, col - row) >= 0
    tri = jnp.where(mask, 1.0, 0.0).astype(BF16)
    head_of_lane = lax.broadcasted_iota(jnp.int32, (LANES, SSD_INNER), 1) // SSD_HEADDIM
    expand = jnp.where(lax.broadcasted_iota(jnp.int32, (LANES, SSD_INNER), 0) == head_of_lane, 1.0, 0.0).astype(BF16)
    ys = [_ssd_chunk(bi, is_fwd, mask, tri, expand, xbc_ref, dt_ref, par_ref, st_scr) for bi in range(NB)]

    @pl.when(jnp.logical_not(is_fwd))
    def _():
        for bi in range(NB):
            yb_scr[bi, c] = ys[bi]

    @pl.when(is_fwd)
    def _():
        for bi in range(NB):
            z = z_ref[bi]
            xs = xbc_ref[bi, :, :SSD_INNER]
            yt = (ys[bi] + yb_scr[bi, c] + xs * dskip_ref[...]) * (z * jax.nn.sigmoid(z))
            ms = jnp.mean(yt * yt, axis=-1, keepdims=True)
            y_ref[bi] = yt * lax.rsqrt(ms + RMS_EPS) * ng_ref[...]

    @pl.when((j == n_chunks - 1) & is_fwd)
    def _():
        sf_ref[...] = st_scr[...]

    @pl.when((j == n_chunks - 1) & jnp.logical_not(is_fwd))
    def _():
        sb_ref[...] = st_scr[...]


def _ssd_chunk(bi, is_fwd, mask, tri, expand, xbc_ref, dt_ref, par_ref, st_scr):
    R = xbc_ref.shape[1]
    dsel = lambda v: jnp.where(is_fwd, v, pltpu.roll(v, LANES - SSD_HEADS, 1))
    dt_raw = dsel(dt_ref[bi]) + dsel(par_ref[8:16, :])[0:1, :]
    dt = jnp.maximum(dt_raw, 0.0) + jnp.log(1.0 + jnp.exp(-jnp.abs(dt_raw)))
    a = -jnp.exp(dsel(par_ref[0:8, :])[0:1, :]) * dt
    cs = _split_dot(tri, a, terms=3)
    cs_t = cs.T
    total = jnp.sum(a, axis=0, keepdims=True)

    spread = lambda v: jnp.dot(v.astype(BF16), expand, preferred_element_type=F32)
    dt_bc = spread(dt)
    in_decay = spread(jnp.exp(cs))
    out_decay = spread(jnp.exp(total - cs))
    e_total = jnp.exp(total)

    xbc = xbc_ref[bi]
    xdt = xbc[:, :SSD_INNER] * dt_bc
    xdt_b = xdt.astype(BF16)
    xout_b = (xdt * out_decay).astype(BF16)
    bm = xbc[:, SSD_INNER:SSD_INNER + LANES]
    cm_b = xbc[:, SSD_INNER + LANES:SSD_INNER + 2 * LANES].astype(BF16)
    lane = lax.broadcasted_iota(jnp.int32, (R, LANES), 1)
    lo_half = lane < SSD_HEADDIM
    lo_half_st = lax.broadcasted_iota(jnp.int32, (LANES, LANES), 1) < SSD_HEADDIM
    y_groups = []
    for g in range(SSD_GROUPS):
        h0, h1 = 2 * g, 2 * g + 1
        cols = slice(g * LANES, (g + 1) * LANES)
        bm_g = jnp.where((lane >= g * SSD_STATE) & (lane < (g + 1) * SSD_STATE), bm, 0.0).astype(BF16)
        G = lax.dot_general(cm_b, bm_g, (((1,), (1,)), ((), ())), preferred_element_type=F32)
        st_old = st_scr[bi, g]
        y_off = in_decay[:, cols] * jnp.dot(cm_b, st_old.astype(BF16), preferred_element_type=F32)
        y_diag = []
        for h in (h0, h1):
            decay = jnp.exp(jnp.where(mask, cs[:, h:h + 1] - cs_t[h:h + 1, :], NEG_INF))
            y_diag.append(jnp.dot((G * decay).astype(BF16), xdt_b[:, cols], preferred_element_type=F32))
        y_groups.append(jnp.where(lo_half, y_diag[0], y_diag[1]) + y_off)
        upd = lax.dot_general(bm_g, xout_b[:, cols], (((0,), (0,)), ((), ())), preferred_element_type=F32)
        st_scr[bi, g] = jnp.where(lo_half_st, e_total[:, h0:h0 + 1], e_total[:, h1:h1 + 1]) * st_old + upd
    return jnp.concatenate(y_groups, axis=1)


def ssd_scan(ssd, xbc, par, d_skip, norm_g, s_f0, s_b0, *, z_blk, dt_blk, chunk=256):
    B, L, _ = ssd.shape
    R = min(chunk, L)
    assert L % R == 0 and B % SCAN_BATCH == 0
    n = L // R
    NB = SCAN_BATCH
    cidx = lambda ph, j: _scan_chunk_index(ph, j, n)
    st_spec = pl.BlockSpec((NB, SSD_GROUPS, LANES, LANES), lambda b, ph, j: (b, 0, 0, 0))
    st_shape = jax.ShapeDtypeStruct((B, SSD_GROUPS, LANES, LANES), F32)
    return pl.pallas_call(
        functools.partial(_ssd_kernel, n_chunks=n),
        grid=(B // NB, 2, n),
        in_specs=[
            pl.BlockSpec((NB, R, SSD_INNER), lambda b, ph, j: (b, cidx(ph, j), z_blk)),
            pl.BlockSpec((NB, R, xbc.shape[2]), lambda b, ph, j: (b, cidx(ph, j), 0)),
            pl.BlockSpec((NB, R, LANES), lambda b, ph, j: (b, cidx(ph, j), dt_blk)),
            pl.BlockSpec((16, LANES), lambda b, ph, j: (0, 0)),
            pl.BlockSpec((1, SSD_INNER), lambda b, ph, j: (0, 0)),
            pl.BlockSpec((1, SSD_INNER), lambda b, ph, j: (0, 0)),
            st_spec, st_spec,
        ],
        out_specs=[pl.BlockSpec((NB, R, SSD_INNER), lambda b, ph, j: (b, ph * j, 0)), st_spec, st_spec],
        out_shape=[jax.ShapeDtypeStruct((B, L, SSD_INNER), F32), st_shape, st_shape],
        scratch_shapes=[pltpu.VMEM((NB, n, R, SSD_INNER), F32), pltpu.VMEM((NB, SSD_GROUPS, LANES, LANES), F32)],
        compiler_params=pltpu.CompilerParams(
            dimension_semantics=("parallel", "arbitrary", "arbitrary"), vmem_limit_bytes=VMEM_LIMIT_BYTES),
        name="ssd_scan",
    )(ssd, xbc, ssd, par, d_skip.reshape(1, SSD_INNER), norm_g.reshape(1, SSD_INNER), s_f0, s_b0)


HG_HEADS = 4
HG_DK = 64
HG_DV = 64
HG_KEY = HG_HEADS * HG_DK
HG_SUB = 16
HG_CHUNK = 64
HG_SUBS_PER_CHUNK = HG_CHUNK // HG_SUB


def _split_dot(mask_b, x, terms=2):
    out = None
    for _ in range(terms):
        part = x.astype(BF16)
        prod = jnp.dot(mask_b, part, preferred_element_type=F32)
        out = prod if out is None else out + prod
        x = x - part.astype(F32)
    return out


def _hgrn_kernel(q_ref, f_ref, v_ref, g_ref, lbl_ref, ng_ref, sf0_ref, sb0_ref,
                 y_ref, sf_ref, sb_ref, ob_scr, st_scr, *, n_blocks, layer, depth):
    ph = pl.program_id(1)
    j = pl.program_id(2)
    is_fwd = ph == 1
    c = _scan_chunk_index(ph, j, n_blocks)
    NB, R = q_ref.shape[0], q_ref.shape[1]
    W = HG_KEY

    @pl.when((j == 0) & is_fwd)
    def _():
        st_scr[...] = sf0_ref[...]

    @pl.when((j == 0) & jnp.logical_not(is_fwd))
    def _():
        st_scr[...] = sb0_ref[...]

    def lower_bound(d):
        x = lbl_ref[d * depth:(d + 1) * depth, :]
        e = jnp.exp(x - jnp.max(x, axis=0, keepdims=True))
        p = e / jnp.sum(e, axis=0, keepdims=True)
        return jnp.sum(p[1:layer + 1, :], axis=0, keepdims=True) if layer > 0 else jnp.zeros((1, W), F32)

    lb = jnp.where(is_fwd, lower_bound(0), lower_bound(1))
    row = lax.broadcasted_iota(jnp.int32, (R, R), 0)
    col = lax.broadcasted_iota(jnp.int32, (R, R), 1)
    dirge = jnp.where(is_fwd, row - col, col - row) >= 0
    same_sub = (row // HG_SUB) == (col // HG_SUB)
    same_chunk = (row // HG_CHUNK) == (col // HG_CHUNK)
    one = lambda m: jnp.where(m, 1.0, 0.0).astype(BF16)
    dist = jnp.where(is_fwd, row // HG_SUB - col // HG_SUB, col // HG_SUB - row // HG_SUB)
    masks = dict(
        cum16=one(same_sub & dirge), tot16=one(same_sub), cum64=one(same_chunk & dirge), tot64=one(same_chunk),
        dist=[same_sub & dirge] + [same_chunk & (dist == d) for d in range(1, HG_SUBS_PER_CHUNK)])
    lane = lax.broadcasted_iota(jnp.int32, (R, W), 1)
    head_mask = [(lane // HG_DK) == h for h in range(HG_HEADS)]
    blocks = [_hgrn_block(bi, is_fwd, lb, masks, head_mask, q_ref, f_ref, v_ref) for bi in range(NB)]
    srow = lax.broadcasted_iota(jnp.int32, (W, W), 0)
    scol = lax.broadcasted_iota(jnp.int32, (W, W), 1)
    bd_mask = (srow // HG_DV) == (scol // HG_DK)

    def through_state(bi, reverse):
        o_intra, qd, ke, v_b, tot64 = blocks[bi]
        n_chunks = R // HG_CHUNK
        st = st_scr[bi]
        parts = [None] * n_chunks
        for i in (reversed(range(n_chunks)) if reverse else range(n_chunks)):
            rows = slice(i * HG_CHUNK, (i + 1) * HG_CHUNK)
            parts[i] = lax.dot_general(qd[rows], st.astype(BF16), (((1,), (1,)), ((), ())), preferred_element_type=F32)
            upd = lax.dot_general(v_b[rows], ke[rows], (((0,), (0,)), ((), ())), preferred_element_type=F32)
            st = jnp.exp(tot64[i * HG_CHUNK:i * HG_CHUNK + 1, :]) * st + jnp.where(bd_mask, upd, 0.0)
        st_scr[bi] = st
        return o_intra + jnp.concatenate(parts, axis=0)

    @pl.when(jnp.logical_not(is_fwd))
    def _():
        for bi in range(NB):
            ob_scr[bi, c] = through_state(bi, reverse=True)

    @pl.when(is_fwd)
    def _():
        for bi in range(NB):
            ot = through_state(bi, reverse=False) + ob_scr[bi, c]
            sq = ot * ot
            ms = jnp.zeros((R, W), F32)
            for h in range(HG_HEADS):
                s = jnp.sum(jnp.where(head_mask[h], sq, 0.0), axis=-1, keepdims=True) * (1.0 / HG_DV)
                ms = jnp.where(head_mask[h], s, ms)
            g = g_ref[bi]
            y_ref[bi] = ot * lax.rsqrt(ms + RMS_EPS) * ng_ref[...] * (g * jax.nn.sigmoid(g))

    @pl.when((j == n_blocks - 1) & is_fwd)
    def _():
        sf_ref[...] = st_scr[...]

    @pl.when((j == n_blocks - 1) & jnp.logical_not(is_fwd))
    def _():
        sb_ref[...] = st_scr[...]


def _hgrn_block(bi, is_fwd, lb, masks, head_mask, q_ref, f_ref, v_ref):
    R = q_ref.shape[1]
    W = HG_KEY
    sig = jax.nn.sigmoid(f_ref[bi])
    logf = jnp.log(lb + (1.0 - lb) * sig)
    k = (1.0 - lb) * (1.0 - sig)
    q = q_ref[bi] * (HG_DK ** -0.5)
    v_b = v_ref[bi].astype(BF16)
    logf_hi = logf.astype(BF16)
    logf_lo = (logf - logf_hi.astype(F32)).astype(BF16)
    msum = lambda m: (jnp.dot(m, logf_hi, preferred_element_type=F32) + jnp.dot(m, logf_lo, preferred_element_type=F32))
    b16 = msum(masks["cum16"])
    tot16 = msum(masks["tot16"])
    b64 = msum(masks["cum64"])
    tot64 = msum(masks["tot64"])

    shifted = lambda s: jnp.where(is_fwd, pltpu.roll(tot16, R - s, 0), pltpu.roll(tot16, s, 0))
    nx1 = shifted(HG_SUB)
    nx2 = nx1 + shifted(2 * HG_SUB)
    k_end16 = k * jnp.exp(tot16 - b16)
    keys = [(k * jnp.exp(-b16)).astype(BF16), k_end16.astype(BF16),
            (k_end16 * jnp.exp(nx1)).astype(BF16), (k_end16 * jnp.exp(nx2)).astype(BF16)]
    q16 = q * jnp.exp(b16)
    qm =jnp.concatenate([jnp.where(m, q16, 0.0) for m in head_mask], axis=0).astype(BF16)
    att = [jnp.zeros((R, R), F32)] * HG_HEADS
    for d in range(HG_SUBS_PER_CHUNK):
        sc = lax.dot_general(qm, keys[d], (((1,), (1,)), ((), ())), preferred_element_type=F32)
        att = [jnp.where(masks["dist"][d], sc[h * R:(h + 1) * R], att[h]) for h in range(HG_HEADS)]
    o_all = jnp.dot(jnp.concatenate(att, axis=0).astype(BF16), v_b, preferred_element_type=F32)
    o = jnp.zeros((R, W), F32)
    for h in range(HG_HEADS):
        o = jnp.where(head_mask[h], o_all[h * R:(h + 1) * R], o)

    return o, (q * jnp.exp(b64)).astype(BF16), (k * jnp.exp(tot64 - b64)).astype(BF16), v_b, tot64


def hgrn_scan(hg, lb_logits, norm_g, s_f0, s_b0, *, layer, block=256):
    B, L, _ = hg.shape
    W = HG_KEY
    R = min(block, L)
    assert L % R == 0 and R % HG_CHUNK == 0 and B % SCAN_BATCH == 0
    n = L // R
    NB = SCAN_BATCH
    depth = lb_logits.shape[0] // 2
    cidx = lambda ph, j: _scan_chunk_index(ph, j, n)
    col = lambda blk: pl.BlockSpec((NB, R, W), lambda b, ph, j: (b, cidx(ph, j), blk))
    st_spec = pl.BlockSpec((NB, W, W), lambda b, ph, j: (b, 0, 0))
    st_shape = jax.ShapeDtypeStruct((B, W, W), F32)
    return pl.pallas_call(
        functools.partial(_hgrn_kernel, n_blocks=n, layer=layer, depth=depth),
        grid=(B // NB, 2, n),
        in_specs=[
            col(0),
            pl.BlockSpec((NB, R, W), lambda b, ph, j: (b, cidx(ph, j), 2 - ph)),
            col(3), col(4),
            pl.BlockSpec(lb_logits.shape, lambda b, ph, j: (0, 0)),
            pl.BlockSpec((1, W), lambda b, ph, j: (0, 0)),
            st_spec, st_spec,
        ],
        out_specs=[pl.BlockSpec((NB, R, W), lambda b, ph, j: (b, ph * j, 0)), st_spec, st_spec],
        out_shape=[jax.ShapeDtypeStruct((B, L, W), F32), st_shape, st_shape],
        scratch_shapes=[pltpu.VMEM((NB, n, R, W), F32), pltpu.VMEM((NB, W, W), F32)],
        compiler_params=pltpu.CompilerParams(
            dimension_semantics=("parallel", "arbitrary", "arbitrary"), vmem_limit_bytes=VMEM_LIMIT_BYTES),
        name="hgrn_scan",
    )(hg, hg, hg, hg, lb_logits, jnp.tile(norm_g, HG_HEADS).reshape(1, W), s_f0, s_b0)


HY_WIDTH = 256
HY_BANDS = 16
HY_HIDDEN = 64
SUBLANES = 8


def _dot_hi(a, b):
    return jnp.dot(a, b, precision=lax.Precision.HIGHEST, preferred_element_type=F32)


def _hy_filter_kernel(bands_ref, w1t_ref, w1c_ref, w1s_ref, b1_ref, fr1_ref, w2_ref, b2_ref, fr2_ref, w3_ref,
                      decay_ref, f_ref, den_ref, *, L):
    i = pl.program_id(0)
    TR = f_ref.shape[1]
    m = i * TR + lax.broadcasted_iota(jnp.int32, (TR, LANES), 0)
    total = None
    for half in range(2):
        idx = m if half == 0 else jnp.where(m == 0, 0, L - m)
        idx_f = idx.astype(F32)
        t = idx_f * (1.0 / (L - 1))
        ang = ((2.0 * math.pi / L) * idx_f) * bands_ref[...]
        pre = (t[:, :HY_HIDDEN] * w1t_ref[...] + _dot_hi(jnp.cos(ang), w1c_ref[...])
               - _dot_hi(jnp.sin(ang), w1s_ref[...]) + b1_ref[...])
        h = jnp.sin(fr1_ref[...] * pre)
        h = jnp.sin(fr2_ref[...] * (_dot_hi(h, w2_ref[...]) + b2_ref[...]))
        h = _dot_hi(h, w3_ref[:, half * HY_WIDTH:(half + 1) * HY_WIDTH])
        tw = jnp.concatenate([t, t], axis=1)
        filt = h * jnp.exp(-tw * jnp.abs(decay_ref[...]))
        if half == 1:
            filt = jnp.where(jnp.concatenate([m, m], axis=1) == 0, 0.0, filt)
        f_ref[half] = filt
        s = jnp.sum(jnp.abs(filt), axis=0, keepdims=True)
        total = s if total is None else total + s

    @pl.when(i == 0)
    def _():
        den_ref[...] = jnp.zeros_like(den_ref)

    den_ref[...] += total


def hy_filter(L, w1, b1, fr1, w2, b2, fr2, w3, decay, *, tr=512):
    tr = min(tr, L)
    bands = np.zeros((1, LANES), np.float32)
    bands[0, :HY_BANDS] = np.linspace(1e-4, HY_BANDS - 1, HY_BANDS, dtype=np.float32)
    pad_rows = lambda w: jnp.pad(w, ((0, LANES - HY_BANDS), (0, 0)))
    row = lambda v: v.reshape(1, -1)
    args = (jnp.asarray(bands), w1[0:1], pad_rows(w1[1:1 + HY_BANDS]), pad_rows(w1[1 + HY_BANDS:]), row(b1), row(fr1),
            w2, row(b2), row(fr2), w3, row(decay))
    return pl.pallas_call(
        functools.partial(_hy_filter_kernel, L=L),
        grid=(L // tr,),
        in_specs=[pl.BlockSpec(a.shape, lambda i: (0, 0)) for a in args],
        out_specs=[pl.BlockSpec((2, tr, HY_WIDTH), lambda i: (0, i, 0)), pl.BlockSpec((1, HY_WIDTH), lambda i: (0, 0))],
        out_shape=[jax.ShapeDtypeStruct((2, L, HY_WIDTH), F32), jax.ShapeDtypeStruct((1, HY_WIDTH), F32)],
        compiler_params=pltpu.CompilerParams(
            dimension_semantics=("arbitrary",), vmem_limit_bytes=VMEM_LIMIT_BYTES),
        name="hy_filter",
    )(*args)


FFT_N1 = 64
FFT_N2 = 128
FFT_N = FFT_N1 * FFT_N2
FFT_L = FFT_N // 2
FFT_N1_NZ = FFT_N1 // 2
FFT_UNROLL_N1_STAGE = 2
FFT_UNROLL_N2_STAGE = 8


@functools.lru_cache(maxsize=None)
def _fft_constants():
    eye = np.eye(SUBLANES)
    k1 = np.arange(FFT_N1)[:, None]
    n1 = np.arange(FFT_N1_NZ)[None, :]
    ang1 = 2.0 * np.pi * ((k1 * n1) % FFT_N1) / FFT_N1
    kron_fwd = np.concatenate([np.kron(np.cos(ang1), eye), np.kron(np.sin(ang1), eye)], axis=0)
    kron_inv = np.concatenate([np.kron(np.cos(ang1).T, eye), np.kron(np.sin(ang1).T, eye)], axis=0) / FFT_N
    k = np.arange(FFT_N1)[:, None, None] + FFT_N1 * np.arange(FFT_N2)[None, :, None]
    n2 = np.arange(FFT_N2)[None, None, :]
    ang2 = 2.0 * np.pi * ((k * n2) % FFT_N) / FFT_N
    g_fwd = np.concatenate([np.cos(ang2), np.sin(ang2)], axis=1)
    g_inv = np.concatenate([np.cos(ang2).transpose(0, 2, 1), np.sin(ang2).transpose(0, 2, 1)], axis=1)
    n1_all = np.arange(FFT_N1)[None, :]
    ang1_all = 2.0 * np.pi * ((k1 * n1_all) % FFT_N1) / FFT_N1
    kron_full = np.concatenate([np.kron(np.cos(ang1_all), eye), np.kron(np.sin(ang1_all), eye)], axis=0)
    to_b = lambda a: jnp.asarray(a, dtype=F32).astype(BF16)
    return to_b(kron_fwd), to_b(g_fwd), to_b(g_inv), to_b(kron_inv), to_b(kron_full)


def _cplx_fwd(r, half_rows, cb):
    cr, sr = r[:half_rows], r[half_rows:]
    return cr[:, :cb] + sr[:, cb:], cr[:, cb:] - sr[:, :cb]


def _cplx_inv(r, half_rows, cb):
    cr, sr = r[:half_rows], r[half_rows:]
    return cr[:, :cb] - sr[:, cb:], cr[:, cb:] + sr[:, :cb]


def _fft_stage1(load_group, kron_ref, a_scr, cb):
    def body(jt, carry):
        rows = pl.ds(pl.multiple_of(jt * SUBLANES, SUBLANES), SUBLANES)
        d = load_group(rows).reshape(kron_ref.shape[1], 2 * cb).astype(BF16)
        r = jnp.dot(kron_ref[...], d, preferred_element_type=F32)
        a_re, a_im = _cplx_fwd(r, FFT_N1 * SUBLANES, cb)
        a_scr[:, rows, :] = jnp.concatenate([a_re, a_im], axis=1).reshape(FFT_N1, SUBLANES, 2 * cb)
        return carry
    lax.fori_loop(0, FFT_N2 // SUBLANES, body, 0, unroll=FFT_UNROLL_N1_STAGE)


def _fft_fwd_data_kernel(z_ref, h_ref, kron_ref, g_ref, p_ref, a_scr):
    cb = z_ref.shape[4]
    load = lambda rows: jnp.concatenate([z_ref[0, 0, :, rows, :], z_ref[0, 1, :, rows, :]], axis=-1)
    _fft_stage1(load, kron_ref, a_scr, cb)

    def body(k1, carry):
        r = jnp.dot(g_ref[k1], a_scr[k1].astype(BF16), preferred_element_type=F32)
        x_re, x_im = _cplx_fwd(r, FFT_N2, cb)
        h = h_ref[0, k1]
        h_re, h_im = h[:, :cb], h[:, cb:]
        p_ref[0, 0, k1] = jnp.concatenate([x_re * h_re - x_im * h_im, x_re * h_im + x_im * h_re], axis=1).astype(BF16)
        return carry
    lax.fori_loop(0, FFT_N1, body, 0, unroll=FFT_UNROLL_N2_STAGE)


def _fft_fwd_filter_kernel(f_ref, den_ref, kron_ref, g_ref, h_ref, a_scr):
    cb = f_ref.shape[3]
    inv_den = 1.0 / den_ref[...]

    def load(rows):
        re = jnp.concatenate([f_ref[0, :, rows, :], f_ref[1, :, rows, :]], axis=0) * inv_den
        return jnp.concatenate([re, jnp.zeros_like(re)], axis=-1)
    _fft_stage1(load, kron_ref, a_scr, cb)

    def body(k1, carry):
        r = jnp.dot(g_ref[k1], a_scr[k1].astype(BF16), preferred_element_type=F32)
        x_re, x_im = _cplx_fwd(r, FFT_N2, cb)
        h_ref[0, k1] = jnp.concatenate([x_re, x_im], axis=1)
        return carry
    lax.fori_loop(0, FFT_N1, body, 0, unroll=FFT_UNROLL_N2_STAGE)


def _fft_inv_kernel(p_ref, x0_ref, w_ref, bias_ref, gi_ref, kron_ref, o_ref, b_scr):
    cb = o_ref.shape[4]

    def body3(k1, carry):
        r = jnp.dot(gi_ref[k1], p_ref[0, 0, k1], preferred_element_type=F32)
        b_re, b_im = _cplx_inv(r, FFT_N2, cb)
        b_scr[k1] = jnp.concatenate([b_re, b_im], axis=1)
        return carry
    lax.fori_loop(0, FFT_N1, body3, 0, unroll=FFT_UNROLL_N2_STAGE)

    bias = bias_ref[...].reshape(1, 1, cb)

    def body4(jt, carry):
        rows = pl.ds(pl.multiple_of(jt * SUBLANES, SUBLANES), SUBLANES)
        d = b_scr[:, rows, :].reshape(FFT_N1 * SUBLANES, 2 * cb).astype(BF16)
        r = jnp.dot(kron_ref[...], d, preferred_element_type=F32)
        y_re, y_im = _cplx_inv(r, FFT_N1_NZ * SUBLANES, cb)
        for which, y in enumerate((y_re, y_im)):
            y3 = y.reshape(FFT_N1_NZ, SUBLANES, cb)
            o_ref[0, which, :, rows, :] = x0_ref[0, which, :, rows, :] * (y3 + w_ref[0, which, :, rows, :] * bias)
        return carry
    lax.fori_loop(0, FFT_N2 // SUBLANES, body4, 0, unroll=FFT_UNROLL_N1_STAGE)


def hyena_spectrum(fu, den, *, cb=128):
    _, g_fwd, _, _, kron_full = _fft_constants()
    W = fu.shape[2]
    ncb = W // cb
    f4 = fu.reshape(2, FFT_N1_NZ, FFT_N2, W)
    const = lambda a: pl.BlockSpec(a.shape, lambda c: (0,) * a.ndim, pipeline_mode=pl.Buffered(1))
    return pl.pallas_call(
        _fft_fwd_filter_kernel,
        grid=(ncb,),
        in_specs=[
            pl.BlockSpec((2, FFT_N1_NZ, FFT_N2, cb), lambda c: (0, 0, 0, c)),
            pl.BlockSpec((1, cb), lambda c: (0, c)),
            const(kron_full), const(g_fwd),
        ],
        out_specs=pl.BlockSpec((1, FFT_N1, FFT_N2, 2 * cb), lambda c: (c, 0, 0, 0)),
        out_shape=jax.ShapeDtypeStruct((ncb, FFT_N1, FFT_N2, 2 * cb), F32),
        scratch_shapes=[pltpu.VMEM((FFT_N1, FFT_N2, 2 * cb), F32)],
        compiler_params=pltpu.CompilerParams(
            dimension_semantics=("parallel",), vmem_limit_bytes=VMEM_LIMIT_BYTES),
        name="hyena_spectrum",
    )(f4, den, kron_full, g_fwd)


def hyena_conv(x0c, hw, spec, bias, *, cb=128):
    kron_fwd, g_fwd, g_inv, kron_inv, _ = _fft_constants()
    B, L, W = hw.shape
    assert L == FFT_L and B % 2 == 0
    ncb = W // cb
    pair_shape = (B // 2, 2, FFT_N1_NZ, FFT_N2, W)
    x5 = x0c.reshape(pair_shape)
    w5 = hw.reshape(pair_shape)
    const = lambda a: pl.BlockSpec(a.shape, lambda c, p: (0,) * a.ndim, pipeline_mode=pl.Buffered(1))
    pair = pl.BlockSpec((1, 2, FFT_N1_NZ, FFT_N2, cb), lambda c, p: (p, 0, 0, 0, c))
    pspec = pl.BlockSpec((1, 1, FFT_N1, FFT_N2, 2 * cb), lambda c, p: (c, p, 0, 0, 0))
    prod = pl.pallas_call(
        _fft_fwd_data_kernel,
        grid=(ncb, B // 2),
        in_specs=[pair, pl.BlockSpec((1, FFT_N1, FFT_N2, 2 * cb), lambda c, p: (c, 0, 0, 0)),
                  const(kron_fwd), const(g_fwd)],
        out_specs=pspec,
        out_shape=jax.ShapeDtypeStruct((ncb, B // 2, FFT_N1, FFT_N2, 2 * cb), BF16),
        scratch_shapes=[pltpu.VMEM((FFT_N1, FFT_N2, 2 * cb), F32)],
        compiler_params=pltpu.CompilerParams(
            dimension_semantics=("parallel", "parallel"), vmem_limit_bytes=VMEM_LIMIT_BYTES),
        name="hyena_fft_fwd",
    )(w5, spec, kron_fwd, g_fwd)
    out = pl.pallas_call(
        _fft_inv_kernel,
        grid=(ncb, B // 2),
        in_specs=[pspec, pair, pair, pl.BlockSpec((1, cb), lambda c, p: (0, c)), const(g_inv), const(kron_inv)],
        out_specs=pair,
        out_shape=jax.ShapeDtypeStruct(pair_shape, F32),
        scratch_shapes=[pltpu.VMEM((FFT_N1, FFT_N2, 2 * cb), F32)],
        compiler_params=pltpu.CompilerParams(
            dimension_semantics=("parallel", "parallel"), vmem_limit_bytes=VMEM_LIMIT_BYTES),
        name="hyena_fft_inv",
    )(prod, x5, w5, bias.reshape(1, W), g_inv, kron_inv)
    return out.reshape(B, L, W)


@functools.lru_cache(maxsize=None)
def _dense_dft_constants(L):
    n_full = 2 * L
    k = np.arange(n_full)[:, None]
    n = np.arange(n_full)[None, :]
    ang = 2.0 * np.pi * ((k * n) % n_full) / n_full
    fwd = np.concatenate([np.cos(ang), np.sin(ang)], axis=0)
    inv = np.concatenate([np.cos(ang[:L]), np.sin(ang[:L])], axis=0) / n_full
    return jnp.asarray(fwd, dtype=F32), jnp.asarray(inv, dtype=F32)


def _hyena_short_kernel(x0_ref, w_ref, f_ref, den_ref, bias_ref, ff_ref, fi_ref, o_ref):
    L, cb = o_ref.shape[2], o_ref.shape[3]
    filt = jnp.concatenate([f_ref[0], f_ref[1]], axis=0) / den_ref[...]
    r = _dot_hi(ff_ref[...], filt)
    h_re, h_im = r[:2 * L], -r[2 * L:]
    z = jnp.concatenate([w_ref[0, 0], w_ref[0, 1]], axis=1)
    x_re, x_im = _cplx_fwd(_dot_hi(ff_ref[:, :L], z), 2 * L, cb)
    p = jnp.concatenate([x_re * h_re - x_im * h_im, x_re * h_im + x_im * h_re], axis=1)
    y_re, y_im = _cplx_inv(_dot_hi(fi_ref[...], p), L, cb)
    for which, y in enumerate((y_re, y_im)):
        o_ref[0, which] = x0_ref[0, which] * (y + w_ref[0, which] * bias_ref[...])


def hyena_conv_short(x0c, hw, fu, den, bias, *, cb=128):
    B, L, W = hw.shape
    fwd, inv = _dense_dft_constants(L)
    pair_shape = (B // 2, 2, L, W)
    pair = pl.BlockSpec((1, 2, L, cb), lambda p, c: (p, 0, 0, c))
    vec = pl.BlockSpec((1, cb), lambda p, c: (0, c))
    const = lambda a: pl.BlockSpec(a.shape, lambda p, c: (0, 0))
    out = pl.pallas_call(
        _hyena_short_kernel,
        grid=(B // 2, W // cb),
        in_specs=[pair, pair, pl.BlockSpec((2, L, cb), lambda p, c: (0, 0, c)), vec, vec, const(fwd), const(inv)],
        out_specs=pair,
        out_shape=jax.ShapeDtypeStruct(pair_shape, F32),
        compiler_params=pltpu.CompilerParams(
            dimension_semantics=("parallel", "parallel"), vmem_limit_bytes=VMEM_LIMIT_BYTES),
        name="hyena_conv_short",
    )(x0c.reshape(pair_shape), hw.reshape(pair_shape), fu, den, bias.reshape(1, W), fwd, inv)
    return out.reshape(B, L, W)


GRID_W = 64
SC_COLS = 3 * BRANCH_W
HG_COLS = 3 * HG_KEY + 2 * HG_HEADS * HG_DV
SSD_XBC = SSD_INNER + 2 * SSD_GROUPS * SSD_STATE
assert SSD_XBC == SSD_XBC_COLS and HY_COLS == 3 * HY_WIDTH


def _sincos_1d(pos, dim):
    omega = 1.0 / (10000.0 ** (jnp.arange(dim // 2, dtype=F32) / (dim // 2)))
    ang = pos.astype(F32)[:, None] * omega[None]
    return jnp.concatenate([jnp.sin(ang), jnp.cos(ang)], -1)


def _grid_pos_embed(rows, dim):
    row = jnp.repeat(jnp.arange(rows), GRID_W)
    col = jnp.tile(jnp.arange(GRID_W), rows)
    return jnp.concatenate([_sincos_1d(row, dim // 2), _sincos_1d(col, dim // 2)], -1)


def kernel(x, c, ctx, c_ctx, w_ada, b_ada, w_in, hy_conv_w, hy_conv_b, hy_w1, hy_b1, hy_freq1,
           hy_w2, hy_b2, hy_freq2, hy_w3, hy_decay, hy_bias, sc_conv_w, hg_lb_logits, hg_norm_g,
           ssd_conv_w, ssd_conv_b, ssd_a_log, ssd_dt_bias, ssd_d, ssd_norm_g, w_gate, b_gate, w_br,
           w_o, ln1_g, ln1_b, ln2_g, ln2_b, w_router, b_router, w_e1, w_e3, w_e2):
    B, L, D = x.shape
    LC = ctx.shape[1]
    depth = w_in.shape[0]
    pos_table = _grid_pos_embed(L // GRID_W, D).astype(x.dtype)
    lat = x.reshape(B * L, D)
    cx = ctx.reshape(B * LC, D)

    n_vec = -(-(B + 1) // SUBLANES) * SUBLANES
    cvecs = jnp.concatenate([c, c_ctx[None], jnp.zeros((n_vec - B - 1, D), c.dtype)], axis=0)
    mods = ada_mod(cvecs, w_ada, b_ada).reshape(depth, n_vec, 6, D)
    lb_logits = hg_lb_logits.reshape(2 * depth, HG_KEY)

    out_widths = (BRANCH_W, BRANCH_W, BRANCH_W, SSD_XBC, HG_COLS, SSD_INNER + LANES)
    z_state = lambda *shape: jnp.zeros(shape, F32)
    c_hy, c_sc, c_hg = HY_COLS, HY_COLS + SC_COLS, HY_COLS + SC_COLS + HG_COLS
    c_z, c_xbc = c_hg + SSD_INNER, c_hg + SSD_INNER + SSD_XBC

    for l in range(depth):
        ctx_out = l < depth - 1
        mod_lat, mod_ctx = mods[l, :B], mods[l, B:B + 1]
        wl = w_in[l]
        w_in_b = jnp.concatenate([
            wl[:, :c_hy], wl[:, c_hy + BRANCH_W:c_sc], wl[:, c_z:c_xbc],
            wl[:, c_hy:c_hy + BRANCH_W], wl[:, c_sc:c_hg], wl[:, c_hg:c_z],
            jnp.pad(wl[:, c_xbc:], ((0, 0), (0, LANES - 2 * SSD_HEADS))),
        ], axis=1).astype(BF16)
        conv_par = (hy_conv_w[l], hy_conv_b[l], sc_conv_w[l], ssd_conv_w[l], ssd_conv_b[l])
        ssd_par = jnp.zeros((2 * SUBLANES, LANES), F32)
        ssd_par = ssd_par.at[:SUBLANES, :2 * SSD_HEADS].set(jnp.broadcast_to(ssd_a_log[l].reshape(1, -1), (SUBLANES, 2 * SSD_HEADS)))
        ssd_par = ssd_par.at[SUBLANES:, :2 * SSD_HEADS].set(jnp.broadcast_to(ssd_dt_bias[l].reshape(1, -1), (SUBLANES, 2 * SSD_HEADS)))
        d_skip = jnp.repeat(ssd_d[l], SSD_HEADDIM)
        filt_args = (hy_w1[l], hy_b1[l], hy_freq1[l], hy_w2[l], hy_b2[l], hy_freq2[l], hy_w3[l], hy_decay[l])

        def mixers(tokens, mod, seg_len, hg_state, ssd_state, want_out, pos=None):
            n_seq = tokens.shape[0] // seg_len
            x0c, hw, sc_y, xbc, hg, zdt = in_proj(
                tokens, mod, w_in_b, conv_par, out_widths, seq_len=seg_len, rows_per_mod=mod_rows(mod, tokens),
                tm=min(512, seg_len), pos=pos)
            seq = lambda a: a.reshape(n_seq, seg_len, a.shape[1])
            hg_y, hg_f, hg_b = hgrn_scan(seq(hg), lb_logits, hg_norm_g[l], *hg_state, layer=l)
            ssd_y, ssd_f, ssd_b = ssd_scan(seq(zdt), seq(xbc), ssd_par, d_skip, ssd_norm_g[l], *ssd_state,
                                           z_blk=0, dt_blk=SSD_INNER // LANES)
            branches = None
            if want_out:
                fu, den = hy_filter(seg_len, *filt_args)
                if seg_len == FFT_L:
                    hy_y = hyena_conv(seq(x0c), seq(hw), hyena_spectrum(fu, den), hy_bias[l])
                else:
                    hy_y = hyena_conv_short(seq(x0c), seq(hw), fu, den, hy_bias[l])
                flat = lambda a: a.reshape(tokens.shape[0], a.shape[2])
                branches = [flat(hy_y), sc_y, flat(hg_y), flat(ssd_y)]
            return branches, (hg_f, hg_b), (ssd_f, ssd_b)

        def mod_rows(mod, tokens):
            return tokens.shape[0] // mod.shape[0]

        def finish(tokens, mod, branches, pos=None):
            rows = mod_rows(mod, tokens)
            t1 = merge_ln(tokens, mod, branches, w_gate[l].astype(BF16), b_gate[l], w_br[l].astype(BF16),
                          w_o[l].astype(BF16), ln1_g[l], ln1_b[l], rows_per_mod=rows, tm=512, pos=pos)
            return moe_ln(t1, mod, w_router, b_router, w_e1[l].astype(BF16), w_e3[l].astype(BF16),
                          w_e2[l].astype(BF16), ln2_g[l], ln2_b[l], rows_per_mod=rows, tm=1024)

        zero_hg = (z_state(B, HG_KEY, HG_KEY),) * 2
        zero_ssd = (z_state(B, SSD_GROUPS, LANES, LANES),) * 2
        br_ctx, hg_state, ssd_state = mixers(cx, mod_ctx, LC, zero_hg, zero_ssd, ctx_out)
        pos = pos_table if l == 0 else None
        br_lat, _, _ = mixers(lat, mod_lat, L, hg_state, ssd_state, True, pos=pos)
        lat = finish(lat, mod_lat, br_lat, pos=pos)
        if ctx_out:
            cx = finish(cx, mod_ctx, br_ctx)
    return lat.reshape(B, L, D)
```

```python
import functools
import math

import jax
import jax.numpy as jnp
import numpy as np
from jax import lax
from jax.experimental import pallas as pl
from jax.experimental.pallas import tpu as pltpu

F32 = jnp.float32
BF16 = jnp.bfloat16

N_EXPERTS = 16
N_EXPERT_GROUPS = 4
GROUP_SIZE = N_EXPERTS // N_EXPERT_GROUPS
DEPTH = 2
DEEPNORM_ALPHA = (2 * DEPTH) ** 0.25
LN_EPS = 1e-5
RMS_EPS = 1e-6

VMEM_LIMIT_BYTES = 52 * 1024 * 1024


def _layernorm_rows(z, g, b):
    mu = jnp.mean(z, axis=-1, keepdims=True)
    zc = z - mu
    var = jnp.mean(zc * zc, axis=-1, keepdims=True)
    return zc * lax.rsqrt(var + LN_EPS) * g + b


def _route_t(logits_t, bias_col):
    z = logits_t - jnp.max(logits_t, axis=0, keepdims=True)
    ex = jnp.exp(z)
    scores = ex / jnp.sum(ex, axis=0, keepdims=True)
    sel = scores + bias_col
    eidx = lax.broadcasted_iota(jnp.int32, sel.shape, 0)
    neg = jnp.float32(-jnp.inf)

    def first_argmax(m):
        top = jnp.max(m, axis=0, keepdims=True)
        return top, jnp.min(jnp.where(m == top, eidx, N_EXPERTS), axis=0, keepdims=True)

    best_score = None
    best_grp = None
    for g in range(N_EXPERT_GROUPS):
        m = jnp.where((eidx >= g * GROUP_SIZE) & (eidx < (g + 1) * GROUP_SIZE), sel, neg)
        t1, i1 = first_argmax(m)
        t2 = jnp.max(jnp.where(eidx == i1, neg, m), axis=0, keepdims=True)
        s = t1 + t2
        if g == 0:
            best_score, best_grp = s, jnp.zeros_like(i1)
        else:
            better = s > best_score
            best_score = jnp.where(better, s, best_score)
            best_grp = jnp.where(better, g, best_grp)
    lo = best_grp * GROUP_SIZE
    masked = jnp.where((eidx >= lo) & (eidx < lo + GROUP_SIZE), sel, neg)
    _, ia = first_argmax(masked)
    _, ib = first_argmax(jnp.where(eidx == ia, neg, masked))
    w = jnp.where((eidx == ia) | (eidx == ib), scores, 0.0)
    return w / jnp.sum(w, axis=0, keepdims=True), best_grp


MOE_CHUNK = 64
MOE_ARMS = (1024, 512, 256, 128, 64)
MOE_STEP_EXPERTS = 2
GATE_TERMS = 3
POS_LANE = GATE_TERMS * N_EXPERTS


def _moe_ln_kernel(x_ref, mod_ref, wr_ref, br_ref, w1_ref, w3_ref, w2_ref, lng_ref, lnb_ref,
                   o_ref, pt_scr, hc_scr, gc_scr, yc_scr, seg_smem):
    step = pl.program_id(1)
    TM, R = pt_scr.shape
    D = x_ref.shape[1]
    G = N_EXPERT_GROUPS

    @pl.when(step == 0)
    def _():
        h = _modulate(x_ref[...], mod_ref, MOD_SHIFT2, MOD_SCALE2)
        h_hi = h.astype(BF16)
        h_lo = (h - h_hi.astype(F32)).astype(BF16)
        r_hi = jnp.dot(h_hi, wr_ref[...], preferred_element_type=F32)
        r_lo = jnp.dot(h_lo, wr_ref[:, :LANES], preferred_element_type=F32)
        logits_t = (r_hi[:, :LANES] + r_hi[:, LANES:] + r_lo).T[:N_EXPERTS]
        gate_t, grp = _route_t(logits_t, br_ref[:, 0:1])

        gidx = lax.broadcasted_iota(jnp.int32, (SUBLANES, TM), 0)
        tok = lax.broadcasted_iota(jnp.int32, (SUBLANES, TM), 1)
        member = jnp.where(gidx == grp, 1.0, 0.0)
        incl = member
        shift = 1
        while shift < TM:
            incl = incl + jnp.where(tok >= shift, pltpu.roll(incl, shift, 1), 0.0)
            shift *= 2
        rank = incl - member
        offset = jnp.int32(0)
        pos_row = jnp.zeros((1, TM), F32)
        for g in range(G):
            n_chunks = (jnp.sum(member[g:g + 1, :]).astype(jnp.int32) + (MOE_CHUNK - 1)) // MOE_CHUNK
            seg_smem[g] = n_chunks
            seg_smem[G + g] = offset
            pos_row = pos_row + member[g:g + 1, :] * (rank[g:g + 1, :] + offset.astype(F32))
            offset = offset + n_chunks * MOE_CHUNK

        terms, rest = [], gate_t
        for _ in range(GATE_TERMS):
            part = rest.astype(BF16).astype(F32)
            terms.append(part)
            rest = rest - part
        pad = jnp.zeros((LANES - POS_LANE - 1, TM), F32)
        side = jnp.concatenate(terms + [pos_row, pad], axis=0).T
        pos_col = side[:, POS_LANE:POS_LANE + 1].astype(jnp.int32)
        pt_scr[...] = jnp.where(lax.broadcasted_iota(jnp.int32, (TM, R), 1) == pos_col, 1.0, 0.0).astype(BF16)
        p = jnp.where(lax.broadcasted_iota(jnp.int32, (R, TM), 0) == pos_row.astype(jnp.int32), 1.0, 0.0).astype(BF16)
        gathered = jnp.dot(p, jnp.concatenate([h_hi, side.astype(BF16)], axis=1), preferred_element_type=F32)
        hc_scr[...] = gathered[:, :D].astype(BF16)
        gc_scr[...] = gathered[:, D:]
        yc_scr[...] = jnp.zeros_like(yc_scr)

    g = step // (GROUP_SIZE // MOE_STEP_EXPERTS)
    n_chunks = seg_smem[g]
    base = seg_smem[G + g]

    def expert_rows(start, size):
        rows = pl.ds(pl.multiple_of(start, MOE_CHUNK), size)
        hc = hc_scr[rows, :]
        gc = gc_scr[rows, :]
        lane = lax.broadcasted_iota(jnp.int32, gc.shape, 1)
        acc = None
        for k in range(MOE_STEP_EXPERTS):
            e = step * MOE_STEP_EXPERTS + k
            a = jnp.dot(hc, w1_ref[k], preferred_element_type=F32)
            b = jnp.dot(hc, w3_ref[k], preferred_element_type=F32)
            mid = (a * jax.nn.sigmoid(a) * b).astype(BF16)
            y = jnp.dot(mid, w2_ref[k], preferred_element_type=F32)
            gcol = jnp.sum(jnp.where((lane % N_EXPERTS == e) & (lane < POS_LANE), gc, 0.0), axis=-1, keepdims=True)
            acc = gcol * y if acc is None else acc + gcol * y
        yc_scr[rows, :] += acc

    for size in MOE_ARMS:
        if size > TM:
            continue
        taken = (n_chunks & (size // MOE_CHUNK)) != 0

        @pl.when(taken)
        def _(base=base, size=size):
            expert_rows(base, size)

        base = base + jnp.where(taken, size, 0)

    @pl.when(step == N_EXPERTS // MOE_STEP_EXPERTS - 1)
    def _():
        moe = jnp.dot(pt_scr[...], yc_scr[...].astype(BF16), preferred_element_type=F32)
        z = DEEPNORM_ALPHA * x_ref[...] + mod_ref[0, MOD_GATE2:MOD_GATE2 + 1, :] * moe
        o_ref[...] = _layernorm_rows(z, lng_ref[...], lnb_ref[...])


def moe_ln(x, mod, w_router, b_router, w1, w3, w2, ln_g, ln_b, *, layer, rows_per_mod, tm):
    T, D = x.shape
    _, E, _, FF = w1.shape
    assert T % tm == 0 and rows_per_mod % tm == 0 and tm <= MOE_ARMS[0] and tm % MOE_CHUNK == 0
    tiles_per_mod = rows_per_mod // tm
    sorted_rows = tm + N_EXPERT_GROUPS * MOE_CHUNK
    wr_hi = w_router.astype(BF16)
    wr_lo = (w_router - wr_hi.astype(F32)).astype(BF16)
    lane_pad = lambda w: jnp.pad(w, ((0, 0), (0, LANES - E)))
    wr_split = jnp.concatenate([lane_pad(wr_hi), lane_pad(wr_lo)], axis=1)
    br_col = jnp.broadcast_to(b_router.reshape(E, 1), (E, LANES))
    return pl.pallas_call(
        _moe_ln_kernel,
        grid=(T // tm, E // MOE_STEP_EXPERTS),
        in_specs=[
            pl.BlockSpec((tm, D), lambda i, e: (i, 0)),
            pl.BlockSpec((1, 6, D), lambda i, e: (i // tiles_per_mod, 0, 0)),
            pl.BlockSpec((D, 2 * LANES), lambda i, e: (0, 0)),
            pl.BlockSpec((E, LANES), lambda i, e: (0, 0)),
            pl.BlockSpec((None, MOE_STEP_EXPERTS, D, FF), lambda i, e: (layer, e, 0, 0)),
            pl.BlockSpec((None, MOE_STEP_EXPERTS, D, FF), lambda i, e: (layer, e, 0, 0)),
            pl.BlockSpec((None, MOE_STEP_EXPERTS, FF, D), lambda i, e: (layer, e, 0, 0)),
            pl.BlockSpec((1, D), lambda i, e: (0, 0)),
            pl.BlockSpec((1, D), lambda i, e: (0, 0)),
        ],
        out_specs=pl.BlockSpec((tm, D), lambda i, e: (i, 0)),
        out_shape=jax.ShapeDtypeStruct((T, D), F32),
        scratch_shapes=[
            pltpu.VMEM((tm, sorted_rows), BF16),
            pltpu.VMEM((sorted_rows, D), BF16),
            pltpu.VMEM((sorted_rows, LANES), F32),
            pltpu.VMEM((sorted_rows, D), F32),
            pltpu.SMEM((2 * N_EXPERT_GROUPS,), jnp.int32),
        ],
        compiler_params=pltpu.CompilerParams(
            dimension_semantics=("parallel", "arbitrary"), vmem_limit_bytes=VMEM_LIMIT_BYTES),
        name="moe_ln",
    )(x, mod, wr_split, br_col, w1, w3, w2, ln_g.reshape(1, D), ln_b.reshape(1, D))


def _ada_kernel(c_ref, w_ref, b_ref, o_ref):
    c = c_ref[...]
    s = c * jax.nn.sigmoid(c)
    o_ref[0] = jnp.dot(s, w_ref[0], precision=lax.Precision.HIGHEST, preferred_element_type=F32) + b_ref[0]


def ada_mod(cvecs, w_ada, b_ada, *, tn=1536):
    R, D = cvecs.shape
    depth, _, N = w_ada.shape
    assert N % tn == 0
    return pl.pallas_call(
        _ada_kernel,
        grid=(depth, N // tn),
        in_specs=[
            pl.BlockSpec((R, D), lambda l, j: (0, 0)),
            pl.BlockSpec((1, D, tn), lambda l, j: (l, 0, j)),
            pl.BlockSpec((1, 1, tn), lambda l, j: (l, 0, j)),
        ],
        out_specs=pl.BlockSpec((1, R, tn), lambda l, j: (l, 0, j)),
        out_shape=jax.ShapeDtypeStruct((depth, R, N), F32),
        compiler_params=pltpu.CompilerParams(
            dimension_semantics=("parallel", "parallel"), vmem_limit_bytes=VMEM_LIMIT_BYTES),
        name="ada_mod",
    )(cvecs, w_ada, b_ada.reshape(depth, 1, N))


MOD_SHIFT1, MOD_SCALE1, MOD_GATE1, MOD_SHIFT2, MOD_SCALE2, MOD_GATE2 = range(6)


def _modulate(x, mod_ref, shift_row, scale_row):
    return x * (1.0 + mod_ref[0, scale_row:scale_row + 1, :]) + mod_ref[0, shift_row:shift_row + 1, :]


HALO = 8
BRANCH_W = 256
HY_COLS = 3 * BRANCH_W
SSD_XBC_COLS = 512
IN_CONV_COLS = HY_COLS + 2 * BRANCH_W + SSD_XBC_COLS


def _conv3(u, w_ref, tm):
    return (u[HALO - 1:HALO - 1 + tm] * w_ref[0:1, :] + u[HALO:HALO + tm] * w_ref[1:2, :]
            + u[HALO + 1:HALO + 1 + tm] * w_ref[2:3, :])


def _in_proj_kernel(x_ref, xp_ref, xn_ref, mod_ref, w_ref, hcw_ref, hcb_ref, scw_ref, xcw_ref, xcb_ref, *rest,
                    has_pos, tiles_per_seq):
    if has_pos:
        pos_ref, pp_ref, pn_ref = rest[:3]
        rest = rest[3:]
    x0c_ref, hw_ref, scy_ref, xbc_ref, hg_ref, zdt_ref = rest
    tm = x_ref.shape[0]
    t = pl.program_id(0) % tiles_per_seq
    keep_prev = jnp.where(t > 0, 1.0, 0.0)
    keep_next = jnp.where(t < tiles_per_seq - 1, 1.0, 0.0)

    def tokens(main_ref, prev_ref, next_ref):
        return main_ref[...], prev_ref[...], next_ref[...]

    xm, xp, xn = tokens(x_ref, xp_ref, xn_ref)
    if has_pos:
        pm, pp, pn = tokens(pos_ref, pp_ref, pn_ref)
        xm, xp, xn = xm + pm, xp + pp, xn + pn
    mod = lambda v: _modulate(v, mod_ref, MOD_SHIFT1, MOD_SCALE1)
    h_main = mod(xm).astype(BF16)
    h_all = jnp.concatenate([mod(xp) * keep_prev, mod(xm), mod(xn) * keep_next], axis=0).astype(BF16)
    uc = jnp.dot(h_all, w_ref[:, :IN_CONV_COLS], preferred_element_type=F32)
    ur = jnp.dot(h_main, w_ref[:, IN_CONV_COLS:], preferred_element_type=F32)

    hy = _conv3(uc[:, :HY_COLS], hcw_ref, tm) + hcb_ref[...]
    x0c_ref[...] = hy[:, :BRANCH_W]
    hw_ref[...] = hy[:, BRANCH_W:2 * BRANCH_W] * hy[:, 2 * BRANCH_W:]
    cg_xs = uc[:, HY_COLS:HY_COLS + BRANCH_W] * uc[:, HY_COLS + BRANCH_W:HY_COLS + 2 * BRANCH_W]
    scy_ref[...] = ur[:, :BRANCH_W] * _conv3(cg_xs, scw_ref, tm)
    xa = _conv3(uc[:, HY_COLS + 2 * BRANCH_W:], xcw_ref, tm) + xcb_ref[...]
    xbc_ref[...] = xa * jax.nn.sigmoid(xa)
    n_hg = hg_ref.shape[1]
    hg_ref[...] = ur[:, BRANCH_W:BRANCH_W + n_hg]
    zdt_ref[...] = ur[:, BRANCH_W + n_hg:]


def in_proj(x, mod, w, conv, out_widths, *, seq_len, rows_per_mod, tm, pos=None):
    T, D = x.shape
    assert T % tm == 0 and seq_len % tm == 0 and rows_per_mod % tm == 0 and tm % HALO == 0
    tiles_per_seq = seq_len // tm
    tiles_per_mod = rows_per_mod // tm
    hb = tm // HALO
    last_blk = T // HALO - 1
    hy_w, hy_b, sc_w, xbc_w, xbc_b = conv
    row = lambda v: v.reshape(1, -1)
    const = lambda a: pl.BlockSpec(a.shape, lambda i: (0,) * a.ndim)
    consts = [hy_w, row(hy_b), sc_w, xbc_w, row(xbc_b)]
    in_specs = [
        pl.BlockSpec((tm, D), lambda i: (i, 0)),
        pl.BlockSpec((HALO, D), lambda i: (jnp.maximum(i * hb - 1, 0), 0)),
        pl.BlockSpec((HALO, D), lambda i: (jnp.minimum((i + 1) * hb, last_blk), 0)),
        pl.BlockSpec((1, 6, D), lambda i: (i // tiles_per_mod, 0, 0)),
        pl.BlockSpec(w.shape, lambda i: (0, 0), pipeline_mode=pl.Buffered(1)),
    ] + [const(a) for a in consts]
    args = [x, x, x, mod, w] + consts
    if pos is not None:
        assert pos.shape[0] == seq_len
        last_pos = seq_len // HALO - 1
        in_specs += [
            pl.BlockSpec((tm, D), lambda i: (i % tiles_per_seq, 0)),
            pl.BlockSpec((HALO, D), lambda i: (jnp.maximum((i % tiles_per_seq) * hb - 1, 0), 0)),
            pl.BlockSpec((HALO, D), lambda i: (jnp.minimum((i % tiles_per_seq + 1) * hb, last_pos), 0)),
        ]
        args += [pos, pos, pos]
    return pl.pallas_call(
        functools.partial(_in_proj_kernel, has_pos=pos is not None, tiles_per_seq=tiles_per_seq),
        grid=(T // tm,),
        in_specs=in_specs,
        out_specs=[pl.BlockSpec((tm, n), lambda i: (i, 0)) for n in out_widths],
        out_shape=[jax.ShapeDtypeStruct((T, n), F32) for n in out_widths],
        compiler_params=pltpu.CompilerParams(
            dimension_semantics=("parallel",), vmem_limit_bytes=VMEM_LIMIT_BYTES),
        name="in_proj",
    )(*args)


def _pos_spec(pos, rows_per_mod, tm):
    assert pos.shape[0] == rows_per_mod
    tiles = rows_per_mod // tm
    return pl.BlockSpec((tm, pos.shape[1]), lambda i: (i % tiles, 0))


N_BRANCH = 4


def _merge_kernel(x_ref, mod_ref, hy_ref, sc_ref, hg_ref, ssd_ref, wg_ref, bg_ref, wbr_ref, wo_ref,
                  lng_ref, lnb_ref, *rest, has_pos):
    x = x_ref[...] + rest[0][...] if has_pos else x_ref[...]
    o_ref = rest[-1]
    D = x.shape[1]
    h = _modulate(x, mod_ref, MOD_SHIFT1, MOD_SCALE1).astype(BF16)
    y = None
    for k, br_ref in enumerate((hy_ref, sc_ref, hg_ref, ssd_ref)):
        gate = jax.nn.sigmoid(
            jnp.dot(h, wg_ref[:, k * D:(k + 1) * D], preferred_element_type=F32) + bg_ref[:, k * D:(k + 1) * D])
        term = gate * jnp.dot(br_ref[...].astype(BF16), wbr_ref[k], preferred_element_type=F32)
        y = term if y is None else y + term
    y = jnp.dot(y.astype(BF16), wo_ref[...], preferred_element_type=F32)
    z = DEEPNORM_ALPHA * x + mod_ref[0, MOD_GATE1:MOD_GATE1 + 1, :] * y
    o_ref[...] = _layernorm_rows(z, lng_ref[...], lnb_ref[...])


def merge_ln(x, mod, branches, w_gate, b_gate, w_br, w_o, ln_g, ln_b, *, rows_per_mod, tm, pos=None):
    T, D = x.shape
    BW = branches[0].shape[1]
    assert T % tm == 0 and rows_per_mod % tm == 0
    tiles_per_mod = rows_per_mod // tm
    const = lambda shape: pl.BlockSpec(shape, lambda i: (0,) * len(shape), pipeline_mode=pl.Buffered(1))
    extra = [] if pos is None else [pos]
    return pl.pallas_call(
        functools.partial(_merge_kernel, has_pos=pos is not None),
        grid=(T // tm,),
        in_specs=[
            pl.BlockSpec((tm, D), lambda i: (i, 0)),
            pl.BlockSpec((1, 6, D), lambda i: (i // tiles_per_mod, 0, 0)),
        ] + [pl.BlockSpec((tm, BW), lambda i: (i, 0))] * N_BRANCH + [
            const((D, N_BRANCH * D)), const((1, N_BRANCH * D)), const((N_BRANCH, BW, D)), const((D, D)),
            const((1, D)), const((1, D)),
        ] + [_pos_spec(p, rows_per_mod, tm) for p in extra],
        out_specs=pl.BlockSpec((tm, D), lambda i: (i, 0)),
        out_shape=jax.ShapeDtypeStruct((T, D), F32),
        compiler_params=pltpu.CompilerParams(
            dimension_semantics=("parallel",), vmem_limit_bytes=VMEM_LIMIT_BYTES),
        name="merge_ln",
    )(x, mod, *branches, w_gate, b_gate.reshape(1, -1), w_br, w_o, ln_g.reshape(1, D), ln_b.reshape(1, D), *extra)


SSD_HEADS = 4
SSD_HEADDIM = 64
SSD_STATE = 64
SSD_GROUPS = 2
SSD_INNER = SSD_HEADS * SSD_HEADDIM
LANES = 128
NEG_INF = float("-inf")


SCAN_BATCH = 2


def _scan_chunk_index(ph, j, n):
    return j + (1 - ph) * (n - 1 - 2 * j)


def _ssd_kernel(z_ref, xbc_ref, dt_ref, par_ref, dskip_ref, ng_ref, sf0_ref, sb0_ref,
                y_ref, sf_ref, sb_ref, yb_scr, st_scr, *, n_chunks):
    ph = pl.program_id(1)
    j = pl.program_id(2)
    is_fwd = ph == 1
    c = _scan_chunk_index(ph, j, n_chunks)
    NB, R = xbc_ref.shape[0], xbc_ref.shape[1]

    @pl.when((j == 0) & is_fwd)
    def _():
        st_scr[...] = sf0_ref[...]

    @pl.when((j == 0) & jnp.logical_not(is_fwd))
    def _():
        st_scr[...] = sb0_ref[...]

    row = lax.broadcasted_iota(jnp.int32, (R, R), 0)
    col = lax.broadcasted_iota(jnp.int32, (R, R), 1)
    mask = jnp.where(is_fwd, row - col, col - row) >= 0
    tri = jnp.where(mask, 1.0, 0.0).astype(BF16)
    head_of_lane = lax.broadcasted_iota(jnp.int32, (LANES, SSD_INNER), 1) // SSD_HEADDIM
    expand = jnp.where(lax.broadcasted_iota(jnp.int32, (LANES, SSD_INNER), 0) == head_of_lane, 1.0, 0.0).astype(BF16)
    ys = [_ssd_chunk(bi, is_fwd, mask, tri, expand, xbc_ref, dt_ref, par_ref, st_scr) for bi in range(NB)]

    @pl.when(jnp.logical_not(is_fwd))
    def _():
        for bi in range(NB):
            yb_scr[bi, c] = ys[bi]

    @pl.when(is_fwd)
    def _():
        for bi in range(NB):
            z = z_ref[bi]
            xs = xbc_ref[bi, :, :SSD_INNER]
            yt = (ys[bi] + yb_scr[bi, c] + xs * dskip_ref[...]) * (z * jax.nn.sigmoid(z))
            ms = jnp.mean(yt * yt, axis=-1, keepdims=True)
            y_ref[bi] = yt * lax.rsqrt(ms + RMS_EPS) * ng_ref[...]

    @pl.when((j == n_chunks - 1) & is_fwd)
    def _():
        sf_ref[...] = st_scr[...]

    @pl.when((j == n_chunks - 1) & jnp.logical_not(is_fwd))
    def _():
        sb_ref[...] = st_scr[...]


def _ssd_chunk(bi, is_fwd, mask, tri, expand, xbc_ref, dt_ref, par_ref, st_scr):
    R = xbc_ref.shape[1]
    dsel = lambda v: jnp.where(is_fwd, v, pltpu.roll(v, LANES - SSD_HEADS, 1))
    dt_raw = dsel(dt_ref[bi]) + dsel(par_ref[8:16, :])[0:1, :]
    dt = jnp.maximum(dt_raw, 0.0) + jnp.log(1.0 + jnp.exp(-jnp.abs(dt_raw)))
    a = -jnp.exp(dsel(par_ref[0:8, :])[0:1, :]) * dt
    cs = _split_dot(tri, a, terms=3)
    cs_t = cs.T
    total = jnp.sum(a, axis=0, keepdims=True)

    spread = lambda v: jnp.dot(v.astype(BF16), expand, preferred_element_type=F32)
    dt_bc = spread(dt)
    in_decay = spread(jnp.exp(cs))
    out_decay = spread(jnp.exp(total - cs))
    e_total = jnp.exp(total)

    xbc = xbc_ref[bi]
    xdt = xbc[:, :SSD_INNER] * dt_bc
    xdt_b = xdt.astype(BF16)
    xout_b = (xdt * out_decay).astype(BF16)
    bm = xbc[:, SSD_INNER:SSD_INNER + LANES]
    cm_b = xbc[:, SSD_INNER + LANES:SSD_INNER + 2 * LANES].astype(BF16)
    lane = lax.broadcasted_iota(jnp.int32, (R, LANES), 1)
    lo_half = lane < SSD_HEADDIM
    lo_half_st = lax.broadcasted_iota(jnp.int32, (LANES, LANES), 1) < SSD_HEADDIM
    y_groups = []
    for g in range(SSD_GROUPS):
        h0, h1 = 2 * g, 2 * g + 1
        cols = slice(g * LANES, (g + 1) * LANES)
        bm_g = jnp.where((lane >= g * SSD_STATE) & (lane < (g + 1) * SSD_STATE), bm, 0.0).astype(BF16)
        G = lax.dot_general(cm_b, bm_g, (((1,), (1,)), ((), ())), preferred_element_type=F32)
        st_old = st_scr[bi, g]
        y_off = in_decay[:, cols] * jnp.dot(cm_b, st_old.astype(BF16), preferred_element_type=F32)
        y_diag = []
        for h in (h0, h1):
            decay = jnp.exp(jnp.where(mask, cs[:, h:h + 1] - cs_t[h:h + 1, :], NEG_INF))
            y_diag.append(jnp.dot((G * decay).astype(BF16), xdt_b[:, cols], preferred_element_type=F32))
        y_groups.append(jnp.where(lo_half, y_diag[0], y_diag[1]) + y_off)
        upd = lax.dot_general(bm_g, xout_b[:, cols], (((0,), (0,)), ((), ())), preferred_element_type=F32)
        st_scr[bi, g] = jnp.where(lo_half_st, e_total[:, h0:h0 + 1], e_total[:, h1:h1 + 1]) * st_old + upd
    return jnp.concatenate(y_groups, axis=1)


def ssd_scan(ssd, xbc, par, d_skip, norm_g, s_f0, s_b0, *, z_blk, dt_blk, chunk=256):
    B, L, _ = ssd.shape
    R = min(chunk, L)
    assert L % R == 0 and B % SCAN_BATCH == 0
    n = L // R
    NB = SCAN_BATCH
    cidx = lambda ph, j: _scan_chunk_index(ph, j, n)
    st_spec = pl.BlockSpec((NB, SSD_GROUPS, LANES, LANES), lambda b, ph, j: (b, 0, 0, 0))
    st_shape = jax.ShapeDtypeStruct((B, SSD_GROUPS, LANES, LANES), F32)
    return pl.pallas_call(
        functools.partial(_ssd_kernel, n_chunks=n),
        grid=(B // NB, 2, n),
        in_specs=[
            pl.BlockSpec((NB, R, SSD_INNER), lambda b, ph, j: (b, cidx(ph, j), z_blk)),
            pl.BlockSpec((NB, R, xbc.shape[2]), lambda b, ph, j: (b, cidx(ph, j), 0)),
            pl.BlockSpec((NB, R, LANES), lambda b, ph, j: (b, cidx(ph, j), dt_blk)),
            pl.BlockSpec((16, LANES), lambda b, ph, j: (0, 0)),
            pl.BlockSpec((1, SSD_INNER), lambda b, ph, j: (0, 0)),
            pl.BlockSpec((1, SSD_INNER), lambda b, ph, j: (0, 0)),
            st_spec, st_spec,
        ],
        out_specs=[pl.BlockSpec((NB, R, SSD_INNER), lambda b, ph, j: (b, ph * j, 0)), st_spec, st_spec],
        out_shape=[jax.ShapeDtypeStruct((B, L, SSD_INNER), F32), st_shape, st_shape],
        scratch_shapes=[pltpu.VMEM((NB, n, R, SSD_INNER), F32), pltpu.VMEM((NB, SSD_GROUPS, LANES, LANES), F32)],
        compiler_params=pltpu.CompilerParams(
            dimension_semantics=("parallel", "arbitrary", "arbitrary"), vmem_limit_bytes=VMEM_LIMIT_BYTES),
        name="ssd_scan",
    )(ssd, xbc, ssd, par, d_skip.reshape(1, SSD_INNER), norm_g.reshape(1, SSD_INNER), s_f0, s_b0)


HG_HEADS = 4
HG_DK = 64
HG_DV = 64
HG_KEY = HG_HEADS * HG_DK
HG_SUB = 16
HG_CHUNK = 64
HG_SUBS_PER_CHUNK = HG_CHUNK // HG_SUB


def _split_dot(mask_b, x, terms=2):
    out = None
    for _ in range(terms):
        part = x.astype(BF16)
        prod = jnp.dot(mask_b, part, preferred_element_type=F32)
        out = prod if out is None else out + prod
        x = x - part.astype(F32)
    return out


def _hgrn_kernel(q_ref, f_ref, v_ref, g_ref, lbl_ref, ng_ref, sf0_ref, sb0_ref,
                 y_ref, sf_ref, sb_ref, ob_scr, st_scr, *, n_blocks, layer, depth):
    ph = pl.program_id(1)
    j = pl.program_id(2)
    is_fwd = ph == 1
    c = _scan_chunk_index(ph, j, n_blocks)
    NB, R = q_ref.shape[0], q_ref.shape[1]
    W = HG_KEY

    @pl.when((j == 0) & is_fwd)
    def _():
        st_scr[...] = sf0_ref[...]

    @pl.when((j == 0) & jnp.logical_not(is_fwd))
    def _():
        st_scr[...] = sb0_ref[...]

    def lower_bound(d):
        x = lbl_ref[d * depth:(d + 1) * depth, :]
        e = jnp.exp(x - jnp.max(x, axis=0, keepdims=True))
        p = e / jnp.sum(e, axis=0, keepdims=True)
        return jnp.sum(p[1:layer + 1, :], axis=0, keepdims=True) if layer > 0 else jnp.zeros((1, W), F32)

    lb = jnp.where(is_fwd, lower_bound(0), lower_bound(1))
    row = lax.broadcasted_iota(jnp.int32, (R, R), 0)
    col = lax.broadcasted_iota(jnp.int32, (R, R), 1)
    dirge = jnp.where(is_fwd, row - col, col - row) >= 0
    same_sub = (row // HG_SUB) == (col // HG_SUB)
    same_chunk = (row // HG_CHUNK) == (col // HG_CHUNK)
    one = lambda m: jnp.where(m, 1.0, 0.0).astype(BF16)
    dist = jnp.where(is_fwd, row // HG_SUB - col // HG_SUB, col // HG_SUB - row // HG_SUB)
    masks = dict(
        cum16=one(same_sub & dirge), tot16=one(same_sub), cum64=one(same_chunk & dirge), tot64=one(same_chunk),
        dist=[same_sub & dirge] + [same_chunk & (dist == d) for d in range(1, HG_SUBS_PER_CHUNK)])
    lane = lax.broadcasted_iota(jnp.int32, (R, W), 1)
    head_mask = [(lane // HG_DK) == h for h in range(HG_HEADS)]
    blocks = [_hgrn_block(bi, is_fwd, lb, masks, head_mask, q_ref, f_ref, v_ref) for bi in range(NB)]
    srow = lax.broadcasted_iota(jnp.int32, (W, W), 0)
    scol = lax.broadcasted_iota(jnp.int32, (W, W), 1)
    bd_mask = (srow // HG_DV) == (scol // HG_DK)

    def through_state(bi, reverse):
        o_intra, qd, ke, v_b, tot64 = blocks[bi]
        n_chunks = R // HG_CHUNK
        st = st_scr[bi]
        parts = [None] * n_chunks
        for i in (reversed(range(n_chunks)) if reverse else range(n_chunks)):
            rows = slice(i * HG_CHUNK, (i + 1) * HG_CHUNK)
            parts[i] = lax.dot_general(qd[rows], st.astype(BF16), (((1,), (1,)), ((), ())), preferred_element_type=F32)
            upd = lax.dot_general(v_b[rows], ke[rows], (((0,), (0,)), ((), ())), preferred_element_type=F32)
            st = jnp.exp(tot64[i * HG_CHUNK:i * HG_CHUNK + 1, :]) * st + jnp.where(bd_mask, upd, 0.0)
        st_scr[bi] = st
        return o_intra + jnp.concatenate(parts, axis=0)

    @pl.when(jnp.logical_not(is_fwd))
    def _():
        for bi in range(NB):
            ob_scr[bi, c] = through_state(bi, reverse=True)

    @pl.when(is_fwd)
    def _():
        for bi in range(NB):
            ot = through_state(bi, reverse=False) + ob_scr[bi, c]
            sq = ot * ot
            ms = jnp.zeros((R, W), F32)
            for h in range(HG_HEADS):
                s = jnp.sum(jnp.where(head_mask[h], sq, 0.0), axis=-1, keepdims=True) * (1.0 / HG_DV)
                ms = jnp.where(head_mask[h], s, ms)
            g = g_ref[bi]
            y_ref[bi] = ot * lax.rsqrt(ms + RMS_EPS) * ng_ref[...] * (g * jax.nn.sigmoid(g))

    @pl.when((j == n_blocks - 1) & is_fwd)
    def _():
        sf_ref[...] = st_scr[...]

    @pl.when((j == n_blocks - 1) & jnp.logical_not(is_fwd))
    def _():
        sb_ref[...] = st_scr[...]


def _hgrn_block(bi, is_fwd, lb, masks, head_mask, q_ref, f_ref, v_ref):
    R = q_ref.shape[1]
    W = HG_KEY
    sig = jax.nn.sigmoid(f_ref[bi])
    logf = jnp.log(lb + (1.0 - lb) * sig)
    k = (1.0 - lb) * (1.0 - sig)
    q = q_ref[bi] * (HG_DK ** -0.5)
    v_b = v_ref[bi].astype(BF16)
    logf_hi = logf.astype(BF16)
    logf_lo = (logf - logf_hi.astype(F32)).astype(BF16)
    msum = lambda m: (jnp.dot(m, logf_hi, preferred_element_type=F32) + jnp.dot(m, logf_lo, preferred_element_type=F32))
    b16 = msum(masks["cum16"])
    tot16 = msum(masks["tot16"])
    b64 = msum(masks["cum64"])
    tot64 = msum(masks["tot64"])

    shifted = lambda s: jnp.where(is_fwd, pltpu.roll(tot16, R - s, 0), pltpu.roll(tot16, s, 0))
    nx1 = shifted(HG_SUB)
    nx2 = nx1 + shifted(2 * HG_SUB)
    k_end16 = k * jnp.exp(tot16 - b16)
    keys = [(k * jnp.exp(-b16)).astype(BF16), k_end16.astype(BF16),
            (k_end16 * jnp.exp(nx1)).astype(BF16), (k_end16 * jnp.exp(nx2)).astype(BF16)]
    q16 = q * jnp.exp(b16)
    qm =jnp.concatenate([jnp.where(m, q16, 0.0) for m in head_mask], axis=0).astype(BF16)
    att = [jnp.zeros((R, R), F32)] * HG_HEADS
    for d in range(HG_SUBS_PER_CHUNK):
        sc = lax.dot_general(qm, keys[d], (((1,), (1,)), ((), ())), preferred_element_type=F32)
        att = [jnp.where(masks["dist"][d], sc[h * R:(h + 1) * R], att[h]) for h in range(HG_HEADS)]
    o_all = jnp.dot(jnp.concatenate(att, axis=0).astype(BF16), v_b, preferred_element_type=F32)
    o = jnp.zeros((R, W), F32)
    for h in range(HG_HEADS):
        o = jnp.where(head_mask[h], o_all[h * R:(h + 1) * R], o)

    return o, (q * jnp.exp(b64)).astype(BF16), (k * jnp.exp(tot64 - b64)).astype(BF16), v_b, tot64


def hgrn_scan(hg, lb_logits, norm_g, s_f0, s_b0, *, layer, block=256):
    B, L, _ = hg.shape
    W = HG_KEY
    R = min(block, L)
    assert L % R == 0 and R % HG_CHUNK == 0 and B % SCAN_BATCH == 0
    n = L // R
    NB = SCAN_BATCH
    depth = lb_logits.shape[0] // 2
    cidx = lambda ph, j: _scan_chunk_index(ph, j, n)
    col = lambda blk: pl.BlockSpec((NB, R, W), lambda b, ph, j: (b, cidx(ph, j), blk))
    st_spec = pl.BlockSpec((NB, W, W), lambda b, ph, j: (b, 0, 0))
    st_shape = jax.ShapeDtypeStruct((B, W, W), F32)
    return pl.pallas_call(
        functools.partial(_hgrn_kernel, n_blocks=n, layer=layer, depth=depth),
        grid=(B // NB, 2, n),
        in_specs=[
            col(0),
            pl.BlockSpec((NB, R, W), lambda b, ph, j: (b, cidx(ph, j), 2 - ph)),
            col(3), col(4),
            pl.BlockSpec(lb_logits.shape, lambda b, ph, j: (0, 0)),
            pl.BlockSpec((1, W), lambda b, ph, j: (0, 0)),
            st_spec, st_spec,
        ],
        out_specs=[pl.BlockSpec((NB, R, W), lambda b, ph, j: (b, ph * j, 0)), st_spec, st_spec],
        out_shape=[jax.ShapeDtypeStruct((B, L, W), F32), st_shape, st_shape],
        scratch_shapes=[pltpu.VMEM((NB, n, R, W), F32), pltpu.VMEM((NB, W, W), F32)],
        compiler_params=pltpu.CompilerParams(
            dimension_semantics=("parallel", "arbitrary", "arbitrary"), vmem_limit_bytes=VMEM_LIMIT_BYTES),
        name="hgrn_scan",
    )(hg, hg, hg, hg, lb_logits, jnp.tile(norm_g, HG_HEADS).reshape(1, W), s_f0, s_b0)


HY_WIDTH = 256
HY_BANDS = 16
HY_HIDDEN = 64
SUBLANES = 8


def _dot_hi(a, b):
    return jnp.dot(a, b, precision=lax.Precision.HIGHEST, preferred_element_type=F32)


def _hy_filter_kernel(bands_ref, w1t_ref, w1f_ref, b1_ref, fr1_ref, w2_ref, b2_ref, fr2_ref, w3a_ref, w3b_ref,
                      decay_ref, f_ref, den_ref, cosb_scr, sinb_scr, *, L):
    i = pl.program_id(0)
    TR = f_ref.shape[1]
    step = 2.0 * math.pi / L
    bands = bands_ref[...]

    @pl.when(i == 0)
    def _():
        r = lax.broadcasted_iota(jnp.int32, (TR, LANES), 0).astype(F32)
        ang = (step * r) * bands
        cosb_scr[...] = jnp.cos(ang)
        sinb_scr[...] = jnp.sin(ang)
        den_ref[...] = jnp.zeros_like(den_ref)

    one_row = lambda v: jnp.broadcast_to(v, (SUBLANES, LANES))
    base = one_row((step * (i * TR).astype(F32)) * bands)
    cos_a, sin_a = jnp.cos(base)[0:1], jnp.sin(base)[0:1]
    full = one_row((2.0 * math.pi) * bands)
    cos_p, sin_p = jnp.cos(full)[0:1], jnp.sin(full)[0:1]
    cos_b, sin_b = cosb_scr[...], sinb_scr[...]
    cos0 = cos_a * cos_b - sin_a * sin_b
    sin0 = sin_a * cos_b + cos_a * sin_b
    cos1 = cos_p * cos0 + sin_p * sin0
    sin1 = sin_p * cos0 - cos_p * sin0

    m = i * TR + lax.broadcasted_iota(jnp.int32, (TR, HY_WIDTH), 0)
    t0 = m.astype(F32) * (1.0 / (L - 1))
    t1 = jnp.where(m == 0, 0, L - m).astype(F32) * (1.0 / (L - 1))
    lane = lax.broadcasted_iota(jnp.int32, (TR, LANES), 1)
    t_both = jnp.where(lane < HY_HIDDEN, t0[:, :LANES], t1[:, :LANES])
    feats = jnp.concatenate([cos0, sin0, cos1, sin1], axis=1)
    pre = t_both * w1t_ref[...] + _dot_hi(feats, w1f_ref[...]) + b1_ref[...]
    h = jnp.sin(fr1_ref[...] * pre)
    h = jnp.sin(fr2_ref[...] * (_dot_hi(h, w2_ref[...]) + b2_ref[...]))
    decay = jnp.abs(decay_ref[...])
    filt0 = _dot_hi(h, w3a_ref[...]) * jnp.exp(-t0 * decay)
    filt1 = jnp.where(m == 0, 0.0, _dot_hi(h, w3b_ref[...]) * jnp.exp(-t1 * decay))
    f_ref[0] = filt0
    f_ref[1] = filt1
    den_ref[...] += jnp.sum(jnp.abs(filt0) + jnp.abs(filt1), axis=0, keepdims=True)


def hy_filter(L, w1, b1, fr1, w2, b2, fr2, w3, decay, *, tr=512):
    tr = min(tr, L)
    H = HY_HIDDEN
    bands = np.zeros((1, LANES), np.float32)
    bands[0, :HY_BANDS] = np.linspace(1e-4, HY_BANDS - 1, HY_BANDS, dtype=np.float32)
    pad_rows = lambda w: jnp.pad(w, ((0, LANES - HY_BANDS), (0, 0)))
    twice = lambda v: jnp.concatenate([v, v]).reshape(1, 2 * H)
    zeros = jnp.zeros((LANES, H), F32)
    w_cos, w_sin = pad_rows(w1[1:1 + HY_BANDS]), pad_rows(w1[1 + HY_BANDS:])
    w1_feat = jnp.concatenate([
        jnp.concatenate([w_cos, zeros], axis=1), jnp.concatenate([-w_sin, zeros], axis=1),
        jnp.concatenate([zeros, w_cos], axis=1), jnp.concatenate([zeros, -w_sin], axis=1)], axis=0)
    zh = jnp.zeros((H, H), F32)
    w2_both = jnp.concatenate([jnp.concatenate([w2, zh], axis=1), jnp.concatenate([zh, w2], axis=1)], axis=0)
    zw = jnp.zeros((H, HY_WIDTH), F32)
    w3_a = jnp.concatenate([w3[:, :HY_WIDTH], zw], axis=0)
    w3_b = jnp.concatenate([zw, w3[:, HY_WIDTH:]], axis=0)
    args = (jnp.asarray(bands), twice(w1[0]), w1_feat, twice(b1), twice(fr1), w2_both, twice(b2), twice(fr2),
            w3_a, w3_b, decay.reshape(1, -1))
    return pl.pallas_call(
        functools.partial(_hy_filter_kernel, L=L),
        grid=(L // tr,),
        in_specs=[pl.BlockSpec(a.shape, lambda i: (0, 0)) for a in args],
        out_specs=[pl.BlockSpec((2, tr, HY_WIDTH), lambda i: (0, i, 0)), pl.BlockSpec((1, HY_WIDTH), lambda i: (0, 0))],
        out_shape=[jax.ShapeDtypeStruct((2, L, HY_WIDTH), F32), jax.ShapeDtypeStruct((1, HY_WIDTH), F32)],
        scratch_shapes=[pltpu.VMEM((tr, LANES), F32), pltpu.VMEM((tr, LANES), F32)],
        compiler_params=pltpu.CompilerParams(
            dimension_semantics=("arbitrary",), vmem_limit_bytes=VMEM_LIMIT_BYTES),
        name="hy_filter",
    )(*args)


FFT_N1 = 64
FFT_N2 = 128
FFT_N = FFT_N1 * FFT_N2
FFT_L = FFT_N // 2
FFT_N1_NZ = FFT_N1 // 2
FFT_UNROLL_N1_STAGE = 2
FFT_UNROLL_N2_STAGE = 8


@functools.lru_cache(maxsize=None)
def _fft_constants():
    eye = np.eye(SUBLANES)
    k1 = np.arange(FFT_N1)[:, None]
    n1 = np.arange(FFT_N1_NZ)[None, :]
    ang1 = 2.0 * np.pi * ((k1 * n1) % FFT_N1) / FFT_N1
    kron_fwd = np.concatenate([np.kron(np.cos(ang1), eye), np.kron(np.sin(ang1), eye)], axis=0)
    kron_inv = np.concatenate([np.kron(np.cos(ang1).T, eye), np.kron(np.sin(ang1).T, eye)], axis=0) / FFT_N
    k = np.arange(FFT_N1)[:, None, None] + FFT_N1 * np.arange(FFT_N2)[None, :, None]
    n2 = np.arange(FFT_N2)[None, None, :]
    ang2 = 2.0 * np.pi * ((k * n2) % FFT_N) / FFT_N
    g_fwd = np.concatenate([np.cos(ang2), np.sin(ang2)], axis=1)
    g_inv = np.concatenate([np.cos(ang2).transpose(0, 2, 1), np.sin(ang2).transpose(0, 2, 1)], axis=1)
    n1_all = np.arange(FFT_N1)[None, :]
    ang1_all = 2.0 * np.pi * ((k1 * n1_all) % FFT_N1) / FFT_N1
    kron_full = np.concatenate([np.kron(np.cos(ang1_all), eye), np.kron(np.sin(ang1_all), eye)], axis=0)
    to_b = lambda a: jnp.asarray(a, dtype=F32).astype(BF16)
    return to_b(kron_fwd), to_b(g_fwd), to_b(g_inv), to_b(kron_inv), to_b(kron_full)


def _cplx_fwd(r, half_rows, cb):
    cr, sr = r[:half_rows], r[half_rows:]
    return cr[:, :cb] + sr[:, cb:], cr[:, cb:] - sr[:, :cb]


def _cplx_inv(r, half_rows, cb):
    cr, sr = r[:half_rows], r[half_rows:]
    return cr[:, :cb] - sr[:, cb:], cr[:, cb:] + sr[:, :cb]


def _fft_stage1(load_group, kron_ref, a_scr, cb):
    def body(jt, carry):
        rows = pl.ds(pl.multiple_of(jt * SUBLANES, SUBLANES), SUBLANES)
        d = load_group(rows).reshape(kron_ref.shape[1], 2 * cb).astype(BF16)
        r = jnp.dot(kron_ref[...], d, preferred_element_type=F32)
        a_re, a_im = _cplx_fwd(r, FFT_N1 * SUBLANES, cb)
        a_scr[:, rows, :] = jnp.concatenate([a_re, a_im], axis=1).reshape(FFT_N1, SUBLANES, 2 * cb)
        return carry
    lax.fori_loop(0, FFT_N2 // SUBLANES, body, 0, unroll=FFT_UNROLL_N1_STAGE)


def _fft_fwd_data_kernel(z_ref, h_ref, kron_ref, g_ref, p_ref, a_scr):
    cb = z_ref.shape[4]
    load = lambda rows: jnp.concatenate([z_ref[0, 0, :, rows, :], z_ref[0, 1, :, rows, :]], axis=-1)
    _fft_stage1(load, kron_ref, a_scr, cb)

    def body(k1, carry):
        r = jnp.dot(g_ref[k1], a_scr[k1].astype(BF16), preferred_element_type=F32)
        x_re, x_im = _cplx_fwd(r, FFT_N2, cb)
        h = h_ref[0, k1]
        h_re, h_im = h[:, :cb], h[:, cb:]
        p_ref[0, 0, k1] = jnp.concatenate([x_re * h_re - x_im * h_im, x_re * h_im + x_im * h_re], axis=1).astype(BF16)
        return carry
    lax.fori_loop(0, FFT_N1, body, 0, unroll=FFT_UNROLL_N2_STAGE)


def _fft_fwd_filter_kernel(f_ref, den_ref, kron_ref, g_ref, h_ref, a_scr):
    cb = f_ref.shape[3]
    inv_den = 1.0 / den_ref[...]

    def load(rows):
        re = jnp.concatenate([f_ref[0, :, rows, :], f_ref[1, :, rows, :]], axis=0) * inv_den
        return jnp.concatenate([re, jnp.zeros_like(re)], axis=-1)
    _fft_stage1(load, kron_ref, a_scr, cb)

    def body(k1, carry):
        r = jnp.dot(g_ref[k1], a_scr[k1].astype(BF16), preferred_element_type=F32)
        x_re, x_im = _cplx_fwd(r, FFT_N2, cb)
        h_ref[0, k1] = jnp.concatenate([x_re, x_im], axis=1)
        return carry
    lax.fori_loop(0, FFT_N1, body, 0, unroll=FFT_UNROLL_N2_STAGE)


def _fft_inv_kernel(p_ref, x0_ref, w_ref, bias_ref, gi_ref, kron_ref, o_ref, b_scr):
    cb = o_ref.shape[4]

    def body3(k1, carry):
        r = jnp.dot(gi_ref[k1], p_ref[0, 0, k1], preferred_element_type=F32)
        b_re, b_im = _cplx_inv(r, FFT_N2, cb)
        b_scr[k1] = jnp.concatenate([b_re, b_im], axis=1)
        return carry
    lax.fori_loop(0, FFT_N1, body3, 0, unroll=FFT_UNROLL_N2_STAGE)

    bias = bias_ref[...].reshape(1, 1, cb)

    def body4(jt, carry):
        rows = pl.ds(pl.multiple_of(jt * SUBLANES, SUBLANES), SUBLANES)
        d = b_scr[:, rows, :].reshape(FFT_N1 * SUBLANES, 2 * cb).astype(BF16)
        r = jnp.dot(kron_ref[...], d, preferred_element_type=F32)
        y_re, y_im = _cplx_inv(r, FFT_N1_NZ * SUBLANES, cb)
        for which, y in enumerate((y_re, y_im)):
            y3 = y.reshape(FFT_N1_NZ, SUBLANES, cb)
            o_ref[0, which, :, rows, :] = x0_ref[0, which, :, rows, :] * (y3 + w_ref[0, which, :, rows, :] * bias)
        return carry
    lax.fori_loop(0, FFT_N2 // SUBLANES, body4, 0, unroll=FFT_UNROLL_N1_STAGE)


def hyena_spectrum(fu, den, *, cb=128):
    _, g_fwd, _, _, kron_full = _fft_constants()
    W = fu.shape[2]
    ncb = W // cb
    f4 = fu.reshape(2, FFT_N1_NZ, FFT_N2, W)
    const = lambda a: pl.BlockSpec(a.shape, lambda c: (0,) * a.ndim, pipeline_mode=pl.Buffered(1))
    return pl.pallas_call(
        _fft_fwd_filter_kernel,
        grid=(ncb,),
        in_specs=[
            pl.BlockSpec((2, FFT_N1_NZ, FFT_N2, cb), lambda c: (0, 0, 0, c)),
            pl.BlockSpec((1, cb), lambda c: (0, c)),
            const(kron_full), const(g_fwd),
        ],
        out_specs=pl.BlockSpec((1, FFT_N1, FFT_N2, 2 * cb), lambda c: (c, 0, 0, 0)),
        out_shape=jax.ShapeDtypeStruct((ncb, FFT_N1, FFT_N2, 2 * cb), F32),
        scratch_shapes=[pltpu.VMEM((FFT_N1, FFT_N2, 2 * cb), F32)],
        compiler_params=pltpu.CompilerParams(
            dimension_semantics=("parallel",), vmem_limit_bytes=VMEM_LIMIT_BYTES),
        name="hyena_spectrum",
    )(f4, den, kron_full, g_fwd)


def hyena_conv(x0c, hw, spec, bias, *, cb=128):
    kron_fwd, g_fwd, g_inv, kron_inv, _ = _fft_constants()
    B, L, W = hw.shape
    assert L == FFT_L and B % 2 == 0
    ncb = W // cb
    pair_shape = (B // 2, 2, FFT_N1_NZ, FFT_N2, W)
    x5 = x0c.reshape(pair_shape)
    w5 = hw.reshape(pair_shape)
    const = lambda a: pl.BlockSpec(a.shape, lambda c, p: (0,) * a.ndim, pipeline_mode=pl.Buffered(1))
    pair = pl.BlockSpec((1, 2, FFT_N1_NZ, FFT_N2, cb), lambda c, p: (p, 0, 0, 0, c))
    pspec = pl.BlockSpec((1, 1, FFT_N1, FFT_N2, 2 * cb), lambda c, p: (c, p, 0, 0, 0))
    prod = pl.pallas_call(
        _fft_fwd_data_kernel,
        grid=(ncb, B // 2),
        in_specs=[pair, pl.BlockSpec((1, FFT_N1, FFT_N2, 2 * cb), lambda c, p: (c, 0, 0, 0)),
                  const(kron_fwd), const(g_fwd)],
        out_specs=pspec,
        out_shape=jax.ShapeDtypeStruct((ncb, B // 2, FFT_N1, FFT_N2, 2 * cb), BF16),
        scratch_shapes=[pltpu.VMEM((FFT_N1, FFT_N2, 2 * cb), F32)],
        compiler_params=pltpu.CompilerParams(
            dimension_semantics=("parallel", "parallel"), vmem_limit_bytes=VMEM_LIMIT_BYTES),
        name="hyena_fft_fwd",
    )(w5, spec, kron_fwd, g_fwd)
    out = pl.pallas_call(
        _fft_inv_kernel,
        grid=(ncb, B // 2),
        in_specs=[pspec, pair, pair, pl.BlockSpec((1, cb), lambda c, p: (0, c)), const(g_inv), const(kron_inv)],
        out_specs=pair,
        out_shape=jax.ShapeDtypeStruct(pair_shape, F32),
        scratch_shapes=[pltpu.VMEM((FFT_N1, FFT_N2, 2 * cb), F32)],
        compiler_params=pltpu.CompilerParams(
            dimension_semantics=("parallel", "parallel"), vmem_limit_bytes=VMEM_LIMIT_BYTES),
        name="hyena_fft_inv",
    )(prod, x5, w5, bias.reshape(1, W), g_inv, kron_inv)
    return out.reshape(B, L, W)


@functools.lru_cache(maxsize=None)
def _dense_dft_constants(L):
    n_full = 2 * L
    k = np.arange(n_full)[:, None]
    n = np.arange(n_full)[None, :]
    ang = 2.0 * np.pi * ((k * n) % n_full) / n_full
    fwd = np.concatenate([np.cos(ang), np.sin(ang)], axis=0)
    inv = np.concatenate([np.cos(ang[:L]), np.sin(ang[:L])], axis=0) / n_full
    return jnp.asarray(fwd, dtype=F32), jnp.asarray(inv, dtype=F32)


def _hyena_short_kernel(x0_ref, w_ref, f_ref, den_ref, bias_ref, ff_ref, fi_ref, o_ref):
    L, cb = o_ref.shape[2], o_ref.shape[3]
    filt = jnp.concatenate([f_ref[0], f_ref[1]], axis=0) / den_ref[...]
    r = _dot_hi(ff_ref[...], filt)
    h_re, h_im = r[:2 * L], -r[2 * L:]
    z = jnp.concatenate([w_ref[0, 0], w_ref[0, 1]], axis=1)
    x_re, x_im = _cplx_fwd(_dot_hi(ff_ref[:, :L], z), 2 * L, cb)
    p = jnp.concatenate([x_re * h_re - x_im * h_im, x_re * h_im + x_im * h_re], axis=1)
    y_re, y_im = _cplx_inv(_dot_hi(fi_ref[...], p), L, cb)
    for which, y in enumerate((y_re, y_im)):
        o_ref[0, which] = x0_ref[0, which] * (y + w_ref[0, which] * bias_ref[...])


def hyena_conv_short(x0c, hw, fu, den, bias, *, cb=128):
    B, L, W = hw.shape
    fwd, inv = _dense_dft_constants(L)
    pair_shape = (B // 2, 2, L, W)
    pair = pl.BlockSpec((1, 2, L, cb), lambda p, c: (p, 0, 0, c))
    vec = pl.BlockSpec((1, cb), lambda p, c: (0, c))
    const = lambda a: pl.BlockSpec(a.shape, lambda p, c: (0, 0))
    out = pl.pallas_call(
        _hyena_short_kernel,
        grid=(B // 2, W // cb),
        in_specs=[pair, pair, pl.BlockSpec((2, L, cb), lambda p, c: (0, 0, c)), vec, vec, const(fwd), const(inv)],
        out_specs=pair,
        out_shape=jax.ShapeDtypeStruct(pair_shape, F32),
        compiler_params=pltpu.CompilerParams(
            dimension_semantics=("parallel", "parallel"), vmem_limit_bytes=VMEM_LIMIT_BYTES),
        name="hyena_conv_short",
    )(x0c.reshape(pair_shape), hw.reshape(pair_shape), fu, den, bias.reshape(1, W), fwd, inv)
    return out.reshape(B, L, W)


GRID_W = 64
SC_COLS = 3 * BRANCH_W
HG_COLS = 3 * HG_KEY + 2 * HG_HEADS * HG_DV
SSD_XBC = SSD_INNER + 2 * SSD_GROUPS * SSD_STATE
assert SSD_XBC == SSD_XBC_COLS and HY_COLS == 3 * HY_WIDTH


def _sincos_1d(pos, dim):
    omega = 1.0 / (10000.0 ** (jnp.arange(dim // 2, dtype=F32) / (dim // 2)))
    ang = pos.astype(F32)[:, None] * omega[None]
    return jnp.concatenate([jnp.sin(ang), jnp.cos(ang)], -1)


def _grid_pos_embed(rows, dim):
    row = jnp.repeat(jnp.arange(rows), GRID_W)
    col = jnp.tile(jnp.arange(GRID_W), rows)
    return jnp.concatenate([_sincos_1d(row, dim // 2), _sincos_1d(col, dim // 2)], -1)


def kernel(x, c, ctx, c_ctx, w_ada, b_ada, w_in, hy_conv_w, hy_conv_b, hy_w1, hy_b1, hy_freq1,
           hy_w2, hy_b2, hy_freq2, hy_w3, hy_decay, hy_bias, sc_conv_w, hg_lb_logits, hg_norm_g,
           ssd_conv_w, ssd_conv_b, ssd_a_log, ssd_dt_bias, ssd_d, ssd_norm_g, w_gate, b_gate, w_br,
           w_o, ln1_g, ln1_b, ln2_g, ln2_b, w_router, b_router, w_e1, w_e3, w_e2):
    B, L, D = x.shape
    LC = ctx.shape[1]
    depth = w_in.shape[0]
    pos_table = _grid_pos_embed(L // GRID_W, D).astype(x.dtype)
    lat = x.reshape(B * L, D)
    cx = ctx.reshape(B * LC, D)

    n_vec = -(-(B + 1) // SUBLANES) * SUBLANES
    cvecs = jnp.concatenate([c, c_ctx[None], jnp.zeros((n_vec - B - 1, D), c.dtype)], axis=0)
    mods = ada_mod(cvecs, w_ada, b_ada).reshape(depth, n_vec, 6, D)
    lb_logits = hg_lb_logits.reshape(2 * depth, HG_KEY)

    out_widths = (BRANCH_W, BRANCH_W, BRANCH_W, SSD_XBC, HG_COLS, SSD_INNER + LANES)
    z_state = lambda *shape: jnp.zeros(shape, F32)
    w_e1_b, w_e3_b, w_e2_b = w_e1.astype(BF16), w_e3.astype(BF16), w_e2.astype(BF16)
    c_hy, c_sc, c_hg = HY_COLS, HY_COLS + SC_COLS, HY_COLS + SC_COLS + HG_COLS
    c_z, c_xbc = c_hg + SSD_INNER, c_hg + SSD_INNER + SSD_XBC

    for l in range(depth):
        ctx_out = l < depth - 1
        mod_lat, mod_ctx = mods[l, :B], mods[l, B:B + 1]
        wl = w_in[l]
        w_in_b = jnp.concatenate([
            wl[:, :c_hy], wl[:, c_hy + BRANCH_W:c_sc], wl[:, c_z:c_xbc],
            wl[:, c_hy:c_hy + BRANCH_W], wl[:, c_sc:c_hg], wl[:, c_hg:c_z],
            jnp.pad(wl[:, c_xbc:], ((0, 0), (0, LANES - 2 * SSD_HEADS))),
        ], axis=1).astype(BF16)
        conv_par = (hy_conv_w[l], hy_conv_b[l], sc_conv_w[l], ssd_conv_w[l], ssd_conv_b[l])
        ssd_par = jnp.zeros((2 * SUBLANES, LANES), F32)
        ssd_par = ssd_par.at[:SUBLANES, :2 * SSD_HEADS].set(jnp.broadcast_to(ssd_a_log[l].reshape(1, -1), (SUBLANES, 2 * SSD_HEADS)))
        ssd_par = ssd_par.at[SUBLANES:, :2 * SSD_HEADS].set(jnp.broadcast_to(ssd_dt_bias[l].reshape(1, -1), (SUBLANES, 2 * SSD_HEADS)))
        d_skip = jnp.repeat(ssd_d[l], SSD_HEADDIM)
        filt_args = (hy_w1[l], hy_b1[l], hy_freq1[l], hy_w2[l], hy_b2[l], hy_freq2[l], hy_w3[l], hy_decay[l])

        def mixers(tokens, mod, seg_len, hg_state, ssd_state, want_out, pos=None):
            n_seq = tokens.shape[0] // seg_len
            x0c, hw, sc_y, xbc, hg, zdt = in_proj(
                tokens, mod, w_in_b, conv_par, out_widths, seq_len=seg_len, rows_per_mod=mod_rows(mod, tokens),
                tm=min(512, seg_len), pos=pos)
            seq = lambda a: a.reshape(n_seq, seg_len, a.shape[1])
            hg_y, hg_f, hg_b = hgrn_scan(seq(hg), lb_logits, hg_norm_g[l], *hg_state, layer=l)
            ssd_y, ssd_f, ssd_b = ssd_scan(seq(zdt), seq(xbc), ssd_par, d_skip, ssd_norm_g[l], *ssd_state,
                                           z_blk=0, dt_blk=SSD_INNER // LANES)
            branches = None
            if want_out:
                fu, den = hy_filter(seg_len, *filt_args)
                if seg_len == FFT_L:
                    hy_y = hyena_conv(seq(x0c), seq(hw), hyena_spectrum(fu, den), hy_bias[l])
                else:
                    hy_y = hyena_conv_short(seq(x0c), seq(hw), fu, den, hy_bias[l])
                flat = lambda a: a.reshape(tokens.shape[0], a.shape[2])
                branches = [flat(hy_y), sc_y, flat(hg_y), flat(ssd_y)]
            return branches, (hg_f, hg_b), (ssd_f, ssd_b)

        def mod_rows(mod, tokens):
            return tokens.shape[0] // mod.shape[0]

        def finish(tokens, mod, branches, pos=None):
            rows = mod_rows(mod, tokens)
            t1 = merge_ln(tokens, mod, branches, w_gate[l].astype(BF16), b_gate[l], w_br[l].astype(BF16),
                          w_o[l].astype(BF16), ln1_g[l], ln1_b[l], rows_per_mod=rows, tm=512, pos=pos)
            return moe_ln(t1, mod, w_router, b_router, w_e1_b, w_e3_b, w_e2_b, ln2_g[l], ln2_b[l],
                          layer=l, rows_per_mod=rows, tm=1024)

        zero_hg = (z_state(B, HG_KEY, HG_KEY),) * 2
        zero_ssd = (z_state(B, SSD_GROUPS, LANES, LANES),) * 2
        br_ctx, hg_state, ssd_state = mixers(cx, mod_ctx, LC, zero_hg, zero_ssd, ctx_out)
        pos = pos_table if l == 0 else None
        br_lat, _, _ = mixers(lat, mod_lat, L, hg_state, ssd_state, True, pos=pos)
        lat = finish(lat, mod_lat, br_lat, pos=pos)
        if ctx_out:
            cx = finish(cx, mod_ctx, br_ctx)
    return lat.reshape(B, L, D)
```

```python
import functools
import math

import jax
import jax.numpy as jnp
import numpy as np
from jax import lax
from jax.experimental import pallas as pl
from jax.experimental.pallas import tpu as pltpu

F32 = jnp.float32
BF16 = jnp.bfloat16

N_EXPERTS = 16
N_EXPERT_GROUPS = 4
GROUP_SIZE = N_EXPERTS // N_EXPERT_GROUPS
DEPTH = 2
DEEPNORM_ALPHA = (2 * DEPTH) ** 0.25
LN_EPS = 1e-5
RMS_EPS = 1e-6

VMEM_LIMIT_BYTES = 52 * 1024 * 1024


def _layernorm_rows(z, g, b):
    mu = jnp.mean(z, axis=-1, keepdims=True)
    zc = z - mu
    var = jnp.mean(zc * zc, axis=-1, keepdims=True)
    return zc * lax.rsqrt(var + LN_EPS) * g + b


def _route_t(logits_t, bias_col):
    z = logits_t - jnp.max(logits_t, axis=0, keepdims=True)
    ex = jnp.exp(z)
    scores = ex / jnp.sum(ex, axis=0, keepdims=True)
    sel = scores + bias_col
    eidx = lax.broadcasted_iota(jnp.int32, sel.shape, 0)
    neg = jnp.float32(-jnp.inf)

    def first_argmax(m):
        top = jnp.max(m, axis=0, keepdims=True)
        return top, jnp.min(jnp.where(m == top, eidx, N_EXPERTS), axis=0, keepdims=True)

    best_score = None
    best_grp = None
    for g in range(N_EXPERT_GROUPS):
        m = jnp.where((eidx >= g * GROUP_SIZE) & (eidx < (g + 1) * GROUP_SIZE), sel, neg)
        t1, i1 = first_argmax(m)
        t2 = jnp.max(jnp.where(eidx == i1, neg, m), axis=0, keepdims=True)
        s = t1 + t2
        if g == 0:
            best_score, best_grp = s, jnp.zeros_like(i1)
        else:
            better = s > best_score
            best_score = jnp.where(better, s, best_score)
            best_grp = jnp.where(better, g, best_grp)
    lo = best_grp * GROUP_SIZE
    masked = jnp.where((eidx >= lo) & (eidx < lo + GROUP_SIZE), sel, neg)
    _, ia = first_argmax(masked)
    _, ib = first_argmax(jnp.where(eidx == ia, neg, masked))
    w = jnp.where((eidx == ia) | (eidx == ib), scores, 0.0)
    return w / jnp.sum(w, axis=0, keepdims=True), best_grp


MOE_CHUNK = 64
MOE_ARMS = (1024, 512, 256, 128, 64)
MOE_STEP_EXPERTS = 2
GATE_TERMS = 3
POS_LANE = GATE_TERMS * N_EXPERTS


def _moe_ln_kernel(x_ref, mod_ref, wr_ref, br_ref, w1_ref, w3_ref, w2_ref, lng_ref, lnb_ref,
                   o_ref, pt_scr, hc_scr, gc_scr, yc_scr, seg_smem):
    step = pl.program_id(1)
    TM, R = pt_scr.shape
    D = x_ref.shape[1]
    G = N_EXPERT_GROUPS

    @pl.when(step == 0)
    def _():
        h = _modulate(x_ref[...], mod_ref, MOD_SHIFT2, MOD_SCALE2)
        h_hi = h.astype(BF16)
        h_lo = (h - h_hi.astype(F32)).astype(BF16)
        r_hi = jnp.dot(h_hi, wr_ref[...], preferred_element_type=F32)
        r_lo = jnp.dot(h_lo, wr_ref[:, :LANES], preferred_element_type=F32)
        logits_t = (r_hi[:, :LANES] + r_hi[:, LANES:] + r_lo).T[:N_EXPERTS]
        gate_t, grp = _route_t(logits_t, br_ref[:, 0:1])

        gidx = lax.broadcasted_iota(jnp.int32, (SUBLANES, TM), 0)
        tok = lax.broadcasted_iota(jnp.int32, (SUBLANES, TM), 1)
        member = jnp.where(gidx == grp, 1.0, 0.0)
        incl = member
        shift = 1
        while shift < TM:
            incl = incl + jnp.where(tok >= shift, pltpu.roll(incl, shift, 1), 0.0)
            shift *= 2
        rank = incl - member
        offset = jnp.int32(0)
        pos_row = jnp.zeros((1, TM), F32)
        for g in range(G):
            n_chunks = (jnp.sum(member[g:g + 1, :]).astype(jnp.int32) + (MOE_CHUNK - 1)) // MOE_CHUNK
            seg_smem[g] = n_chunks
            seg_smem[G + g] = offset
            pos_row = pos_row + member[g:g + 1, :] * (rank[g:g + 1, :] + offset.astype(F32))
            offset = offset + n_chunks * MOE_CHUNK

        terms, rest = [], gate_t
        for _ in range(GATE_TERMS):
            part = rest.astype(BF16).astype(F32)
            terms.append(part)
            rest = rest - part
        pad = jnp.zeros((LANES - POS_LANE - 1, TM), F32)
        side = jnp.concatenate(terms + [pos_row, pad], axis=0).T
        pos_col = side[:, POS_LANE:POS_LANE + 1].astype(jnp.int32)
        pt_scr[...] = jnp.where(lax.broadcasted_iota(jnp.int32, (TM, R), 1) == pos_col, 1.0, 0.0).astype(BF16)
        p = jnp.where(lax.broadcasted_iota(jnp.int32, (R, TM), 0) == pos_row.astype(jnp.int32), 1.0, 0.0).astype(BF16)
        gathered = jnp.dot(p, jnp.concatenate([h_hi, side.astype(BF16)], axis=1), preferred_element_type=F32)
        hc_scr[...] = gathered[:, :D].astype(BF16)
        gc_scr[...] = gathered[:, D:]
        yc_scr[...] = jnp.zeros_like(yc_scr)

    g = step // (GROUP_SIZE // MOE_STEP_EXPERTS)
    n_chunks = seg_smem[g]
    base = seg_smem[G + g]

    def expert_rows(start, size):
        rows = pl.ds(pl.multiple_of(start, MOE_CHUNK), size)
        hc = hc_scr[rows, :]
        gc = gc_scr[rows, :]
        lane = lax.broadcasted_iota(jnp.int32, gc.shape, 1)
        acc = None
        for k in range(MOE_STEP_EXPERTS):
            e = step * MOE_STEP_EXPERTS + k
            a = jnp.dot(hc, w1_ref[k], preferred_element_type=F32)
            b = jnp.dot(hc, w3_ref[k], preferred_element_type=F32)
            mid = (a * jax.nn.sigmoid(a) * b).astype(BF16)
            y = jnp.dot(mid, w2_ref[k], preferred_element_type=F32)
            gcol = jnp.sum(jnp.where((lane % N_EXPERTS == e) & (lane < POS_LANE), gc, 0.0), axis=-1, keepdims=True)
            acc = gcol * y if acc is None else acc + gcol * y
        yc_scr[rows, :] += acc

    for size in MOE_ARMS:
        if size > TM:
            continue
        taken = (n_chunks & (size // MOE_CHUNK)) != 0

        @pl.when(taken)
        def _(base=base, size=size):
            expert_rows(base, size)

        base = base + jnp.where(taken, size, 0)

    @pl.when(step == N_EXPERTS // MOE_STEP_EXPERTS - 1)
    def _():
        moe = jnp.dot(pt_scr[...], yc_scr[...].astype(BF16), preferred_element_type=F32)
        z = DEEPNORM_ALPHA * x_ref[...] + mod_ref[0, MOD_GATE2:MOD_GATE2 + 1, :] * moe
        o_ref[...] = _layernorm_rows(z, lng_ref[...], lnb_ref[...])


def moe_ln(x, mod, w_router, b_router, w1, w3, w2, ln_g, ln_b, *, layer, rows_per_mod, tm):
    T, D = x.shape
    _, E, _, FF = w1.shape
    assert T % tm == 0 and rows_per_mod % tm == 0 and tm <= MOE_ARMS[0] and tm % MOE_CHUNK == 0
    tiles_per_mod = rows_per_mod // tm
    sorted_rows = tm + N_EXPERT_GROUPS * MOE_CHUNK
    wr_hi = w_router.astype(BF16)
    wr_lo = (w_router - wr_hi.astype(F32)).astype(BF16)
    lane_pad = lambda w: jnp.pad(w, ((0, 0), (0, LANES - E)))
    wr_split = jnp.concatenate([lane_pad(wr_hi), lane_pad(wr_lo)], axis=1)
    br_col = jnp.broadcast_to(b_router.reshape(E, 1), (E, LANES))
    return pl.pallas_call(
        _moe_ln_kernel,
        grid=(T // tm, E // MOE_STEP_EXPERTS),
        in_specs=[
            pl.BlockSpec((tm, D), lambda i, e: (i, 0)),
            pl.BlockSpec((1, 6, D), lambda i, e: (i // tiles_per_mod, 0, 0)),
            pl.BlockSpec((D, 2 * LANES), lambda i, e: (0, 0)),
            pl.BlockSpec((E, LANES), lambda i, e: (0, 0)),
            pl.BlockSpec((None, MOE_STEP_EXPERTS, D, FF), lambda i, e: (layer, e, 0, 0)),
            pl.BlockSpec((None, MOE_STEP_EXPERTS, D, FF), lambda i, e: (layer, e, 0, 0)),
            pl.BlockSpec((None, MOE_STEP_EXPERTS, FF, D), lambda i, e: (layer, e, 0, 0)),
            pl.BlockSpec((1, D), lambda i, e: (0, 0)),
            pl.BlockSpec((1, D), lambda i, e: (0, 0)),
        ],
        out_specs=pl.BlockSpec((tm, D), lambda i, e: (i, 0)),
        out_shape=jax.ShapeDtypeStruct((T, D), F32),
        scratch_shapes=[
            pltpu.VMEM((tm, sorted_rows), BF16),
            pltpu.VMEM((sorted_rows, D), BF16),
            pltpu.VMEM((sorted_rows, LANES), F32),
            pltpu.VMEM((sorted_rows, D), F32),
            pltpu.SMEM((2 * N_EXPERT_GROUPS,), jnp.int32),
        ],
        compiler_params=pltpu.CompilerParams(
            dimension_semantics=("parallel", "arbitrary"), vmem_limit_bytes=VMEM_LIMIT_BYTES),
        name="moe_ln",
    )(x, mod, wr_split, br_col, w1, w3, w2, ln_g.reshape(1, D), ln_b.reshape(1, D))


def _ada_kernel(c_ref, w_ref, b_ref, o_ref):
    c = c_ref[...]
    s = c * jax.nn.sigmoid(c)
    o_ref[0] = jnp.dot(s, w_ref[0], precision=lax.Precision.HIGHEST, preferred_element_type=F32) + b_ref[0]


def ada_mod(cvecs, w_ada, b_ada, *, tn=1536):
    R, D = cvecs.shape
    depth, _, N = w_ada.shape
    assert N % tn == 0
    return pl.pallas_call(
        _ada_kernel,
        grid=(depth, N // tn),
        in_specs=[
            pl.BlockSpec((R, D), lambda l, j: (0, 0)),
            pl.BlockSpec((1, D, tn), lambda l, j: (l, 0, j)),
            pl.BlockSpec((1, 1, tn), lambda l, j: (l, 0, j)),
        ],
        out_specs=pl.BlockSpec((1, R, tn), lambda l, j: (l, 0, j)),
        out_shape=jax.ShapeDtypeStruct((depth, R, N), F32),
        compiler_params=pltpu.CompilerParams(
            dimension_semantics=("parallel", "parallel"), vmem_limit_bytes=VMEM_LIMIT_BYTES),
        name="ada_mod",
    )(cvecs, w_ada, b_ada.reshape(depth, 1, N))


MOD_SHIFT1, MOD_SCALE1, MOD_GATE1, MOD_SHIFT2, MOD_SCALE2, MOD_GATE2 = range(6)


def _modulate(x, mod_ref, shift_row, scale_row):
    return x * (1.0 + mod_ref[0, scale_row:scale_row + 1, :]) + mod_ref[0, shift_row:shift_row + 1, :]


HALO = 8
BRANCH_W = 256
HY_COLS = 3 * BRANCH_W
SSD_XBC_COLS = 512
IN_CONV_COLS = HY_COLS + 2 * BRANCH_W + SSD_XBC_COLS


def _conv3(u, w_ref, tm):
    return (u[HALO - 1:HALO - 1 + tm] * w_ref[0:1, :] + u[HALO:HALO + tm] * w_ref[1:2, :]
            + u[HALO + 1:HALO + 1 + tm] * w_ref[2:3, :])


def _in_proj_kernel(x_ref, xp_ref, xn_ref, mod_ref, w_ref, hcw_ref, hcb_ref, scw_ref, xcw_ref, xcb_ref, *rest,
                    has_pos, tiles_per_seq):
    if has_pos:
        pos_ref, pp_ref, pn_ref = rest[:3]
        rest = rest[3:]
    x0c_ref, hw_ref, scy_ref, xbc_ref, hg_ref, zdt_ref = rest
    tm = x_ref.shape[0]
    t = pl.program_id(0) % tiles_per_seq
    keep_prev = jnp.where(t > 0, 1.0, 0.0)
    keep_next = jnp.where(t < tiles_per_seq - 1, 1.0, 0.0)

    def tokens(main_ref, prev_ref, next_ref):
        return main_ref[...], prev_ref[...], next_ref[...]

    xm, xp, xn = tokens(x_ref, xp_ref, xn_ref)
    if has_pos:
        pm, pp, pn = tokens(pos_ref, pp_ref, pn_ref)
        xm, xp, xn = xm + pm, xp + pp, xn + pn
    mod = lambda v: _modulate(v, mod_ref, MOD_SHIFT1, MOD_SCALE1)
    h_main = mod(xm).astype(BF16)
    h_all = jnp.concatenate([mod(xp) * keep_prev, mod(xm), mod(xn) * keep_next], axis=0).astype(BF16)
    uc = jnp.dot(h_all, w_ref[:, :IN_CONV_COLS], preferred_element_type=F32)
    ur = jnp.dot(h_main, w_ref[:, IN_CONV_COLS:], preferred_element_type=F32)

    hy = _conv3(uc[:, :HY_COLS], hcw_ref, tm) + hcb_ref[...]
    x0c_ref[...] = hy[:, :BRANCH_W]
    hw_ref[...] = hy[:, BRANCH_W:2 * BRANCH_W] * hy[:, 2 * BRANCH_W:]
    cg_xs = uc[:, HY_COLS:HY_COLS + BRANCH_W] * uc[:, HY_COLS + BRANCH_W:HY_COLS + 2 * BRANCH_W]
    scy_ref[...] = ur[:, :BRANCH_W] * _conv3(cg_xs, scw_ref, tm)
    xa = _conv3(uc[:, HY_COLS + 2 * BRANCH_W:], xcw_ref, tm) + xcb_ref[...]
    xbc_ref[...] = xa * jax.nn.sigmoid(xa)
    n_hg = hg_ref.shape[1]
    hg_ref[...] = ur[:, BRANCH_W:BRANCH_W + n_hg]
    zdt_ref[...] = ur[:, BRANCH_W + n_hg:]


def in_proj(x, mod, w, conv, out_widths, *, seq_len, rows_per_mod, tm, pos=None):
    T, D = x.shape
    assert T % tm == 0 and seq_len % tm == 0 and rows_per_mod % tm == 0 and tm % HALO == 0
    tiles_per_seq = seq_len // tm
    tiles_per_mod = rows_per_mod // tm
    hb = tm // HALO
    last_blk = T // HALO - 1
    hy_w, hy_b, sc_w, xbc_w, xbc_b = conv
    row = lambda v: v.reshape(1, -1)
    const = lambda a: pl.BlockSpec(a.shape, lambda i: (0,) * a.ndim)
    consts = [hy_w, row(hy_b), sc_w, xbc_w, row(xbc_b)]
    in_specs = [
        pl.BlockSpec((tm, D), lambda i: (i, 0)),
        pl.BlockSpec((HALO, D), lambda i: (jnp.maximum(i * hb - 1, 0), 0)),
        pl.BlockSpec((HALO, D), lambda i: (jnp.minimum((i + 1) * hb, last_blk), 0)),
        pl.BlockSpec((1, 6, D), lambda i: (i // tiles_per_mod, 0, 0)),
        pl.BlockSpec(w.shape, lambda i: (0, 0), pipeline_mode=pl.Buffered(1)),
    ] + [const(a) for a in consts]
    args = [x, x, x, mod, w] + consts
    if pos is not None:
        assert pos.shape[0] == seq_len
        last_pos = seq_len // HALO - 1
        in_specs += [
            pl.BlockSpec((tm, D), lambda i: (i % tiles_per_seq, 0)),
            pl.BlockSpec((HALO, D), lambda i: (jnp.maximum((i % tiles_per_seq) * hb - 1, 0), 0)),
            pl.BlockSpec((HALO, D), lambda i: (jnp.minimum((i % tiles_per_seq + 1) * hb, last_pos), 0)),
        ]
        args += [pos, pos, pos]
    return pl.pallas_call(
        functools.partial(_in_proj_kernel, has_pos=pos is not None, tiles_per_seq=tiles_per_seq),
        grid=(T // tm,),
        in_specs=in_specs,
        out_specs=[pl.BlockSpec((tm, n), lambda i: (i, 0)) for n in out_widths],
        out_shape=[jax.ShapeDtypeStruct((T, n), F32) for n in out_widths],
        compiler_params=pltpu.CompilerParams(
            dimension_semantics=("parallel",), vmem_limit_bytes=VMEM_LIMIT_BYTES),
        name="in_proj",
    )(*args)


def _pos_spec(pos, rows_per_mod, tm):
    assert pos.shape[0] == rows_per_mod
    tiles = rows_per_mod // tm
    return pl.BlockSpec((tm, pos.shape[1]), lambda i: (i % tiles, 0))


N_BRANCH = 4


def _merge_kernel(x_ref, mod_ref, hy_ref, sc_ref, hg_ref, ssd_ref, wg_ref, bg_ref, wbr_ref, wo_ref,
                  lng_ref, lnb_ref, *rest, has_pos):
    x = x_ref[...] + rest[0][...] if has_pos else x_ref[...]
    o_ref = rest[-1]
    D = x.shape[1]
    h = _modulate(x, mod_ref, MOD_SHIFT1, MOD_SCALE1).astype(BF16)
    y = None
    for k, br_ref in enumerate((hy_ref, sc_ref, hg_ref, ssd_ref)):
        gate = jax.nn.sigmoid(
            jnp.dot(h, wg_ref[:, k * D:(k + 1) * D], preferred_element_type=F32) + bg_ref[:, k * D:(k + 1) * D])
        term = gate * jnp.dot(br_ref[...].astype(BF16), wbr_ref[k], preferred_element_type=F32)
        y = term if y is None else y + term
    y = jnp.dot(y.astype(BF16), wo_ref[...], preferred_element_type=F32)
    z = DEEPNORM_ALPHA * x + mod_ref[0, MOD_GATE1:MOD_GATE1 + 1, :] * y
    o_ref[...] = _layernorm_rows(z, lng_ref[...], lnb_ref[...])


def merge_ln(x, mod, branches, w_gate, b_gate, w_br, w_o, ln_g, ln_b, *, rows_per_mod, tm, pos=None):
    T, D = x.shape
    BW = branches[0].shape[1]
    assert T % tm == 0 and rows_per_mod % tm == 0
    tiles_per_mod = rows_per_mod // tm
    const = lambda shape: pl.BlockSpec(shape, lambda i: (0,) * len(shape), pipeline_mode=pl.Buffered(1))
    extra = [] if pos is None else [pos]
    return pl.pallas_call(
        functools.partial(_merge_kernel, has_pos=pos is not None),
        grid=(T // tm,),
        in_specs=[
            pl.BlockSpec((tm, D), lambda i: (i, 0)),
            pl.BlockSpec((1, 6, D), lambda i: (i // tiles_per_mod, 0, 0)),
        ] + [pl.BlockSpec((tm, BW), lambda i: (i, 0))] * N_BRANCH + [
            const((D, N_BRANCH * D)), const((1, N_BRANCH * D)), const((N_BRANCH, BW, D)), const((D, D)),
            const((1, D)), const((1, D)),
        ] + [_pos_spec(p, rows_per_mod, tm) for p in extra],
        out_specs=pl.BlockSpec((tm, D), lambda i: (i, 0)),
        out_shape=jax.ShapeDtypeStruct((T, D), F32),
        compiler_params=pltpu.CompilerParams(
            dimension_semantics=("parallel",), vmem_limit_bytes=VMEM_LIMIT_BYTES),
        name="merge_ln",
    )(x, mod, *branches, w_gate, b_gate.reshape(1, -1), w_br, w_o, ln_g.reshape(1, D), ln_b.reshape(1, D), *extra)


SSD_HEADS = 4
SSD_HEADDIM = 64
SSD_STATE = 64
SSD_GROUPS = 2
SSD_INNER = SSD_HEADS * SSD_HEADDIM
LANES = 128
NEG_INF = float("-inf")


SCAN_BATCH = 4


def _scan_chunk_index(ph, j, n):
    return j + (1 - ph) * (n - 1 - 2 * j)


def _ssd_kernel(z_ref, xbc_ref, dt_ref, par_ref, dskip_ref, ng_ref, sf0_ref, sb0_ref,
                y_ref, sf_ref, sb_ref, yb_scr, st_scr, *, n_chunks):
    ph = pl.program_id(1)
    j = pl.program_id(2)
    is_fwd = ph == 1
    c = _scan_chunk_index(ph, j, n_chunks)
    NB, R = xbc_ref.shape[0], xbc_ref.shape[1]

    @pl.when((j == 0) & is_fwd)
    def _():
        st_scr[...] = sf0_ref[...]

    @pl.when((j == 0) & jnp.logical_not(is_fwd))
    def _():
        st_scr[...] = sb0_ref[...]

    row = lax.broadcasted_iota(jnp.int32, (R, R), 0)
    col = lax.broadcasted_iota(jnp.int32, (R, R), 1)
    mask = jnp.where(is_fwd, row - col, col - row) >= 0
    tri = jnp.where(mask, 1.0, 0.0).astype(BF16)
    head_of_lane = lax.broadcasted_iota(jnp.int32, (LANES, SSD_INNER), 1) // SSD_HEADDIM
    expand = jnp.where(lax.broadcasted_iota(jnp.int32, (LANES, SSD_INNER), 0) == head_of_lane, 1.0, 0.0).astype(BF16)
    ys = [_ssd_chunk(bi, is_fwd, mask, tri, expand, xbc_ref, dt_ref, par_ref, st_scr) for bi in range(NB)]

    @pl.when(jnp.logical_not(is_fwd))
    def _():
        for bi in range(NB):
            yb_scr[bi, c] = ys[bi]

    @pl.when(is_fwd)
    def _():
        for bi in range(NB):
            z = z_ref[bi]
            xs = xbc_ref[bi, :, :SSD_INNER]
            yt = (ys[bi] + yb_scr[bi, c] + xs * dskip_ref[...]) * (z * jax.nn.sigmoid(z))
            ms = jnp.mean(yt * yt, axis=-1, keepdims=True)
            y_ref[bi] = yt * lax.rsqrt(ms + RMS_EPS) * ng_ref[...]

    @pl.when((j == n_chunks - 1) & is_fwd)
    def _():
        sf_ref[...] = st_scr[...]

    @pl.when((j == n_chunks - 1) & jnp.logical_not(is_fwd))
    def _():
        sb_ref[...] = st_scr[...]


def _ssd_chunk(bi, is_fwd, mask, tri, expand, xbc_ref, dt_ref, par_ref, st_scr):
    R = xbc_ref.shape[1]
    dsel = lambda v: jnp.where(is_fwd, v, pltpu.roll(v, LANES - SSD_HEADS, 1))
    dt_raw = dsel(dt_ref[bi]) + dsel(par_ref[8:16, :])[0:1, :]
    dt = jnp.maximum(dt_raw, 0.0) + jnp.log(1.0 + jnp.exp(-jnp.abs(dt_raw)))
    a = -jnp.exp(dsel(par_ref[0:8, :])[0:1, :]) * dt
    cs = _split_dot(tri, a, terms=3)
    cs_t = cs.T
    total = jnp.sum(a, axis=0, keepdims=True)

    spread = lambda v: jnp.dot(v.astype(BF16), expand, preferred_element_type=F32)
    dt_bc = spread(dt)
    in_decay = spread(jnp.exp(cs))
    out_decay = spread(jnp.exp(total - cs))
    e_total = jnp.exp(total)

    xbc = xbc_ref[bi]
    xdt = xbc[:, :SSD_INNER] * dt_bc
    xdt_b = xdt.astype(BF16)
    xout_b = (xdt * out_decay).astype(BF16)
    bm = xbc[:, SSD_INNER:SSD_INNER + LANES]
    cm_b = xbc[:, SSD_INNER + LANES:SSD_INNER + 2 * LANES].astype(BF16)
    lane = lax.broadcasted_iota(jnp.int32, (R, LANES), 1)
    lo_half = lane < SSD_HEADDIM
    lo_half_st = lax.broadcasted_iota(jnp.int32, (LANES, LANES), 1) < SSD_HEADDIM
    y_groups = []
    for g in range(SSD_GROUPS):
        h0, h1 = 2 * g, 2 * g + 1
        cols = slice(g * LANES, (g + 1) * LANES)
        bm_g = jnp.where((lane >= g * SSD_STATE) & (lane < (g + 1) * SSD_STATE), bm, 0.0).astype(BF16)
        G = lax.dot_general(cm_b, bm_g, (((1,), (1,)), ((), ())), preferred_element_type=F32)
        st_old = st_scr[bi, g]
        y_off = in_decay[:, cols] * jnp.dot(cm_b, st_old.astype(BF16), preferred_element_type=F32)
        y_diag = []
        for h in (h0, h1):
            decay = jnp.exp(jnp.where(mask, cs[:, h:h + 1] - cs_t[h:h + 1, :], NEG_INF))
            y_diag.append(jnp.dot((G * decay).astype(BF16), xdt_b[:, cols], preferred_element_type=F32))
        y_groups.append(jnp.where(lo_half, y_diag[0], y_diag[1]) + y_off)
        upd = lax.dot_general(bm_g, xout_b[:, cols], (((0,), (0,)), ((), ())), preferred_element_type=F32)
        st_scr[bi, g] = jnp.where(lo_half_st, e_total[:, h0:h0 + 1], e_total[:, h1:h1 + 1]) * st_old + upd
    return jnp.concatenate(y_groups, axis=1)


def ssd_scan(ssd, xbc, par, d_skip, norm_g, s_f0, s_b0, *, z_blk, dt_blk, chunk=256):
    B, L, _ = ssd.shape
    R = min(chunk, L)
    assert L % R == 0 and B % SCAN_BATCH == 0
    n = L // R
    NB = SCAN_BATCH
    cidx = lambda ph, j: _scan_chunk_index(ph, j, n)
    st_spec = pl.BlockSpec((NB, SSD_GROUPS, LANES, LANES), lambda b, ph, j: (b, 0, 0, 0))
    st_shape = jax.ShapeDtypeStruct((B, SSD_GROUPS, LANES, LANES), F32)
    return pl.pallas_call(
        functools.partial(_ssd_kernel, n_chunks=n),
        grid=(B // NB, 2, n),
        in_specs=[
            pl.BlockSpec((NB, R, SSD_INNER), lambda b, ph, j: (b, cidx(ph, j), z_blk)),
            pl.BlockSpec((NB, R, xbc.shape[2]), lambda b, ph, j: (b, cidx(ph, j), 0)),
            pl.BlockSpec((NB, R, LANES), lambda b, ph, j: (b, cidx(ph, j), dt_blk)),
            pl.BlockSpec((16, LANES), lambda b, ph, j: (0, 0)),
            pl.BlockSpec((1, SSD_INNER), lambda b, ph, j: (0, 0)),
            pl.BlockSpec((1, SSD_INNER), lambda b, ph, j: (0, 0)),
            st_spec, st_spec,
        ],
        out_specs=[pl.BlockSpec((NB, R, SSD_INNER), lambda b, ph, j: (b, ph * j, 0)), st_spec, st_spec],
        out_shape=[jax.ShapeDtypeStruct((B, L, SSD_INNER), F32), st_shape, st_shape],
        scratch_shapes=[pltpu.VMEM((NB, n, R, SSD_INNER), F32), pltpu.VMEM((NB, SSD_GROUPS, LANES, LANES), F32)],
        compiler_params=pltpu.CompilerParams(
            dimension_semantics=("parallel", "arbitrary", "arbitrary"), vmem_limit_bytes=VMEM_LIMIT_BYTES),
        name="ssd_scan",
    )(ssd, xbc, ssd, par, d_skip.reshape(1, SSD_INNER), norm_g.reshape(1, SSD_INNER), s_f0, s_b0)


HG_HEADS = 4
HG_DK = 64
HG_DV = 64
HG_KEY = HG_HEADS * HG_DK
HG_SUB = 16
HG_CHUNK = 64
HG_SUBS_PER_CHUNK = HG_CHUNK // HG_SUB


def _split_dot(mask_b, x, terms=2):
    out = None
    for _ in range(terms):
        part = x.astype(BF16)
        prod = jnp.dot(mask_b, part, preferred_element_type=F32)
        out = prod if out is None else out + prod
        x = x - part.astype(F32)
    return out


def _hgrn_kernel(q_ref, f_ref, v_ref, g_ref, lbl_ref, ng_ref, sf0_ref, sb0_ref,
                 y_ref, sf_ref, sb_ref, ob_scr, st_scr, *, n_blocks, layer, depth):
    ph = pl.program_id(1)
    j = pl.program_id(2)
    is_fwd = ph == 1
    c = _scan_chunk_index(ph, j, n_blocks)
    NB, R = q_ref.shape[0], q_ref.shape[1]
    W = HG_KEY

    @pl.when((j == 0) & is_fwd)
    def _():
        st_scr[...] = sf0_ref[...]

    @pl.when((j == 0) & jnp.logical_not(is_fwd))
    def _():
        st_scr[...] = sb0_ref[...]

    def lower_bound(d):
        x = lbl_ref[d * depth:(d + 1) * depth, :]
        e = jnp.exp(x - jnp.max(x, axis=0, keepdims=True))
        p = e / jnp.sum(e, axis=0, keepdims=True)
        return jnp.sum(p[1:layer + 1, :], axis=0, keepdims=True) if layer > 0 else jnp.zeros((1, W), F32)

    lb = jnp.where(is_fwd, lower_bound(0), lower_bound(1))
    row = lax.broadcasted_iota(jnp.int32, (R, R), 0)
    col = lax.broadcasted_iota(jnp.int32, (R, R), 1)
    dirge = jnp.where(is_fwd, row - col, col - row) >= 0
    same_sub = (row // HG_SUB) == (col // HG_SUB)
    same_chunk = (row // HG_CHUNK) == (col // HG_CHUNK)
    one = lambda m: jnp.where(m, 1.0, 0.0).astype(BF16)
    dist = jnp.where(is_fwd, row // HG_SUB - col // HG_SUB, col // HG_SUB - row // HG_SUB)
    masks = dict(
        cum16=one(same_sub & dirge), tot16=one(same_sub), cum64=one(same_chunk & dirge), tot64=one(same_chunk),
        dist=[same_sub & dirge] + [same_chunk & (dist == d) for d in range(1, HG_SUBS_PER_CHUNK)])
    lane = lax.broadcasted_iota(jnp.int32, (R, W), 1)
    head_mask = [(lane // HG_DK) == h for h in range(HG_HEADS)]
    blocks = [_hgrn_block(bi, is_fwd, lb, masks, head_mask, q_ref, f_ref, v_ref) for bi in range(NB)]
    srow = lax.broadcasted_iota(jnp.int32, (W, W), 0)
    scol = lax.broadcasted_iota(jnp.int32, (W, W), 1)
    bd_mask = (srow // HG_DV) == (scol // HG_DK)

    def through_state(bi, reverse):
        o_intra, qd, ke, v_b, tot64 = blocks[bi]
        n_chunks = R // HG_CHUNK
        st = st_scr[bi]
        parts = [None] * n_chunks
        for i in (reversed(range(n_chunks)) if reverse else range(n_chunks)):
            rows = slice(i * HG_CHUNK, (i + 1) * HG_CHUNK)
            parts[i] = lax.dot_general(qd[rows], st.astype(BF16), (((1,), (1,)), ((), ())), preferred_element_type=F32)
            upd = lax.dot_general(v_b[rows], ke[rows], (((0,), (0,)), ((), ())), preferred_element_type=F32)
            st = jnp.exp(tot64[i * HG_CHUNK:i * HG_CHUNK + 1, :]) * st + jnp.where(bd_mask, upd, 0.0)
        st_scr[bi] = st
        return o_intra + jnp.concatenate(parts, axis=0)

    @pl.when(jnp.logical_not(is_fwd))
    def _():
        for bi in range(NB):
            ob_scr[bi, c] = through_state(bi, reverse=True)

    @pl.when(is_fwd)
    def _():
        for bi in range(NB):
            ot = through_state(bi, reverse=False) + ob_scr[bi, c]
            sq = ot * ot
            ms = jnp.zeros((R, W), F32)
            for h in range(HG_HEADS):
                s = jnp.sum(jnp.where(head_mask[h], sq, 0.0), axis=-1, keepdims=True) * (1.0 / HG_DV)
                ms = jnp.where(head_mask[h], s, ms)
            g = g_ref[bi]
            y_ref[bi] = ot * lax.rsqrt(ms + RMS_EPS) * ng_ref[...] * (g * jax.nn.sigmoid(g))

    @pl.when((j == n_blocks - 1) & is_fwd)
    def _():
        sf_ref[...] = st_scr[...]

    @pl.when((j == n_blocks - 1) & jnp.logical_not(is_fwd))
    def _():
        sb_ref[...] = st_scr[...]


def _hgrn_block(bi, is_fwd, lb, masks, head_mask, q_ref, f_ref, v_ref):
    R = q_ref.shape[1]
    W = HG_KEY
    sig = jax.nn.sigmoid(f_ref[bi])
    logf = jnp.log(lb + (1.0 - lb) * sig)
    k = (1.0 - lb) * (1.0 - sig)
    q = q_ref[bi] * (HG_DK ** -0.5)
    v_b = v_ref[bi].astype(BF16)
    logf_hi = logf.astype(BF16)
    logf_lo = (logf - logf_hi.astype(F32)).astype(BF16)
    msum = lambda m: (jnp.dot(m, logf_hi, preferred_element_type=F32) + jnp.dot(m, logf_lo, preferred_element_type=F32))
    b16 = msum(masks["cum16"])
    tot16 = msum(masks["tot16"])
    b64 = msum(masks["cum64"])
    tot64 = msum(masks["tot64"])

    shifted = lambda s: jnp.where(is_fwd, pltpu.roll(tot16, R - s, 0), pltpu.roll(tot16, s, 0))
    nx1 = shifted(HG_SUB)
    nx2 = nx1 + shifted(2 * HG_SUB)
    k_end16 = k * jnp.exp(tot16 - b16)
    keys = [(k * jnp.exp(-b16)).astype(BF16), k_end16.astype(BF16),
            (k_end16 * jnp.exp(nx1)).astype(BF16), (k_end16 * jnp.exp(nx2)).astype(BF16)]
    q16 = q * jnp.exp(b16)
    qm =jnp.concatenate([jnp.where(m, q16, 0.0) for m in head_mask], axis=0).astype(BF16)
    att = [jnp.zeros((R, R), F32)] * HG_HEADS
    for d in range(HG_SUBS_PER_CHUNK):
        sc = lax.dot_general(qm, keys[d], (((1,), (1,)), ((), ())), preferred_element_type=F32)
        att = [jnp.where(masks["dist"][d], sc[h * R:(h + 1) * R], att[h]) for h in range(HG_HEADS)]
    o_all = jnp.dot(jnp.concatenate(att, axis=0).astype(BF16), v_b, preferred_element_type=F32)
    o = jnp.zeros((R, W), F32)
    for h in range(HG_HEADS):
        o = jnp.where(head_mask[h], o_all[h * R:(h + 1) * R], o)

    return o, (q * jnp.exp(b64)).astype(BF16), (k * jnp.exp(tot64 - b64)).astype(BF16), v_b, tot64


def hgrn_scan(hg, lb_logits, norm_g, s_f0, s_b0, *, layer, block=256):
    B, L, _ = hg.shape
    W = HG_KEY
    R = min(block, L)
    assert L % R == 0 and R % HG_CHUNK == 0 and B % SCAN_BATCH == 0
    n = L // R
    NB = SCAN_BATCH
    depth = lb_logits.shape[0] // 2
    cidx = lambda ph, j: _scan_chunk_index(ph, j, n)
    col = lambda blk: pl.BlockSpec((NB, R, W), lambda b, ph, j: (b, cidx(ph, j), blk))
    st_spec = pl.BlockSpec((NB, W, W), lambda b, ph, j: (b, 0, 0))
    st_shape = jax.ShapeDtypeStruct((B, W, W), F32)
    return pl.pallas_call(
        functools.partial(_hgrn_kernel, n_blocks=n, layer=layer, depth=depth),
        grid=(B // NB, 2, n),
        in_specs=[
            col(0),
            pl.BlockSpec((NB, R, W), lambda b, ph, j: (b, cidx(ph, j), 2 - ph)),
            col(3), col(4),
            pl.BlockSpec(lb_logits.shape, lambda b, ph, j: (0, 0)),
            pl.BlockSpec((1, W), lambda b, ph, j: (0, 0)),
            st_spec, st_spec,
        ],
        out_specs=[pl.BlockSpec((NB, R, W), lambda b, ph, j: (b, ph * j, 0)), st_spec, st_spec],
        out_shape=[jax.ShapeDtypeStruct((B, L, W), F32), st_shape, st_shape],
        scratch_shapes=[pltpu.VMEM((NB, n, R, W), F32), pltpu.VMEM((NB, W, W), F32)],
        compiler_params=pltpu.CompilerParams(
            dimension_semantics=("parallel", "arbitrary", "arbitrary"), vmem_limit_bytes=VMEM_LIMIT_BYTES),
        name="hgrn_scan",
    )(hg, hg, hg, hg, lb_logits, jnp.tile(norm_g, HG_HEADS).reshape(1, W), s_f0, s_b0)


HY_WIDTH = 256
HY_BANDS = 16
HY_HIDDEN = 64
SUBLANES = 8


def _dot_hi(a, b):
    return jnp.dot(a, b, precision=lax.Precision.HIGHEST, preferred_element_type=F32)


def _hy_filter_kernel(bands_ref, w1t_ref, w1f_ref, b1_ref, fr1_ref, w2_ref, b2_ref, fr2_ref, w3a_ref, w3b_ref,
                      decay_ref, f_ref, den_ref, cosb_scr, sinb_scr, *, L):
    i = pl.program_id(0)
    TR = f_ref.shape[1]
    step = 2.0 * math.pi / L
    bands = bands_ref[...]

    @pl.when(i == 0)
    def _():
        r = lax.broadcasted_iota(jnp.int32, (TR, LANES), 0).astype(F32)
        ang = (step * r) * bands
        cosb_scr[...] = jnp.cos(ang)
        sinb_scr[...] = jnp.sin(ang)
        den_ref[...] = jnp.zeros_like(den_ref)

    one_row = lambda v: jnp.broadcast_to(v, (SUBLANES, LANES))
    base = one_row((step * (i * TR).astype(F32)) * bands)
    cos_a, sin_a = jnp.cos(base)[0:1], jnp.sin(base)[0:1]
    full = one_row((2.0 * math.pi) * bands)
    cos_p, sin_p = jnp.cos(full)[0:1], jnp.sin(full)[0:1]
    cos_b, sin_b = cosb_scr[...], sinb_scr[...]
    cos0 = cos_a * cos_b - sin_a * sin_b
    sin0 = sin_a * cos_b + cos_a * sin_b
    cos1 = cos_p * cos0 + sin_p * sin0
    sin1 = sin_p * cos0 - cos_p * sin0

    m = i * TR + lax.broadcasted_iota(jnp.int32, (TR, HY_WIDTH), 0)
    t0 = m.astype(F32) * (1.0 / (L - 1))
    t1 = jnp.where(m == 0, 0, L - m).astype(F32) * (1.0 / (L - 1))
    lane = lax.broadcasted_iota(jnp.int32, (TR, LANES), 1)
    t_both = jnp.where(lane < HY_HIDDEN, t0[:, :LANES], t1[:, :LANES])
    feats = jnp.concatenate([cos0, sin0, cos1, sin1], axis=1)
    pre = t_both * w1t_ref[...] + _dot_hi(feats, w1f_ref[...]) + b1_ref[...]
    h = jnp.sin(fr1_ref[...] * pre)
    h = jnp.sin(fr2_ref[...] * (_dot_hi(h, w2_ref[...]) + b2_ref[...]))
    decay = jnp.abs(decay_ref[...])
    filt0 = _dot_hi(h, w3a_ref[...]) * jnp.exp(-t0 * decay)
    filt1 = jnp.where(m == 0, 0.0, _dot_hi(h, w3b_ref[...]) * jnp.exp(-t1 * decay))
    f_ref[0] = filt0
    f_ref[1] = filt1
    den_ref[...] += jnp.sum(jnp.abs(filt0) + jnp.abs(filt1), axis=0, keepdims=True)


def hy_filter(L, w1, b1, fr1, w2, b2, fr2, w3, decay, *, tr=512):
    tr = min(tr, L)
    H = HY_HIDDEN
    bands = np.zeros((1, LANES), np.float32)
    bands[0, :HY_BANDS] = np.linspace(1e-4, HY_BANDS - 1, HY_BANDS, dtype=np.float32)
    pad_rows = lambda w: jnp.pad(w, ((0, LANES - HY_BANDS), (0, 0)))
    twice = lambda v: jnp.concatenate([v, v]).reshape(1, 2 * H)
    zeros = jnp.zeros((LANES, H), F32)
    w_cos, w_sin = pad_rows(w1[1:1 + HY_BANDS]), pad_rows(w1[1 + HY_BANDS:])
    w1_feat = jnp.concatenate([
        jnp.concatenate([w_cos, zeros], axis=1), jnp.concatenate([-w_sin, zeros], axis=1),
        jnp.concatenate([zeros, w_cos], axis=1), jnp.concatenate([zeros, -w_sin], axis=1)], axis=0)
    zh = jnp.zeros((H, H), F32)
    w2_both = jnp.concatenate([jnp.concatenate([w2, zh], axis=1), jnp.concatenate([zh, w2], axis=1)], axis=0)
    zw = jnp.zeros((H, HY_WIDTH), F32)
    w3_a = jnp.concatenate([w3[:, :HY_WIDTH], zw], axis=0)
    w3_b = jnp.concatenate([zw, w3[:, HY_WIDTH:]], axis=0)
    args = (jnp.asarray(bands), twice(w1[0]), w1_feat, twice(b1), twice(fr1), w2_both, twice(b2), twice(fr2),
            w3_a, w3_b, decay.reshape(1, -1))
    return pl.pallas_call(
        functools.partial(_hy_filter_kernel, L=L),
        grid=(L // tr,),
        in_specs=[pl.BlockSpec(a.shape, lambda i: (0, 0)) for a in args],
        out_specs=[pl.BlockSpec((2, tr, HY_WIDTH), lambda i: (0, i, 0)), pl.BlockSpec((1, HY_WIDTH), lambda i: (0, 0))],
        out_shape=[jax.ShapeDtypeStruct((2, L, HY_WIDTH), F32), jax.ShapeDtypeStruct((1, HY_WIDTH), F32)],
        scratch_shapes=[pltpu.VMEM((tr, LANES), F32), pltpu.VMEM((tr, LANES), F32)],
        compiler_params=pltpu.CompilerParams(
            dimension_semantics=("arbitrary",), vmem_limit_bytes=VMEM_LIMIT_BYTES),
        name="hy_filter",
    )(*args)


FFT_N1 = 64
FFT_N2 = 128
FFT_N = FFT_N1 * FFT_N2
FFT_L = FFT_N // 2
FFT_N1_NZ = FFT_N1 // 2
FFT_UNROLL_N1_STAGE = 2
FFT_UNROLL_N2_STAGE = 8


@functools.lru_cache(maxsize=None)
def _fft_constants():
    eye = np.eye(SUBLANES)
    k1 = np.arange(FFT_N1)[:, None]
    n1 = np.arange(FFT_N1_NZ)[None, :]
    ang1 = 2.0 * np.pi * ((k1 * n1) % FFT_N1) / FFT_N1
    kron_fwd = np.concatenate([np.kron(np.cos(ang1), eye), np.kron(np.sin(ang1), eye)], axis=0)
    kron_inv = np.concatenate([np.kron(np.cos(ang1).T, eye), np.kron(np.sin(ang1).T, eye)], axis=0) / FFT_N
    k = np.arange(FFT_N1)[:, None, None] + FFT_N1 * np.arange(FFT_N2)[None, :, None]
    n2 = np.arange(FFT_N2)[None, None, :]
    ang2 = 2.0 * np.pi * ((k * n2) % FFT_N) / FFT_N
    g_fwd = np.concatenate([np.cos(ang2), np.sin(ang2)], axis=1)
    g_inv = np.concatenate([np.cos(ang2).transpose(0, 2, 1), np.sin(ang2).transpose(0, 2, 1)], axis=1)
    n1_all = np.arange(FFT_N1)[None, :]
    ang1_all = 2.0 * np.pi * ((k1 * n1_all) % FFT_N1) / FFT_N1
    kron_full = np.concatenate([np.kron(np.cos(ang1_all), eye), np.kron(np.sin(ang1_all), eye)], axis=0)
    to_b = lambda a: jnp.asarray(a, dtype=F32).astype(BF16)
    return to_b(kron_fwd), to_b(g_fwd), to_b(g_inv), to_b(kron_inv), to_b(kron_full)


def _cplx_fwd(r, half_rows, cb):
    cr, sr = r[:half_rows], r[half_rows:]
    return cr[:, :cb] + sr[:, cb:], cr[:, cb:] - sr[:, :cb]


def _cplx_inv(r, half_rows, cb):
    cr, sr = r[:half_rows], r[half_rows:]
    return cr[:, :cb] - sr[:, cb:], cr[:, cb:] + sr[:, :cb]


def _fft_stage1(load_group, kron_ref, a_scr, cb):
    def body(jt, carry):
        rows = pl.ds(pl.multiple_of(jt * SUBLANES, SUBLANES), SUBLANES)
        d = load_group(rows).reshape(kron_ref.shape[1], 2 * cb).astype(BF16)
        r = jnp.dot(kron_ref[...], d, preferred_element_type=F32)
        a_re, a_im = _cplx_fwd(r, FFT_N1 * SUBLANES, cb)
        a_scr[:, rows, :] = jnp.concatenate([a_re, a_im], axis=1).reshape(FFT_N1, SUBLANES, 2 * cb)
        return carry
    lax.fori_loop(0, FFT_N2 // SUBLANES, body, 0, unroll=FFT_UNROLL_N1_STAGE)


def _fft_fwd_data_kernel(z_ref, h_ref, kron_ref, g_ref, p_ref, a_scr):
    cb = z_ref.shape[4]
    load = lambda rows: jnp.concatenate([z_ref[0, 0, :, rows, :], z_ref[0, 1, :, rows, :]], axis=-1)
    _fft_stage1(load, kron_ref, a_scr, cb)

    def body(k1, carry):
        r = jnp.dot(g_ref[k1], a_scr[k1].astype(BF16), preferred_element_type=F32)
        x_re, x_im = _cplx_fwd(r, FFT_N2, cb)
        h = h_ref[0, k1]
        h_re, h_im = h[:, :cb], h[:, cb:]
        p_ref[0, 0, k1] = jnp.concatenate([x_re * h_re - x_im * h_im, x_re * h_im + x_im * h_re], axis=1).astype(BF16)
        return carry
    lax.fori_loop(0, FFT_N1, body, 0, unroll=FFT_UNROLL_N2_STAGE)


def _fft_fwd_filter_kernel(f_ref, den_ref, kron_ref, g_ref, h_ref, a_scr):
    cb = f_ref.shape[3]
    inv_den = 1.0 / den_ref[...]

    def load(rows):
        re = jnp.concatenate([f_ref[0, :, rows, :], f_ref[1, :, rows, :]], axis=0) * inv_den
        return jnp.concatenate([re, jnp.zeros_like(re)], axis=-1)
    _fft_stage1(load, kron_ref, a_scr, cb)

    def body(k1, carry):
        r = jnp.dot(g_ref[k1], a_scr[k1].astype(BF16), preferred_element_type=F32)
        x_re, x_im = _cplx_fwd(r, FFT_N2, cb)
        h_ref[0, k1] = jnp.concatenate([x_re, x_im], axis=1)
        return carry
    lax.fori_loop(0, FFT_N1, body, 0, unroll=FFT_UNROLL_N2_STAGE)


def _fft_inv_kernel(p_ref, x0_ref, w_ref, bias_ref, gi_ref, kron_ref, o_ref, b_scr):
    cb = o_ref.shape[4]

    def body3(k1, carry):
        r = jnp.dot(gi_ref[k1], p_ref[0, 0, k1], preferred_element_type=F32)
        b_re, b_im = _cplx_inv(r, FFT_N2, cb)
        b_scr[k1] = jnp.concatenate([b_re, b_im], axis=1)
        return carry
    lax.fori_loop(0, FFT_N1, body3, 0, unroll=FFT_UNROLL_N2_STAGE)

    bias = bias_ref[...].reshape(1, 1, cb)

    def body4(jt, carry):
        rows = pl.ds(pl.multiple_of(jt * SUBLANES, SUBLANES), SUBLANES)
        d = b_scr[:, rows, :].reshape(FFT_N1 * SUBLANES, 2 * cb).astype(BF16)
        r = jnp.dot(kron_ref[...], d, preferred_element_type=F32)
        y_re, y_im = _cplx_inv(r, FFT_N1_NZ * SUBLANES, cb)
        for which, y in enumerate((y_re, y_im)):
            y3 = y.reshape(FFT_N1_NZ, SUBLANES, cb)
            o_ref[0, which, :, rows, :] = x0_ref[0, which, :, rows, :] * (y3 + w_ref[0, which, :, rows, :] * bias)
        return carry
    lax.fori_loop(0, FFT_N2 // SUBLANES, body4, 0, unroll=FFT_UNROLL_N1_STAGE)


def hyena_spectrum(fu, den, *, cb=128):
    _, g_fwd, _, _, kron_full = _fft_constants()
    W = fu.shape[2]
    ncb = W // cb
    f4 = fu.reshape(2, FFT_N1_NZ, FFT_N2, W)
    const = lambda a: pl.BlockSpec(a.shape, lambda c: (0,) * a.ndim, pipeline_mode=pl.Buffered(1))
    return pl.pallas_call(
        _fft_fwd_filter_kernel,
        grid=(ncb,),
        in_specs=[
            pl.BlockSpec((2, FFT_N1_NZ, FFT_N2, cb), lambda c: (0, 0, 0, c)),
            pl.BlockSpec((1, cb), lambda c: (0, c)),
            const(kron_full), const(g_fwd),
        ],
        out_specs=pl.BlockSpec((1, FFT_N1, FFT_N2, 2 * cb), lambda c: (c, 0, 0, 0)),
        out_shape=jax.ShapeDtypeStruct((ncb, FFT_N1, FFT_N2, 2 * cb), F32),
        scratch_shapes=[pltpu.VMEM((FFT_N1, FFT_N2, 2 * cb), F32)],
        compiler_params=pltpu.CompilerParams(
            dimension_semantics=("parallel",), vmem_limit_bytes=VMEM_LIMIT_BYTES),
        name="hyena_spectrum",
    )(f4, den, kron_full, g_fwd)


def hyena_conv(x0c, hw, spec, bias, *, cb=128):
    kron_fwd, g_fwd, g_inv, kron_inv, _ = _fft_constants()
    B, L, W = hw.shape
    assert L == FFT_L and B % 2 == 0
    ncb = W // cb
    pair_shape = (B // 2, 2, FFT_N1_NZ, FFT_N2, W)
    x5 = x0c.reshape(pair_shape)
    w5 = hw.reshape(pair_shape)
    const = lambda a: pl.BlockSpec(a.shape, lambda c, p: (0,) * a.ndim, pipeline_mode=pl.Buffered(1))
    pair = pl.BlockSpec((1, 2, FFT_N1_NZ, FFT_N2, cb), lambda c, p: (p, 0, 0, 0, c))
    pspec = pl.BlockSpec((1, 1, FFT_N1, FFT_N2, 2 * cb), lambda c, p: (c, p, 0, 0, 0))
    prod = pl.pallas_call(
        _fft_fwd_data_kernel,
        grid=(ncb, B // 2),
        in_specs=[pair, pl.BlockSpec((1, FFT_N1, FFT_N2, 2 * cb), lambda c, p: (c, 0, 0, 0)),
                  const(kron_fwd), const(g_fwd)],
        out_specs=pspec,
        out_shape=jax.ShapeDtypeStruct((ncb, B // 2, FFT_N1, FFT_N2, 2 * cb), BF16),
        scratch_shapes=[pltpu.VMEM((FFT_N1, FFT_N2, 2 * cb), F32)],
        compiler_params=pltpu.CompilerParams(
            dimension_semantics=("parallel", "parallel"), vmem_limit_bytes=VMEM_LIMIT_BYTES),
        name="hyena_fft_fwd",
    )(w5, spec, kron_fwd, g_fwd)
    out = pl.pallas_call(
        _fft_inv_kernel,
        grid=(ncb, B // 2),
        in_specs=[pspec, pair, pair, pl.BlockSpec((1, cb), lambda c, p: (0, c)), const(g_inv), const(kron_inv)],
        out_specs=pair,
        out_shape=jax.ShapeDtypeStruct(pair_shape, F32),
        scratch_shapes=[pltpu.VMEM((FFT_N1, FFT_N2, 2 * cb), F32)],
        compiler_params=pltpu.CompilerParams(
            dimension_semantics=("parallel", "parallel"), vmem_limit_bytes=VMEM_LIMIT_BYTES),
        name="hyena_fft_inv",
    )(prod, x5, w5, bias.reshape(1, W), g_inv, kron_inv)
    return out.reshape(B, L, W)


@functools.lru_cache(maxsize=None)
def _dense_dft_constants(L):
    n_full = 2 * L
    k = np.arange(n_full)[:, None]
    n = np.arange(n_full)[None, :]
    ang = 2.0 * np.pi * ((k * n) % n_full) / n_full
    fwd = np.concatenate([np.cos(ang), np.sin(ang)], axis=0)
    inv = np.concatenate([np.cos(ang[:L]), np.sin(ang[:L])], axis=0) / n_full
    return jnp.asarray(fwd, dtype=F32), jnp.asarray(inv, dtype=F32)


def _hyena_short_kernel(x0_ref, w_ref, f_ref, den_ref, bias_ref, ff_ref, fi_ref, o_ref, h_scr):
    L, cb = o_ref.shape[2], o_ref.shape[3]

    @pl.when(pl.program_id(1) == 0)
    def _():
        filt = jnp.concatenate([f_ref[0], f_ref[1]], axis=0) / den_ref[...]
        r = _dot_hi(ff_ref[...], filt)
        h_scr[0] = r[:2 * L]
        h_scr[1] = -r[2 * L:]

    h_re, h_im = h_scr[0], h_scr[1]
    z = jnp.concatenate([w_ref[0, 0], w_ref[0, 1]], axis=1).astype(BF16)
    x_re, x_im = _cplx_fwd(jnp.dot(ff_ref[:, :L].astype(BF16), z, preferred_element_type=F32), 2 * L, cb)
    p = jnp.concatenate([x_re * h_re - x_im * h_im, x_re * h_im + x_im * h_re], axis=1).astype(BF16)
    y_re, y_im = _cplx_inv(jnp.dot(fi_ref[...].astype(BF16), p, preferred_element_type=F32), L, cb)
    for which, y in enumerate((y_re, y_im)):
        o_ref[0, which] = x0_ref[0, which] * (y + w_ref[0, which] * bias_ref[...])


def hyena_conv_short(x0c, hw, fu, den, bias, *, cb=128):
    B, L, W = hw.shape
    fwd, inv = _dense_dft_constants(L)
    pair_shape = (B // 2, 2, L, W)
    pair = pl.BlockSpec((1, 2, L, cb), lambda c, p: (p, 0, 0, c))
    vec = pl.BlockSpec((1, cb), lambda c, p: (0, c))
    const = lambda a: pl.BlockSpec(a.shape, lambda c, p: (0, 0))
    out = pl.pallas_call(
        _hyena_short_kernel,
        grid=(W // cb, B // 2),
        in_specs=[pair, pair, pl.BlockSpec((2, L, cb), lambda c, p: (0, 0, c)), vec, vec, const(fwd), const(inv)],
        out_specs=pair,
        out_shape=jax.ShapeDtypeStruct(pair_shape, F32),
        scratch_shapes=[pltpu.VMEM((2, 2 * L, cb), F32)],
        compiler_params=pltpu.CompilerParams(
            dimension_semantics=("parallel", "arbitrary"), vmem_limit_bytes=VMEM_LIMIT_BYTES),
        name="hyena_conv_short",
    )(x0c.reshape(pair_shape), hw.reshape(pair_shape), fu, den, bias.reshape(1, W), fwd, inv)
    return out.reshape(B, L, W)


GRID_W = 64
SC_COLS = 3 * BRANCH_W
HG_COLS = 3 * HG_KEY + 2 * HG_HEADS * HG_DV
SSD_XBC = SSD_INNER + 2 * SSD_GROUPS * SSD_STATE
assert SSD_XBC == SSD_XBC_COLS and HY_COLS == 3 * HY_WIDTH


def _sincos_1d(pos, dim):
    omega = 1.0 / (10000.0 ** (jnp.arange(dim // 2, dtype=F32) / (dim // 2)))
    ang = pos.astype(F32)[:, None] * omega[None]
    return jnp.concatenate([jnp.sin(ang), jnp.cos(ang)], -1)


def _grid_pos_embed(rows, dim):
    row = jnp.repeat(jnp.arange(rows), GRID_W)
    col = jnp.tile(jnp.arange(GRID_W), rows)
    return jnp.concatenate([_sincos_1d(row, dim // 2), _sincos_1d(col, dim // 2)], -1)


def kernel(x, c, ctx, c_ctx, w_ada, b_ada, w_in, hy_conv_w, hy_conv_b, hy_w1, hy_b1, hy_freq1,
           hy_w2, hy_b2, hy_freq2, hy_w3, hy_decay, hy_bias, sc_conv_w, hg_lb_logits, hg_norm_g,
           ssd_conv_w, ssd_conv_b, ssd_a_log, ssd_dt_bias, ssd_d, ssd_norm_g, w_gate, b_gate, w_br,
           w_o, ln1_g, ln1_b, ln2_g, ln2_b, w_router, b_router, w_e1, w_e3, w_e2):
    B, L, D = x.shape
    LC = ctx.shape[1]
    depth = w_in.shape[0]
    pos_table = _grid_pos_embed(L // GRID_W, D).astype(x.dtype)
    lat = x.reshape(B * L, D)
    cx = ctx.reshape(B * LC, D)

    n_vec = -(-(B + 1) // SUBLANES) * SUBLANES
    cvecs = jnp.concatenate([c, c_ctx[None], jnp.zeros((n_vec - B - 1, D), c.dtype)], axis=0)
    mods = ada_mod(cvecs, w_ada, b_ada).reshape(depth, n_vec, 6, D)
    lb_logits = hg_lb_logits.reshape(2 * depth, HG_KEY)

    out_widths = (BRANCH_W, BRANCH_W, BRANCH_W, SSD_XBC, HG_COLS, SSD_INNER + LANES)
    z_state = lambda *shape: jnp.zeros(shape, F32)
    w_e1_b, w_e3_b, w_e2_b = w_e1.astype(BF16), w_e3.astype(BF16), w_e2.astype(BF16)
    c_hy, c_sc, c_hg = HY_COLS, HY_COLS + SC_COLS, HY_COLS + SC_COLS + HG_COLS
    c_z, c_xbc = c_hg + SSD_INNER, c_hg + SSD_INNER + SSD_XBC

    for l in range(depth):
        ctx_out = l < depth - 1
        mod_lat, mod_ctx = mods[l, :B], mods[l, B:B + 1]
        wl = w_in[l]
        w_in_b = jnp.concatenate([
            wl[:, :c_hy], wl[:, c_hy + BRANCH_W:c_sc], wl[:, c_z:c_xbc],
            wl[:, c_hy:c_hy + BRANCH_W], wl[:, c_sc:c_hg], wl[:, c_hg:c_z],
            jnp.pad(wl[:, c_xbc:], ((0, 0), (0, LANES - 2 * SSD_HEADS))),
        ], axis=1).astype(BF16)
        conv_par = (hy_conv_w[l], hy_conv_b[l], sc_conv_w[l], ssd_conv_w[l], ssd_conv_b[l])
        ssd_par = jnp.zeros((2 * SUBLANES, LANES), F32)
        ssd_par = ssd_par.at[:SUBLANES, :2 * SSD_HEADS].set(jnp.broadcast_to(ssd_a_log[l].reshape(1, -1), (SUBLANES, 2 * SSD_HEADS)))
        ssd_par = ssd_par.at[SUBLANES:, :2 * SSD_HEADS].set(jnp.broadcast_to(ssd_dt_bias[l].reshape(1, -1), (SUBLANES, 2 * SSD_HEADS)))
        d_skip = jnp.repeat(ssd_d[l], SSD_HEADDIM)
        filt_args = (hy_w1[l], hy_b1[l], hy_freq1[l], hy_w2[l], hy_b2[l], hy_freq2[l], hy_w3[l], hy_decay[l])

        def mixers(tokens, mod, seg_len, hg_state, ssd_state, want_out, pos=None):
            n_seq = tokens.shape[0] // seg_len
            x0c, hw, sc_y, xbc, hg, zdt = in_proj(
                tokens, mod, w_in_b, conv_par, out_widths, seq_len=seg_len, rows_per_mod=mod_rows(mod, tokens),
                tm=min(512, seg_len), pos=pos)
            seq = lambda a: a.reshape(n_seq, seg_len, a.shape[1])
            hg_y, hg_f, hg_b = hgrn_scan(seq(hg), lb_logits, hg_norm_g[l], *hg_state, layer=l)
            ssd_y, ssd_f, ssd_b = ssd_scan(seq(zdt), seq(xbc), ssd_par, d_skip, ssd_norm_g[l], *ssd_state,
                                           z_blk=0, dt_blk=SSD_INNER // LANES)
            branches = None
            if want_out:
                fu, den = hy_filter(seg_len, *filt_args)
                if seg_len == FFT_L:
                    hy_y = hyena_conv(seq(x0c), seq(hw), hyena_spectrum(fu, den), hy_bias[l])
                else:
                    hy_y = hyena_conv_short(seq(x0c), seq(hw), fu, den, hy_bias[l])
                flat = lambda a: a.reshape(tokens.shape[0], a.shape[2])
                branches = [flat(hy_y), sc_y, flat(hg_y), flat(ssd_y)]
            return branches, (hg_f, hg_b), (ssd_f, ssd_b)

        def mod_rows(mod, tokens):
            return tokens.shape[0] // mod.shape[0]

        def finish(tokens, mod, branches, pos=None):
            rows = mod_rows(mod, tokens)
            t1 = merge_ln(tokens, mod, branches, w_gate[l].astype(BF16), b_gate[l], w_br[l].astype(BF16),
                          w_o[l].astype(BF16), ln1_g[l], ln1_b[l], rows_per_mod=rows, tm=512, pos=pos)
            return moe_ln(t1, mod, w_router, b_router, w_e1_b, w_e3_b, w_e2_b, ln2_g[l], ln2_b[l],
                          layer=l, rows_per_mod=rows, tm=1024)

        zero_hg = (z_state(B, HG_KEY, HG_KEY),) * 2
        zero_ssd = (z_state(B, SSD_GROUPS, LANES, LANES),) * 2
        br_ctx, hg_state, ssd_state = mixers(cx, mod_ctx, LC, zero_hg, zero_ssd, ctx_out)
        pos = pos_table if l == 0 else None
        br_lat, _, _ = mixers(lat, mod_lat, L, hg_state, ssd_state, True, pos=pos)
        lat = finish(lat, mod_lat, br_lat, pos=pos)
        if ctx_out:
            cx = finish(cx, mod_ctx, br_ctx)
    return lat.reshape(B, L, D)
```

```python
import functools
import math

import jax
import jax.numpy as jnp
import numpy as np
from jax import lax
from jax.experimental import pallas as pl
from jax.experimental.pallas import tpu as pltpu

F32 = jnp.float32
BF16 = jnp.bfloat16

N_EXPERTS = 16
N_EXPERT_GROUPS = 4
GROUP_SIZE = N_EXPERTS // N_EXPERT_GROUPS
DEPTH = 2
DEEPNORM_ALPHA = (2 * DEPTH) ** 0.25
LN_EPS = 1e-5
RMS_EPS = 1e-6

VMEM_LIMIT_BYTES = 52 * 1024 * 1024


def _layernorm_rows(z, g, b):
    mu = jnp.mean(z, axis=-1, keepdims=True)
    zc = z - mu
    var = jnp.mean(zc * zc, axis=-1, keepdims=True)
    return zc * lax.rsqrt(var + LN_EPS) * g + b


def _route_t(logits_t, bias_col):
    z = logits_t - jnp.max(logits_t, axis=0, keepdims=True)
    ex = jnp.exp(z)
    scores = ex / jnp.sum(ex, axis=0, keepdims=True)
    sel = scores + bias_col
    eidx = lax.broadcasted_iota(jnp.int32, sel.shape, 0)
    neg = jnp.float32(-jnp.inf)

    def first_argmax(m):
        top = jnp.max(m, axis=0, keepdims=True)
        return top, jnp.min(jnp.where(m == top, eidx, N_EXPERTS), axis=0, keepdims=True)

    best_score = None
    best_grp = None
    for g in range(N_EXPERT_GROUPS):
        m = jnp.where((eidx >= g * GROUP_SIZE) & (eidx < (g + 1) * GROUP_SIZE), sel, neg)
        t1, i1 = first_argmax(m)
        t2 = jnp.max(jnp.where(eidx == i1, neg, m), axis=0, keepdims=True)
        s = t1 + t2
        if g == 0:
            best_score, best_grp = s, jnp.zeros_like(i1)
        else:
            better = s > best_score
            best_score = jnp.where(better, s, best_score)
            best_grp = jnp.where(better, g, best_grp)
    lo = best_grp * GROUP_SIZE
    masked = jnp.where((eidx >= lo) & (eidx < lo + GROUP_SIZE), sel, neg)
    _, ia = first_argmax(masked)
    _, ib = first_argmax(jnp.where(eidx == ia, neg, masked))
    w = jnp.where((eidx == ia) | (eidx == ib), scores, 0.0)
    return w / jnp.sum(w, axis=0, keepdims=True), best_grp


MOE_CHUNK = 32
MOE_ARMS = (1024, 512, 256, 128, 64, 32)
MOE_SINGLE_ARMS = (256, 288, 320)
MOE_STEP_EXPERTS = 2
GATE_TERMS = 3
POS_LANE = GATE_TERMS * N_EXPERTS


def _moe_ln_kernel(x_ref, mod_ref, wr_ref, br_ref, w1_ref, w3_ref, w2_ref, lng_ref, lnb_ref,
                   o_ref, pt_scr, hc_scr, gc_scr, yc_scr, seg_smem):
    step = pl.program_id(1)
    TM, R = pt_scr.shape
    D = x_ref.shape[1]
    G = N_EXPERT_GROUPS

    @pl.when(step == 0)
    def _():
        h = _modulate(x_ref[...], mod_ref, MOD_SHIFT2, MOD_SCALE2)
        h_hi = h.astype(BF16)
        h_lo = (h - h_hi.astype(F32)).astype(BF16)
        r_hi = jnp.dot(h_hi, wr_ref[...], preferred_element_type=F32)
        r_lo = jnp.dot(h_lo, wr_ref[:, :LANES], preferred_element_type=F32)
        logits_t = (r_hi[:, :LANES] + r_hi[:, LANES:] + r_lo).T[:N_EXPERTS]
        gate_t, grp = _route_t(logits_t, br_ref[:, 0:1])

        gidx = lax.broadcasted_iota(jnp.int32, (SUBLANES, TM), 0)
        tok = lax.broadcasted_iota(jnp.int32, (SUBLANES, TM), 1)
        member = jnp.where(gidx == grp, 1.0, 0.0)
        incl = member
        shift = 1
        while shift < TM:
            incl = incl + jnp.where(tok >= shift, pltpu.roll(incl, shift, 1), 0.0)
            shift *= 2
        rank = incl - member
        offset = jnp.int32(0)
        pos_row = jnp.zeros((1, TM), F32)
        for g in range(G):
            n_chunks = (jnp.sum(member[g:g + 1, :]).astype(jnp.int32) + (MOE_CHUNK - 1)) // MOE_CHUNK
            seg_smem[g] = n_chunks
            seg_smem[G + g] = offset
            pos_row = pos_row + member[g:g + 1, :] * (rank[g:g + 1, :] + offset.astype(F32))
            offset = offset + n_chunks * MOE_CHUNK

        terms, rest = [], gate_t
        for _ in range(GATE_TERMS):
            part = rest.astype(BF16).astype(F32)
            terms.append(part)
            rest = rest - part
        pad = jnp.zeros((LANES - POS_LANE - 1, TM), F32)
        side = jnp.concatenate(terms + [pos_row, pad], axis=0).T
        pos_col = side[:, POS_LANE:POS_LANE + 1].astype(jnp.int32)
        pt_scr[...] = jnp.where(lax.broadcasted_iota(jnp.int32, (TM, R), 1) == pos_col, 1.0, 0.0).astype(BF16)
        p = jnp.where(lax.broadcasted_iota(jnp.int32, (R, TM), 0) == pos_row.astype(jnp.int32), 1.0, 0.0).astype(BF16)
        gathered = jnp.dot(p, jnp.concatenate([h_hi, side.astype(BF16)], axis=1), preferred_element_type=F32)
        hc_scr[...] = gathered[:, :D].astype(BF16)
        gc_scr[...] = gathered[:, D:]
        yc_scr[...] = jnp.zeros_like(yc_scr)

    g = step // (GROUP_SIZE // MOE_STEP_EXPERTS)
    n_chunks = seg_smem[g]
    base = seg_smem[G + g]

    def expert_rows(start, size):
        rows = pl.ds(pl.multiple_of(start, MOE_CHUNK), size)
        hc = hc_scr[rows, :]
        gc = gc_scr[rows, :]
        lane = lax.broadcasted_iota(jnp.int32, gc.shape, 1)
        acc = None
        for k in range(MOE_STEP_EXPERTS):
            e = step * MOE_STEP_EXPERTS + k
            a = jnp.dot(hc, w1_ref[k], preferred_element_type=F32)
            b = jnp.dot(hc, w3_ref[k], preferred_element_type=F32)
            mid = (a * jax.nn.sigmoid(a) * b).astype(BF16)
            y = jnp.dot(mid, w2_ref[k], preferred_element_type=F32)
            gcol = jnp.sum(jnp.where((lane % N_EXPERTS == e) & (lane < POS_LANE), gc, 0.0), axis=-1, keepdims=True)
            acc = gcol * y if acc is None else acc + gcol * y
        yc_scr[rows, :] += acc

    single = None
    for size in MOE_SINGLE_ARMS:
        if size > TM:
            continue
        hit = n_chunks == size // MOE_CHUNK

        @pl.when(hit)
        def _(size=size):
            expert_rows(base, size)

        single = hit if single is None else single | hit

    for size in MOE_ARMS:
        if size > TM:
            continue
        taken = (n_chunks & (size // MOE_CHUNK)) != 0
        if single is not None:
            taken = taken & jnp.logical_not(single)

        @pl.when(taken)
        def _(base=base, size=size):
            expert_rows(base, size)

        base = base + jnp.where(taken, size, 0)

    @pl.when(step == N_EXPERTS // MOE_STEP_EXPERTS - 1)
    def _():
        moe = jnp.dot(pt_scr[...], yc_scr[...].astype(BF16), preferred_element_type=F32)
        z = DEEPNORM_ALPHA * x_ref[...] + mod_ref[0, MOD_GATE2:MOD_GATE2 + 1, :] * moe
        o_ref[...] = _layernorm_rows(z, lng_ref[...], lnb_ref[...])


def moe_ln(x, mod, w_router, b_router, w1, w3, w2, ln_g, ln_b, *, layer, rows_per_mod, tm):
    T, D = x.shape
    _, E, _, FF = w1.shape
    assert T % tm == 0 and rows_per_mod % tm == 0 and tm <= MOE_ARMS[0] and tm % MOE_CHUNK == 0
    tiles_per_mod = rows_per_mod // tm
    sorted_rows = tm + N_EXPERT_GROUPS * MOE_CHUNK
    wr_hi = w_router.astype(BF16)
    wr_lo = (w_router - wr_hi.astype(F32)).astype(BF16)
    lane_pad = lambda w: jnp.pad(w, ((0, 0), (0, LANES - E)))
    wr_split = jnp.concatenate([lane_pad(wr_hi), lane_pad(wr_lo)], axis=1)
    br_col = jnp.broadcast_to(b_router.reshape(E, 1), (E, LANES))
    return pl.pallas_call(
        _moe_ln_kernel,
        grid=(T // tm, E // MOE_STEP_EXPERTS),
        in_specs=[
            pl.BlockSpec((tm, D), lambda i, e: (i, 0)),
            pl.BlockSpec((1, 6, D), lambda i, e: (i // tiles_per_mod, 0, 0)),
            pl.BlockSpec((D, 2 * LANES), lambda i, e: (0, 0)),
            pl.BlockSpec((E, LANES), lambda i, e: (0, 0)),
            pl.BlockSpec((None, MOE_STEP_EXPERTS, D, FF), lambda i, e: (layer, e, 0, 0)),
            pl.BlockSpec((None, MOE_STEP_EXPERTS, D, FF), lambda i, e: (layer, e, 0, 0)),
            pl.BlockSpec((None, MOE_STEP_EXPERTS, FF, D), lambda i, e: (layer, e, 0, 0)),
            pl.BlockSpec((1, D), lambda i, e: (0, 0)),
            pl.BlockSpec((1, D), lambda i, e: (0, 0)),
        ],
        out_specs=pl.BlockSpec((tm, D), lambda i, e: (i, 0)),
        out_shape=jax.ShapeDtypeStruct((T, D), F32),
        scratch_shapes=[
            pltpu.VMEM((tm, sorted_rows), BF16),
            pltpu.VMEM((sorted_rows, D), BF16),
            pltpu.VMEM((sorted_rows, LANES), F32),
            pltpu.VMEM((sorted_rows, D), F32),
            pltpu.SMEM((2 * N_EXPERT_GROUPS,), jnp.int32),
        ],
        compiler_params=pltpu.CompilerParams(
            dimension_semantics=("parallel", "arbitrary"), vmem_limit_bytes=VMEM_LIMIT_BYTES),
        name="moe_ln",
    )(x, mod, wr_split, br_col, w1, w3, w2, ln_g.reshape(1, D), ln_b.reshape(1, D))


def _ada_kernel(c_ref, w_ref, b_ref, o_ref):
    c = c_ref[...]
    s = c * jax.nn.sigmoid(c)
    o_ref[0] = jnp.dot(s, w_ref[0], precision=lax.Precision.HIGHEST, preferred_element_type=F32) + b_ref[0]


def ada_mod(cvecs, w_ada, b_ada, *, tn=1536):
    R, D = cvecs.shape
    depth, _, N = w_ada.shape
    assert N % tn == 0
    return pl.pallas_call(
        _ada_kernel,
        grid=(depth, N // tn),
        in_specs=[
            pl.BlockSpec((R, D), lambda l, j: (0, 0)),
            pl.BlockSpec((1, D, tn), lambda l, j: (l, 0, j)),
            pl.BlockSpec((1, 1, tn), lambda l, j: (l, 0, j)),
        ],
        out_specs=pl.BlockSpec((1, R, tn), lambda l, j: (l, 0, j)),
        out_shape=jax.ShapeDtypeStruct((depth, R, N), F32),
        compiler_params=pltpu.CompilerParams(
            dimension_semantics=("parallel", "parallel"), vmem_limit_bytes=VMEM_LIMIT_BYTES),
        name="ada_mod",
    )(cvecs, w_ada, b_ada.reshape(depth, 1, N))


MOD_SHIFT1, MOD_SCALE1, MOD_GATE1, MOD_SHIFT2, MOD_SCALE2, MOD_GATE2 = range(6)


def _modulate(x, mod_ref, shift_row, scale_row):
    return x * (1.0 + mod_ref[0, scale_row:scale_row + 1, :]) + mod_ref[0, shift_row:shift_row + 1, :]


HALO = 8
BRANCH_W = 256
HY_COLS = 3 * BRANCH_W
SSD_XBC_COLS = 512
IN_CONV_COLS = HY_COLS + 2 * BRANCH_W + SSD_XBC_COLS


def _conv3(u, w_ref, tm):
    return (u[HALO - 1:HALO - 1 + tm] * w_ref[0:1, :] + u[HALO:HALO + tm] * w_ref[1:2, :]
            + u[HALO + 1:HALO + 1 + tm] * w_ref[2:3, :])


def _in_proj_kernel(x_ref, xp_ref, xn_ref, mod_ref, w_ref, hcw_ref, hcb_ref, scw_ref, xcw_ref, xcb_ref, *rest,
                    has_pos, tiles_per_seq):
    if has_pos:
        pos_ref, pp_ref, pn_ref = rest[:3]
        rest = rest[3:]
    x0c_ref, hw_ref, scy_ref, xbc_ref, hg_ref, zdt_ref = rest
    tm = x_ref.shape[0]
    t = pl.program_id(0) % tiles_per_seq
    keep_prev = jnp.where(t > 0, 1.0, 0.0)
    keep_next = jnp.where(t < tiles_per_seq - 1, 1.0, 0.0)

    def tokens(main_ref, prev_ref, next_ref):
        return main_ref[...], prev_ref[...], next_ref[...]

    xm, xp, xn = tokens(x_ref, xp_ref, xn_ref)
    if has_pos:
        pm, pp, pn = tokens(pos_ref, pp_ref, pn_ref)
        xm, xp, xn = xm + pm, xp + pp, xn + pn
    mod = lambda v: _modulate(v, mod_ref, MOD_SHIFT1, MOD_SCALE1)
    h_main = mod(xm).astype(BF16)
    h_all = jnp.concatenate([mod(xp) * keep_prev, mod(xm), mod(xn) * keep_next], axis=0).astype(BF16)
    uc = jnp.dot(h_all, w_ref[:, :IN_CONV_COLS], preferred_element_type=F32)
    ur = jnp.dot(h_main, w_ref[:, IN_CONV_COLS:], preferred_element_type=F32)

    hy = _conv3(uc[:, :HY_COLS], hcw_ref, tm) + hcb_ref[...]
    x0c_ref[...] = hy[:, :BRANCH_W]
    hw_ref[...] = hy[:, BRANCH_W:2 * BRANCH_W] * hy[:, 2 * BRANCH_W:]
    cg_xs = uc[:, HY_COLS:HY_COLS + BRANCH_W] * uc[:, HY_COLS + BRANCH_W:HY_COLS + 2 * BRANCH_W]
    scy_ref[...] = ur[:, :BRANCH_W] * _conv3(cg_xs, scw_ref, tm)
    xa = _conv3(uc[:, HY_COLS + 2 * BRANCH_W:], xcw_ref, tm) + xcb_ref[...]
    xbc_ref[...] = xa * jax.nn.sigmoid(xa)
    n_hg = hg_ref.shape[1]
    hg_ref[...] = ur[:, BRANCH_W:BRANCH_W + n_hg]
    zdt_ref[...] = ur[:, BRANCH_W + n_hg:]


def in_proj(x, mod, w, conv, out_widths, *, seq_len, rows_per_mod, tm, pos=None):
    T, D = x.shape
    assert T % tm == 0 and seq_len % tm == 0 and rows_per_mod % tm == 0 and tm % HALO == 0
    tiles_per_seq = seq_len // tm
    tiles_per_mod = rows_per_mod // tm
    hb = tm // HALO
    last_blk = T // HALO - 1
    hy_w, hy_b, sc_w, xbc_w, xbc_b = conv
    row = lambda v: v.reshape(1, -1)
    const = lambda a: pl.BlockSpec(a.shape, lambda i: (0,) * a.ndim)
    consts = [hy_w, row(hy_b), sc_w, xbc_w, row(xbc_b)]
    in_specs = [
        pl.BlockSpec((tm, D), lambda i: (i, 0)),
        pl.BlockSpec((HALO, D), lambda i: (jnp.maximum(i * hb - 1, 0), 0)),
        pl.BlockSpec((HALO, D), lambda i: (jnp.minimum((i + 1) * hb, last_blk), 0)),
        pl.BlockSpec((1, 6, D), lambda i: (i // tiles_per_mod, 0, 0)),
        pl.BlockSpec(w.shape, lambda i: (0, 0), pipeline_mode=pl.Buffered(1)),
    ] + [const(a) for a in consts]
    args = [x, x, x, mod, w] + consts
    if pos is not None:
        assert pos.shape[0] == seq_len
        last_pos = seq_len // HALO - 1
        in_specs += [
            pl.BlockSpec((tm, D), lambda i: (i % tiles_per_seq, 0)),
            pl.BlockSpec((HALO, D), lambda i: (jnp.maximum((i % tiles_per_seq) * hb - 1, 0), 0)),
            pl.BlockSpec((HALO, D), lambda i: (jnp.minimum((i % tiles_per_seq + 1) * hb, last_pos), 0)),
        ]
        args += [pos, pos, pos]
    return pl.pallas_call(
        functools.partial(_in_proj_kernel, has_pos=pos is not None, tiles_per_seq=tiles_per_seq),
        grid=(T // tm,),
        in_specs=in_specs,
        out_specs=[pl.BlockSpec((tm, n), lambda i: (i, 0)) for n in out_widths],
        out_shape=[jax.ShapeDtypeStruct((T, n), F32) for n in out_widths],
        compiler_params=pltpu.CompilerParams(
            dimension_semantics=("parallel",), vmem_limit_bytes=VMEM_LIMIT_BYTES),
        name="in_proj",
    )(*args)


def _pos_spec(pos, rows_per_mod, tm):
    assert pos.shape[0] == rows_per_mod
    tiles = rows_per_mod // tm
    return pl.BlockSpec((tm, pos.shape[1]), lambda i: (i % tiles, 0))


N_BRANCH = 4


def _merge_kernel(x_ref, mod_ref, hy_ref, sc_ref, hg_ref, ssd_ref, wg_ref, bg_ref, wbr_ref, wo_ref,
                  lng_ref, lnb_ref, *rest, has_pos):
    x = x_ref[...] + rest[0][...] if has_pos else x_ref[...]
    o_ref = rest[-1]
    D = x.shape[1]
    h = _modulate(x, mod_ref, MOD_SHIFT1, MOD_SCALE1).astype(BF16)
    y = None
    for k, br_ref in enumerate((hy_ref, sc_ref, hg_ref, ssd_ref)):
        gate = jax.nn.sigmoid(
            jnp.dot(h, wg_ref[:, k * D:(k + 1) * D], preferred_element_type=F32) + bg_ref[:, k * D:(k + 1) * D])
        term = gate * jnp.dot(br_ref[...].astype(BF16), wbr_ref[k], preferred_element_type=F32)
        y = term if y is None else y + term
    y = jnp.dot(y.astype(BF16), wo_ref[...], preferred_element_type=F32)
    z = DEEPNORM_ALPHA * x + mod_ref[0, MOD_GATE1:MOD_GATE1 + 1, :] * y
    o_ref[...] = _layernorm_rows(z, lng_ref[...], lnb_ref[...])


def merge_ln(x, mod, branches, w_gate, b_gate, w_br, w_o, ln_g, ln_b, *, rows_per_mod, tm, pos=None):
    T, D = x.shape
    BW = branches[0].shape[1]
    assert T % tm == 0 and rows_per_mod % tm == 0
    tiles_per_mod = rows_per_mod // tm
    const = lambda shape: pl.BlockSpec(shape, lambda i: (0,) * len(shape), pipeline_mode=pl.Buffered(1))
    extra = [] if pos is None else [pos]
    return pl.pallas_call(
        functools.partial(_merge_kernel, has_pos=pos is not None),
        grid=(T // tm,),
        in_specs=[
            pl.BlockSpec((tm, D), lambda i: (i, 0)),
            pl.BlockSpec((1, 6, D), lambda i: (i // tiles_per_mod, 0, 0)),
        ] + [pl.BlockSpec((tm, BW), lambda i: (i, 0))] * N_BRANCH + [
            const((D, N_BRANCH * D)), const((1, N_BRANCH * D)), const((N_BRANCH, BW, D)), const((D, D)),
            const((1, D)), const((1, D)),
        ] + [_pos_spec(p, rows_per_mod, tm) for p in extra],
        out_specs=pl.BlockSpec((tm, D), lambda i: (i, 0)),
        out_shape=jax.ShapeDtypeStruct((T, D), F32),
        compiler_params=pltpu.CompilerParams(
            dimension_semantics=("parallel",), vmem_limit_bytes=VMEM_LIMIT_BYTES),
        name="merge_ln",
    )(x, mod, *branches, w_gate, b_gate.reshape(1, -1), w_br, w_o, ln_g.reshape(1, D), ln_b.reshape(1, D), *extra)


SSD_HEADS = 4
SSD_HEADDIM = 64
SSD_STATE = 64
SSD_GROUPS = 2
SSD_INNER = SSD_HEADS * SSD_HEADDIM
LANES = 128
NEG_INF = float("-inf")


SCAN_BATCH = 4


def _scan_chunk_index(ph, j, n):
    return j + (1 - ph) * (n - 1 - 2 * j)


def _ssd_kernel(z_ref, xbc_ref, dt_ref, par_ref, dskip_ref, ng_ref, sf0_ref, sb0_ref,
                y_ref, sf_ref, sb_ref, yb_scr, st_scr, *, n_chunks):
    ph = pl.program_id(1)
    j = pl.program_id(2)
    is_fwd = ph == 1
    c = _scan_chunk_index(ph, j, n_chunks)
    NB, R = xbc_ref.shape[0], xbc_ref.shape[1]

    @pl.when((j == 0) & is_fwd)
    def _():
        st_scr[...] = sf0_ref[...]

    @pl.when((j == 0) & jnp.logical_not(is_fwd))
    def _():
        st_scr[...] = sb0_ref[...]

    row = lax.broadcasted_iota(jnp.int32, (R, R), 0)
    col = lax.broadcasted_iota(jnp.int32, (R, R), 1)
    mask = jnp.where(is_fwd, row - col, col - row) >= 0
    tri = jnp.where(mask, 1.0, 0.0).astype(BF16)
    head_of_lane = lax.broadcasted_iota(jnp.int32, (LANES, SSD_INNER), 1) // SSD_HEADDIM
    expand = jnp.where(lax.broadcasted_iota(jnp.int32, (LANES, SSD_INNER), 0) == head_of_lane, 1.0, 0.0).astype(BF16)
    ys = [_ssd_chunk(bi, is_fwd, mask, tri, expand, xbc_ref, dt_ref, par_ref, st_scr) for bi in range(NB)]

    @pl.when(jnp.logical_not(is_fwd))
    def _():
        for bi in range(NB):
            yb_scr[bi, c] = ys[bi]

    @pl.when(is_fwd)
    def _():
        for bi in range(NB):
            z = z_ref[bi]
            xs = xbc_ref[bi, :, :SSD_INNER]
            yt = (ys[bi] + yb_scr[bi, c] + xs * dskip_ref[...]) * (z * jax.nn.sigmoid(z))
            ms = jnp.mean(yt * yt, axis=-1, keepdims=True)
            y_ref[bi] = yt * lax.rsqrt(ms + RMS_EPS) * ng_ref[...]

    @pl.when((j == n_chunks - 1) & is_fwd)
    def _():
        sf_ref[...] = st_scr[...]

    @pl.when((j == n_chunks - 1) & jnp.logical_not(is_fwd))
    def _():
        sb_ref[...] = st_scr[...]


def _ssd_chunk(bi, is_fwd, mask, tri, expand, xbc_ref, dt_ref, par_ref, st_scr):
    R = xbc_ref.shape[1]
    dsel = lambda v: jnp.where(is_fwd, v, pltpu.roll(v, LANES - SSD_HEADS, 1))
    dt_raw = dsel(dt_ref[bi]) + dsel(par_ref[8:16, :])[0:1, :]
    dt = jnp.maximum(dt_raw, 0.0) + jnp.log(1.0 + jnp.exp(-jnp.abs(dt_raw)))
    a = -jnp.exp(dsel(par_ref[0:8, :])[0:1, :]) * dt
    cs = _split_dot(tri, a, terms=3)
    cs_t = cs.T
    total = jnp.sum(a, axis=0, keepdims=True)

    spread = lambda v: jnp.dot(v.astype(BF16), expand, preferred_element_type=F32)
    dt_bc = spread(dt)
    in_decay = spread(jnp.exp(cs))
    out_decay = spread(jnp.exp(total - cs))
    e_total = jnp.exp(total)

    xbc = xbc_ref[bi]
    xdt = xbc[:, :SSD_INNER] * dt_bc
    xdt_b = xdt.astype(BF16)
    xout_b = (xdt * out_decay).astype(BF16)
    bm = xbc[:, SSD_INNER:SSD_INNER + LANES]
    cm_b = xbc[:, SSD_INNER + LANES:SSD_INNER + 2 * LANES].astype(BF16)
    lane = lax.broadcasted_iota(jnp.int32, (R, LANES), 1)
    lo_half = lane < SSD_HEADDIM
    lo_half_st = lax.broadcasted_iota(jnp.int32, (LANES, LANES), 1) < SSD_HEADDIM
    y_groups = []
    for g in range(SSD_GROUPS):
        h0, h1 = 2 * g, 2 * g + 1
        cols = slice(g * LANES, (g + 1) * LANES)
        bm_g = jnp.where((lane >= g * SSD_STATE) & (lane < (g + 1) * SSD_STATE), bm, 0.0).astype(BF16)
        G = lax.dot_general(cm_b, bm_g, (((1,), (1,)), ((), ())), preferred_element_type=F32)
        st_old = st_scr[bi, g]
        y_off = in_decay[:, cols] * jnp.dot(cm_b, st_old.astype(BF16), preferred_element_type=F32)
        y_diag = []
        for h in (h0, h1):
            decay = jnp.exp(jnp.where(mask, cs[:, h:h + 1] - cs_t[h:h + 1, :], NEG_INF))
            y_diag.append(jnp.dot((G * decay).astype(BF16), xdt_b[:, cols], preferred_element_type=F32))
        y_groups.append(jnp.where(lo_half, y_diag[0], y_diag[1]) + y_off)
        upd = lax.dot_general(bm_g, xout_b[:, cols], (((0,), (0,)), ((), ())), preferred_element_type=F32)
        st_scr[bi, g] = jnp.where(lo_half_st, e_total[:, h0:h0 + 1], e_total[:, h1:h1 + 1]) * st_old + upd
    return jnp.concatenate(y_groups, axis=1)


def ssd_scan(ssd, xbc, par, d_skip, norm_g, s_f0, s_b0, *, z_blk, dt_blk, chunk=256):
    B, L, _ = ssd.shape
    R = min(chunk, L)
    assert L % R == 0 and B % SCAN_BATCH == 0
    n = L // R
    NB = SCAN_BATCH
    cidx = lambda ph, j: _scan_chunk_index(ph, j, n)
    st_spec = pl.BlockSpec((NB, SSD_GROUPS, LANES, LANES), lambda b, ph, j: (b, 0, 0, 0))
    st_shape = jax.ShapeDtypeStruct((B, SSD_GROUPS, LANES, LANES), F32)
    return pl.pallas_call(
        functools.partial(_ssd_kernel, n_chunks=n),
        grid=(B // NB, 2, n),
        in_specs=[
            pl.BlockSpec((NB, R, SSD_INNER), lambda b, ph, j: (b, cidx(ph, j), z_blk)),
            pl.BlockSpec((NB, R, xbc.shape[2]), lambda b, ph, j: (b, cidx(ph, j), 0)),
            pl.BlockSpec((NB, R, LANES), lambda b, ph, j: (b, cidx(ph, j), dt_blk)),
            pl.BlockSpec((16, LANES), lambda b, ph, j: (0, 0)),
            pl.BlockSpec((1, SSD_INNER), lambda b, ph, j: (0, 0)),
            pl.BlockSpec((1, SSD_INNER), lambda b, ph, j: (0, 0)),
            st_spec, st_spec,
        ],
        out_specs=[pl.BlockSpec((NB, R, SSD_INNER), lambda b, ph, j: (b, ph * j, 0)), st_spec, st_spec],
        out_shape=[jax.ShapeDtypeStruct((B, L, SSD_INNER), F32), st_shape, st_shape],
        scratch_shapes=[pltpu.VMEM((NB, n, R, SSD_INNER), F32), pltpu.VMEM((NB, SSD_GROUPS, LANES, LANES), F32)],
        compiler_params=pltpu.CompilerParams(
            dimension_semantics=("parallel", "arbitrary", "arbitrary"), vmem_limit_bytes=VMEM_LIMIT_BYTES),
        name="ssd_scan",
    )(ssd, xbc, ssd, par, d_skip.reshape(1, SSD_INNER), norm_g.reshape(1, SSD_INNER), s_f0, s_b0)


HG_HEADS = 4
HG_DK = 64
HG_DV = 64
HG_KEY = HG_HEADS * HG_DK
HG_SUB = 16
HG_CHUNK = 64
HG_SUBS_PER_CHUNK = HG_CHUNK // HG_SUB


def _split_dot(mask_b, x, terms=2):
    out = None
    for _ in range(terms):
        part = x.astype(BF16)
        prod = jnp.dot(mask_b, part, preferred_element_type=F32)
        out = prod if out is None else out + prod
        x = x - part.astype(F32)
    return out


def _hgrn_kernel(q_ref, f_ref, v_ref, g_ref, lbl_ref, ng_ref, sf0_ref, sb0_ref,
                 y_ref, sf_ref, sb_ref, ob_scr, st_scr, *, n_blocks, layer, depth):
    ph = pl.program_id(1)
    j = pl.program_id(2)
    is_fwd = ph == 1
    c = _scan_chunk_index(ph, j, n_blocks)
    NB, R = q_ref.shape[0], q_ref.shape[1]
    W = HG_KEY

    @pl.when((j == 0) & is_fwd)
    def _():
        st_scr[...] = sf0_ref[...]

    @pl.when((j == 0) & jnp.logical_not(is_fwd))
    def _():
        st_scr[...] = sb0_ref[...]

    def lower_bound(d):
        x = lbl_ref[d * depth:(d + 1) * depth, :]
        e = jnp.exp(x - jnp.max(x, axis=0, keepdims=True))
        p = e / jnp.sum(e, axis=0, keepdims=True)
        return jnp.sum(p[1:layer + 1, :], axis=0, keepdims=True) if layer > 0 else jnp.zeros((1, W), F32)

    lb = jnp.where(is_fwd, lower_bound(0), lower_bound(1))
    row = lax.broadcasted_iota(jnp.int32, (R, R), 0)
    col = lax.broadcasted_iota(jnp.int32, (R, R), 1)
    dirge = jnp.where(is_fwd, row - col, col - row) >= 0
    same_sub = (row // HG_SUB) == (col // HG_SUB)
    same_chunk = (row // HG_CHUNK) == (col // HG_CHUNK)
    one = lambda m: jnp.where(m, 1.0, 0.0).astype(BF16)
    dist = jnp.where(is_fwd, row // HG_SUB - col // HG_SUB, col // HG_SUB - row // HG_SUB)
    masks = dict(
        cum16=one(same_sub & dirge), tot16=one(same_sub), cum64=one(same_chunk & dirge), tot64=one(same_chunk),
        dist=[same_sub & dirge] + [same_chunk & (dist == d) for d in range(1, HG_SUBS_PER_CHUNK)])
    lane = lax.broadcasted_iota(jnp.int32, (R, W), 1)
    head_mask = [(lane // HG_DK) == h for h in range(HG_HEADS)]
    blocks = [_hgrn_block(bi, is_fwd, lb, masks, head_mask, q_ref, f_ref, v_ref) for bi in range(NB)]
    srow = lax.broadcasted_iota(jnp.int32, (W, W), 0)
    scol = lax.broadcasted_iota(jnp.int32, (W, W), 1)
    bd_mask = (srow // HG_DV) == (scol // HG_DK)

    def through_state(bi, reverse):
        o_intra, qd, ke, v_b, tot64 = blocks[bi]
        n_chunks = R // HG_CHUNK
        st = st_scr[bi]
        parts = [None] * n_chunks
        for i in (reversed(range(n_chunks)) if reverse else range(n_chunks)):
            rows = slice(i * HG_CHUNK, (i + 1) * HG_CHUNK)
            parts[i] = lax.dot_general(qd[rows], st.astype(BF16), (((1,), (1,)), ((), ())), preferred_element_type=F32)
            upd = lax.dot_general(v_b[rows], ke[rows], (((0,), (0,)), ((), ())), preferred_element_type=F32)
            st = jnp.exp(tot64[i * HG_CHUNK:i * HG_CHUNK + 1, :]) * st + jnp.where(bd_mask, upd, 0.0)
        st_scr[bi] = st
        return o_intra + jnp.concatenate(parts, axis=0)

    @pl.when(jnp.logical_not(is_fwd))
    def _():
        for bi in range(NB):
            ob_scr[bi, c] = through_state(bi, reverse=True)

    @pl.when(is_fwd)
    def _():
        for bi in range(NB):
            ot = through_state(bi, reverse=False) + ob_scr[bi, c]
            sq = ot * ot
            ms = jnp.zeros((R, W), F32)
            for h in range(HG_HEADS):
                s = jnp.sum(jnp.where(head_mask[h], sq, 0.0), axis=-1, keepdims=True) * (1.0 / HG_DV)
                ms = jnp.where(head_mask[h], s, ms)
            g = g_ref[bi]
            y_ref[bi] = ot * lax.rsqrt(ms + RMS_EPS) * ng_ref[...] * (g * jax.nn.sigmoid(g))

    @pl.when((j == n_blocks - 1) & is_fwd)
    def _():
        sf_ref[...] = st_scr[...]

    @pl.when((j == n_blocks - 1) & jnp.logical_not(is_fwd))
    def _():
        sb_ref[...] = st_scr[...]


def _hgrn_block(bi, is_fwd, lb, masks, head_mask, q_ref, f_ref, v_ref):
    R = q_ref.shape[1]
    W = HG_KEY
    sig = jax.nn.sigmoid(f_ref[bi])
    logf = jnp.log(lb + (1.0 - lb) * sig)
    k = (1.0 - lb) * (1.0 - sig)
    q = q_ref[bi] * (HG_DK ** -0.5)
    v_b = v_ref[bi].astype(BF16)
    logf_hi = logf.astype(BF16)
    logf_lo = (logf - logf_hi.astype(F32)).astype(BF16)
    msum = lambda m: (jnp.dot(m, logf_hi, preferred_element_type=F32) + jnp.dot(m, logf_lo, preferred_element_type=F32))
    b16 = msum(masks["cum16"])
    tot16 = msum(masks["tot16"])
    b64 = msum(masks["cum64"])
    tot64 = msum(masks["tot64"])

    shifted = lambda s: jnp.where(is_fwd, pltpu.roll(tot16, R - s, 0), pltpu.roll(tot16, s, 0))
    nx1 = shifted(HG_SUB)
    nx2 = nx1 + shifted(2 * HG_SUB)
    k_end16 = k * jnp.exp(tot16 - b16)
    keys = [(k * jnp.exp(-b16)).astype(BF16), k_end16.astype(BF16),
            (k_end16 * jnp.exp(nx1)).astype(BF16), (k_end16 * jnp.exp(nx2)).astype(BF16)]
    q16 = q * jnp.exp(b16)
    qm =jnp.concatenate([jnp.where(m, q16, 0.0) for m in head_mask], axis=0).astype(BF16)
    att = [jnp.zeros((R, R), F32)] * HG_HEADS
    for d in range(HG_SUBS_PER_CHUNK):
        sc = lax.dot_general(qm, keys[d], (((1,), (1,)), ((), ())), preferred_element_type=F32)
        att = [jnp.where(masks["dist"][d], sc[h * R:(h + 1) * R], att[h]) for h in range(HG_HEADS)]
    o_all = jnp.dot(jnp.concatenate(att, axis=0).astype(BF16), v_b, preferred_element_type=F32)
    o = jnp.zeros((R, W), F32)
    for h in range(HG_HEADS):
        o = jnp.where(head_mask[h], o_all[h * R:(h + 1) * R], o)

    return o, (q * jnp.exp(b64)).astype(BF16), (k * jnp.exp(tot64 - b64)).astype(BF16), v_b, tot64


def hgrn_scan(hg, lb_logits, norm_g, s_f0, s_b0, *, layer, block=256):
    B, L, _ = hg.shape
    W = HG_KEY
    R = min(block, L)
    assert L % R == 0 and R % HG_CHUNK == 0 and B % SCAN_BATCH == 0
    n = L // R
    NB = SCAN_BATCH
    depth = lb_logits.shape[0] // 2
    cidx = lambda ph, j: _scan_chunk_index(ph, j, n)
    col = lambda blk: pl.BlockSpec((NB, R, W), lambda b, ph, j: (b, cidx(ph, j), blk))
    st_spec = pl.BlockSpec((NB, W, W), lambda b, ph, j: (b, 0, 0))
    st_shape = jax.ShapeDtypeStruct((B, W, W), F32)
    return pl.pallas_call(
        functools.partial(_hgrn_kernel, n_blocks=n, layer=layer, depth=depth),
        grid=(B // NB, 2, n),
        in_specs=[
            col(0),
            pl.BlockSpec((NB, R, W), lambda b, ph, j: (b, cidx(ph, j), 2 - ph)),
            col(3), col(4),
            pl.BlockSpec(lb_logits.shape, lambda b, ph, j: (0, 0)),
            pl.BlockSpec((1, W), lambda b, ph, j: (0, 0)),
            st_spec, st_spec,
        ],
        out_specs=[pl.BlockSpec((NB, R, W), lambda b, ph, j: (b, ph * j, 0)), st_spec, st_spec],
        out_shape=[jax.ShapeDtypeStruct((B, L, W), F32), st_shape, st_shape],
        scratch_shapes=[pltpu.VMEM((NB, n, R, W), F32), pltpu.VMEM((NB, W, W), F32)],
        compiler_params=pltpu.CompilerParams(
            dimension_semantics=("parallel", "arbitrary", "arbitrary"), vmem_limit_bytes=VMEM_LIMIT_BYTES),
        name="hgrn_scan",
    )(hg, hg, hg, hg, lb_logits, jnp.tile(norm_g, HG_HEADS).reshape(1, W), s_f0, s_b0)


HY_WIDTH = 256
HY_BANDS = 16
HY_HIDDEN = 64
SUBLANES = 8


def _dot_hi(a, b):
    return jnp.dot(a, b, precision=lax.Precision.HIGHEST, preferred_element_type=F32)


def _hy_filter_kernel(bands_ref, w1t_ref, w1f_ref, b1_ref, fr1_ref, w2_ref, b2_ref, fr2_ref, w3a_ref, w3b_ref,
                      decay_ref, f_ref, den_ref, cosb_scr, sinb_scr, *, L):
    i = pl.program_id(0)
    TR = f_ref.shape[1]
    step = 2.0 * math.pi / L
    bands = bands_ref[...]

    @pl.when(i == 0)
    def _():
        r = lax.broadcasted_iota(jnp.int32, (TR, LANES), 0).astype(F32)
        ang = (step * r) * bands
        cosb_scr[...] = jnp.cos(ang)
        sinb_scr[...] = jnp.sin(ang)
        den_ref[...] = jnp.zeros_like(den_ref)

    one_row = lambda v: jnp.broadcast_to(v, (SUBLANES, LANES))
    base = one_row((step * (i * TR).astype(F32)) * bands)
    cos_a, sin_a = jnp.cos(base)[0:1], jnp.sin(base)[0:1]
    full = one_row((2.0 * math.pi) * bands)
    cos_p, sin_p = jnp.cos(full)[0:1], jnp.sin(full)[0:1]
    cos_b, sin_b = cosb_scr[...], sinb_scr[...]
    cos0 = cos_a * cos_b - sin_a * sin_b
    sin0 = sin_a * cos_b + cos_a * sin_b
    cos1 = cos_p * cos0 + sin_p * sin0
    sin1 = sin_p * cos0 - cos_p * sin0

    m = i * TR + lax.broadcasted_iota(jnp.int32, (TR, HY_WIDTH), 0)
    t0 = m.astype(F32) * (1.0 / (L - 1))
    t1 = jnp.where(m == 0, 0, L - m).astype(F32) * (1.0 / (L - 1))
    lane = lax.broadcasted_iota(jnp.int32, (TR, LANES), 1)
    t_both = jnp.where(lane < HY_HIDDEN, t0[:, :LANES], t1[:, :LANES])
    feats = jnp.concatenate([cos0, sin0, cos1, sin1], axis=1)
    pre = t_both * w1t_ref[...] + _dot_hi(feats, w1f_ref[...]) + b1_ref[...]
    h = jnp.sin(fr1_ref[...] * pre)
    h = jnp.sin(fr2_ref[...] * (_dot_hi(h, w2_ref[...]) + b2_ref[...]))
    decay = jnp.abs(decay_ref[...])
    filt0 = _dot_hi(h, w3a_ref[...]) * jnp.exp(-t0 * decay)
    filt1 = jnp.where(m == 0, 0.0, _dot_hi(h, w3b_ref[...]) * jnp.exp(-t1 * decay))
    f_ref[0] = filt0
    f_ref[1] = filt1
    den_ref[...] += jnp.sum(jnp.abs(filt0) + jnp.abs(filt1), axis=0, keepdims=True)


def hy_filter(L, w1, b1, fr1, w2, b2, fr2, w3, decay, *, tr=512):
    tr = min(tr, L)
    H = HY_HIDDEN
    bands = np.zeros((1, LANES), np.float32)
    bands[0, :HY_BANDS] = np.linspace(1e-4, HY_BANDS - 1, HY_BANDS, dtype=np.float32)
    pad_rows = lambda w: jnp.pad(w, ((0, LANES - HY_BANDS), (0, 0)))
    twice = lambda v: jnp.concatenate([v, v]).reshape(1, 2 * H)
    zeros = jnp.zeros((LANES, H), F32)
    w_cos, w_sin = pad_rows(w1[1:1 + HY_BANDS]), pad_rows(w1[1 + HY_BANDS:])
    w1_feat = jnp.concatenate([
        jnp.concatenate([w_cos, zeros], axis=1), jnp.concatenate([-w_sin, zeros], axis=1),
        jnp.concatenate([zeros, w_cos], axis=1), jnp.concatenate([zeros, -w_sin], axis=1)], axis=0)
    zh = jnp.zeros((H, H), F32)
    w2_both = jnp.concatenate([jnp.concatenate([w2, zh], axis=1), jnp.concatenate([zh, w2], axis=1)], axis=0)
    zw = jnp.zeros((H, HY_WIDTH), F32)
    w3_a = jnp.concatenate([w3[:, :HY_WIDTH], zw], axis=0)
    w3_b = jnp.concatenate([zw, w3[:, HY_WIDTH:]], axis=0)
    args = (jnp.asarray(bands), twice(w1[0]), w1_feat, twice(b1), twice(fr1), w2_both, twice(b2), twice(fr2),
            w3_a, w3_b, decay.reshape(1, -1))
    return pl.pallas_call(
        functools.partial(_hy_filter_kernel, L=L),
        grid=(L // tr,),
        in_specs=[pl.BlockSpec(a.shape, lambda i: (0, 0)) for a in args],
        out_specs=[pl.BlockSpec((2, tr, HY_WIDTH), lambda i: (0, i, 0)), pl.BlockSpec((1, HY_WIDTH), lambda i: (0, 0))],
        out_shape=[jax.ShapeDtypeStruct((2, L, HY_WIDTH), F32), jax.ShapeDtypeStruct((1, HY_WIDTH), F32)],
        scratch_shapes=[pltpu.VMEM((tr, LANES), F32), pltpu.VMEM((tr, LANES), F32)],
        compiler_params=pltpu.CompilerParams(
            dimension_semantics=("arbitrary",), vmem_limit_bytes=VMEM_LIMIT_BYTES),
        name="hy_filter",
    )(*args)


FFT_N1 = 64
FFT_N2 = 128
FFT_N = FFT_N1 * FFT_N2
FFT_L = FFT_N // 2
FFT_N1_NZ = FFT_N1 // 2
FFT_UNROLL_N1_STAGE = 2
FFT_UNROLL_N2_STAGE = 8


@functools.lru_cache(maxsize=None)
def _fft_constants():
    eye = np.eye(SUBLANES)
    k1 = np.arange(FFT_N1)[:, None]
    n1 = np.arange(FFT_N1_NZ)[None, :]
    ang1 = 2.0 * np.pi * ((k1 * n1) % FFT_N1) / FFT_N1
    kron_fwd = np.concatenate([np.kron(np.cos(ang1), eye), np.kron(np.sin(ang1), eye)], axis=0)
    kron_inv = np.concatenate([np.kron(np.cos(ang1).T, eye), np.kron(np.sin(ang1).T, eye)], axis=0) / FFT_N
    k = np.arange(FFT_N1)[:, None, None] + FFT_N1 * np.arange(FFT_N2)[None, :, None]
    n2 = np.arange(FFT_N2)[None, None, :]
    ang2 = 2.0 * np.pi * ((k * n2) % FFT_N) / FFT_N
    g_fwd = np.concatenate([np.cos(ang2), np.sin(ang2)], axis=1)
    g_inv = np.concatenate([np.cos(ang2).transpose(0, 2, 1), np.sin(ang2).transpose(0, 2, 1)], axis=1)
    n1_all = np.arange(FFT_N1)[None, :]
    ang1_all = 2.0 * np.pi * ((k1 * n1_all) % FFT_N1) / FFT_N1
    kron_full = np.concatenate([np.kron(np.cos(ang1_all), eye), np.kron(np.sin(ang1_all), eye)], axis=0)
    to_b = lambda a: jnp.asarray(a, dtype=F32).astype(BF16)
    return to_b(kron_fwd), to_b(g_fwd), to_b(g_inv), to_b(kron_inv), to_b(kron_full)


def _cplx_fwd(r, half_rows, cb):
    cr, sr = r[:half_rows], r[half_rows:]
    return cr[:, :cb] + sr[:, cb:], cr[:, cb:] - sr[:, :cb]


def _cplx_inv(r, half_rows, cb):
    cr, sr = r[:half_rows], r[half_rows:]
    return cr[:, :cb] - sr[:, cb:], cr[:, cb:] + sr[:, :cb]


def _fft_stage1(load_group, kron_ref, a_scr, cb):
    def body(jt, carry):
        rows = pl.ds(pl.multiple_of(jt * SUBLANES, SUBLANES), SUBLANES)
        d = load_group(rows).reshape(kron_ref.shape[1], 2 * cb).astype(BF16)
        r = jnp.dot(kron_ref[...], d, preferred_element_type=F32)
        a_re, a_im = _cplx_fwd(r, FFT_N1 * SUBLANES, cb)
        a_scr[:, rows, :] = jnp.concatenate([a_re, a_im], axis=1).reshape(FFT_N1, SUBLANES, 2 * cb)
        return carry
    lax.fori_loop(0, FFT_N2 // SUBLANES, body, 0, unroll=FFT_UNROLL_N1_STAGE)


def _fft_fwd_data_kernel(z_ref, h_ref, kron_ref, g_ref, p_ref, a_scr):
    cb = z_ref.shape[4]
    load = lambda rows: jnp.concatenate([z_ref[0, 0, :, rows, :], z_ref[0, 1, :, rows, :]], axis=-1)
    _fft_stage1(load, kron_ref, a_scr, cb)

    def body(k1, carry):
        r = jnp.dot(g_ref[k1], a_scr[k1].astype(BF16), preferred_element_type=F32)
        x_re, x_im = _cplx_fwd(r, FFT_N2, cb)
        h = h_ref[0, k1]
        h_re, h_im = h[:, :cb], h[:, cb:]
        p_ref[0, 0, k1] = jnp.concatenate([x_re * h_re - x_im * h_im, x_re * h_im + x_im * h_re], axis=1).astype(BF16)
        return carry
    lax.fori_loop(0, FFT_N1, body, 0, unroll=FFT_UNROLL_N2_STAGE)


def _fft_fwd_filter_kernel(f_ref, den_ref, kron_ref, g_ref, h_ref, a_scr):
    cb = f_ref.shape[3]
    inv_den = 1.0 / den_ref[...]

    def load(rows):
        re = jnp.concatenate([f_ref[0, :, rows, :], f_ref[1, :, rows, :]], axis=0) * inv_den
        return jnp.concatenate([re, jnp.zeros_like(re)], axis=-1)
    _fft_stage1(load, kron_ref, a_scr, cb)

    def body(k1, carry):
        r = jnp.dot(g_ref[k1], a_scr[k1].astype(BF16), preferred_element_type=F32)
        x_re, x_im = _cplx_fwd(r, FFT_N2, cb)
        h_ref[0, k1] = jnp.concatenate([x_re, x_im], axis=1)
        return carry
    lax.fori_loop(0, FFT_N1, body, 0, unroll=FFT_UNROLL_N2_STAGE)


def _fft_inv_kernel(p_ref, x0_ref, w_ref, bias_ref, gi_ref, kron_ref, o_ref, b_scr):
    cb = o_ref.shape[4]

    def body3(k1, carry):
        r = jnp.dot(gi_ref[k1], p_ref[0, 0, k1], preferred_element_type=F32)
        b_re, b_im = _cplx_inv(r, FFT_N2, cb)
        b_scr[k1] = jnp.concatenate([b_re, b_im], axis=1)
        return carry
    lax.fori_loop(0, FFT_N1, body3, 0, unroll=FFT_UNROLL_N2_STAGE)

    bias = bias_ref[...].reshape(1, 1, cb)

    def body4(jt, carry):
        rows = pl.ds(pl.multiple_of(jt * SUBLANES, SUBLANES), SUBLANES)
        d = b_scr[:, rows, :].reshape(FFT_N1 * SUBLANES, 2 * cb).astype(BF16)
        r = jnp.dot(kron_ref[...], d, preferred_element_type=F32)
        y_re, y_im = _cplx_inv(r, FFT_N1_NZ * SUBLANES, cb)
        for which, y in enumerate((y_re, y_im)):
            y3 = y.reshape(FFT_N1_NZ, SUBLANES, cb)
            o_ref[0, which, :, rows, :] = x0_ref[0, which, :, rows, :] * (y3 + w_ref[0, which, :, rows, :] * bias)
        return carry
    lax.fori_loop(0, FFT_N2 // SUBLANES, body4, 0, unroll=FFT_UNROLL_N1_STAGE)


def hyena_spectrum(fu, den, *, cb=128):
    _, g_fwd, _, _, kron_full = _fft_constants()
    W = fu.shape[2]
    ncb = W // cb
    f4 = fu.reshape(2, FFT_N1_NZ, FFT_N2, W)
    const = lambda a: pl.BlockSpec(a.shape, lambda c: (0,) * a.ndim, pipeline_mode=pl.Buffered(1))
    return pl.pallas_call(
        _fft_fwd_filter_kernel,
        grid=(ncb,),
        in_specs=[
            pl.BlockSpec((2, FFT_N1_NZ, FFT_N2, cb), lambda c: (0, 0, 0, c)),
            pl.BlockSpec((1, cb), lambda c: (0, c)),
            const(kron_full), const(g_fwd),
        ],
        out_specs=pl.BlockSpec((1, FFT_N1, FFT_N2, 2 * cb), lambda c: (c, 0, 0, 0)),
        out_shape=jax.ShapeDtypeStruct((ncb, FFT_N1, FFT_N2, 2 * cb), F32),
        scratch_shapes=[pltpu.VMEM((FFT_N1, FFT_N2, 2 * cb), F32)],
        compiler_params=pltpu.CompilerParams(
            dimension_semantics=("parallel",), vmem_limit_bytes=VMEM_LIMIT_BYTES),
        name="hyena_spectrum",
    )(f4, den, kron_full, g_fwd)


def hyena_conv(x0c, hw, spec, bias, *, cb=128):
    kron_fwd, g_fwd, g_inv, kron_inv, _ = _fft_constants()
    B, L, W = hw.shape
    assert L == FFT_L and B % 2 == 0
    ncb = W // cb
    pair_shape = (B // 2, 2, FFT_N1_NZ, FFT_N2, W)
    x5 = x0c.reshape(pair_shape)
    w5 = hw.reshape(pair_shape)
    const = lambda a: pl.BlockSpec(a.shape, lambda c, p: (0,) * a.ndim, pipeline_mode=pl.Buffered(1))
    pair = pl.BlockSpec((1, 2, FFT_N1_NZ, FFT_N2, cb), lambda c, p: (p, 0, 0, 0, c))
    pspec = pl.BlockSpec((1, 1, FFT_N1, FFT_N2, 2 * cb), lambda c, p: (c, p, 0, 0, 0))
    prod = pl.pallas_call(
        _fft_fwd_data_kernel,
        grid=(ncb, B // 2),
        in_specs=[pair, pl.BlockSpec((1, FFT_N1, FFT_N2, 2 * cb), lambda c, p: (c, 0, 0, 0)),
                  const(kron_fwd), const(g_fwd)],
        out_specs=pspec,
        out_shape=jax.ShapeDtypeStruct((ncb, B // 2, FFT_N1, FFT_N2, 2 * cb), BF16),
        scratch_shapes=[pltpu.VMEM((FFT_N1, FFT_N2, 2 * cb), F32)],
        compiler_params=pltpu.CompilerParams(
            dimension_semantics=("parallel", "parallel"), vmem_limit_bytes=VMEM_LIMIT_BYTES),
        name="hyena_fft_fwd",
    )(w5, spec, kron_fwd, g_fwd)
    out = pl.pallas_call(
        _fft_inv_kernel,
        grid=(ncb, B // 2),
        in_specs=[pspec, pair, pair, pl.BlockSpec((1, cb), lambda c, p: (0, c)), const(g_inv), const(kron_inv)],
        out_specs=pair,
        out_shape=jax.ShapeDtypeStruct(pair_shape, F32),
        scratch_shapes=[pltpu.VMEM((FFT_N1, FFT_N2, 2 * cb), F32)],
        compiler_params=pltpu.CompilerParams(
            dimension_semantics=("parallel", "parallel"), vmem_limit_bytes=VMEM_LIMIT_BYTES),
        name="hyena_fft_inv",
    )(prod, x5, w5, bias.reshape(1, W), g_inv, kron_inv)
    return out.reshape(B, L, W)


@functools.lru_cache(maxsize=None)
def _dense_dft_constants(L):
    n_full = 2 * L
    k = np.arange(n_full)[:, None]
    n = np.arange(n_full)[None, :]
    ang = 2.0 * np.pi * ((k * n) % n_full) / n_full
    fwd = np.concatenate([np.cos(ang), np.sin(ang)], axis=0)
    inv = np.concatenate([np.cos(ang[:L]), np.sin(ang[:L])], axis=0) / n_full
    return jnp.asarray(fwd, dtype=F32), jnp.asarray(inv, dtype=F32)


def _hyena_short_kernel(x0_ref, w_ref, f_ref, den_ref, bias_ref, ff_ref, fi_ref, o_ref, h_scr):
    L, cb = o_ref.shape[2], o_ref.shape[3]

    @pl.when(pl.program_id(1) == 0)
    def _():
        filt = jnp.concatenate([f_ref[0], f_ref[1]], axis=0) / den_ref[...]
        r = _dot_hi(ff_ref[...], filt)
        h_scr[0] = r[:2 * L]
        h_scr[1] = -r[2 * L:]

    h_re, h_im = h_scr[0], h_scr[1]
    z = jnp.concatenate([w_ref[0, 0], w_ref[0, 1]], axis=1).astype(BF16)
    x_re, x_im = _cplx_fwd(jnp.dot(ff_ref[:, :L].astype(BF16), z, preferred_element_type=F32), 2 * L, cb)
    p = jnp.concatenate([x_re * h_re - x_im * h_im, x_re * h_im + x_im * h_re], axis=1).astype(BF16)
    y_re, y_im = _cplx_inv(jnp.dot(fi_ref[...].astype(BF16), p, preferred_element_type=F32), L, cb)
    for which, y in enumerate((y_re, y_im)):
        o_ref[0, which] = x0_ref[0, which] * (y + w_ref[0, which] * bias_ref[...])


def hyena_conv_short(x0c, hw, fu, den, bias, *, cb=128):
    B, L, W = hw.shape
    fwd, inv = _dense_dft_constants(L)
    pair_shape = (B // 2, 2, L, W)
    pair = pl.BlockSpec((1, 2, L, cb), lambda c, p: (p, 0, 0, c))
    vec = pl.BlockSpec((1, cb), lambda c, p: (0, c))
    const = lambda a: pl.BlockSpec(a.shape, lambda c, p: (0, 0))
    out = pl.pallas_call(
        _hyena_short_kernel,
        grid=(W // cb, B // 2),
        in_specs=[pair, pair, pl.BlockSpec((2, L, cb), lambda c, p: (0, 0, c)), vec, vec, const(fwd), const(inv)],
        out_specs=pair,
        out_shape=jax.ShapeDtypeStruct(pair_shape, F32),
        scratch_shapes=[pltpu.VMEM((2, 2 * L, cb), F32)],
        compiler_params=pltpu.CompilerParams(
            dimension_semantics=("parallel", "arbitrary"), vmem_limit_bytes=VMEM_LIMIT_BYTES),
        name="hyena_conv_short",
    )(x0c.reshape(pair_shape), hw.reshape(pair_shape), fu, den, bias.reshape(1, W), fwd, inv)
    return out.reshape(B, L, W)


GRID_W = 64
SC_COLS = 3 * BRANCH_W
HG_COLS = 3 * HG_KEY + 2 * HG_HEADS * HG_DV
SSD_XBC = SSD_INNER + 2 * SSD_GROUPS * SSD_STATE
assert SSD_XBC == SSD_XBC_COLS and HY_COLS == 3 * HY_WIDTH


def _sincos_1d(pos, dim):
    omega = 1.0 / (10000.0 ** (jnp.arange(dim // 2, dtype=F32) / (dim // 2)))
    ang = pos.astype(F32)[:, None] * omega[None]
    return jnp.concatenate([jnp.sin(ang), jnp.cos(ang)], -1)


def _grid_pos_embed(rows, dim):
    row = jnp.repeat(jnp.arange(rows), GRID_W)
    col = jnp.tile(jnp.arange(GRID_W), rows)
    return jnp.concatenate([_sincos_1d(row, dim // 2), _sincos_1d(col, dim // 2)], -1)


def kernel(x, c, ctx, c_ctx, w_ada, b_ada, w_in, hy_conv_w, hy_conv_b, hy_w1, hy_b1, hy_freq1,
           hy_w2, hy_b2, hy_freq2, hy_w3, hy_decay, hy_bias, sc_conv_w, hg_lb_logits, hg_norm_g,
           ssd_conv_w, ssd_conv_b, ssd_a_log, ssd_dt_bias, ssd_d, ssd_norm_g, w_gate, b_gate, w_br,
           w_o, ln1_g, ln1_b, ln2_g, ln2_b, w_router, b_router, w_e1, w_e3, w_e2):
    B, L, D = x.shape
    LC = ctx.shape[1]
    depth = w_in.shape[0]
    pos_table = _grid_pos_embed(L // GRID_W, D).astype(x.dtype)
    lat = x.reshape(B * L, D)
    cx = ctx.reshape(B * LC, D)

    n_vec = -(-(B + 1) // SUBLANES) * SUBLANES
    cvecs = jnp.concatenate([c, c_ctx[None], jnp.zeros((n_vec - B - 1, D), c.dtype)], axis=0)
    mods = ada_mod(cvecs, w_ada, b_ada).reshape(depth, n_vec, 6, D)
    lb_logits = hg_lb_logits.reshape(2 * depth, HG_KEY)

    out_widths = (BRANCH_W, BRANCH_W, BRANCH_W, SSD_XBC, HG_COLS, SSD_INNER + LANES)
    z_state = lambda *shape: jnp.zeros(shape, F32)
    w_e1_b, w_e3_b, w_e2_b = w_e1.astype(BF16), w_e3.astype(BF16), w_e2.astype(BF16)
    c_hy, c_sc, c_hg = HY_COLS, HY_COLS + SC_COLS, HY_COLS + SC_COLS + HG_COLS
    c_z, c_xbc = c_hg + SSD_INNER, c_hg + SSD_INNER + SSD_XBC

    for l in range(depth):
        ctx_out = l < depth - 1
        mod_lat, mod_ctx = mods[l, :B], mods[l, B:B + 1]
        wl = w_in[l]
        w_in_b = jnp.concatenate([
            wl[:, :c_hy], wl[:, c_hy + BRANCH_W:c_sc], wl[:, c_z:c_xbc],
            wl[:, c_hy:c_hy + BRANCH_W], wl[:, c_sc:c_hg], wl[:, c_hg:c_z],
            jnp.pad(wl[:, c_xbc:], ((0, 0), (0, LANES - 2 * SSD_HEADS))),
        ], axis=1).astype(BF16)
        conv_par = (hy_conv_w[l], hy_conv_b[l], sc_conv_w[l], ssd_conv_w[l], ssd_conv_b[l])
        ssd_par = jnp.zeros((2 * SUBLANES, LANES), F32)
        ssd_par = ssd_par.at[:SUBLANES, :2 * SSD_HEADS].set(jnp.broadcast_to(ssd_a_log[l].reshape(1, -1), (SUBLANES, 2 * SSD_HEADS)))
        ssd_par = ssd_par.at[SUBLANES:, :2 * SSD_HEADS].set(jnp.broadcast_to(ssd_dt_bias[l].reshape(1, -1), (SUBLANES, 2 * SSD_HEADS)))
        d_skip = jnp.repeat(ssd_d[l], SSD_HEADDIM)
        filt_args = (hy_w1[l], hy_b1[l], hy_freq1[l], hy_w2[l], hy_b2[l], hy_freq2[l], hy_w3[l], hy_decay[l])

        def mixers(tokens, mod, seg_len, hg_state, ssd_state, want_out, pos=None):
            n_seq = tokens.shape[0] // seg_len
            x0c, hw, sc_y, xbc, hg, zdt = in_proj(
                tokens, mod, w_in_b, conv_par, out_widths, seq_len=seg_len, rows_per_mod=mod_rows(mod, tokens),
                tm=min(512, seg_len), pos=pos)
            seq = lambda a: a.reshape(n_seq, seg_len, a.shape[1])
            hg_y, hg_f, hg_b = hgrn_scan(seq(hg), lb_logits, hg_norm_g[l], *hg_state, layer=l)
            ssd_y, ssd_f, ssd_b = ssd_scan(seq(zdt), seq(xbc), ssd_par, d_skip, ssd_norm_g[l], *ssd_state,
                                           z_blk=0, dt_blk=SSD_INNER // LANES)
            branches = None
            if want_out:
                fu, den = hy_filter(seg_len, *filt_args)
                if seg_len == FFT_L:
                    hy_y = hyena_conv(seq(x0c), seq(hw), hyena_spectrum(fu, den), hy_bias[l])
                else:
                    hy_y = hyena_conv_short(seq(x0c), seq(hw), fu, den, hy_bias[l])
                flat = lambda a: a.reshape(tokens.shape[0], a.shape[2])
                branches = [flat(hy_y), sc_y, flat(hg_y), flat(ssd_y)]
            return branches, (hg_f, hg_b), (ssd_f, ssd_b)

        def mod_rows(mod, tokens):
            return tokens.shape[0] // mod.shape[0]

        def finish(tokens, mod, branches, pos=None):
            rows = mod_rows(mod, tokens)
            t1 = merge_ln(tokens, mod, branches, w_gate[l].astype(BF16), b_gate[l], w_br[l].astype(BF16),
                          w_o[l].astype(BF16), ln1_g[l], ln1_b[l], rows_per_mod=rows, tm=512, pos=pos)
            return moe_ln(t1, mod, w_router, b_router, w_e1_b, w_e3_b, w_e2_b, ln2_g[l], ln2_b[l],
                          layer=l, rows_per_mod=rows, tm=1024)

        zero_hg = (z_state(B, HG_KEY, HG_KEY),) * 2
        zero_ssd = (z_state(B, SSD_GROUPS, LANES, LANES),) * 2
        br_ctx, hg_state, ssd_state = mixers(cx, mod_ctx, LC, zero_hg, zero_ssd, ctx_out)
        pos = pos_table if l == 0 else None
        br_lat, _, _ = mixers(lat, mod_lat, L, hg_state, ssd_state, True, pos=pos)
        lat = finish(lat, mod_lat, br_lat, pos=pos)
        if ctx_out:
            cx = finish(cx, mod_ctx, br_ctx)
    return lat.reshape(B, L, D)
```

```python
import functools
import math

import jax
import jax.numpy as jnp
import numpy as np
from jax import lax
from jax.experimental import pallas as pl
from jax.experimental.pallas import tpu as pltpu

F32 = jnp.float32
BF16 = jnp.bfloat16

N_EXPERTS = 16
N_EXPERT_GROUPS = 4
GROUP_SIZE = N_EXPERTS // N_EXPERT_GROUPS
DEPTH = 2
DEEPNORM_ALPHA = (2 * DEPTH) ** 0.25
LN_EPS = 1e-5
RMS_EPS = 1e-6

VMEM_LIMIT_BYTES = 52 * 1024 * 1024


def _layernorm_rows(z, g, b):
    mu = jnp.mean(z, axis=-1, keepdims=True)
    zc = z - mu
    var = jnp.mean(zc * zc, axis=-1, keepdims=True)
    return zc * lax.rsqrt(var + LN_EPS) * g + b


def _route_t(logits_t, bias_col):
    z = logits_t - jnp.max(logits_t, axis=0, keepdims=True)
    ex = jnp.exp(z)
    scores = ex / jnp.sum(ex, axis=0, keepdims=True)
    sel = scores + bias_col
    eidx = lax.broadcasted_iota(jnp.int32, sel.shape, 0)
    neg = jnp.float32(-jnp.inf)

    def first_argmax(m):
        top = jnp.max(m, axis=0, keepdims=True)
        return top, jnp.min(jnp.where(m == top, eidx, N_EXPERTS), axis=0, keepdims=True)

    best_score = None
    best_grp = None
    for g in range(N_EXPERT_GROUPS):
        m = jnp.where((eidx >= g * GROUP_SIZE) & (eidx < (g + 1) * GROUP_SIZE), sel, neg)
        t1, i1 = first_argmax(m)
        t2 = jnp.max(jnp.where(eidx == i1, neg, m), axis=0, keepdims=True)
        s = t1 + t2
        if g == 0:
            best_score, best_grp = s, jnp.zeros_like(i1)
        else:
            better = s > best_score
            best_score = jnp.where(better, s, best_score)
            best_grp = jnp.where(better, g, best_grp)
    lo = best_grp * GROUP_SIZE
    masked = jnp.where((eidx >= lo) & (eidx < lo + GROUP_SIZE), sel, neg)
    _, ia = first_argmax(masked)
    _, ib = first_argmax(jnp.where(eidx == ia, neg, masked))
    w = jnp.where((eidx == ia) | (eidx == ib), scores, 0.0)
    return w / jnp.sum(w, axis=0, keepdims=True), best_grp


MOE_CHUNK = 64
MOE_MAX_BLOCK_CHUNKS = 8
MOE_MAX_TILE = 1024
MOE_STEP_EXPERTS = 2
GATE_TERMS = 3
POS_LANE = GATE_TERMS * N_EXPERTS


def _moe_ln_kernel(x_ref, mod_ref, wr_ref, br_ref, w1_ref, w3_ref, w2_ref, lng_ref, lnb_ref,
                   o_ref, pt_scr, hc_scr, gc_scr, yc_scr, seg_smem):
    step = pl.program_id(1)
    TM, R = pt_scr.shape
    D = x_ref.shape[1]
    G = N_EXPERT_GROUPS

    @pl.when(step == 0)
    def _():
        h = _modulate(x_ref[...], mod_ref, MOD_SHIFT2, MOD_SCALE2)
        h_hi = h.astype(BF16)
        h_lo = (h - h_hi.astype(F32)).astype(BF16)
        r_hi = jnp.dot(h_hi, wr_ref[...], preferred_element_type=F32)
        r_lo = jnp.dot(h_lo, wr_ref[:, :LANES], preferred_element_type=F32)
        logits_t = (r_hi[:, :LANES] + r_hi[:, LANES:] + r_lo).T[:N_EXPERTS]
        gate_t, grp = _route_t(logits_t, br_ref[:, 0:1])

        gidx = lax.broadcasted_iota(jnp.int32, (SUBLANES, TM), 0)
        tok = lax.broadcasted_iota(jnp.int32, (SUBLANES, TM), 1)
        member = jnp.where(gidx == grp, 1.0, 0.0)
        incl = member
        shift = 1
        while shift < TM:
            incl = incl + jnp.where(tok >= shift, pltpu.roll(incl, shift, 1), 0.0)
            shift *= 2
        rank = incl - member
        offset = jnp.int32(0)
        pos_row = jnp.zeros((1, TM), F32)
        for g in range(G):
            n_chunks = (jnp.sum(member[g:g + 1, :]).astype(jnp.int32) + (MOE_CHUNK - 1)) // MOE_CHUNK
            seg_smem[g] = n_chunks
            seg_smem[G + g] = offset
            pos_row = pos_row + member[g:g + 1, :] * (rank[g:g + 1, :] + offset.astype(F32))
            offset = offset + n_chunks * MOE_CHUNK

        terms, rest = [], gate_t
        for _ in range(GATE_TERMS):
            part = rest.astype(BF16).astype(F32)
            terms.append(part)
            rest = rest - part
        pad = jnp.zeros((LANES - POS_LANE - 1, TM), F32)
        side = jnp.concatenate(terms + [pos_row, pad], axis=0).T
        pos_col = side[:, POS_LANE:POS_LANE + 1].astype(jnp.int32)
        pt_scr[...] = jnp.where(lax.broadcasted_iota(jnp.int32, (TM, R), 1) == pos_col, 1.0, 0.0).astype(BF16)
        p = jnp.where(lax.broadcasted_iota(jnp.int32, (R, TM), 0) == pos_row.astype(jnp.int32), 1.0, 0.0).astype(BF16)
        gathered = jnp.dot(p, jnp.concatenate([h_hi, side.astype(BF16)], axis=1), preferred_element_type=F32)
        hc_scr[...] = gathered[:, :D].astype(BF16)
        gc_scr[...] = gathered[:, D:]
        yc_scr[...] = jnp.zeros_like(yc_scr)

    g = step // (GROUP_SIZE // MOE_STEP_EXPERTS)
    n_chunks = seg_smem[g]
    base = seg_smem[G + g]

    def expert_rows(start, size):
        rows = pl.ds(pl.multiple_of(start, MOE_CHUNK), size)
        hc = hc_scr[rows, :]
        gc = gc_scr[rows, :]
        lane = lax.broadcasted_iota(jnp.int32, gc.shape, 1)
        acc = None
        for k in range(MOE_STEP_EXPERTS):
            e = step * MOE_STEP_EXPERTS + k
            a = jnp.dot(hc, w1_ref[k], preferred_element_type=F32)
            b = jnp.dot(hc, w3_ref[k], preferred_element_type=F32)
            mid = (a * jax.nn.sigmoid(a) * b).astype(BF16)
            y = jnp.dot(mid, w2_ref[k], preferred_element_type=F32)
            gcol = jnp.sum(jnp.where((lane % N_EXPERTS == e) & (lane < POS_LANE), gc, 0.0), axis=-1, keepdims=True)
            acc = gcol * y if acc is None else acc + gcol * y
        yc_scr[rows, :] += acc

    full = MOE_MAX_BLOCK_CHUNKS * MOE_CHUNK
    for _ in range(max(TM // full - 1, 0)):
        big = n_chunks > MOE_MAX_BLOCK_CHUNKS

        @pl.when(big)
        def _(base=base):
            expert_rows(base, full)

        base = base + jnp.where(big, full, 0)
        n_chunks = n_chunks - jnp.where(big, MOE_MAX_BLOCK_CHUNKS, 0)

    for c in range(1, min(MOE_MAX_BLOCK_CHUNKS, TM // MOE_CHUNK) + 1):
        @pl.when(n_chunks == c)
        def _(base=base, c=c):
            expert_rows(base, c * MOE_CHUNK)

    @pl.when(step == N_EXPERTS // MOE_STEP_EXPERTS - 1)
    def _():
        moe = jnp.dot(pt_scr[...], yc_scr[...].astype(BF16), preferred_element_type=F32)
        z = DEEPNORM_ALPHA * x_ref[...] + mod_ref[0, MOD_GATE2:MOD_GATE2 + 1, :] * moe
        o_ref[...] = _layernorm_rows(z, lng_ref[...], lnb_ref[...])


def moe_ln(x, mod, w_router, b_router, w1, w3, w2, ln_g, ln_b, *, layer, rows_per_mod, tm):
    T, D = x.shape
    _, E, _, FF = w1.shape
    assert T % tm == 0 and rows_per_mod % tm == 0 and tm <= MOE_MAX_TILE and tm % MOE_CHUNK == 0
    tiles_per_mod = rows_per_mod // tm
    sorted_rows = tm + N_EXPERT_GROUPS * MOE_CHUNK
    wr_hi = w_router.astype(BF16)
    wr_lo = (w_router - wr_hi.astype(F32)).astype(BF16)
    lane_pad = lambda w: jnp.pad(w, ((0, 0), (0, LANES - E)))
    wr_split = jnp.concatenate([lane_pad(wr_hi), lane_pad(wr_lo)], axis=1)
    br_col = jnp.broadcast_to(b_router.reshape(E, 1), (E, LANES))
    return pl.pallas_call(
        _moe_ln_kernel,
        grid=(T // tm, E // MOE_STEP_EXPERTS),
        in_specs=[
            pl.BlockSpec((tm, D), lambda i, e: (i, 0)),
            pl.BlockSpec((1, 6, D), lambda i, e: (i // tiles_per_mod, 0, 0)),
            pl.BlockSpec((D, 2 * LANES), lambda i, e: (0, 0)),
            pl.BlockSpec((E, LANES), lambda i, e: (0, 0)),
            pl.BlockSpec((None, MOE_STEP_EXPERTS, D, FF), lambda i, e: (layer, e, 0, 0)),
            pl.BlockSpec((None, MOE_STEP_EXPERTS, D, FF), lambda i, e: (layer, e, 0, 0)),
            pl.BlockSpec((None, MOE_STEP_EXPERTS, FF, D), lambda i, e: (layer, e, 0, 0)),
            pl.BlockSpec((1, D), lambda i, e: (0, 0)),
            pl.BlockSpec((1, D), lambda i, e: (0, 0)),
        ],
        out_specs=pl.BlockSpec((tm, D), lambda i, e: (i, 0)),
        out_shape=jax.ShapeDtypeStruct((T, D), F32),
        scratch_shapes=[
            pltpu.VMEM((tm, sorted_rows), BF16),
            pltpu.VMEM((sorted_rows, D), BF16),
            pltpu.VMEM((sorted_rows, LANES), F32),
            pltpu.VMEM((sorted_rows, D), F32),
            pltpu.SMEM((2 * N_EXPERT_GROUPS,), jnp.int32),
        ],
        compiler_params=pltpu.CompilerParams(
            dimension_semantics=("parallel", "arbitrary"), vmem_limit_bytes=VMEM_LIMIT_BYTES),
        name="moe_ln",
    )(x, mod, wr_split, br_col, w1, w3, w2, ln_g.reshape(1, D), ln_b.reshape(1, D))


def _ada_kernel(c_ref, w_ref, b_ref, o_ref):
    c = c_ref[...]
    s = c * jax.nn.sigmoid(c)
    o_ref[0] = jnp.dot(s, w_ref[0], precision=lax.Precision.HIGHEST, preferred_element_type=F32) + b_ref[0]


def ada_mod(cvecs, w_ada, b_ada, *, tn=1536):
    R, D = cvecs.shape
    depth, _, N = w_ada.shape
    assert N % tn == 0
    return pl.pallas_call(
        _ada_kernel,
        grid=(depth, N // tn),
        in_specs=[
            pl.BlockSpec((R, D), lambda l, j: (0, 0)),
            pl.BlockSpec((1, D, tn), lambda l, j: (l, 0, j)),
            pl.BlockSpec((1, 1, tn), lambda l, j: (l, 0, j)),
        ],
        out_specs=pl.BlockSpec((1, R, tn), lambda l, j: (l, 0, j)),
        out_shape=jax.ShapeDtypeStruct((depth, R, N), F32),
        compiler_params=pltpu.CompilerParams(
            dimension_semantics=("parallel", "parallel"), vmem_limit_bytes=VMEM_LIMIT_BYTES),
        name="ada_mod",
    )(cvecs, w_ada, b_ada.reshape(depth, 1, N))


MOD_SHIFT1, MOD_SCALE1, MOD_GATE1, MOD_SHIFT2, MOD_SCALE2, MOD_GATE2 = range(6)


def _modulate(x, mod_ref, shift_row, scale_row):
    return x * (1.0 + mod_ref[0, scale_row:scale_row + 1, :]) + mod_ref[0, shift_row:shift_row + 1, :]


HALO = 8
BRANCH_W = 256
HY_COLS = 3 * BRANCH_W
SSD_XBC_COLS = 512
IN_CONV_COLS = HY_COLS + 2 * BRANCH_W + SSD_XBC_COLS


def _conv3(u, w_ref, tm):
    return (u[HALO - 1:HALO - 1 + tm] * w_ref[0:1, :] + u[HALO:HALO + tm] * w_ref[1:2, :]
            + u[HALO + 1:HALO + 1 + tm] * w_ref[2:3, :])


def _in_proj_kernel(x_ref, xp_ref, xn_ref, mod_ref, w_ref, hcw_ref, hcb_ref, scw_ref, xcw_ref, xcb_ref, *rest,
                    has_pos, tiles_per_seq):
    if has_pos:
        pos_ref, pp_ref, pn_ref = rest[:3]
        rest = rest[3:]
    x0c_ref, hw_ref, scy_ref, xbc_ref, hg_ref, zdt_ref = rest
    tm = x_ref.shape[0]
    t = pl.program_id(0) % tiles_per_seq
    keep_prev = jnp.where(t > 0, 1.0, 0.0)
    keep_next = jnp.where(t < tiles_per_seq - 1, 1.0, 0.0)

    def tokens(main_ref, prev_ref, next_ref):
        return main_ref[...], prev_ref[...], next_ref[...]

    xm, xp, xn = tokens(x_ref, xp_ref, xn_ref)
    if has_pos:
        pm, pp, pn = tokens(pos_ref, pp_ref, pn_ref)
        xm, xp, xn = xm + pm, xp + pp, xn + pn
    mod = lambda v: _modulate(v, mod_ref, MOD_SHIFT1, MOD_SCALE1)
    h_main = mod(xm).astype(BF16)
    h_all = jnp.concatenate([mod(xp) * keep_prev, mod(xm), mod(xn) * keep_next], axis=0).astype(BF16)
    uc = jnp.dot(h_all, w_ref[:, :IN_CONV_COLS], preferred_element_type=F32)
    ur = jnp.dot(h_main, w_ref[:, IN_CONV_COLS:], preferred_element_type=F32)

    hy = _conv3(uc[:, :HY_COLS], hcw_ref, tm) + hcb_ref[...]
    x0c_ref[...] = hy[:, :BRANCH_W]
    hw_ref[...] = hy[:, BRANCH_W:2 * BRANCH_W] * hy[:, 2 * BRANCH_W:]
    cg_xs = uc[:, HY_COLS:HY_COLS + BRANCH_W] * uc[:, HY_COLS + BRANCH_W:HY_COLS + 2 * BRANCH_W]
    scy_ref[...] = ur[:, :BRANCH_W] * _conv3(cg_xs, scw_ref, tm)
    xa = _conv3(uc[:, HY_COLS + 2 * BRANCH_W:], xcw_ref, tm) + xcb_ref[...]
    xbc_ref[...] = xa * jax.nn.sigmoid(xa)
    n_hg = hg_ref.shape[1]
    hg_ref[...] = ur[:, BRANCH_W:BRANCH_W + n_hg]
    zdt_ref[...] = ur[:, BRANCH_W + n_hg:]


def in_proj(x, mod, w, conv, out_widths, *, seq_len, rows_per_mod, tm, pos=None):
    T, D = x.shape
    assert T % tm == 0 and seq_len % tm == 0 and rows_per_mod % tm == 0 and tm % HALO == 0
    tiles_per_seq = seq_len // tm
    tiles_per_mod = rows_per_mod // tm
    hb = tm // HALO
    last_blk = T // HALO - 1
    hy_w, hy_b, sc_w, xbc_w, xbc_b = conv
    row = lambda v: v.reshape(1, -1)
    const = lambda a: pl.BlockSpec(a.shape, lambda i: (0,) * a.ndim)
    consts = [hy_w, row(hy_b), sc_w, xbc_w, row(xbc_b)]
    in_specs = [
        pl.BlockSpec((tm, D), lambda i: (i, 0)),
        pl.BlockSpec((HALO, D), lambda i: (jnp.maximum(i * hb - 1, 0), 0)),
        pl.BlockSpec((HALO, D), lambda i: (jnp.minimum((i + 1) * hb, last_blk), 0)),
        pl.BlockSpec((1, 6, D), lambda i: (i // tiles_per_mod, 0, 0)),
        pl.BlockSpec(w.shape, lambda i: (0, 0), pipeline_mode=pl.Buffered(1)),
    ] + [const(a) for a in consts]
    args = [x, x, x, mod, w] + consts
    if pos is not None:
        assert pos.shape[0] == seq_len
        last_pos = seq_len // HALO - 1
        in_specs += [
            pl.BlockSpec((tm, D), lambda i: (i % tiles_per_seq, 0)),
            pl.BlockSpec((HALO, D), lambda i: (jnp.maximum((i % tiles_per_seq) * hb - 1, 0), 0)),
            pl.BlockSpec((HALO, D), lambda i: (jnp.minimum((i % tiles_per_seq + 1) * hb, last_pos), 0)),
        ]
        args += [pos, pos, pos]
    return pl.pallas_call(
        functools.partial(_in_proj_kernel, has_pos=pos is not None, tiles_per_seq=tiles_per_seq),
        grid=(T // tm,),
        in_specs=in_specs,
        out_specs=[pl.BlockSpec((tm, n), lambda i: (i, 0)) for n in out_widths],
        out_shape=[jax.ShapeDtypeStruct((T, n), F32) for n in out_widths],
        compiler_params=pltpu.CompilerParams(
            dimension_semantics=("parallel",), vmem_limit_bytes=VMEM_LIMIT_BYTES),
        name="in_proj",
    )(*args)


def _pos_spec(pos, rows_per_mod, tm):
    assert pos.shape[0] == rows_per_mod
    tiles = rows_per_mod // tm
    return pl.BlockSpec((tm, pos.shape[1]), lambda i: (i % tiles, 0))


N_BRANCH = 4


def _merge_kernel(x_ref, mod_ref, hy_ref, sc_ref, hg_ref, ssd_ref, wg_ref, bg_ref, wbr_ref, wo_ref,
                  lng_ref, lnb_ref, *rest, has_pos):
    x = x_ref[...] + rest[0][...] if has_pos else x_ref[...]
    o_ref = rest[-1]
    D = x.shape[1]
    h = _modulate(x, mod_ref, MOD_SHIFT1, MOD_SCALE1).astype(BF16)
    y = None
    for k, br_ref in enumerate((hy_ref, sc_ref, hg_ref, ssd_ref)):
        gate = jax.nn.sigmoid(
            jnp.dot(h, wg_ref[:, k * D:(k + 1) * D], preferred_element_type=F32) + bg_ref[:, k * D:(k + 1) * D])
        term = gate * jnp.dot(br_ref[...].astype(BF16), wbr_ref[k], preferred_element_type=F32)
        y = term if y is None else y + term
    y = jnp.dot(y.astype(BF16), wo_ref[...], preferred_element_type=F32)
    z = DEEPNORM_ALPHA * x + mod_ref[0, MOD_GATE1:MOD_GATE1 + 1, :] * y
    o_ref[...] = _layernorm_rows(z, lng_ref[...], lnb_ref[...])


def merge_ln(x, mod, branches, w_gate, b_gate, w_br, w_o, ln_g, ln_b, *, rows_per_mod, tm, pos=None):
    T, D = x.shape
    BW = branches[0].shape[1]
    assert T % tm == 0 and rows_per_mod % tm == 0
    tiles_per_mod = rows_per_mod // tm
    const = lambda shape: pl.BlockSpec(shape, lambda i: (0,) * len(shape), pipeline_mode=pl.Buffered(1))
    extra = [] if pos is None else [pos]
    return pl.pallas_call(
        functools.partial(_merge_kernel, has_pos=pos is not None),
        grid=(T // tm,),
        in_specs=[
            pl.BlockSpec((tm, D), lambda i: (i, 0)),
            pl.BlockSpec((1, 6, D), lambda i: (i // tiles_per_mod, 0, 0)),
        ] + [pl.BlockSpec((tm, BW), lambda i: (i, 0))] * N_BRANCH + [
            const((D, N_BRANCH * D)), const((1, N_BRANCH * D)), const((N_BRANCH, BW, D)), const((D, D)),
            const((1, D)), const((1, D)),
        ] + [_pos_spec(p, rows_per_mod, tm) for p in extra],
        out_specs=pl.BlockSpec((tm, D), lambda i: (i, 0)),
        out_shape=jax.ShapeDtypeStruct((T, D), F32),
        compiler_params=pltpu.CompilerParams(
            dimension_semantics=("parallel",), vmem_limit_bytes=VMEM_LIMIT_BYTES),
        name="merge_ln",
    )(x, mod, *branches, w_gate, b_gate.reshape(1, -1), w_br, w_o, ln_g.reshape(1, D), ln_b.reshape(1, D), *extra)


SSD_HEADS = 4
SSD_HEADDIM = 64
SSD_STATE = 64
SSD_GROUPS = 2
SSD_INNER = SSD_HEADS * SSD_HEADDIM
LANES = 128
NEG_INF = float("-inf")


SCAN_BATCH = 4


def _scan_chunk_index(ph, j, n):
    return j + (1 - ph) * (n - 1 - 2 * j)


def _ssd_kernel(z_ref, xbc_ref, dt_ref, par_ref, dskip_ref, ng_ref, sf0_ref, sb0_ref,
                y_ref, sf_ref, sb_ref, yb_scr, st_scr, *, n_chunks):
    ph = pl.program_id(1)
    j = pl.program_id(2)
    is_fwd = ph == 1
    c = _scan_chunk_index(ph, j, n_chunks)
    NB, R = xbc_ref.shape[0], xbc_ref.shape[1]

    @pl.when((j == 0) & is_fwd)
    def _():
        st_scr[...] = sf0_ref[...]

    @pl.when((j == 0) & jnp.logical_not(is_fwd))
    def _():
        st_scr[...] = sb0_ref[...]

    row = lax.broadcasted_iota(jnp.int32, (R, R), 0)
    col = lax.broadcasted_iota(jnp.int32, (R, R), 1)
    mask = jnp.where(is_fwd, row - col, col - row) >= 0
    tri = jnp.where(mask, 1.0, 0.0).astype(BF16)
    head_of_lane = lax.broadcasted_iota(jnp.int32, (LANES, SSD_INNER), 1) // SSD_HEADDIM
    expand = jnp.where(lax.broadcasted_iota(jnp.int32, (LANES, SSD_INNER), 0) == head_of_lane, 1.0, 0.0).astype(BF16)
    ys = [_ssd_chunk(bi, is_fwd, mask, tri, expand, xbc_ref, dt_ref, par_ref, st_scr) for bi in range(NB)]

    @pl.when(jnp.logical_not(is_fwd))
    def _():
        for bi in range(NB):
            yb_scr[bi, c] = ys[bi]

    @pl.when(is_fwd)
    def _():
        for bi in range(NB):
            z = z_ref[bi]
            xs = xbc_ref[bi, :, :SSD_INNER]
            yt = (ys[bi] + yb_scr[bi, c] + xs * dskip_ref[...]) * (z * jax.nn.sigmoid(z))
            ms = jnp.mean(yt * yt, axis=-1, keepdims=True)
            y_ref[bi] = yt * lax.rsqrt(ms + RMS_EPS) * ng_ref[...]

    @pl.when((j == n_chunks - 1) & is_fwd)
    def _():
        sf_ref[...] = st_scr[...]

    @pl.when((j == n_chunks - 1) & jnp.logical_not(is_fwd))
    def _():
        sb_ref[...] = st_scr[...]


def _ssd_chunk(bi, is_fwd, mask, tri, expand, xbc_ref, dt_ref, par_ref, st_scr):
    R = xbc_ref.shape[1]
    dsel = lambda v: jnp.where(is_fwd, v, pltpu.roll(v, LANES - SSD_HEADS, 1))
    dt_raw = dsel(dt_ref[bi]) + dsel(par_ref[8:16, :])[0:1, :]
    dt = jnp.maximum(dt_raw, 0.0) + jnp.log(1.0 + jnp.exp(-jnp.abs(dt_raw)))
    a = -jnp.exp(dsel(par_ref[0:8, :])[0:1, :]) * dt
    cs = _split_dot(tri, a, terms=3)
    cs_t = cs.T
    total = jnp.sum(a, axis=0, keepdims=True)

    spread = lambda v: jnp.dot(v.astype(BF16), expand, preferred_element_type=F32)
    dt_bc = spread(dt)
    in_decay = spread(jnp.exp(cs))
    out_decay = spread(jnp.exp(total - cs))
    e_total = jnp.exp(total)

    xbc = xbc_ref[bi]
    xdt = xbc[:, :SSD_INNER] * dt_bc
    xdt_b = xdt.astype(BF16)
    xout_b = (xdt * out_decay).astype(BF16)
    bm = xbc[:, SSD_INNER:SSD_INNER + LANES]
    cm_b = xbc[:, SSD_INNER + LANES:SSD_INNER + 2 * LANES].astype(BF16)
    lane = lax.broadcasted_iota(jnp.int32, (R, LANES), 1)
    lo_half = lane < SSD_HEADDIM
    lo_half_st = lax.broadcasted_iota(jnp.int32, (LANES, LANES), 1) < SSD_HEADDIM
    y_groups = []
    for g in range(SSD_GROUPS):
        h0, h1 = 2 * g, 2 * g + 1
        cols = slice(g * LANES, (g + 1) * LANES)
        bm_g = jnp.where((lane >= g * SSD_STATE) & (lane < (g + 1) * SSD_STATE), bm, 0.0).astype(BF16)
        G = lax.dot_general(cm_b, bm_g, (((1,), (1,)), ((), ())), preferred_element_type=F32)
        st_old = st_scr[bi, g]
        y_off = in_decay[:, cols] * jnp.dot(cm_b, st_old.astype(BF16), preferred_element_type=F32)
        y_diag = []
        for h in (h0, h1):
            decay = jnp.exp(jnp.where(mask, cs[:, h:h + 1] - cs_t[h:h + 1, :], NEG_INF))
            y_diag.append(jnp.dot((G * decay).astype(BF16), xdt_b[:, cols], preferred_element_type=F32))
        y_groups.append(jnp.where(lo_half, y_diag[0], y_diag[1]) + y_off)
        upd = lax.dot_general(bm_g, xout_b[:, cols], (((0,), (0,)), ((), ())), preferred_element_type=F32)
        st_scr[bi, g] = jnp.where(lo_half_st, e_total[:, h0:h0 + 1], e_total[:, h1:h1 + 1]) * st_old + upd
    return jnp.concatenate(y_groups, axis=1)


def ssd_scan(ssd, xbc, par, d_skip, norm_g, s_f0, s_b0, *, z_blk, dt_blk, chunk=256):
    B, L, _ = ssd.shape
    R = min(chunk, L)
    assert L % R == 0 and B % SCAN_BATCH == 0
    n = L // R
    NB = SCAN_BATCH
    cidx = lambda ph, j: _scan_chunk_index(ph, j, n)
    st_spec = pl.BlockSpec((NB, SSD_GROUPS, LANES, LANES), lambda b, ph, j: (b, 0, 0, 0))
    st_shape = jax.ShapeDtypeStruct((B, SSD_GROUPS, LANES, LANES), F32)
    return pl.pallas_call(
        functools.partial(_ssd_kernel, n_chunks=n),
        grid=(B // NB, 2, n),
        in_specs=[
            pl.BlockSpec((NB, R, SSD_INNER), lambda b, ph, j: (b, cidx(ph, j), z_blk)),
            pl.BlockSpec((NB, R, xbc.shape[2]), lambda b, ph, j: (b, cidx(ph, j), 0)),
            pl.BlockSpec((NB, R, LANES), lambda b, ph, j: (b, cidx(ph, j), dt_blk)),
            pl.BlockSpec((16, LANES), lambda b, ph, j: (0, 0)),
            pl.BlockSpec((1, SSD_INNER), lambda b, ph, j: (0, 0)),
            pl.BlockSpec((1, SSD_INNER), lambda b, ph, j: (0, 0)),
            st_spec, st_spec,
        ],
        out_specs=[pl.BlockSpec((NB, R, SSD_INNER), lambda b, ph, j: (b, ph * j, 0)), st_spec, st_spec],
        out_shape=[jax.ShapeDtypeStruct((B, L, SSD_INNER), F32), st_shape, st_shape],
        scratch_shapes=[pltpu.VMEM((NB, n, R, SSD_INNER), F32), pltpu.VMEM((NB, SSD_GROUPS, LANES, LANES), F32)],
        compiler_params=pltpu.CompilerParams(
            dimension_semantics=("parallel", "arbitrary", "arbitrary"), vmem_limit_bytes=VMEM_LIMIT_BYTES),
        name="ssd_scan",
    )(ssd, xbc, ssd, par, d_skip.reshape(1, SSD_INNER), norm_g.reshape(1, SSD_INNER), s_f0, s_b0)


HG_HEADS = 4
HG_DK = 64
HG_DV = 64
HG_KEY = HG_HEADS * HG_DK
HG_SUB = 16
HG_CHUNK = 64
HG_SUBS_PER_CHUNK = HG_CHUNK // HG_SUB


def _split_dot(mask_b, x, terms=2):
    out = None
    for _ in range(terms):
        part = x.astype(BF16)
        prod = jnp.dot(mask_b, part, preferred_element_type=F32)
        out = prod if out is None else out + prod
        x = x - part.astype(F32)
    return out


def _hgrn_kernel(q_ref, f_ref, v_ref, g_ref, lbl_ref, ng_ref, sf0_ref, sb0_ref,
                 y_ref, sf_ref, sb_ref, ob_scr, st_scr, *, n_blocks, layer, depth):
    ph = pl.program_id(1)
    j = pl.program_id(2)
    is_fwd = ph == 1
    c = _scan_chunk_index(ph, j, n_blocks)
    NB, R = q_ref.shape[0], q_ref.shape[1]
    W = HG_KEY

    @pl.when((j == 0) & is_fwd)
    def _():
        st_scr[...] = sf0_ref[...]

    @pl.when((j == 0) & jnp.logical_not(is_fwd))
    def _():
        st_scr[...] = sb0_ref[...]

    def lower_bound(d):
        x = lbl_ref[d * depth:(d + 1) * depth, :]
        e = jnp.exp(x - jnp.max(x, axis=0, keepdims=True))
        p = e / jnp.sum(e, axis=0, keepdims=True)
        return jnp.sum(p[1:layer + 1, :], axis=0, keepdims=True) if layer > 0 else jnp.zeros((1, W), F32)

    lb = jnp.where(is_fwd, lower_bound(0), lower_bound(1))
    row = lax.broadcasted_iota(jnp.int32, (R, R), 0)
    col = lax.broadcasted_iota(jnp.int32, (R, R), 1)
    dirge = jnp.where(is_fwd, row - col, col - row) >= 0
    same_sub = (row // HG_SUB) == (col // HG_SUB)
    same_chunk = (row // HG_CHUNK) == (col // HG_CHUNK)
    one = lambda m: jnp.where(m, 1.0, 0.0).astype(BF16)
    dist = jnp.where(is_fwd, row // HG_SUB - col // HG_SUB, col // HG_SUB - row // HG_SUB)
    masks = dict(
        cum16=one(same_sub & dirge), tot16=one(same_sub), cum64=one(same_chunk & dirge), tot64=one(same_chunk),
        dist=[same_sub & dirge] + [same_chunk & (dist == d) for d in range(1, HG_SUBS_PER_CHUNK)])
    lane = lax.broadcasted_iota(jnp.int32, (R, W), 1)
    head_mask = [(lane // HG_DK) == h for h in range(HG_HEADS)]
    blocks = [_hgrn_block(bi, is_fwd, lb, masks, head_mask, q_ref, f_ref, v_ref) for bi in range(NB)]
    srow = lax.broadcasted_iota(jnp.int32, (W, W), 0)
    scol = lax.broadcasted_iota(jnp.int32, (W, W), 1)
    bd_mask = (srow // HG_DV) == (scol // HG_DK)

    def through_state(bi, reverse):
        o_intra, qd, ke, v_b, tot64 = blocks[bi]
        n_chunks = R // HG_CHUNK
        st = st_scr[bi]
        parts = [None] * n_chunks
        for i in (reversed(range(n_chunks)) if reverse else range(n_chunks)):
            rows = slice(i * HG_CHUNK, (i + 1) * HG_CHUNK)
            parts[i] = lax.dot_general(qd[rows], st.astype(BF16), (((1,), (1,)), ((), ())), preferred_element_type=F32)
            upd = lax.dot_general(v_b[rows], ke[rows], (((0,), (0,)), ((), ())), preferred_element_type=F32)
            st = jnp.exp(tot64[i * HG_CHUNK:i * HG_CHUNK + 1, :]) * st + jnp.where(bd_mask, upd, 0.0)
        st_scr[bi] = st
        return o_intra + jnp.concatenate(parts, axis=0)

    @pl.when(jnp.logical_not(is_fwd))
    def _():
        for bi in range(NB):
            ob_scr[bi, c] = through_state(bi, reverse=True)

    @pl.when(is_fwd)
    def _():
        for bi in range(NB):
            ot = through_state(bi, reverse=False) + ob_scr[bi, c]
            sq = ot * ot
            ms = jnp.zeros((R, W), F32)
            for h in range(HG_HEADS):
                s = jnp.sum(jnp.where(head_mask[h], sq, 0.0), axis=-1, keepdims=True) * (1.0 / HG_DV)
                ms = jnp.where(head_mask[h], s, ms)
            g = g_ref[bi]
            y_ref[bi] = ot * lax.rsqrt(ms + RMS_EPS) * ng_ref[...] * (g * jax.nn.sigmoid(g))

    @pl.when((j == n_blocks - 1) & is_fwd)
    def _():
        sf_ref[...] = st_scr[...]

    @pl.when((j == n_blocks - 1) & jnp.logical_not(is_fwd))
    def _():
        sb_ref[...] = st_scr[...]


def _hgrn_block(bi, is_fwd, lb, masks, head_mask, q_ref, f_ref, v_ref):
    R = q_ref.shape[1]
    W = HG_KEY
    sig = jax.nn.sigmoid(f_ref[bi])
    logf = jnp.log(lb + (1.0 - lb) * sig)
    k = (1.0 - lb) * (1.0 - sig)
    q = q_ref[bi] * (HG_DK ** -0.5)
    v_b = v_ref[bi].astype(BF16)
    logf_hi = logf.astype(BF16)
    logf_lo = (logf - logf_hi.astype(F32)).astype(BF16)
    msum = lambda m: (jnp.dot(m, logf_hi, preferred_element_type=F32) + jnp.dot(m, logf_lo, preferred_element_type=F32))
    b16 = msum(masks["cum16"])
    tot16 = msum(masks["tot16"])
    b64 = msum(masks["cum64"])
    tot64 = msum(masks["tot64"])

    shifted = lambda s: jnp.where(is_fwd, pltpu.roll(tot16, R - s, 0), pltpu.roll(tot16, s, 0))
    nx1 = shifted(HG_SUB)
    nx2 = nx1 + shifted(2 * HG_SUB)
    k_end16 = k * jnp.exp(tot16 - b16)
    keys = [(k * jnp.exp(-b16)).astype(BF16), k_end16.astype(BF16),
            (k_end16 * jnp.exp(nx1)).astype(BF16), (k_end16 * jnp.exp(nx2)).astype(BF16)]
    q16 = q * jnp.exp(b16)
    qm =jnp.concatenate([jnp.where(m, q16, 0.0) for m in head_mask], axis=0).astype(BF16)
    att = [jnp.zeros((R, R), F32)] * HG_HEADS
    for d in range(HG_SUBS_PER_CHUNK):
        sc = lax.dot_general(qm, keys[d], (((1,), (1,)), ((), ())), preferred_element_type=F32)
        att = [jnp.where(masks["dist"][d], sc[h * R:(h + 1) * R], att[h]) for h in range(HG_HEADS)]
    o_all = jnp.dot(jnp.concatenate(att, axis=0).astype(BF16), v_b, preferred_element_type=F32)
    o = jnp.zeros((R, W), F32)
    for h in range(HG_HEADS):
        o = jnp.where(head_mask[h], o_all[h * R:(h + 1) * R], o)

    return o, (q * jnp.exp(b64)).astype(BF16), (k * jnp.exp(tot64 - b64)).astype(BF16), v_b, tot64


def hgrn_scan(hg, lb_logits, norm_g, s_f0, s_b0, *, layer, block=256):
    B, L, _ = hg.shape
    W = HG_KEY
    R = min(block, L)
    assert L % R == 0 and R % HG_CHUNK == 0 and B % SCAN_BATCH == 0
    n = L // R
    NB = SCAN_BATCH
    depth = lb_logits.shape[0] // 2
    cidx = lambda ph, j: _scan_chunk_index(ph, j, n)
    col = lambda blk: pl.BlockSpec((NB, R, W), lambda b, ph, j: (b, cidx(ph, j), blk))
    st_spec = pl.BlockSpec((NB, W, W), lambda b, ph, j: (b, 0, 0))
    st_shape = jax.ShapeDtypeStruct((B, W, W), F32)
    return pl.pallas_call(
        functools.partial(_hgrn_kernel, n_blocks=n, layer=layer, depth=depth),
        grid=(B // NB, 2, n),
        in_specs=[
            col(0),
            pl.BlockSpec((NB, R, W), lambda b, ph, j: (b, cidx(ph, j), 2 - ph)),
            col(3), col(4),
            pl.BlockSpec(lb_logits.shape, lambda b, ph, j: (0, 0)),
            pl.BlockSpec((1, W), lambda b, ph, j: (0, 0)),
            st_spec, st_spec,
        ],
        out_specs=[pl.BlockSpec((NB, R, W), lambda b, ph, j: (b, ph * j, 0)), st_spec, st_spec],
        out_shape=[jax.ShapeDtypeStruct((B, L, W), F32), st_shape, st_shape],
        scratch_shapes=[pltpu.VMEM((NB, n, R, W), F32), pltpu.VMEM((NB, W, W), F32)],
        compiler_params=pltpu.CompilerParams(
            dimension_semantics=("parallel", "arbitrary", "arbitrary"), vmem_limit_bytes=VMEM_LIMIT_BYTES),
        name="hgrn_scan",
    )(hg, hg, hg, hg, lb_logits, jnp.tile(norm_g, HG_HEADS).reshape(1, W), s_f0, s_b0)


HY_WIDTH = 256
HY_BANDS = 16
HY_HIDDEN = 64
SUBLANES = 8


def _dot_hi(a, b):
    return jnp.dot(a, b, precision=lax.Precision.HIGHEST, preferred_element_type=F32)


def _hy_filter_kernel(bands_ref, w1t_ref, w1f_ref, b1_ref, fr1_ref, w2_ref, b2_ref, fr2_ref, w3a_ref, w3b_ref,
                      decay_ref, f_ref, den_ref, cosb_scr, sinb_scr, *, L):
    i = pl.program_id(0)
    TR = f_ref.shape[1]
    step = 2.0 * math.pi / L
    bands = bands_ref[...]

    @pl.when(i == 0)
    def _():
        r = lax.broadcasted_iota(jnp.int32, (TR, LANES), 0).astype(F32)
        ang = (step * r) * bands
        cosb_scr[...] = jnp.cos(ang)
        sinb_scr[...] = jnp.sin(ang)
        den_ref[...] = jnp.zeros_like(den_ref)

    one_row = lambda v: jnp.broadcast_to(v, (SUBLANES, LANES))
    base = one_row((step * (i * TR).astype(F32)) * bands)
    cos_a, sin_a = jnp.cos(base)[0:1], jnp.sin(base)[0:1]
    full = one_row((2.0 * math.pi) * bands)
    cos_p, sin_p = jnp.cos(full)[0:1], jnp.sin(full)[0:1]
    cos_b, sin_b = cosb_scr[...], sinb_scr[...]
    cos0 = cos_a * cos_b - sin_a * sin_b
    sin0 = sin_a * cos_b + cos_a * sin_b
    cos1 = cos_p * cos0 + sin_p * sin0
    sin1 = sin_p * cos0 - cos_p * sin0

    m = i * TR + lax.broadcasted_iota(jnp.int32, (TR, HY_WIDTH), 0)
    t0 = m.astype(F32) * (1.0 / (L - 1))
    t1 = jnp.where(m == 0, 0, L - m).astype(F32) * (1.0 / (L - 1))
    lane = lax.broadcasted_iota(jnp.int32, (TR, LANES), 1)
    t_both = jnp.where(lane < HY_HIDDEN, t0[:, :LANES], t1[:, :LANES])
    feats = jnp.concatenate([cos0, sin0, cos1, sin1], axis=1)
    pre = t_both * w1t_ref[...] + _dot_hi(feats, w1f_ref[...]) + b1_ref[...]
    h = jnp.sin(fr1_ref[...] * pre)
    h = jnp.sin(fr2_ref[...] * (_dot_hi(h, w2_ref[...]) + b2_ref[...]))
    decay = jnp.abs(decay_ref[...])
    filt0 = _dot_hi(h, w3a_ref[...]) * jnp.exp(-t0 * decay)
    filt1 = jnp.where(m == 0, 0.0, _dot_hi(h, w3b_ref[...]) * jnp.exp(-t1 * decay))
    f_ref[0] = filt0
    f_ref[1] = filt1
    den_ref[...] += jnp.sum(jnp.abs(filt0) + jnp.abs(filt1), axis=0, keepdims=True)


def hy_filter(L, w1, b1, fr1, w2, b2, fr2, w3, decay, *, tr=512):
    tr = min(tr, L)
    H = HY_HIDDEN
    bands = np.zeros((1, LANES), np.float32)
    bands[0, :HY_BANDS] = np.linspace(1e-4, HY_BANDS - 1, HY_BANDS, dtype=np.float32)
    pad_rows = lambda w: jnp.pad(w, ((0, LANES - HY_BANDS), (0, 0)))
    twice = lambda v: jnp.concatenate([v, v]).reshape(1, 2 * H)
    zeros = jnp.zeros((LANES, H), F32)
    w_cos, w_sin = pad_rows(w1[1:1 + HY_BANDS]), pad_rows(w1[1 + HY_BANDS:])
    w1_feat = jnp.concatenate([
        jnp.concatenate([w_cos, zeros], axis=1), jnp.concatenate([-w_sin, zeros], axis=1),
        jnp.concatenate([zeros, w_cos], axis=1), jnp.concatenate([zeros, -w_sin], axis=1)], axis=0)
    zh = jnp.zeros((H, H), F32)
    w2_both = jnp.concatenate([jnp.concatenate([w2, zh], axis=1), jnp.concatenate([zh, w2], axis=1)], axis=0)
    zw = jnp.zeros((H, HY_WIDTH), F32)
    w3_a = jnp.concatenate([w3[:, :HY_WIDTH], zw], axis=0)
    w3_b = jnp.concatenate([zw, w3[:, HY_WIDTH:]], axis=0)
    args = (jnp.asarray(bands), twice(w1[0]), w1_feat, twice(b1), twice(fr1), w2_both, twice(b2), twice(fr2),
            w3_a, w3_b, decay.reshape(1, -1))
    return pl.pallas_call(
        functools.partial(_hy_filter_kernel, L=L),
        grid=(L // tr,),
        in_specs=[pl.BlockSpec(a.shape, lambda i: (0, 0)) for a in args],
        out_specs=[pl.BlockSpec((2, tr, HY_WIDTH), lambda i: (0, i, 0)), pl.BlockSpec((1, HY_WIDTH), lambda i: (0, 0))],
        out_shape=[jax.ShapeDtypeStruct((2, L, HY_WIDTH), F32), jax.ShapeDtypeStruct((1, HY_WIDTH), F32)],
        scratch_shapes=[pltpu.VMEM((tr, LANES), F32), pltpu.VMEM((tr, LANES), F32)],
        compiler_params=pltpu.CompilerParams(
            dimension_semantics=("arbitrary",), vmem_limit_bytes=VMEM_LIMIT_BYTES),
        name="hy_filter",
    )(*args)


FFT_N1 = 64
FFT_N2 = 128
FFT_N = FFT_N1 * FFT_N2
FFT_L = FFT_N // 2
FFT_N1_NZ = FFT_N1 // 2
FFT_UNROLL_N1_STAGE = 2
FFT_UNROLL_N2_STAGE = 8


@functools.lru_cache(maxsize=None)
def _fft_constants():
    eye = np.eye(SUBLANES)
    k1 = np.arange(FFT_N1)[:, None]
    n1 = np.arange(FFT_N1_NZ)[None, :]
    ang1 = 2.0 * np.pi * ((k1 * n1) % FFT_N1) / FFT_N1
    kron_fwd = np.concatenate([np.kron(np.cos(ang1), eye), np.kron(np.sin(ang1), eye)], axis=0)
    kron_inv = np.concatenate([np.kron(np.cos(ang1).T, eye), np.kron(np.sin(ang1).T, eye)], axis=0) / FFT_N
    k = np.arange(FFT_N1)[:, None, None] + FFT_N1 * np.arange(FFT_N2)[None, :, None]
    n2 = np.arange(FFT_N2)[None, None, :]
    ang2 = 2.0 * np.pi * ((k * n2) % FFT_N) / FFT_N
    g_fwd = np.concatenate([np.cos(ang2), np.sin(ang2)], axis=1)
    g_inv = np.concatenate([np.cos(ang2).transpose(0, 2, 1), np.sin(ang2).transpose(0, 2, 1)], axis=1)
    n1_all = np.arange(FFT_N1)[None, :]
    ang1_all = 2.0 * np.pi * ((k1 * n1_all) % FFT_N1) / FFT_N1
    kron_full = np.concatenate([np.kron(np.cos(ang1_all), eye), np.kron(np.sin(ang1_all), eye)], axis=0)
    to_b = lambda a: jnp.asarray(a, dtype=F32).astype(BF16)
    return to_b(kron_fwd), to_b(g_fwd), to_b(g_inv), to_b(kron_inv), to_b(kron_full)


def _cplx_fwd(r, half_rows, cb):
    cr, sr = r[:half_rows], r[half_rows:]
    return cr[:, :cb] + sr[:, cb:], cr[:, cb:] - sr[:, :cb]


def _cplx_inv(r, half_rows, cb):
    cr, sr = r[:half_rows], r[half_rows:]
    return cr[:, :cb] - sr[:, cb:], cr[:, cb:] + sr[:, :cb]


def _fft_stage1(load_group, kron_ref, a_scr, cb):
    def body(jt, carry):
        rows = pl.ds(pl.multiple_of(jt * SUBLANES, SUBLANES), SUBLANES)
        d = load_group(rows).reshape(kron_ref.shape[1], 2 * cb).astype(BF16)
        r = jnp.dot(kron_ref[...], d, preferred_element_type=F32)
        a_re, a_im = _cplx_fwd(r, FFT_N1 * SUBLANES, cb)
        a_scr[:, rows, :] = jnp.concatenate([a_re, a_im], axis=1).reshape(FFT_N1, SUBLANES, 2 * cb)
        return carry
    lax.fori_loop(0, FFT_N2 // SUBLANES, body, 0, unroll=FFT_UNROLL_N1_STAGE)


def _fft_fwd_data_kernel(z_ref, h_ref, kron_ref, g_ref, p_ref, a_scr):
    cb = z_ref.shape[4]
    load = lambda rows: jnp.concatenate([z_ref[0, 0, :, rows, :], z_ref[0, 1, :, rows, :]], axis=-1)
    _fft_stage1(load, kron_ref, a_scr, cb)

    def body(k1, carry):
        r = jnp.dot(g_ref[k1], a_scr[k1].astype(BF16), preferred_element_type=F32)
        x_re, x_im = _cplx_fwd(r, FFT_N2, cb)
        h = h_ref[0, k1]
        h_re, h_im = h[:, :cb], h[:, cb:]
        p_ref[0, 0, k1] = jnp.concatenate([x_re * h_re - x_im * h_im, x_re * h_im + x_im * h_re], axis=1).astype(BF16)
        return carry
    lax.fori_loop(0, FFT_N1, body, 0, unroll=FFT_UNROLL_N2_STAGE)


def _fft_fwd_filter_kernel(f_ref, den_ref, kron_ref, g_ref, h_ref, a_scr):
    cb = f_ref.shape[3]
    inv_den = 1.0 / den_ref[...]

    def load(rows):
        re = jnp.concatenate([f_ref[0, :, rows, :], f_ref[1, :, rows, :]], axis=0) * inv_den
        return jnp.concatenate([re, jnp.zeros_like(re)], axis=-1)
    _fft_stage1(load, kron_ref, a_scr, cb)

    def body(k1, carry):
        r = jnp.dot(g_ref[k1], a_scr[k1].astype(BF16), preferred_element_type=F32)
        x_re, x_im = _cplx_fwd(r, FFT_N2, cb)
        h_ref[0, k1] = jnp.concatenate([x_re, x_im], axis=1)
        return carry
    lax.fori_loop(0, FFT_N1, body, 0, unroll=FFT_UNROLL_N2_STAGE)


def _fft_inv_kernel(p_ref, x0_ref, w_ref, bias_ref, gi_ref, kron_ref, o_ref, b_scr):
    cb = o_ref.shape[4]

    def body3(k1, carry):
        r = jnp.dot(gi_ref[k1], p_ref[0, 0, k1], preferred_element_type=F32)
        b_re, b_im = _cplx_inv(r, FFT_N2, cb)
        b_scr[k1] = jnp.concatenate([b_re, b_im], axis=1)
        return carry
    lax.fori_loop(0, FFT_N1, body3, 0, unroll=FFT_UNROLL_N2_STAGE)

    bias = bias_ref[...].reshape(1, 1, cb)

    def body4(jt, carry):
        rows = pl.ds(pl.multiple_of(jt * SUBLANES, SUBLANES), SUBLANES)
        d = b_scr[:, rows, :].reshape(FFT_N1 * SUBLANES, 2 * cb).astype(BF16)
        r = jnp.dot(kron_ref[...], d, preferred_element_type=F32)
        y_re, y_im = _cplx_inv(r, FFT_N1_NZ * SUBLANES, cb)
        for which, y in enumerate((y_re, y_im)):
            y3 = y.reshape(FFT_N1_NZ, SUBLANES, cb)
            o_ref[0, which, :, rows, :] = x0_ref[0, which, :, rows, :] * (y3 + w_ref[0, which, :, rows, :] * bias)
        return carry
    lax.fori_loop(0, FFT_N2 // SUBLANES, body4, 0, unroll=FFT_UNROLL_N1_STAGE)


def hyena_spectrum(fu, den, *, cb=128):
    _, g_fwd, _, _, kron_full = _fft_constants()
    W = fu.shape[2]
    ncb = W // cb
    f4 = fu.reshape(2, FFT_N1_NZ, FFT_N2, W)
    const = lambda a: pl.BlockSpec(a.shape, lambda c: (0,) * a.ndim, pipeline_mode=pl.Buffered(1))
    return pl.pallas_call(
        _fft_fwd_filter_kernel,
        grid=(ncb,),
        in_specs=[
            pl.BlockSpec((2, FFT_N1_NZ, FFT_N2, cb), lambda c: (0, 0, 0, c)),
            pl.BlockSpec((1, cb), lambda c: (0, c)),
            const(kron_full), const(g_fwd),
        ],
        out_specs=pl.BlockSpec((1, FFT_N1, FFT_N2, 2 * cb), lambda c: (c, 0, 0, 0)),
        out_shape=jax.ShapeDtypeStruct((ncb, FFT_N1, FFT_N2, 2 * cb), F32),
        scratch_shapes=[pltpu.VMEM((FFT_N1, FFT_N2, 2 * cb), F32)],
        compiler_params=pltpu.CompilerParams(
            dimension_semantics=("parallel",), vmem_limit_bytes=VMEM_LIMIT_BYTES),
        name="hyena_spectrum",
    )(f4, den, kron_full, g_fwd)


def hyena_conv(x0c, hw, spec, bias, *, cb=128):
    kron_fwd, g_fwd, g_inv, kron_inv, _ = _fft_constants()
    B, L, W = hw.shape
    assert L == FFT_L and B % 2 == 0
    ncb = W // cb
    pair_shape = (B // 2, 2, FFT_N1_NZ, FFT_N2, W)
    x5 = x0c.reshape(pair_shape)
    w5 = hw.reshape(pair_shape)
    const = lambda a: pl.BlockSpec(a.shape, lambda c, p: (0,) * a.ndim, pipeline_mode=pl.Buffered(1))
    pair = pl.BlockSpec((1, 2, FFT_N1_NZ, FFT_N2, cb), lambda c, p: (p, 0, 0, 0, c))
    pspec = pl.BlockSpec((1, 1, FFT_N1, FFT_N2, 2 * cb), lambda c, p: (c, p, 0, 0, 0))
    prod = pl.pallas_call(
        _fft_fwd_data_kernel,
        grid=(ncb, B // 2),
        in_specs=[pair, pl.BlockSpec((1, FFT_N1, FFT_N2, 2 * cb), lambda c, p: (c, 0, 0, 0)),
                  const(kron_fwd), const(g_fwd)],
        out_specs=pspec,
        out_shape=jax.ShapeDtypeStruct((ncb, B // 2, FFT_N1, FFT_N2, 2 * cb), BF16),
        scratch_shapes=[pltpu.VMEM((FFT_N1, FFT_N2, 2 * cb), F32)],
        compiler_params=pltpu.CompilerParams(
            dimension_semantics=("parallel", "parallel"), vmem_limit_bytes=VMEM_LIMIT_BYTES),
        name="hyena_fft_fwd",
    )(w5, spec, kron_fwd, g_fwd)
    out = pl.pallas_call(
        _fft_inv_kernel,
        grid=(ncb, B // 2),
        in_specs=[pspec, pair, pair, pl.BlockSpec((1, cb), lambda c, p: (0, c)), const(g_inv), const(kron_inv)],
        out_specs=pair,
        out_shape=jax.ShapeDtypeStruct(pair_shape, F32),
        scratch_shapes=[pltpu.VMEM((FFT_N1, FFT_N2, 2 * cb), F32)],
        compiler_params=pltpu.CompilerParams(
            dimension_semantics=("parallel", "parallel"), vmem_limit_bytes=VMEM_LIMIT_BYTES),
        name="hyena_fft_inv",
    )(prod, x5, w5, bias.reshape(1, W), g_inv, kron_inv)
    return out.reshape(B, L, W)


@functools.lru_cache(maxsize=None)
def _dense_dft_constants(L):
    n_full = 2 * L
    k = np.arange(n_full)[:, None]
    n = np.arange(n_full)[None, :]
    ang = 2.0 * np.pi * ((k * n) % n_full) / n_full
    fwd = np.concatenate([np.cos(ang), np.sin(ang)], axis=0)
    inv = np.concatenate([np.cos(ang[:L]), np.sin(ang[:L])], axis=0) / n_full
    return jnp.asarray(fwd, dtype=F32), jnp.asarray(inv, dtype=F32)


def _hyena_short_kernel(x0_ref, w_ref, f_ref, den_ref, bias_ref, ff_ref, fi_ref, o_ref, h_scr):
    L, cb = o_ref.shape[2], o_ref.shape[3]

    @pl.when(pl.program_id(1) == 0)
    def _():
        filt = jnp.concatenate([f_ref[0], f_ref[1]], axis=0) / den_ref[...]
        r = _dot_hi(ff_ref[...], filt)
        h_scr[0] = r[:2 * L]
        h_scr[1] = -r[2 * L:]

    h_re, h_im = h_scr[0], h_scr[1]
    z = jnp.concatenate([w_ref[0, 0], w_ref[0, 1]], axis=1).astype(BF16)
    x_re, x_im = _cplx_fwd(jnp.dot(ff_ref[:, :L].astype(BF16), z, preferred_element_type=F32), 2 * L, cb)
    p = jnp.concatenate([x_re * h_re - x_im * h_im, x_re * h_im + x_im * h_re], axis=1).astype(BF16)
    y_re, y_im = _cplx_inv(jnp.dot(fi_ref[...].astype(BF16), p, preferred_element_type=F32), L, cb)
    for which, y in enumerate((y_re, y_im)):
        o_ref[0, which] = x0_ref[0, which] * (y + w_ref[0, which] * bias_ref[...])


def hyena_conv_short(x0c, hw, fu, den, bias, *, cb=128):
    B, L, W = hw.shape
    fwd, inv = _dense_dft_constants(L)
    pair_shape = (B // 2, 2, L, W)
    pair = pl.BlockSpec((1, 2, L, cb), lambda c, p: (p, 0, 0, c))
    vec = pl.BlockSpec((1, cb), lambda c, p: (0, c))
    const = lambda a: pl.BlockSpec(a.shape, lambda c, p: (0, 0))
    out = pl.pallas_call(
        _hyena_short_kernel,
        grid=(W // cb, B // 2),
        in_specs=[pair, pair, pl.BlockSpec((2, L, cb), lambda c, p: (0, 0, c)), vec, vec, const(fwd), const(inv)],
        out_specs=pair,
        out_shape=jax.ShapeDtypeStruct(pair_shape, F32),
        scratch_shapes=[pltpu.VMEM((2, 2 * L, cb), F32)],
        compiler_params=pltpu.CompilerParams(
            dimension_semantics=("parallel", "arbitrary"), vmem_limit_bytes=VMEM_LIMIT_BYTES),
        name="hyena_conv_short",
    )(x0c.reshape(pair_shape), hw.reshape(pair_shape), fu, den, bias.reshape(1, W), fwd, inv)
    return out.reshape(B, L, W)


GRID_W = 64
SC_COLS = 3 * BRANCH_W
HG_COLS = 3 * HG_KEY + 2 * HG_HEADS * HG_DV
SSD_XBC = SSD_INNER + 2 * SSD_GROUPS * SSD_STATE
assert SSD_XBC == SSD_XBC_COLS and HY_COLS == 3 * HY_WIDTH


def _sincos_1d(pos, dim):
    omega = 1.0 / (10000.0 ** (jnp.arange(dim // 2, dtype=F32) / (dim // 2)))
    ang = pos.astype(F32)[:, None] * omega[None]
    return jnp.concatenate([jnp.sin(ang), jnp.cos(ang)], -1)


def _grid_pos_embed(rows, dim):
    row = jnp.repeat(jnp.arange(rows), GRID_W)
    col = jnp.tile(jnp.arange(GRID_W), rows)
    return jnp.concatenate([_sincos_1d(row, dim // 2), _sincos_1d(col, dim // 2)], -1)


def kernel(x, c, ctx, c_ctx, w_ada, b_ada, w_in, hy_conv_w, hy_conv_b, hy_w1, hy_b1, hy_freq1,
           hy_w2, hy_b2, hy_freq2, hy_w3, hy_decay, hy_bias, sc_conv_w, hg_lb_logits, hg_norm_g,
           ssd_conv_w, ssd_conv_b, ssd_a_log, ssd_dt_bias, ssd_d, ssd_norm_g, w_gate, b_gate, w_br,
           w_o, ln1_g, ln1_b, ln2_g, ln2_b, w_router, b_router, w_e1, w_e3, w_e2):
    B, L, D = x.shape
    LC = ctx.shape[1]
    depth = w_in.shape[0]
    pos_table = _grid_pos_embed(L // GRID_W, D).astype(x.dtype)
    lat = x.reshape(B * L, D)
    cx = ctx.reshape(B * LC, D)

    n_vec = -(-(B + 1) // SUBLANES) * SUBLANES
    cvecs = jnp.concatenate([c, c_ctx[None], jnp.zeros((n_vec - B - 1, D), c.dtype)], axis=0)
    mods = ada_mod(cvecs, w_ada, b_ada).reshape(depth, n_vec, 6, D)
    lb_logits = hg_lb_logits.reshape(2 * depth, HG_KEY)

    out_widths = (BRANCH_W, BRANCH_W, BRANCH_W, SSD_XBC, HG_COLS, SSD_INNER + LANES)
    z_state = lambda *shape: jnp.zeros(shape, F32)
    w_e1_b, w_e3_b, w_e2_b = w_e1.astype(BF16), w_e3.astype(BF16), w_e2.astype(BF16)
    c_hy, c_sc, c_hg = HY_COLS, HY_COLS + SC_COLS, HY_COLS + SC_COLS + HG_COLS
    c_z, c_xbc = c_hg + SSD_INNER, c_hg + SSD_INNER + SSD_XBC

    for l in range(depth):
        ctx_out = l < depth - 1
        mod_lat, mod_ctx = mods[l, :B], mods[l, B:B + 1]
        wl = w_in[l]
        w_in_b = jnp.concatenate([
            wl[:, :c_hy], wl[:, c_hy + BRANCH_W:c_sc], wl[:, c_z:c_xbc],
            wl[:, c_hy:c_hy + BRANCH_W], wl[:, c_sc:c_hg], wl[:, c_hg:c_z],
            jnp.pad(wl[:, c_xbc:], ((0, 0), (0, LANES - 2 * SSD_HEADS))),
        ], axis=1).astype(BF16)
        conv_par = (hy_conv_w[l], hy_conv_b[l], sc_conv_w[l], ssd_conv_w[l], ssd_conv_b[l])
        ssd_par = jnp.zeros((2 * SUBLANES, LANES), F32)
        ssd_par = ssd_par.at[:SUBLANES, :2 * SSD_HEADS].set(jnp.broadcast_to(ssd_a_log[l].reshape(1, -1), (SUBLANES, 2 * SSD_HEADS)))
        ssd_par = ssd_par.at[SUBLANES:, :2 * SSD_HEADS].set(jnp.broadcast_to(ssd_dt_bias[l].reshape(1, -1), (SUBLANES, 2 * SSD_HEADS)))
        d_skip = jnp.repeat(ssd_d[l], SSD_HEADDIM)
        filt_args = (hy_w1[l], hy_b1[l], hy_freq1[l], hy_w2[l], hy_b2[l], hy_freq2[l], hy_w3[l], hy_decay[l])

        def mixers(tokens, mod, seg_len, hg_state, ssd_state, want_out, pos=None):
            n_seq = tokens.shape[0] // seg_len
            x0c, hw, sc_y, xbc, hg, zdt = in_proj(
                tokens, mod, w_in_b, conv_par, out_widths, seq_len=seg_len, rows_per_mod=mod_rows(mod, tokens),
                tm=min(512, seg_len), pos=pos)
            seq = lambda a: a.reshape(n_seq, seg_len, a.shape[1])
            hg_y, hg_f, hg_b = hgrn_scan(seq(hg), lb_logits, hg_norm_g[l], *hg_state, layer=l)
            ssd_y, ssd_f, ssd_b = ssd_scan(seq(zdt), seq(xbc), ssd_par, d_skip, ssd_norm_g[l], *ssd_state,
                                           z_blk=0, dt_blk=SSD_INNER // LANES)
            branches = None
            if want_out:
                fu, den = hy_filter(seg_len, *filt_args)
                if seg_len == FFT_L:
                    hy_y = hyena_conv(seq(x0c), seq(hw), hyena_spectrum(fu, den), hy_bias[l])
                else:
                    hy_y = hyena_conv_short(seq(x0c), seq(hw), fu, den, hy_bias[l])
                flat = lambda a: a.reshape(tokens.shape[0], a.shape[2])
                branches = [flat(hy_y), sc_y, flat(hg_y), flat(ssd_y)]
            return branches, (hg_f, hg_b), (ssd_f, ssd_b)

        def mod_rows(mod, tokens):
            return tokens.shape[0] // mod.shape[0]

        def finish(tokens, mod, branches, pos=None):
            rows = mod_rows(mod, tokens)
            t1 = merge_ln(tokens, mod, branches, w_gate[l].astype(BF16), b_gate[l], w_br[l].astype(BF16),
                          w_o[l].astype(BF16), ln1_g[l], ln1_b[l], rows_per_mod=rows, tm=512, pos=pos)
            return moe_ln(t1, mod, w_router, b_router, w_e1_b, w_e3_b, w_e2_b, ln2_g[l], ln2_b[l],
                          layer=l, rows_per_mod=rows, tm=1024)

        zero_hg = (z_state(B, HG_KEY, HG_KEY),) * 2
        zero_ssd = (z_state(B, SSD_GROUPS, LANES, LANES),) * 2
        br_ctx, hg_state, ssd_state = mixers(cx, mod_ctx, LC, zero_hg, zero_ssd, ctx_out)
        pos = pos_table if l == 0 else None
        br_lat, _, _ = mixers(lat, mod_lat, L, hg_state, ssd_state, True, pos=pos)
        lat = finish(lat, mod_lat, br_lat, pos=pos)
        if ctx_out:
            cx = finish(cx, mod_ctx, br_ctx)
    return lat.reshape(B, L, D)
```

```python
import functools
import math

import jax
import jax.numpy as jnp
import numpy as np
from jax import lax
from jax.experimental import pallas as pl
from jax.experimental.pallas import tpu as pltpu

F32 = jnp.float32
BF16 = jnp.bfloat16

N_EXPERTS = 16
N_EXPERT_GROUPS = 4
GROUP_SIZE = N_EXPERTS // N_EXPERT_GROUPS
DEPTH = 2
DEEPNORM_ALPHA = (2 * DEPTH) ** 0.25
LN_EPS = 1e-5
RMS_EPS = 1e-6

VMEM_LIMIT_BYTES = 52 * 1024 * 1024


def _layernorm_rows(z, g, b):
    mu = jnp.mean(z, axis=-1, keepdims=True)
    zc = z - mu
    var = jnp.mean(zc * zc, axis=-1, keepdims=True)
    return zc * lax.rsqrt(var + LN_EPS) * g + b


def _route_t(logits_t, bias_col):
    z = logits_t - jnp.max(logits_t, axis=0, keepdims=True)
    ex = jnp.exp(z)
    scores = ex / jnp.sum(ex, axis=0, keepdims=True)
    sel = scores + bias_col
    eidx = lax.broadcasted_iota(jnp.int32, sel.shape, 0)
    neg = jnp.float32(-jnp.inf)

    def first_argmax(m):
        top = jnp.max(m, axis=0, keepdims=True)
        return top, jnp.min(jnp.where(m == top, eidx, N_EXPERTS), axis=0, keepdims=True)

    best_score = None
    best_grp = None
    for g in range(N_EXPERT_GROUPS):
        m = jnp.where((eidx >= g * GROUP_SIZE) & (eidx < (g + 1) * GROUP_SIZE), sel, neg)
        t1, i1 = first_argmax(m)
        t2 = jnp.max(jnp.where(eidx == i1, neg, m), axis=0, keepdims=True)
        s = t1 + t2
        if g == 0:
            best_score, best_grp = s, jnp.zeros_like(i1)
        else:
            better = s > best_score
            best_score = jnp.where(better, s, best_score)
            best_grp = jnp.where(better, g, best_grp)
    lo = best_grp * GROUP_SIZE
    masked = jnp.where((eidx >= lo) & (eidx < lo + GROUP_SIZE), sel, neg)
    _, ia = first_argmax(masked)
    _, ib = first_argmax(jnp.where(eidx == ia, neg, masked))
    w = jnp.where((eidx == ia) | (eidx == ib), scores, 0.0)
    return w / jnp.sum(w, axis=0, keepdims=True), best_grp


MOE_CHUNK = 64
MOE_MAX_BLOCK_CHUNKS = 8
MOE_MAX_TILE = 1024
MOE_STEP_EXPERTS = 2
GATE_TERMS = 3
POS_LANE = GATE_TERMS * N_EXPERTS


def _moe_ln_kernel(x_ref, mod_ref, wr_ref, br_ref, w1_ref, w3_ref, w2_ref, lng_ref, lnb_ref,
                   o_ref, pt_scr, hc_scr, gc_scr, yc_scr, seg_smem):
    step = pl.program_id(1)
    TM, R = pt_scr.shape
    D = x_ref.shape[1]
    G = N_EXPERT_GROUPS

    @pl.when(step == 0)
    def _():
        h = _modulate(x_ref[...], mod_ref, MOD_SHIFT2, MOD_SCALE2)
        h_hi = h.astype(BF16)
        h_lo = (h - h_hi.astype(F32)).astype(BF16)
        r_hi = jnp.dot(h_hi, wr_ref[...], preferred_element_type=F32)
        r_lo = jnp.dot(h_lo, wr_ref[:, :LANES], preferred_element_type=F32)
        logits_t = (r_hi[:, :LANES] + r_hi[:, LANES:] + r_lo).T[:N_EXPERTS]
        gate_t, grp = _route_t(logits_t, br_ref[:, 0:1])

        gidx = lax.broadcasted_iota(jnp.int32, (SUBLANES, TM), 0)
        tok = lax.broadcasted_iota(jnp.int32, (SUBLANES, TM), 1)
        member = jnp.where(gidx == grp, 1.0, 0.0)
        incl = member
        shift = 1
        while shift < TM:
            incl = incl + jnp.where(tok >= shift, pltpu.roll(incl, shift, 1), 0.0)
            shift *= 2
        rank = incl - member
        offset = jnp.int32(0)
        pos_row = jnp.zeros((1, TM), F32)
        for g in range(G):
            n_chunks = (jnp.sum(member[g:g + 1, :]).astype(jnp.int32) + (MOE_CHUNK - 1)) // MOE_CHUNK
            seg_smem[g] = n_chunks
            seg_smem[G + g] = offset
            pos_row = pos_row + member[g:g + 1, :] * (rank[g:g + 1, :] + offset.astype(F32))
            offset = offset + n_chunks * MOE_CHUNK

        terms, rest = [], gate_t
        for _ in range(GATE_TERMS):
            part = rest.astype(BF16).astype(F32)
            terms.append(part)
            rest = rest - part
        pad = jnp.zeros((LANES - POS_LANE - 1, TM), F32)
        side = jnp.concatenate(terms + [pos_row, pad], axis=0).T
        pos_col = side[:, POS_LANE:POS_LANE + 1].astype(jnp.int32)
        pt_scr[...] = jnp.where(lax.broadcasted_iota(jnp.int32, (TM, R), 1) == pos_col, 1.0, 0.0).astype(BF16)
        p = jnp.where(lax.broadcasted_iota(jnp.int32, (R, TM), 0) == pos_row.astype(jnp.int32), 1.0, 0.0).astype(BF16)
        gathered = jnp.dot(p, jnp.concatenate([h_hi, side.astype(BF16)], axis=1), preferred_element_type=F32)
        hc_scr[...] = gathered[:, :D].astype(BF16)
        gc_scr[...] = gathered[:, D:]
        yc_scr[...] = jnp.zeros_like(yc_scr)

    g = step // (GROUP_SIZE // MOE_STEP_EXPERTS)
    n_chunks = seg_smem[g]
    base = seg_smem[G + g]

    def expert_rows(start, size):
        rows = pl.ds(pl.multiple_of(start, MOE_CHUNK), size)
        hc = hc_scr[rows, :]
        gc = gc_scr[rows, :]
        lane = lax.broadcasted_iota(jnp.int32, gc.shape, 1)
        acc = None
        for k in range(MOE_STEP_EXPERTS):
            e = step * MOE_STEP_EXPERTS + k
            a = jnp.dot(hc, w1_ref[k], preferred_element_type=F32)
            b = jnp.dot(hc, w3_ref[k], preferred_element_type=F32)
            mid = (a * jax.nn.sigmoid(a) * b).astype(BF16)
            y = jnp.dot(mid, w2_ref[k], preferred_element_type=F32)
            gcol = jnp.sum(jnp.where((lane % N_EXPERTS == e) & (lane < POS_LANE), gc, 0.0), axis=-1, keepdims=True)
            acc = gcol * y if acc is None else acc + gcol * y
        yc_scr[rows, :] += acc

    full = MOE_MAX_BLOCK_CHUNKS * MOE_CHUNK
    for _ in range(max(TM // full - 1, 0)):
        big = n_chunks > MOE_MAX_BLOCK_CHUNKS

        @pl.when(big)
        def _(base=base):
            expert_rows(base, full)

        base = base + jnp.where(big, full, 0)
        n_chunks = n_chunks - jnp.where(big, MOE_MAX_BLOCK_CHUNKS, 0)

    for c in range(1, min(MOE_MAX_BLOCK_CHUNKS, TM // MOE_CHUNK) + 1):
        @pl.when(n_chunks == c)
        def _(base=base, c=c):
            expert_rows(base, c * MOE_CHUNK)

    @pl.when(step == N_EXPERTS // MOE_STEP_EXPERTS - 1)
    def _():
        moe = jnp.dot(pt_scr[...], yc_scr[...].astype(BF16), preferred_element_type=F32)
        z = DEEPNORM_ALPHA * x_ref[...] + mod_ref[0, MOD_GATE2:MOD_GATE2 + 1, :] * moe
        o_ref[...] = _layernorm_rows(z, lng_ref[...], lnb_ref[...])


def moe_ln(x, mod, w_router, b_router, w1, w3, w2, ln_g, ln_b, *, layer, rows_per_mod, tm):
    T, D = x.shape
    _, E, _, FF = w1.shape
    assert T % tm == 0 and rows_per_mod % tm == 0 and tm <= MOE_MAX_TILE and tm % MOE_CHUNK == 0
    tiles_per_mod = rows_per_mod // tm
    sorted_rows = tm + N_EXPERT_GROUPS * MOE_CHUNK
    wr_hi = w_router.astype(BF16)
    wr_lo = (w_router - wr_hi.astype(F32)).astype(BF16)
    lane_pad = lambda w: jnp.pad(w, ((0, 0), (0, LANES - E)))
    wr_split = jnp.concatenate([lane_pad(wr_hi), lane_pad(wr_lo)], axis=1)
    br_col = jnp.broadcast_to(b_router.reshape(E, 1), (E, LANES))
    return pl.pallas_call(
        _moe_ln_kernel,
        grid=(T // tm, E // MOE_STEP_EXPERTS),
        in_specs=[
            pl.BlockSpec((tm, D), lambda i, e: (i, 0)),
            pl.BlockSpec((1, 6, D), lambda i, e: (i // tiles_per_mod, 0, 0)),
            pl.BlockSpec((D, 2 * LANES), lambda i, e: (0, 0)),
            pl.BlockSpec((E, LANES), lambda i, e: (0, 0)),
            pl.BlockSpec((None, MOE_STEP_EXPERTS, D, FF), lambda i, e: (layer, e, 0, 0)),
            pl.BlockSpec((None, MOE_STEP_EXPERTS, D, FF), lambda i, e: (layer, e, 0, 0)),
            pl.BlockSpec((None, MOE_STEP_EXPERTS, FF, D), lambda i, e: (layer, e, 0, 0)),
            pl.BlockSpec((1, D), lambda i, e: (0, 0)),
            pl.BlockSpec((1, D), lambda i, e: (0, 0)),
        ],
        out_specs=pl.BlockSpec((tm, D), lambda i, e: (i, 0)),
        out_shape=jax.ShapeDtypeStruct((T, D), F32),
        scratch_shapes=[
            pltpu.VMEM((tm, sorted_rows), BF16),
            pltpu.VMEM((sorted_rows, D), BF16),
            pltpu.VMEM((sorted_rows, LANES), F32),
            pltpu.VMEM((sorted_rows, D), F32),
            pltpu.SMEM((2 * N_EXPERT_GROUPS,), jnp.int32),
        ],
        compiler_params=pltpu.CompilerParams(
            dimension_semantics=("parallel", "arbitrary"), vmem_limit_bytes=VMEM_LIMIT_BYTES),
        name="moe_ln",
    )(x, mod, wr_split, br_col, w1, w3, w2, ln_g.reshape(1, D), ln_b.reshape(1, D))


def _ada_kernel(c_ref, w_ref, b_ref, o_ref):
    c = c_ref[...]
    s = c * jax.nn.sigmoid(c)
    o_ref[0] = jnp.dot(s, w_ref[0], precision=lax.Precision.HIGHEST, preferred_element_type=F32) + b_ref[0]


def ada_mod(cvecs, w_ada, b_ada, *, tn=1536):
    R, D = cvecs.shape
    depth, _, N = w_ada.shape
    assert N % tn == 0
    return pl.pallas_call(
        _ada_kernel,
        grid=(depth, N // tn),
        in_specs=[
            pl.BlockSpec((R, D), lambda l, j: (0, 0)),
            pl.BlockSpec((1, D, tn), lambda l, j: (l, 0, j)),
            pl.BlockSpec((1, 1, tn), lambda l, j: (l, 0, j)),
        ],
        out_specs=pl.BlockSpec((1, R, tn), lambda l, j: (l, 0, j)),
        out_shape=jax.ShapeDtypeStruct((depth, R, N), F32),
        compiler_params=pltpu.CompilerParams(
            dimension_semantics=("parallel", "parallel"), vmem_limit_bytes=VMEM_LIMIT_BYTES),
        name="ada_mod",
    )(cvecs, w_ada, b_ada.reshape(depth, 1, N))


MOD_SHIFT1, MOD_SCALE1, MOD_GATE1, MOD_SHIFT2, MOD_SCALE2, MOD_GATE2 = range(6)


def _modulate(x, mod_ref, shift_row, scale_row):
    return x * (1.0 + mod_ref[0, scale_row:scale_row + 1, :]) + mod_ref[0, shift_row:shift_row + 1, :]


HALO = 8
BRANCH_W = 256
HY_COLS = 3 * BRANCH_W
SSD_XBC_COLS = 512
IN_CONV_COLS = HY_COLS + 2 * BRANCH_W + SSD_XBC_COLS


def _conv3(u, w_ref, tm):
    return (u[HALO - 1:HALO - 1 + tm] * w_ref[0:1, :] + u[HALO:HALO + tm] * w_ref[1:2, :]
            + u[HALO + 1:HALO + 1 + tm] * w_ref[2:3, :])


def _in_proj_kernel(x_ref, xp_ref, xn_ref, mod_ref, w_ref, hcw_ref, hcb_ref, scw_ref, xcw_ref, xcb_ref, *rest,
                    has_pos, tiles_per_seq):
    if has_pos:
        pos_ref, pp_ref, pn_ref = rest[:3]
        rest = rest[3:]
    x0c_ref, hw_ref, scy_ref, xbc_ref, hg_ref, zdt_ref = rest
    tm = x_ref.shape[0]
    t = pl.program_id(0) % tiles_per_seq
    keep_prev = jnp.where(t > 0, 1.0, 0.0)
    keep_next = jnp.where(t < tiles_per_seq - 1, 1.0, 0.0)

    def tokens(main_ref, prev_ref, next_ref):
        return main_ref[...], prev_ref[...], next_ref[...]

    xm, xp, xn = tokens(x_ref, xp_ref, xn_ref)
    if has_pos:
        pm, pp, pn = tokens(pos_ref, pp_ref, pn_ref)
        xm, xp, xn = xm + pm, xp + pp, xn + pn
    mod = lambda v: _modulate(v, mod_ref, MOD_SHIFT1, MOD_SCALE1)
    h_main = mod(xm).astype(BF16)
    h_all = jnp.concatenate([mod(xp) * keep_prev, mod(xm), mod(xn) * keep_next], axis=0).astype(BF16)
    uc = jnp.dot(h_all, w_ref[:, :IN_CONV_COLS], preferred_element_type=F32)
    ur = jnp.dot(h_main, w_ref[:, IN_CONV_COLS:], preferred_element_type=F32)

    hy = _conv3(uc[:, :HY_COLS], hcw_ref, tm) + hcb_ref[...]
    x0c_ref[...] = hy[:, :BRANCH_W]
    hw_ref[...] = hy[:, BRANCH_W:2 * BRANCH_W] * hy[:, 2 * BRANCH_W:]
    cg_xs = uc[:, HY_COLS:HY_COLS + BRANCH_W] * uc[:, HY_COLS + BRANCH_W:HY_COLS + 2 * BRANCH_W]
    scy_ref[...] = ur[:, :BRANCH_W] * _conv3(cg_xs, scw_ref, tm)
    xa = _conv3(uc[:, HY_COLS + 2 * BRANCH_W:], xcw_ref, tm) + xcb_ref[...]
    xbc_ref[...] = xa * jax.nn.sigmoid(xa)
    n_hg = hg_ref.shape[1]
    hg_ref[...] = ur[:, BRANCH_W:BRANCH_W + n_hg]
    zdt_ref[...] = ur[:, BRANCH_W + n_hg:]


def in_proj(x, mod, w, conv, out_widths, *, seq_len, rows_per_mod, tm, pos=None):
    T, D = x.shape
    assert T % tm == 0 and seq_len % tm == 0 and rows_per_mod % tm == 0 and tm % HALO == 0
    tiles_per_seq = seq_len // tm
    tiles_per_mod = rows_per_mod // tm
    hb = tm // HALO
    last_blk = T // HALO - 1
    hy_w, hy_b, sc_w, xbc_w, xbc_b = conv
    row = lambda v: v.reshape(1, -1)
    const = lambda a: pl.BlockSpec(a.shape, lambda i: (0,) * a.ndim)
    consts = [hy_w, row(hy_b), sc_w, xbc_w, row(xbc_b)]
    in_specs = [
        pl.BlockSpec((tm, D), lambda i: (i, 0)),
        pl.BlockSpec((HALO, D), lambda i: (jnp.maximum(i * hb - 1, 0), 0)),
        pl.BlockSpec((HALO, D), lambda i: (jnp.minimum((i + 1) * hb, last_blk), 0)),
        pl.BlockSpec((1, 6, D), lambda i: (i // tiles_per_mod, 0, 0)),
        pl.BlockSpec(w.shape, lambda i: (0, 0), pipeline_mode=pl.Buffered(1)),
    ] + [const(a) for a in consts]
    args = [x, x, x, mod, w] + consts
    if pos is not None:
        assert pos.shape[0] == seq_len
        last_pos = seq_len // HALO - 1
        in_specs += [
            pl.BlockSpec((tm, D), lambda i: (i % tiles_per_seq, 0)),
            pl.BlockSpec((HALO, D), lambda i: (jnp.maximum((i % tiles_per_seq) * hb - 1, 0), 0)),
            pl.BlockSpec((HALO, D), lambda i: (jnp.minimum((i % tiles_per_seq + 1) * hb, last_pos), 0)),
        ]
        args += [pos, pos, pos]
    return pl.pallas_call(
        functools.partial(_in_proj_kernel, has_pos=pos is not None, tiles_per_seq=tiles_per_seq),
        grid=(T // tm,),
        in_specs=in_specs,
        out_specs=[pl.BlockSpec((tm, n), lambda i: (i, 0)) for n in out_widths],
        out_shape=[jax.ShapeDtypeStruct((T, n), F32) for n in out_widths],
        compiler_params=pltpu.CompilerParams(
            dimension_semantics=("parallel",), vmem_limit_bytes=VMEM_LIMIT_BYTES),
        name="in_proj",
    )(*args)


def _pos_spec(pos, rows_per_mod, tm):
    assert pos.shape[0] == rows_per_mod
    tiles = rows_per_mod // tm
    return pl.BlockSpec((tm, pos.shape[1]), lambda i: (i % tiles, 0))


N_BRANCH = 4


def _merge_kernel(x_ref, mod_ref, hy_ref, sc_ref, hg_ref, ssd_ref, wg_ref, bg_ref, wbr_ref, wo_ref,
                  lng_ref, lnb_ref, *rest, has_pos):
    x = x_ref[...] + rest[0][...] if has_pos else x_ref[...]
    o_ref = rest[-1]
    D = x.shape[1]
    h = _modulate(x, mod_ref, MOD_SHIFT1, MOD_SCALE1).astype(BF16)
    y = None
    for k, br_ref in enumerate((hy_ref, sc_ref, hg_ref, ssd_ref)):
        gate = jax.nn.sigmoid(
            jnp.dot(h, wg_ref[:, k * D:(k + 1) * D], preferred_element_type=F32) + bg_ref[:, k * D:(k + 1) * D])
        term = gate * jnp.dot(br_ref[...].astype(BF16), wbr_ref[k], preferred_element_type=F32)
        y = term if y is None else y + term
    y = jnp.dot(y.astype(BF16), wo_ref[...], preferred_element_type=F32)
    z = DEEPNORM_ALPHA * x + mod_ref[0, MOD_GATE1:MOD_GATE1 + 1, :] * y
    o_ref[...] = _layernorm_rows(z, lng_ref[...], lnb_ref[...])


def merge_ln(x, mod, branches, w_gate, b_gate, w_br, w_o, ln_g, ln_b, *, rows_per_mod, tm, pos=None):
    T, D = x.shape
    BW = branches[0].shape[1]
    assert T % tm == 0 and rows_per_mod % tm == 0
    tiles_per_mod = rows_per_mod // tm
    const = lambda shape: pl.BlockSpec(shape, lambda i: (0,) * len(shape), pipeline_mode=pl.Buffered(1))
    extra = [] if pos is None else [pos]
    return pl.pallas_call(
        functools.partial(_merge_kernel, has_pos=pos is not None),
        grid=(T // tm,),
        in_specs=[
            pl.BlockSpec((tm, D), lambda i: (i, 0)),
            pl.BlockSpec((1, 6, D), lambda i: (i // tiles_per_mod, 0, 0)),
        ] + [pl.BlockSpec((tm, BW), lambda i: (i, 0))] * N_BRANCH + [
            const((D, N_BRANCH * D)), const((1, N_BRANCH * D)), const((N_BRANCH, BW, D)), const((D, D)),
            const((1, D)), const((1, D)),
        ] + [_pos_spec(p, rows_per_mod, tm) for p in extra],
        out_specs=pl.BlockSpec((tm, D), lambda i: (i, 0)),
        out_shape=jax.ShapeDtypeStruct((T, D), F32),
        compiler_params=pltpu.CompilerParams(
            dimension_semantics=("parallel",), vmem_limit_bytes=VMEM_LIMIT_BYTES),
        name="merge_ln",
    )(x, mod, *branches, w_gate, b_gate.reshape(1, -1), w_br, w_o, ln_g.reshape(1, D), ln_b.reshape(1, D), *extra)


SSD_HEADS = 4
SSD_HEADDIM = 64
SSD_STATE = 64
SSD_GROUPS = 2
SSD_INNER = SSD_HEADS * SSD_HEADDIM
LANES = 128
NEG_INF = float("-inf")


SCAN_BATCH = 4


def _scan_chunk_index(ph, j, n):
    return j + (1 - ph) * (n - 1 - 2 * j)


def _ssd_kernel(z_ref, xbc_ref, dt_ref, par_ref, dskip_ref, ng_ref, sf0_ref, sb0_ref,
                y_ref, sf_ref, sb_ref, yb_scr, st_scr, *, n_chunks):
    ph = pl.program_id(1)
    j = pl.program_id(2)
    is_fwd = ph == 1
    c = _scan_chunk_index(ph, j, n_chunks)
    NB, R = xbc_ref.shape[0], xbc_ref.shape[1]

    @pl.when((j == 0) & is_fwd)
    def _():
        st_scr[...] = sf0_ref[...]

    @pl.when((j == 0) & jnp.logical_not(is_fwd))
    def _():
        st_scr[...] = sb0_ref[...]

    row = lax.broadcasted_iota(jnp.int32, (R, R), 0)
    col = lax.broadcasted_iota(jnp.int32, (R, R), 1)
    mask = jnp.where(is_fwd, row - col, col - row) >= 0
    tri = jnp.where(mask, 1.0, 0.0).astype(BF16)
    head_of_lane = lax.broadcasted_iota(jnp.int32, (LANES, SSD_INNER), 1) // SSD_HEADDIM
    expand = jnp.where(lax.broadcasted_iota(jnp.int32, (LANES, SSD_INNER), 0) == head_of_lane, 1.0, 0.0).astype(BF16)
    ys = [_ssd_chunk(bi, is_fwd, mask, tri, expand, xbc_ref, dt_ref, par_ref, st_scr) for bi in range(NB)]

    @pl.when(jnp.logical_not(is_fwd))
    def _():
        for bi in range(NB):
            yb_scr[bi, c] = ys[bi]

    @pl.when(is_fwd)
    def _():
        for bi in range(NB):
            z = z_ref[bi]
            xs = xbc_ref[bi, :, :SSD_INNER]
            yt = (ys[bi] + yb_scr[bi, c] + xs * dskip_ref[...]) * (z * jax.nn.sigmoid(z))
            ms = jnp.mean(yt * yt, axis=-1, keepdims=True)
            y_ref[bi] = yt * lax.rsqrt(ms + RMS_EPS) * ng_ref[...]

    @pl.when((j == n_chunks - 1) & is_fwd)
    def _():
        sf_ref[...] = st_scr[...]

    @pl.when((j == n_chunks - 1) & jnp.logical_not(is_fwd))
    def _():
        sb_ref[...] = st_scr[...]


def _ssd_chunk(bi, is_fwd, mask, tri, expand, xbc_ref, dt_ref, par_ref, st_scr):
    R = xbc_ref.shape[1]
    dsel = lambda v: jnp.where(is_fwd, v, pltpu.roll(v, LANES - SSD_HEADS, 1))
    dt_raw = dsel(dt_ref[bi]) + dsel(par_ref[8:16, :])[0:1, :]
    dt = jnp.maximum(dt_raw, 0.0) + jnp.log(1.0 + jnp.exp(-jnp.abs(dt_raw)))
    a = -jnp.exp(dsel(par_ref[0:8, :])[0:1, :]) * dt
    cs = _split_dot(tri, a, terms=3)
    cs_t = cs.T
    total = jnp.sum(a, axis=0, keepdims=True)

    spread = lambda v: jnp.dot(v.astype(BF16), expand, preferred_element_type=F32)
    dt_bc = spread(dt)
    in_decay = spread(jnp.exp(cs))
    out_decay = spread(jnp.exp(total - cs))
    e_total = jnp.exp(total)

    xbc = xbc_ref[bi]
    xdt = xbc[:, :SSD_INNER] * dt_bc
    xdt_b = xdt.astype(BF16)
    xout_b = (xdt * out_decay).astype(BF16)
    bm = xbc[:, SSD_INNER:SSD_INNER + LANES]
    cm_b = xbc[:, SSD_INNER + LANES:SSD_INNER + 2 * LANES].astype(BF16)
    lane = lax.broadcasted_iota(jnp.int32, (R, LANES), 1)
    lo_half = lane < SSD_HEADDIM
    lo_half_st = lax.broadcasted_iota(jnp.int32, (LANES, LANES), 1) < SSD_HEADDIM
    y_groups = []
    for g in range(SSD_GROUPS):
        h0, h1 = 2 * g, 2 * g + 1
        cols = slice(g * LANES, (g + 1) * LANES)
        bm_g = jnp.where((lane >= g * SSD_STATE) & (lane < (g + 1) * SSD_STATE), bm, 0.0).astype(BF16)
        G = lax.dot_general(cm_b, bm_g, (((1,), (1,)), ((), ())), preferred_element_type=F32)
        st_old = st_scr[bi, g]
        y_off = in_decay[:, cols] * jnp.dot(cm_b, st_old.astype(BF16), preferred_element_type=F32)
        y_diag = []
        for h in (h0, h1):
            decay = jnp.exp(jnp.where(mask, cs[:, h:h + 1] - cs_t[h:h + 1, :], NEG_INF))
            y_diag.append(jnp.dot((G * decay).astype(BF16), xdt_b[:, cols], preferred_element_type=F32))
        y_groups.append(jnp.where(lo_half, y_diag[0], y_diag[1]) + y_off)
        upd = lax.dot_general(bm_g, xout_b[:, cols], (((0,), (0,)), ((), ())), preferred_element_type=F32)
        st_scr[bi, g] = jnp.where(lo_half_st, e_total[:, h0:h0 + 1], e_total[:, h1:h1 + 1]) * st_old + upd
    return jnp.concatenate(y_groups, axis=1)


def ssd_scan(ssd, xbc, par, d_skip, norm_g, s_f0, s_b0, *, z_blk, dt_blk, chunk=256):
    B, L, _ = ssd.shape
    R = min(chunk, L)
    assert L % R == 0 and B % SCAN_BATCH == 0
    n = L // R
    NB = SCAN_BATCH
    cidx = lambda ph, j: _scan_chunk_index(ph, j, n)
    st_spec = pl.BlockSpec((NB, SSD_GROUPS, LANES, LANES), lambda b, ph, j: (b, 0, 0, 0))
    st_shape = jax.ShapeDtypeStruct((B, SSD_GROUPS, LANES, LANES), F32)
    return pl.pallas_call(
        functools.partial(_ssd_kernel, n_chunks=n),
        grid=(B // NB, 2, n),
        in_specs=[
            pl.BlockSpec((NB, R, SSD_INNER), lambda b, ph, j: (b, cidx(ph, j), z_blk)),
            pl.BlockSpec((NB, R, xbc.shape[2]), lambda b, ph, j: (b, cidx(ph, j), 0)),
            pl.BlockSpec((NB, R, LANES), lambda b, ph, j: (b, cidx(ph, j), dt_blk)),
            pl.BlockSpec((16, LANES), lambda b, ph, j: (0, 0)),
            pl.BlockSpec((1, SSD_INNER), lambda b, ph, j: (0, 0)),
            pl.BlockSpec((1, SSD_INNER), lambda b, ph, j: (0, 0)),
            st_spec, st_spec,
        ],
        out_specs=[pl.BlockSpec((NB, R, SSD_INNER), lambda b, ph, j: (b, ph * j, 0)), st_spec, st_spec],
        out_shape=[jax.ShapeDtypeStruct((B, L, SSD_INNER), F32), st_shape, st_shape],
        scratch_shapes=[pltpu.VMEM((NB, n, R, SSD_INNER), F32), pltpu.VMEM((NB, SSD_GROUPS, LANES, LANES), F32)],
        compiler_params=pltpu.CompilerParams(
            dimension_semantics=("parallel", "arbitrary", "arbitrary"), vmem_limit_bytes=VMEM_LIMIT_BYTES),
        name="ssd_scan",
    )(ssd, xbc, ssd, par, d_skip.reshape(1, SSD_INNER), norm_g.reshape(1, SSD_INNER), s_f0, s_b0)


HG_HEADS = 4
HG_DK = 64
HG_DV = 64
HG_KEY = HG_HEADS * HG_DK
HG_SUB = 16
HG_CHUNK = 64
HG_SUBS_PER_CHUNK = HG_CHUNK // HG_SUB


def _split_dot(mask_b, x, terms=2):
    out = None
    for _ in range(terms):
        part = x.astype(BF16)
        prod = jnp.dot(mask_b, part, preferred_element_type=F32)
        out = prod if out is None else out + prod
        x = x - part.astype(F32)
    return out


def _hgrn_kernel(q_ref, f_ref, v_ref, g_ref, lbl_ref, ng_ref, sf0_ref, sb0_ref,
                 y_ref, sf_ref, sb_ref, ob_scr, st_scr, *, n_blocks, layer, depth):
    ph = pl.program_id(1)
    j = pl.program_id(2)
    is_fwd = ph == 1
    c = _scan_chunk_index(ph, j, n_blocks)
    NB, R = q_ref.shape[0], q_ref.shape[1]
    W = HG_KEY

    @pl.when((j == 0) & is_fwd)
    def _():
        st_scr[...] = sf0_ref[...]

    @pl.when((j == 0) & jnp.logical_not(is_fwd))
    def _():
        st_scr[...] = sb0_ref[...]

    def lower_bound(d):
        x = lbl_ref[d * depth:(d + 1) * depth, :]
        e = jnp.exp(x - jnp.max(x, axis=0, keepdims=True))
        p = e / jnp.sum(e, axis=0, keepdims=True)
        return jnp.sum(p[1:layer + 1, :], axis=0, keepdims=True) if layer > 0 else jnp.zeros((1, W), F32)

    lb = jnp.where(is_fwd, lower_bound(0), lower_bound(1))
    row = lax.broadcasted_iota(jnp.int32, (R, R), 0)
    col = lax.broadcasted_iota(jnp.int32, (R, R), 1)
    dirge = jnp.where(is_fwd, row - col, col - row) >= 0
    same_sub = (row // HG_SUB) == (col // HG_SUB)
    same_chunk = (row // HG_CHUNK) == (col // HG_CHUNK)
    one = lambda m: jnp.where(m, 1.0, 0.0).astype(BF16)
    dist = jnp.where(is_fwd, row // HG_SUB - col // HG_SUB, col // HG_SUB - row // HG_SUB)
    masks = dict(
        cum16=one(same_sub & dirge), cum64=one(same_chunk & dirge),
        dist=[same_sub & dirge] + [same_chunk & (dist == d) for d in range(1, HG_SUBS_PER_CHUNK)])
    lane = lax.broadcasted_iota(jnp.int32, (R, W), 1)
    head_mask = [(lane // HG_DK) == h for h in range(HG_HEADS)]
    blocks = [_hgrn_block(bi, is_fwd, lb, masks, head_mask, q_ref, f_ref, v_ref) for bi in range(NB)]
    srow = lax.broadcasted_iota(jnp.int32, (W, W), 0)
    scol = lax.broadcasted_iota(jnp.int32, (W, W), 1)
    bd_mask = (srow // HG_DV) == (scol // HG_DK)

    def through_state(bi, reverse):
        o_intra, qd, ke, v_b, tot64 = blocks[bi]
        n_chunks = R // HG_CHUNK
        st = st_scr[bi]
        parts = [None] * n_chunks
        for i in (reversed(range(n_chunks)) if reverse else range(n_chunks)):
            rows = slice(i * HG_CHUNK, (i + 1) * HG_CHUNK)
            parts[i] = lax.dot_general(qd[rows], st.astype(BF16), (((1,), (1,)), ((), ())), preferred_element_type=F32)
            upd = lax.dot_general(v_b[rows], ke[rows], (((0,), (0,)), ((), ())), preferred_element_type=F32)
            st = jnp.exp(tot64[i * HG_CHUNK:i * HG_CHUNK + 1, :]) * st + jnp.where(bd_mask, upd, 0.0)
        st_scr[bi] = st
        return o_intra + jnp.concatenate(parts, axis=0)

    @pl.when(jnp.logical_not(is_fwd))
    def _():
        for bi in range(NB):
            ob_scr[bi, c] = through_state(bi, reverse=True)

    @pl.when(is_fwd)
    def _():
        for bi in range(NB):
            ot = through_state(bi, reverse=False) + ob_scr[bi, c]
            sq = ot * ot
            ms = jnp.zeros((R, W), F32)
            for h in range(HG_HEADS):
                s = jnp.sum(jnp.where(head_mask[h], sq, 0.0), axis=-1, keepdims=True) * (1.0 / HG_DV)
                ms = jnp.where(head_mask[h], s, ms)
            g = g_ref[bi]
            y_ref[bi] = ot * lax.rsqrt(ms + RMS_EPS) * ng_ref[...] * (g * jax.nn.sigmoid(g))

    @pl.when((j == n_blocks - 1) & is_fwd)
    def _():
        sf_ref[...] = st_scr[...]

    @pl.when((j == n_blocks - 1) & jnp.logical_not(is_fwd))
    def _():
        sb_ref[...] = st_scr[...]


def _hgrn_block(bi, is_fwd, lb, masks, head_mask, q_ref, f_ref, v_ref):
    R = q_ref.shape[1]
    W = HG_KEY
    sig = jax.nn.sigmoid(f_ref[bi])
    logf = jnp.log(lb + (1.0 - lb) * sig)
    k = (1.0 - lb) * (1.0 - sig)
    q = q_ref[bi] * (HG_DK ** -0.5)
    v_b = v_ref[bi].astype(BF16)
    logf_hi = logf.astype(BF16)
    logf_lo = (logf - logf_hi.astype(F32)).astype(BF16)
    msum = lambda m: (jnp.dot(m, logf_hi, preferred_element_type=F32) + jnp.dot(m, logf_lo, preferred_element_type=F32))
    def group_total(cum, size):
        c3 = cum.reshape(R // size, size, W)
        last = jnp.where(is_fwd, c3[:, size - 1:size, :], c3[:, 0:1, :])
        return jnp.broadcast_to(last, c3.shape).reshape(R, W)

    b16 = msum(masks["cum16"])
    tot16 = group_total(b16, HG_SUB)
    b64 = msum(masks["cum64"])
    tot64 = group_total(b64, HG_CHUNK)

    shifted = lambda s: jnp.where(is_fwd, pltpu.roll(tot16, R - s, 0), pltpu.roll(tot16, s, 0))
    nx1 = shifted(HG_SUB)
    nx2 = nx1 + shifted(2 * HG_SUB)
    k_end16 = k * jnp.exp(tot16 - b16)
    keys = [(k * jnp.exp(-b16)).astype(BF16), k_end16.astype(BF16),
            (k_end16 * jnp.exp(nx1)).astype(BF16), (k_end16 * jnp.exp(nx2)).astype(BF16)]
    q16 = q * jnp.exp(b16)
    qm =jnp.concatenate([jnp.where(m, q16, 0.0) for m in head_mask], axis=0).astype(BF16)
    att = [jnp.zeros((R, R), F32)] * HG_HEADS
    for d in range(HG_SUBS_PER_CHUNK):
        sc = lax.dot_general(qm, keys[d], (((1,), (1,)), ((), ())), preferred_element_type=F32)
        att = [jnp.where(masks["dist"][d], sc[h * R:(h + 1) * R], att[h]) for h in range(HG_HEADS)]
    o_all = jnp.dot(jnp.concatenate(att, axis=0).astype(BF16), v_b, preferred_element_type=F32)
    o = jnp.zeros((R, W), F32)
    for h in range(HG_HEADS):
        o = jnp.where(head_mask[h], o_all[h * R:(h + 1) * R], o)

    return o, (q * jnp.exp(b64)).astype(BF16), (k * jnp.exp(tot64 - b64)).astype(BF16), v_b, tot64


def hgrn_scan(hg, lb_logits, norm_g, s_f0, s_b0, *, layer, block=256):
    B, L, _ = hg.shape
    W = HG_KEY
    R = min(block, L)
    assert L % R == 0 and R % HG_CHUNK == 0 and B % SCAN_BATCH == 0
    n = L // R
    NB = SCAN_BATCH
    depth = lb_logits.shape[0] // 2
    cidx = lambda ph, j: _scan_chunk_index(ph, j, n)
    col = lambda blk: pl.BlockSpec((NB, R, W), lambda b, ph, j: (b, cidx(ph, j), blk))
    st_spec = pl.BlockSpec((NB, W, W), lambda b, ph, j: (b, 0, 0))
    st_shape = jax.ShapeDtypeStruct((B, W, W), F32)
    return pl.pallas_call(
        functools.partial(_hgrn_kernel, n_blocks=n, layer=layer, depth=depth),
        grid=(B // NB, 2, n),
        in_specs=[
            col(0),
            pl.BlockSpec((NB, R, W), lambda b, ph, j: (b, cidx(ph, j), 2 - ph)),
            col(3), col(4),
            pl.BlockSpec(lb_logits.shape, lambda b, ph, j: (0, 0)),
            pl.BlockSpec((1, W), lambda b, ph, j: (0, 0)),
            st_spec, st_spec,
        ],
        out_specs=[pl.BlockSpec((NB, R, W), lambda b, ph, j: (b, ph * j, 0)), st_spec, st_spec],
        out_shape=[jax.ShapeDtypeStruct((B, L, W), F32), st_shape, st_shape],
        scratch_shapes=[pltpu.VMEM((NB, n, R, W), F32), pltpu.VMEM((NB, W, W), F32)],
        compiler_params=pltpu.CompilerParams(
            dimension_semantics=("parallel", "arbitrary", "arbitrary"), vmem_limit_bytes=VMEM_LIMIT_BYTES),
        name="hgrn_scan",
    )(hg, hg, hg, hg, lb_logits, jnp.tile(norm_g, HG_HEADS).reshape(1, W), s_f0, s_b0)


HY_WIDTH = 256
HY_BANDS = 16
HY_HIDDEN = 64
SUBLANES = 8


def _dot_hi(a, b):
    return jnp.dot(a, b, precision=lax.Precision.HIGHEST, preferred_element_type=F32)


def _hy_filter_kernel(bands_ref, w1t_ref, w1f_ref, b1_ref, fr1_ref, w2_ref, b2_ref, fr2_ref, w3a_ref, w3b_ref,
                      decay_ref, f_ref, den_ref, cosb_scr, sinb_scr, *, L):
    i = pl.program_id(0)
    TR = f_ref.shape[1]
    step = 2.0 * math.pi / L
    bands = bands_ref[...]

    @pl.when(i == 0)
    def _():
        r = lax.broadcasted_iota(jnp.int32, (TR, LANES), 0).astype(F32)
        ang = (step * r) * bands
        cosb_scr[...] = jnp.cos(ang)
        sinb_scr[...] = jnp.sin(ang)
        den_ref[...] = jnp.zeros_like(den_ref)

    one_row = lambda v: jnp.broadcast_to(v, (SUBLANES, LANES))
    base = one_row((step * (i * TR).astype(F32)) * bands)
    cos_a, sin_a = jnp.cos(base)[0:1], jnp.sin(base)[0:1]
    full = one_row((2.0 * math.pi) * bands)
    cos_p, sin_p = jnp.cos(full)[0:1], jnp.sin(full)[0:1]
    cos_b, sin_b = cosb_scr[...], sinb_scr[...]
    cos0 = cos_a * cos_b - sin_a * sin_b
    sin0 = sin_a * cos_b + cos_a * sin_b
    cos1 = cos_p * cos0 + sin_p * sin0
    sin1 = sin_p * cos0 - cos_p * sin0

    m = i * TR + lax.broadcasted_iota(jnp.int32, (TR, HY_WIDTH), 0)
    t0 = m.astype(F32) * (1.0 / (L - 1))
    t1 = jnp.where(m == 0, 0, L - m).astype(F32) * (1.0 / (L - 1))
    lane = lax.broadcasted_iota(jnp.int32, (TR, LANES), 1)
    t_both = jnp.where(lane < HY_HIDDEN, t0[:, :LANES], t1[:, :LANES])
    feats = jnp.concatenate([cos0, sin0, cos1, sin1], axis=1)
    pre = t_both * w1t_ref[...] + _dot_hi(feats, w1f_ref[...]) + b1_ref[...]
    h = jnp.sin(fr1_ref[...] * pre)
    h = jnp.sin(fr2_ref[...] * (_dot_hi(h, w2_ref[...]) + b2_ref[...]))
    decay = jnp.abs(decay_ref[...])
    filt0 = _dot_hi(h, w3a_ref[...]) * jnp.exp(-t0 * decay)
    filt1 = jnp.where(m == 0, 0.0, _dot_hi(h, w3b_ref[...]) * jnp.exp(-t1 * decay))
    f_ref[0] = filt0
    f_ref[1] = filt1
    den_ref[...] += jnp.sum(jnp.abs(filt0) + jnp.abs(filt1), axis=0, keepdims=True)


def hy_filter(L, w1, b1, fr1, w2, b2, fr2, w3, decay, *, tr=512):
    tr = min(tr, L)
    H = HY_HIDDEN
    bands = np.zeros((1, LANES), np.float32)
    bands[0, :HY_BANDS] = np.linspace(1e-4, HY_BANDS - 1, HY_BANDS, dtype=np.float32)
    pad_rows = lambda w: jnp.pad(w, ((0, LANES - HY_BANDS), (0, 0)))
    twice = lambda v: jnp.concatenate([v, v]).reshape(1, 2 * H)
    zeros = jnp.zeros((LANES, H), F32)
    w_cos, w_sin = pad_rows(w1[1:1 + HY_BANDS]), pad_rows(w1[1 + HY_BANDS:])
    w1_feat = jnp.concatenate([
        jnp.concatenate([w_cos, zeros], axis=1), jnp.concatenate([-w_sin, zeros], axis=1),
        jnp.concatenate([zeros, w_cos], axis=1), jnp.concatenate([zeros, -w_sin], axis=1)], axis=0)
    zh = jnp.zeros((H, H), F32)
    w2_both = jnp.concatenate([jnp.concatenate([w2, zh], axis=1), jnp.concatenate([zh, w2], axis=1)], axis=0)
    zw = jnp.zeros((H, HY_WIDTH), F32)
    w3_a = jnp.concatenate([w3[:, :HY_WIDTH], zw], axis=0)
    w3_b = jnp.concatenate([zw, w3[:, HY_WIDTH:]], axis=0)
    args = (jnp.asarray(bands), twice(w1[0]), w1_feat, twice(b1), twice(fr1), w2_both, twice(b2), twice(fr2),
            w3_a, w3_b, decay.reshape(1, -1))
    return pl.pallas_call(
        functools.partial(_hy_filter_kernel, L=L),
        grid=(L // tr,),
        in_specs=[pl.BlockSpec(a.shape, lambda i: (0, 0)) for a in args],
        out_specs=[pl.BlockSpec((2, tr, HY_WIDTH), lambda i: (0, i, 0)), pl.BlockSpec((1, HY_WIDTH), lambda i: (0, 0))],
        out_shape=[jax.ShapeDtypeStruct((2, L, HY_WIDTH), F32), jax.ShapeDtypeStruct((1, HY_WIDTH), F32)],
        scratch_shapes=[pltpu.VMEM((tr, LANES), F32), pltpu.VMEM((tr, LANES), F32)],
        compiler_params=pltpu.CompilerParams(
            dimension_semantics=("arbitrary",), vmem_limit_bytes=VMEM_LIMIT_BYTES),
        name="hy_filter",
    )(*args)


FFT_N1 = 64
FFT_N2 = 128
FFT_N = FFT_N1 * FFT_N2
FFT_L = FFT_N // 2
FFT_N1_NZ = FFT_N1 // 2
FFT_UNROLL_N1_STAGE = 4
FFT_UNROLL_N2_STAGE = 16


@functools.lru_cache(maxsize=None)
def _fft_constants():
    eye = np.eye(SUBLANES)
    k1 = np.arange(FFT_N1)[:, None]
    n1 = np.arange(FFT_N1_NZ)[None, :]
    ang1 = 2.0 * np.pi * ((k1 * n1) % FFT_N1) / FFT_N1
    kron_fwd = np.concatenate([np.kron(np.cos(ang1), eye), np.kron(np.sin(ang1), eye)], axis=0)
    kron_inv = np.concatenate([np.kron(np.cos(ang1).T, eye), np.kron(np.sin(ang1).T, eye)], axis=0) / FFT_N
    k = np.arange(FFT_N1)[:, None, None] + FFT_N1 * np.arange(FFT_N2)[None, :, None]
    n2 = np.arange(FFT_N2)[None, None, :]
    ang2 = 2.0 * np.pi * ((k * n2) % FFT_N) / FFT_N
    g_fwd = np.concatenate([np.cos(ang2), np.sin(ang2)], axis=1)
    g_inv = np.concatenate([np.cos(ang2).transpose(0, 2, 1), np.sin(ang2).transpose(0, 2, 1)], axis=1)
    n1_all = np.arange(FFT_N1)[None, :]
    ang1_all = 2.0 * np.pi * ((k1 * n1_all) % FFT_N1) / FFT_N1
    kron_full = np.concatenate([np.kron(np.cos(ang1_all), eye), np.kron(np.sin(ang1_all), eye)], axis=0)
    to_b = lambda a: jnp.asarray(a, dtype=F32).astype(BF16)
    return to_b(kron_fwd), to_b(g_fwd), to_b(g_inv), to_b(kron_inv), to_b(kron_full)


def _cplx_fwd(r, half_rows, cb):
    cr, sr = r[:half_rows], r[half_rows:]
    return cr[:, :cb] + sr[:, cb:], cr[:, cb:] - sr[:, :cb]


def _cplx_inv(r, half_rows, cb):
    cr, sr = r[:half_rows], r[half_rows:]
    return cr[:, :cb] - sr[:, cb:], cr[:, cb:] + sr[:, :cb]


def _fft_stage1(load_group, kron_ref, a_scr, cb):
    def body(jt, carry):
        rows = pl.ds(pl.multiple_of(jt * SUBLANES, SUBLANES), SUBLANES)
        d = load_group(rows).reshape(kron_ref.shape[1], 2 * cb).astype(BF16)
        r = jnp.dot(kron_ref[...], d, preferred_element_type=F32)
        a_re, a_im = _cplx_fwd(r, FFT_N1 * SUBLANES, cb)
        a_scr[:, rows, :] = jnp.concatenate([a_re, a_im], axis=1).reshape(FFT_N1, SUBLANES, 2 * cb)
        return carry
    lax.fori_loop(0, FFT_N2 // SUBLANES, body, 0, unroll=FFT_UNROLL_N1_STAGE)


def _fft_fwd_data_kernel(z_ref, h_ref, kron_ref, g_ref, p_ref, a_scr):
    cb = z_ref.shape[4]
    load = lambda rows: jnp.concatenate([z_ref[0, 0, :, rows, :], z_ref[0, 1, :, rows, :]], axis=-1)
    _fft_stage1(load, kron_ref, a_scr, cb)

    def body(k1, carry):
        r = jnp.dot(g_ref[k1], a_scr[k1].astype(BF16), preferred_element_type=F32)
        x_re, x_im = _cplx_fwd(r, FFT_N2, cb)
        h = h_ref[0, k1]
        h_re, h_im = h[:, :cb], h[:, cb:]
        p_ref[0, 0, k1] = jnp.concatenate([x_re * h_re - x_im * h_im, x_re * h_im + x_im * h_re], axis=1).astype(BF16)
        return carry
    lax.fori_loop(0, FFT_N1, body, 0, unroll=FFT_UNROLL_N2_STAGE)


def _fft_fwd_filter_kernel(f_ref, den_ref, kron_ref, g_ref, h_ref, a_scr):
    cb = f_ref.shape[3]
    inv_den = 1.0 / den_ref[...]

    def load(rows):
        re = jnp.concatenate([f_ref[0, :, rows, :], f_ref[1, :, rows, :]], axis=0) * inv_den
        return jnp.concatenate([re, jnp.zeros_like(re)], axis=-1)
    _fft_stage1(load, kron_ref, a_scr, cb)

    def body(k1, carry):
        r = jnp.dot(g_ref[k1], a_scr[k1].astype(BF16), preferred_element_type=F32)
        x_re, x_im = _cplx_fwd(r, FFT_N2, cb)
        h_ref[0, k1] = jnp.concatenate([x_re, x_im], axis=1)
        return carry
    lax.fori_loop(0, FFT_N1, body, 0, unroll=FFT_UNROLL_N2_STAGE)


def _fft_inv_kernel(p_ref, x0_ref, w_ref, bias_ref, gi_ref, kron_ref, o_ref, b_scr):
    cb = o_ref.shape[4]

    def body3(k1, carry):
        r = jnp.dot(gi_ref[k1], p_ref[0, 0, k1], preferred_element_type=F32)
        b_re, b_im = _cplx_inv(r, FFT_N2, cb)
        b_scr[k1] = jnp.concatenate([b_re, b_im], axis=1)
        return carry
    lax.fori_loop(0, FFT_N1, body3, 0, unroll=FFT_UNROLL_N2_STAGE)

    bias = bias_ref[...].reshape(1, 1, cb)

    def body4(jt, carry):
        rows = pl.ds(pl.multiple_of(jt * SUBLANES, SUBLANES), SUBLANES)
        d = b_scr[:, rows, :].reshape(FFT_N1 * SUBLANES, 2 * cb).astype(BF16)
        r = jnp.dot(kron_ref[...], d, preferred_element_type=F32)
        y_re, y_im = _cplx_inv(r, FFT_N1_NZ * SUBLANES, cb)
        for which, y in enumerate((y_re, y_im)):
            y3 = y.reshape(FFT_N1_NZ, SUBLANES, cb)
            o_ref[0, which, :, rows, :] = x0_ref[0, which, :, rows, :] * (y3 + w_ref[0, which, :, rows, :] * bias)
        return carry
    lax.fori_loop(0, FFT_N2 // SUBLANES, body4, 0, unroll=FFT_UNROLL_N1_STAGE)


def hyena_spectrum(fu, den, *, cb=128):
    _, g_fwd, _, _, kron_full = _fft_constants()
    W = fu.shape[2]
    ncb = W // cb
    f4 = fu.reshape(2, FFT_N1_NZ, FFT_N2, W)
    const = lambda a: pl.BlockSpec(a.shape, lambda c: (0,) * a.ndim, pipeline_mode=pl.Buffered(1))
    return pl.pallas_call(
        _fft_fwd_filter_kernel,
        grid=(ncb,),
        in_specs=[
            pl.BlockSpec((2, FFT_N1_NZ, FFT_N2, cb), lambda c: (0, 0, 0, c)),
            pl.BlockSpec((1, cb), lambda c: (0, c)),
            const(kron_full), const(g_fwd),
        ],
        out_specs=pl.BlockSpec((1, FFT_N1, FFT_N2, 2 * cb), lambda c: (c, 0, 0, 0)),
        out_shape=jax.ShapeDtypeStruct((ncb, FFT_N1, FFT_N2, 2 * cb), F32),
        scratch_shapes=[pltpu.VMEM((FFT_N1, FFT_N2, 2 * cb), F32)],
        compiler_params=pltpu.CompilerParams(
            dimension_semantics=("parallel",), vmem_limit_bytes=VMEM_LIMIT_BYTES),
        name="hyena_spectrum",
    )(f4, den, kron_full, g_fwd)


def hyena_conv(x0c, hw, spec, bias, *, cb=128):
    kron_fwd, g_fwd, g_inv, kron_inv, _ = _fft_constants()
    B, L, W = hw.shape
    assert L == FFT_L and B % 2 == 0
    ncb = W // cb
    pair_shape = (B // 2, 2, FFT_N1_NZ, FFT_N2, W)
    x5 = x0c.reshape(pair_shape)
    w5 = hw.reshape(pair_shape)
    const = lambda a: pl.BlockSpec(a.shape, lambda c, p: (0,) * a.ndim, pipeline_mode=pl.Buffered(1))
    pair = pl.BlockSpec((1, 2, FFT_N1_NZ, FFT_N2, cb), lambda c, p: (p, 0, 0, 0, c))
    pspec = pl.BlockSpec((1, 1, FFT_N1, FFT_N2, 2 * cb), lambda c, p: (c, p, 0, 0, 0))
    prod = pl.pallas_call(
        _fft_fwd_data_kernel,
        grid=(ncb, B // 2),
        in_specs=[pair, pl.BlockSpec((1, FFT_N1, FFT_N2, 2 * cb), lambda c, p: (c, 0, 0, 0)),
                  const(kron_fwd), const(g_fwd)],
        out_specs=pspec,
        out_shape=jax.ShapeDtypeStruct((ncb, B // 2, FFT_N1, FFT_N2, 2 * cb), BF16),
        scratch_shapes=[pltpu.VMEM((FFT_N1, FFT_N2, 2 * cb), F32)],
        compiler_params=pltpu.CompilerParams(
            dimension_semantics=("parallel", "parallel"), vmem_limit_bytes=VMEM_LIMIT_BYTES),
        name="hyena_fft_fwd",
    )(w5, spec, kron_fwd, g_fwd)
    out = pl.pallas_call(
        _fft_inv_kernel,
        grid=(ncb, B // 2),
        in_specs=[pspec, pair, pair, pl.BlockSpec((1, cb), lambda c, p: (0, c)), const(g_inv), const(kron_inv)],
        out_specs=pair,
        out_shape=jax.ShapeDtypeStruct(pair_shape, F32),
        scratch_shapes=[pltpu.VMEM((FFT_N1, FFT_N2, 2 * cb), F32)],
        compiler_params=pltpu.CompilerParams(
            dimension_semantics=("parallel", "parallel"), vmem_limit_bytes=VMEM_LIMIT_BYTES),
        name="hyena_fft_inv",
    )(prod, x5, w5, bias.reshape(1, W), g_inv, kron_inv)
    return out.reshape(B, L, W)


@functools.lru_cache(maxsize=None)
def _dense_dft_constants(L):
    n_full = 2 * L
    k = np.arange(n_full)[:, None]
    n = np.arange(n_full)[None, :]
    ang = 2.0 * np.pi * ((k * n) % n_full) / n_full
    fwd = np.concatenate([np.cos(ang), np.sin(ang)], axis=0)
    inv = np.concatenate([np.cos(ang[:L]), np.sin(ang[:L])], axis=0) / n_full
    return jnp.asarray(fwd, dtype=F32), jnp.asarray(inv, dtype=F32)


def _hyena_short_kernel(x0_ref, w_ref, f_ref, den_ref, bias_ref, ff_ref, fi_ref, o_ref, h_scr):
    L, cb = o_ref.shape[2], o_ref.shape[3]

    @pl.when(pl.program_id(1) == 0)
    def _():
        filt = jnp.concatenate([f_ref[0], f_ref[1]], axis=0) / den_ref[...]
        r = _dot_hi(ff_ref[...], filt)
        h_scr[0] = r[:2 * L]
        h_scr[1] = -r[2 * L:]

    h_re, h_im = h_scr[0], h_scr[1]
    z = jnp.concatenate([w_ref[0, 0], w_ref[0, 1]], axis=1).astype(BF16)
    x_re, x_im = _cplx_fwd(jnp.dot(ff_ref[:, :L].astype(BF16), z, preferred_element_type=F32), 2 * L, cb)
    p = jnp.concatenate([x_re * h_re - x_im * h_im, x_re * h_im + x_im * h_re], axis=1).astype(BF16)
    y_re, y_im = _cplx_inv(jnp.dot(fi_ref[...].astype(BF16), p, preferred_element_type=F32), L, cb)
    for which, y in enumerate((y_re, y_im)):
        o_ref[0, which] = x0_ref[0, which] * (y + w_ref[0, which] * bias_ref[...])


def hyena_conv_short(x0c, hw, fu, den, bias, *, cb=128):
    B, L, W = hw.shape
    fwd, inv = _dense_dft_constants(L)
    pair_shape = (B // 2, 2, L, W)
    pair = pl.BlockSpec((1, 2, L, cb), lambda c, p: (p, 0, 0, c))
    vec = pl.BlockSpec((1, cb), lambda c, p: (0, c))
    const = lambda a: pl.BlockSpec(a.shape, lambda c, p: (0, 0))
    out = pl.pallas_call(
        _hyena_short_kernel,
        grid=(W // cb, B // 2),
        in_specs=[pair, pair, pl.BlockSpec((2, L, cb), lambda c, p: (0, 0, c)), vec, vec, const(fwd), const(inv)],
        out_specs=pair,
        out_shape=jax.ShapeDtypeStruct(pair_shape, F32),
        scratch_shapes=[pltpu.VMEM((2, 2 * L, cb), F32)],
        compiler_params=pltpu.CompilerParams(
            dimension_semantics=("parallel", "arbitrary"), vmem_limit_bytes=VMEM_LIMIT_BYTES),
        name="hyena_conv_short",
    )(x0c.reshape(pair_shape), hw.reshape(pair_shape), fu, den, bias.reshape(1, W), fwd, inv)
    return out.reshape(B, L, W)


GRID_W = 64
SC_COLS = 3 * BRANCH_W
HG_COLS = 3 * HG_KEY + 2 * HG_HEADS * HG_DV
SSD_XBC = SSD_INNER + 2 * SSD_GROUPS * SSD_STATE
assert SSD_XBC == SSD_XBC_COLS and HY_COLS == 3 * HY_WIDTH


def _sincos_1d(pos, dim):
    omega = 1.0 / (10000.0 ** (jnp.arange(dim // 2, dtype=F32) / (dim // 2)))
    ang = pos.astype(F32)[:, None] * omega[None]
    return jnp.concatenate([jnp.sin(ang), jnp.cos(ang)], -1)


def _grid_pos_embed(rows, dim):
    row = jnp.repeat(jnp.arange(rows), GRID_W)
    col = jnp.tile(jnp.arange(GRID_W), rows)
    return jnp.concatenate([_sincos_1d(row, dim // 2), _sincos_1d(col, dim // 2)], -1)


def kernel(x, c, ctx, c_ctx, w_ada, b_ada, w_in, hy_conv_w, hy_conv_b, hy_w1, hy_b1, hy_freq1,
           hy_w2, hy_b2, hy_freq2, hy_w3, hy_decay, hy_bias, sc_conv_w, hg_lb_logits, hg_norm_g,
           ssd_conv_w, ssd_conv_b, ssd_a_log, ssd_dt_bias, ssd_d, ssd_norm_g, w_gate, b_gate, w_br,
           w_o, ln1_g, ln1_b, ln2_g, ln2_b, w_router, b_router, w_e1, w_e3, w_e2):
    B, L, D = x.shape
    LC = ctx.shape[1]
    depth = w_in.shape[0]
    pos_table = _grid_pos_embed(L // GRID_W, D).astype(x.dtype)
    lat = x.reshape(B * L, D)
    cx = ctx.reshape(B * LC, D)

    n_vec = -(-(B + 1) // SUBLANES) * SUBLANES
    cvecs = jnp.concatenate([c, c_ctx[None], jnp.zeros((n_vec - B - 1, D), c.dtype)], axis=0)
    mods = ada_mod(cvecs, w_ada, b_ada).reshape(depth, n_vec, 6, D)
    lb_logits = hg_lb_logits.reshape(2 * depth, HG_KEY)

    out_widths = (BRANCH_W, BRANCH_W, BRANCH_W, SSD_XBC, HG_COLS, SSD_INNER + LANES)
    z_state = lambda *shape: jnp.zeros(shape, F32)
    w_e1_b, w_e3_b, w_e2_b = w_e1.astype(BF16), w_e3.astype(BF16), w_e2.astype(BF16)
    c_hy, c_sc, c_hg = HY_COLS, HY_COLS + SC_COLS, HY_COLS + SC_COLS + HG_COLS
    c_z, c_xbc = c_hg + SSD_INNER, c_hg + SSD_INNER + SSD_XBC

    for l in range(depth):
        ctx_out = l < depth - 1
        mod_lat, mod_ctx = mods[l, :B], mods[l, B:B + 1]
        wl = w_in[l]
        w_in_b = jnp.concatenate([
            wl[:, :c_hy], wl[:, c_hy + BRANCH_W:c_sc], wl[:, c_z:c_xbc],
            wl[:, c_hy:c_hy + BRANCH_W], wl[:, c_sc:c_hg], wl[:, c_hg:c_z],
            jnp.pad(wl[:, c_xbc:], ((0, 0), (0, LANES - 2 * SSD_HEADS))),
        ], axis=1).astype(BF16)
        conv_par = (hy_conv_w[l], hy_conv_b[l], sc_conv_w[l], ssd_conv_w[l], ssd_conv_b[l])
        ssd_par = jnp.zeros((2 * SUBLANES, LANES), F32)
        ssd_par = ssd_par.at[:SUBLANES, :2 * SSD_HEADS].set(jnp.broadcast_to(ssd_a_log[l].reshape(1, -1), (SUBLANES, 2 * SSD_HEADS)))
        ssd_par = ssd_par.at[SUBLANES:, :2 * SSD_HEADS].set(jnp.broadcast_to(ssd_dt_bias[l].reshape(1, -1), (SUBLANES, 2 * SSD_HEADS)))
        d_skip = jnp.repeat(ssd_d[l], SSD_HEADDIM)
        filt_args = (hy_w1[l], hy_b1[l], hy_freq1[l], hy_w2[l], hy_b2[l], hy_freq2[l], hy_w3[l], hy_decay[l])

        def mixers(tokens, mod, seg_len, hg_state, ssd_state, want_out, pos=None):
            n_seq = tokens.shape[0] // seg_len
            x0c, hw, sc_y, xbc, hg, zdt = in_proj(
                tokens, mod, w_in_b, conv_par, out_widths, seq_len=seg_len, rows_per_mod=mod_rows(mod, tokens),
                tm=min(512, seg_len), pos=pos)
            seq = lambda a: a.reshape(n_seq, seg_len, a.shape[1])
            hg_y, hg_f, hg_b = hgrn_scan(seq(hg), lb_logits, hg_norm_g[l], *hg_state, layer=l)
            ssd_y, ssd_f, ssd_b = ssd_scan(seq(zdt), seq(xbc), ssd_par, d_skip, ssd_norm_g[l], *ssd_state,
                                           z_blk=0, dt_blk=SSD_INNER // LANES)
            branches = None
            if want_out:
                fu, den = hy_filter(seg_len, *filt_args)
                if seg_len == FFT_L:
                    hy_y = hyena_conv(seq(x0c), seq(hw), hyena_spectrum(fu, den), hy_bias[l])
                else:
                    hy_y = hyena_conv_short(seq(x0c), seq(hw), fu, den, hy_bias[l])
                flat = lambda a: a.reshape(tokens.shape[0], a.shape[2])
                branches = [flat(hy_y), sc_y, flat(hg_y), flat(ssd_y)]
            return branches, (hg_f, hg_b), (ssd_f, ssd_b)

        def mod_rows(mod, tokens):
            return tokens.shape[0] // mod.shape[0]

        def finish(tokens, mod, branches, pos=None):
            rows = mod_rows(mod, tokens)
            t1 = merge_ln(tokens, mod, branches, w_gate[l].astype(BF16), b_gate[l], w_br[l].astype(BF16),
                          w_o[l].astype(BF16), ln1_g[l], ln1_b[l], rows_per_mod=rows, tm=512, pos=pos)
            return moe_ln(t1, mod, w_router, b_router, w_e1_b, w_e3_b, w_e2_b, ln2_g[l], ln2_b[l],
                          layer=l, rows_per_mod=rows, tm=1024)

        zero_hg = (z_state(B, HG_KEY, HG_KEY),) * 2
        zero_ssd = (z_state(B, SSD_GROUPS, LANES, LANES),) * 2
        br_ctx, hg_state, ssd_state = mixers(cx, mod_ctx, LC, zero_hg, zero_ssd, ctx_out)
        pos = pos_table if l == 0 else None
        br_lat, _, _ = mixers(lat, mod_lat, L, hg_state, ssd_state, True, pos=pos)
        lat = finish(lat, mod_lat, br_lat, pos=pos)
        if ctx_out:
            cx = finish(cx, mod_ctx, br_ctx)
    return lat.reshape(B, L, D)
```
